```python
import jax, jax.numpy as jnp
from jax import lax
import numpy as np

D_MODEL = 1024
BATCH = 4
SEQ = 8192
DEPTH = 1

CHUNK = 64
Q_BLOCK = 128
RET_HEADS = 4
RET_HEAD_DIM = 128
RET_WIDTH = RET_HEADS * RET_HEAD_DIM
FOX_HEADS = 8
FOX_HEAD_DIM = 64
FOX_WIDTH = FOX_HEADS * FOX_HEAD_DIM
MIX_WIDTH = RET_WIDTH + FOX_WIDTH
IN_PROJ_WIDTH = 4 * RET_WIDTH + 3 * FOX_WIDTH + FOX_HEADS
ROPE_BASE = 10000.0
N_EXPERTS = 32
TOP_K = 4
D_FF = D_MODEL
SWIGLU_ALPHA = 1.702
SWIGLU_LIMIT = 7.0
EXPERT_BLOCK = 256
RMS_EPS = 1e-5
GN_EPS = 1e-5

kernel_name = "hybrid_retention_fox_moe_block"


def rms_norm(x, g):
    xf = x.astype(jnp.float32)
    y = xf * lax.rsqrt(jnp.mean(xf * xf, axis=-1, keepdims=True) + RMS_EPS)
    return (y * g.astype(jnp.float32)).astype(x.dtype)


def rotary(x, pos):
    half = x.shape[-1] // 2
    inv_freq = ROPE_BASE ** (-jnp.arange(half, dtype=jnp.float32) / half)
    ang = pos[:, None] * inv_freq[None, :]
    cos = jnp.cos(ang)[None, :, None, :]
    sin = jnp.sin(ang)[None, :, None, :]
    x1, x2 = x[..., :half], x[..., half:]
    return jnp.concatenate([x1 * cos - x2 * sin, x1 * sin + x2 * cos], axis=-1)


def retention(q, k, v, g):
    dtype = q.dtype
    B, S, H, d = q.shape
    NC = S // CHUNK
    log_gamma = jnp.log1p(-jnp.exp2(-5.0 - jnp.arange(H, dtype=jnp.float32)))
    qf = q.astype(jnp.float32).reshape(B, NC, CHUNK, H, d)
    kf = (k.astype(jnp.float32) * (d ** -0.5)).reshape(B, NC, CHUNK, H, d)
    vf = v.astype(jnp.float32).reshape(B, NC, CHUNK, H, d)
    pos_c = jnp.arange(CHUNK, dtype=jnp.float32)
    dist = jnp.abs(pos_c[:, None] - pos_c[None, :])
    decay_intra = jnp.exp(log_gamma[:, None, None] * dist)
    scores = jnp.einsum('bnqhd,bnkhd->bnhqk', qf, kf) * decay_intra
    o_intra = jnp.einsum('bnhqk,bnkhe->bnqhe', scores, vf)
    q_w = jnp.exp(log_gamma[None, :] * (pos_c[:, None] + 1.0))
    k_w = jnp.exp(log_gamma[None, :] * (CHUNK - 1.0 - pos_c[:, None]))
    chunk_decay = jnp.exp(log_gamma * CHUNK)[None, :, None, None]
    q_t = jnp.moveaxis(qf * q_w[:, :, None], 1, 0)
    k_t = jnp.moveaxis(kf * k_w[:, :, None], 1, 0)
    v_t = jnp.moveaxis(vf, 1, 0)

    def step(state, inp):
        qc, kc, vc = inp
        o = jnp.einsum('bqhd,bhde->bqhe', qc, state)
        new_state = state * chunk_decay + jnp.einsum('bkhd,bkhe->bhde', kc, vc)
        return new_state, o

    state0 = jnp.zeros((B, H, d, d), jnp.float32)
    _, o_inter = lax.scan(step, state0, (q_t, k_t, v_t))
    o = (o_intra + jnp.moveaxis(o_inter, 0, 1)).reshape(B, S, H, d)
    mu = jnp.mean(o, axis=-1, keepdims=True)
    var = jnp.mean(jnp.square(o - mu), axis=-1, keepdims=True)
    o = (o - mu) * lax.rsqrt(var + GN_EPS)
    o = o * jax.nn.silu(g.astype(jnp.float32))
    return o.reshape(B, S, H * d).astype(dtype)


def forgetting_attention(q, k, v, z_f, b_f):
    dtype = q.dtype
    B, S, H, d = q.shape
    NQB = S // Q_BLOCK
    log_f = jax.nn.log_sigmoid(z_f.astype(jnp.float32) + b_f.astype(jnp.float32))
    F = jnp.cumsum(log_f, axis=1).transpose(0, 2, 1)
    qh = q.astype(jnp.float32).transpose(0, 2, 1, 3) * (d ** -0.5)
    kh = k.astype(jnp.float32).transpose(0, 2, 1, 3)
    vh = v.astype(jnp.float32).transpose(0, 2, 1, 3)
    q_blocks = qh.reshape(B, H, NQB, Q_BLOCK, d).transpose(2, 0, 1, 3, 4)
    F_blocks = F.reshape(B, H, NQB, Q_BLOCK).transpose(2, 0, 1, 3)
    t_blocks = jnp.arange(S, dtype=jnp.int32).reshape(NQB, Q_BLOCK)
    s_idx = jnp.arange(S, dtype=jnp.int32)

    def block(args):
        q_i, F_i, t_i = args
        logits = jnp.einsum('bhqd,bhkd->bhqk', q_i, kh) + F_i[..., None] - F[:, :, None, :]
        mask = s_idx[None, :] <= t_i[:, None]
        logits = jnp.where(mask[None, None], logits, -jnp.inf)
        p = jax.nn.softmax(logits, axis=-1)
        return jnp.einsum('bhqk,bhkd->bhqd', p, vh)

    out = lax.map(block, (q_blocks, F_blocks, t_blocks))
    out = out.transpose(1, 0, 3, 2, 4).reshape(B, S, H * d)
    return out.astype(dtype)


def clamped_swiglu(h):
    x_glu, x_lin = h[..., :D_FF], h[..., D_FF:]
    x_glu = jnp.minimum(x_glu, SWIGLU_LIMIT)
    x_lin = jnp.clip(x_lin, -SWIGLU_LIMIT, SWIGLU_LIMIT)
    return x_glu * jax.nn.sigmoid(SWIGLU_ALPHA * x_glu) * (x_lin + 1.0)


def moe(x, w_router, b_router, w1, b1, w2, b2):
    B, S, D = x.shape
    T = B * S
    A = T * TOP_K
    xs = x.reshape(T, D)
    logits = xs.astype(jnp.float32) @ w_router.astype(jnp.float32) + b_router.astype(jnp.float32)
    top_val, top_idx = lax.top_k(logits, TOP_K)
    gates = jax.nn.softmax(top_val, axis=-1)
    e_flat = top_idx.reshape(A).astype(jnp.int32)
    tok_flat = jnp.repeat(jnp.arange(T, dtype=jnp.int32), TOP_K)
    gate_flat = gates.reshape(A)
    order = jnp.argsort(e_flat, stable=True)
    e_s, tok_s, gate_s = e_flat[order], tok_flat[order], gate_flat[order]
    counts = jax.ops.segment_sum(jnp.ones((A,), jnp.int32), e_flat, num_segments=N_EXPERTS)
    starts = jnp.cumsum(counts) - counts
    padded = (counts + EXPERT_BLOCK - 1) // EXPERT_BLOCK * EXPERT_BLOCK
    pad_ends = jnp.cumsum(padded)
    pad_starts = pad_ends - padded
    dest = pad_starts[e_s] + (jnp.arange(A, dtype=jnp.int32) - starts[e_s])
    NB = (A + EXPERT_BLOCK - 1) // EXPERT_BLOCK + N_EXPERTS
    buf_tok = jnp.full((NB * EXPERT_BLOCK,), T, jnp.int32).at[dest].set(tok_s)
    buf_gate = jnp.zeros((NB * EXPERT_BLOCK,), jnp.float32).at[dest].set(gate_s)
    block_start = jnp.arange(NB, dtype=jnp.int32) * EXPERT_BLOCK
    block_exp = jnp.minimum(jnp.searchsorted(pad_ends, block_start, side='right'),
                            N_EXPERTS - 1).astype(jnp.int32)
    x_pad = jnp.concatenate([xs, jnp.zeros((1, D), xs.dtype)], axis=0)

    def expert_block(args):
        tok_b, gate_b, e = args
        xb = x_pad[tok_b]
        h = xb @ w1[e] + b1[e]
        y = clamped_swiglu(h) @ w2[e] + b2[e]
        return y * gate_b[:, None].astype(y.dtype)

    y_blocks = lax.map(expert_block, (buf_tok.reshape(NB, EXPERT_BLOCK),
                                      buf_gate.reshape(NB, EXPERT_BLOCK), block_exp))
    out = jnp.zeros((T + 1, D), y_blocks.dtype).at[buf_tok].add(y_blocks.reshape(-1, D))
    return out[:T].reshape(B, S, D).astype(x.dtype)


def setup_inputs(seed: int = 0) -> dict:
    key = jax.random.key(seed)
    ks = jax.random.split(key, 14)
    f32 = jnp.float32
    x = jax.random.normal(ks[0], (BATCH, SEQ, D_MODEL), f32)
    norm_mix_g = 1.0 + 0.02 * jax.random.normal(ks[1], (DEPTH, D_MODEL), f32)
    w_in = jax.random.normal(ks[2], (DEPTH, D_MODEL, IN_PROJ_WIDTH), f32) * D_MODEL ** -0.5
    b_forget = jax.random.uniform(ks[3], (DEPTH, FOX_HEADS), f32, minval=1.0, maxval=6.0)
    w_out = jax.random.normal(ks[4], (DEPTH, MIX_WIDTH, D_MODEL), f32) * MIX_WIDTH ** -0.5
    norm_ffn_g = 1.0 + 0.02 * jax.random.normal(ks[5], (DEPTH, D_MODEL), f32)
    w_router = jax.random.normal(ks[6], (DEPTH, D_MODEL, N_EXPERTS), f32) * D_MODEL ** -0.5
    b_router = 0.01 * jax.random.normal(ks[7], (DEPTH, N_EXPERTS), f32)
    w_exp_in = jax.random.normal(ks[8], (DEPTH, N_EXPERTS, D_MODEL, 2 * D_FF), f32) * D_MODEL ** -0.5
    b_exp_in = 0.01 * jax.random.normal(ks[9], (DEPTH, N_EXPERTS, 2 * D_FF), f32)
    w_exp_out = jax.random.normal(ks[10], (DEPTH, N_EXPERTS, D_FF, D_MODEL), f32) * D_FF ** -0.5
    b_exp_out = 0.01 * jax.random.normal(ks[11], (DEPTH, N_EXPERTS, D_MODEL), f32)
    norm_final_g = 1.0 + 0.02 * jax.random.normal(ks[12], (D_MODEL,), f32)
    return {"x": x, "norm_mix_g": norm_mix_g, "w_in": w_in, "b_forget": b_forget,
            "w_out": w_out, "norm_ffn_g": norm_ffn_g, "w_router": w_router,
            "b_router": b_router, "w_exp_in": w_exp_in, "b_exp_in": b_exp_in,
            "w_exp_out": w_exp_out, "b_exp_out": b_exp_out, "norm_final_g": norm_final_g}


def reference(x, norm_mix_g, w_in, b_forget, w_out, norm_ffn_g, w_router, b_router,
              w_exp_in, b_exp_in, w_exp_out, b_exp_out, norm_final_g):
    B, S, _ = x.shape
    pos = jnp.arange(S, dtype=jnp.float32)
    split_points = [RET_WIDTH * i for i in range(1, 5)] + \
                   [4 * RET_WIDTH + FOX_WIDTH * i for i in range(1, 4)]
    h = x
    for l in range(DEPTH):
        u = rms_norm(h, norm_mix_g[l])
        proj = u @ w_in[l]
        rq, rk, rv, rg, fq, fk, fv, fz = jnp.split(proj, split_points, axis=-1)
        rh = lambda t: t.reshape(B, S, RET_HEADS, RET_HEAD_DIM)
        fh = lambda t: t.reshape(B, S, FOX_HEADS, FOX_HEAD_DIM)
        o_ret = retention(rotary(rh(rq), pos), rotary(rh(rk), pos), rh(rv), rh(rg))
        o_fox = forgetting_attention(fh(fq), fh(fk), fh(fv), fz, b_forget[l])
        mix = jnp.concatenate([o_ret, o_fox], axis=-1)
        h = h + mix @ w_out[l]
        h = h + moe(rms_norm(h, norm_ffn_g[l]), w_router[l], b_router[l],
                    w_exp_in[l], b_exp_in[l], w_exp_out[l], b_exp_out[l])
    return rms_norm(h, norm_final_g)
```

```python
import functools

import numpy as np
import jax
import jax.numpy as jnp
from jax import lax
from jax.experimental import pallas as pl
from jax.experimental.pallas import tpu as pltpu

F32 = jnp.float32
BF16 = jnp.bfloat16

D_MODEL = 1024
RET_HEADS, RET_HEAD_DIM = 4, 128
RET_WIDTH = RET_HEADS * RET_HEAD_DIM
FOX_HEADS, FOX_HEAD_DIM = 8, 64
FOX_WIDTH = FOX_HEADS * FOX_HEAD_DIM
CHUNK = 64
ROPE_BASE = 10000.0
N_EXPERTS = 32
TOP_K = 4
D_FF = D_MODEL
SWIGLU_ALPHA = 1.702
SWIGLU_LIMIT = 7.0
RMS_EPS = 1e-5
GN_EPS = 1e-5

LANES = 128
SUBLANES = 8
ROW_TILES = D_MODEL // LANES

PROJ_TILE = 512
RET_BLOCK = 256
FOX_TQ = 512
FOX_TK = PROJ_TILE
AUG = 128
MOE_TILE = 256
EXPERT_BLOCK = 256

NT_DIMS = (((1,), (1,)), ((), ()))


def _split3(a):
    hi = a.astype(BF16)
    r1 = a - hi.astype(F32)
    mid = r1.astype(BF16)
    lo = (r1 - mid.astype(F32)).astype(BF16)
    return hi, mid, lo


def _dot(a, b):
    return jnp.dot(a, b, preferred_element_type=F32)


def _dot_nt(a, b):
    return lax.dot_general(a, b, NT_DIMS, preferred_element_type=F32)


def _rms(x, g):
    return x * lax.rsqrt(jnp.mean(x * x, axis=-1, keepdims=True) + RMS_EPS) * g


def _in_proj_kernel(x_ref, g_ref, cos_ref, sin_ref, wr_ref, wfk_ref, wfqt_ref, wfvt_ref, wz_ref, wzt_ref,
                    rq_ref, rk_ref, rv_ref, rg_ref, fk_ref, fqt_ref, fvt_ref, z_ref, zt_ref):
    u = _rms(x_ref[0], g_ref[...]).astype(BF16)
    r = _dot(u, wr_ref[...])
    cos, sin = cos_ref[...], sin_ref[...]
    k_scale = RET_HEAD_DIM ** -0.5
    for h in range(RET_HEADS):
        lo = h * RET_HEAD_DIM
        q = r[:, lo:lo + RET_HEAD_DIM]
        k = r[:, RET_WIDTH + lo:RET_WIDTH + lo + RET_HEAD_DIM]
        rq_ref[0, :, lo:lo + RET_HEAD_DIM] = (q * cos + pltpu.roll(q, RET_HEAD_DIM // 2, 1) * sin).astype(BF16)
        rk_ref[0, :, lo:lo + RET_HEAD_DIM] = (
            (k * cos + pltpu.roll(k, RET_HEAD_DIM // 2, 1) * sin) * k_scale).astype(BF16)
    rv_ref[0] = r[:, 2 * RET_WIDTH:3 * RET_WIDTH].astype(BF16)
    rg_ref[0] = r[:, 3 * RET_WIDTH:4 * RET_WIDTH].astype(BF16)
    fk_ref[0] = _dot(u, wfk_ref[...]).astype(BF16)
    fqt_ref[0] = _dot_nt(wfqt_ref[...], u).astype(BF16)
    fvt_ref[0, 0] = _dot_nt(wfvt_ref[...], u).astype(BF16)
    z_ref[0] = _dot(u, wz_ref[...])
    zt_ref[0] = _dot_nt(wzt_ref[...], u)


def _in_proj(x, g, cos, sin, wr, wfk, wfqt, wfvt, wz, wzt):
    B, S, D = x.shape
    TM = PROJ_TILE
    ns = S // TM
    const = lambda shape: pl.BlockSpec(shape, lambda b, s: (0,) * len(shape))
    tok = lambda w: pl.BlockSpec((1, TM, w), lambda b, s: (b, s, 0))
    out_shape = (
        jax.ShapeDtypeStruct((B, S, RET_WIDTH), BF16),
        jax.ShapeDtypeStruct((B, S, RET_WIDTH), BF16),
        jax.ShapeDtypeStruct((B, S, RET_WIDTH), BF16),
        jax.ShapeDtypeStruct((B, S, RET_WIDTH), BF16),
        jax.ShapeDtypeStruct((B, S, FOX_WIDTH), BF16),
        jax.ShapeDtypeStruct((B, FOX_WIDTH, S), BF16),
        jax.ShapeDtypeStruct((B, ns, FOX_WIDTH, TM), BF16),
        jax.ShapeDtypeStruct((B, S, LANES), F32),
        jax.ShapeDtypeStruct((B, 16, S), F32),
    )
    return pl.pallas_call(
        _in_proj_kernel,
        grid=(B, ns),
        in_specs=[
            pl.BlockSpec((1, TM, D), lambda b, s: (b, s, 0)),
            const((1, D)),
            pl.BlockSpec((TM, RET_HEAD_DIM), lambda b, s: (s, 0)),
            pl.BlockSpec((TM, RET_HEAD_DIM), lambda b, s: (s, 0)),
            const(wr.shape), const(wfk.shape), const(wfqt.shape), const(wfvt.shape),
            const(wz.shape), const(wzt.shape),
        ],
        out_specs=(
            tok(RET_WIDTH), tok(RET_WIDTH), tok(RET_WIDTH), tok(RET_WIDTH), tok(FOX_WIDTH),
            pl.BlockSpec((1, FOX_WIDTH, TM), lambda b, s: (b, 0, s)),
            pl.BlockSpec((1, 1, FOX_WIDTH, TM), lambda b, s: (b, s, 0, 0)),
            tok(LANES),
            pl.BlockSpec((1, 16, TM), lambda b, s: (b, 0, s)),
        ),
        out_shape=out_shape,
        compiler_params=pltpu.CompilerParams(
            dimension_semantics=("arbitrary", "arbitrary"), vmem_limit_bytes=48 * 1024 * 1024),
        name="in_proj",
    )(x, g, cos, sin, wr, wfk, wfqt, wfvt, wz, wzt)


def _fox_prep_kernel(z_ref, zt_ref, fk_ref, fqt_ref, brow_ref, bcol_ref,
                     selk_ref, selkf_ref, constk_ref, selq_ref, selqf_ref, constq_ref,
                     kaug_ref, qaug_ref, crow, ccol):
    TS = z_ref.shape[1]

    @pl.when(pl.program_id(1) == 0)
    def _():
        crow[...] = jnp.zeros_like(crow)
        ccol[...] = jnp.zeros_like(ccol)

    ri = lax.broadcasted_iota(jnp.int32, (TS, TS), 0)
    ci = lax.broadcasted_iota(jnp.int32, (TS, TS), 1)

    lf = jax.nn.log_sigmoid(z_ref[0] + brow_ref[...])
    ltri = (ri >= ci).astype(BF16)
    h3 = _split3(lf)
    f_row = _dot(ltri, h3[0]) + _dot(ltri, h3[1]) + _dot(ltri, h3[2]) + crow[...]
    crow[...] = f_row[TS - 1:TS, :]
    n3 = _split3(-f_row)
    kaug = (_dot(fk_ref[0], selk_ref[...]) + _dot(n3[0], selkf_ref[0]) + _dot(n3[1], selkf_ref[1])
            + _dot(n3[2], selkf_ref[2]) + constk_ref[...])
    for h in range(FOX_HEADS):
        kaug_ref[0, h] = kaug[:, h * AUG:(h + 1) * AUG].astype(BF16)

    lft = jax.nn.log_sigmoid(zt_ref[0] + bcol_ref[...])
    utri = (ri <= ci).astype(BF16)
    t3 = _split3(lft)
    f_col = _dot(t3[0], utri) + _dot(t3[1], utri) + _dot(t3[2], utri) + ccol[:, 0:1]
    ccol[...] = jnp.broadcast_to(f_col[:, TS - 1:TS], ccol.shape)
    p3 = _split3(f_col)
    qaug = (_dot(selq_ref[...], fqt_ref[0]) + _dot(selqf_ref[0], p3[0]) + _dot(selqf_ref[1], p3[1])
            + _dot(selqf_ref[2], p3[2]) + constq_ref[...])
    for h in range(FOX_HEADS):
        qaug_ref[0, h] = qaug[h * AUG:(h + 1) * AUG, :].astype(BF16)


def _fox_prep_constants():
    W = FOX_HEADS * AUG
    selk = np.zeros((FOX_WIDTH, W), np.float32)
    selkf = np.zeros((3, LANES, W), np.float32)
    constk = np.zeros((1, W), np.float32)
    selq = np.zeros((W, FOX_WIDTH), np.float32)
    selqf = np.zeros((3, W, 16), np.float32)
    constq = np.zeros((W, 1), np.float32)
    d = FOX_HEAD_DIM
    for h in range(FOX_HEADS):
        for j in range(d):
            selk[h * d + j, h * AUG + j] = 1.0
            selq[h * AUG + j, h * d + j] = d ** -0.5
        for p in range(3):
            selqf[p, h * AUG + d + p, h] = 1.0
            constq[h * AUG + d + 3 + p, 0] = 1.0
            constk[0, h * AUG + d + p] = 1.0
            selkf[p, h, h * AUG + d + 3 + p] = 1.0
    return (jnp.asarray(selk, BF16), jnp.asarray(selkf, BF16), jnp.asarray(constk, F32),
            jnp.asarray(selq, BF16), jnp.asarray(selqf, BF16), jnp.asarray(constq, F32))


def _fox_prep(z, zt, fk, fqt, b_forget):
    B, S, _ = z.shape
    TS = PROJ_TILE
    ns = S // TS
    selk, selkf, constk, selq, selqf, constq = _fox_prep_constants()
    brow = jnp.zeros((1, LANES), F32).at[0, :FOX_HEADS].set(b_forget)
    bcol = jnp.zeros((16, 1), F32).at[:FOX_HEADS, 0].set(b_forget)
    const = lambda a: pl.BlockSpec(a.shape, lambda b, s: (0,) * a.ndim)
    return pl.pallas_call(
        _fox_prep_kernel,
        grid=(B, ns),
        in_specs=[
            pl.BlockSpec((1, TS, LANES), lambda b, s: (b, s, 0)),
            pl.BlockSpec((1, 16, TS), lambda b, s: (b, 0, s)),
            pl.BlockSpec((1, TS, FOX_WIDTH), lambda b, s: (b, s, 0)),
            pl.BlockSpec((1, FOX_WIDTH, TS), lambda b, s: (b, 0, s)),
            const(brow), const(bcol), const(selk), const(selkf), const(constk),
            const(selq), const(selqf), const(constq),
        ],
        out_specs=(
            pl.BlockSpec((1, FOX_HEADS, TS, AUG), lambda b, s: (b, 0, s, 0)),
            pl.BlockSpec((1, FOX_HEADS, AUG, TS), lambda b, s: (b, 0, 0, s)),
        ),
        out_shape=(
            jax.ShapeDtypeStruct((B, FOX_HEADS, S, AUG), BF16),
            jax.ShapeDtypeStruct((B, FOX_HEADS, AUG, S), BF16),
        ),
        scratch_shapes=[pltpu.VMEM((1, LANES), F32), pltpu.VMEM((16, LANES), F32)],
        compiler_params=pltpu.CompilerParams(
            dimension_semantics=("arbitrary", "arbitrary"), vmem_limit_bytes=48 * 1024 * 1024),
        name="fox_prep",
    )(z, zt, fk, fqt, brow, bcol, selk, selkf, constk, selq, selqf, constq)


def _retention_kernel(q_ref, k_ref, v_ref, g_ref, dec_ref, qw_ref, kw_ref, cd_ref, o_ref, state):
    @pl.when(pl.program_id(2) == 0)
    def _():
        state[...] = jnp.zeros_like(state)

    q, k, v = q_ref[0], k_ref[0], v_ref[0]
    scores = (_dot_nt(q, k) * dec_ref[0]).astype(BF16)
    st = state[...]
    o = _dot(scores, v) + _dot((q.astype(F32) * qw_ref[0]).astype(BF16), st.astype(BF16))
    kk = k.astype(F32) * kw_ref[0]
    state[...] = st * cd_ref[0, 0:1, :] + _dot(kk.T.astype(BF16), v)
    mu = jnp.mean(o, axis=-1, keepdims=True)
    oc = o - mu
    var = jnp.mean(oc * oc, axis=-1, keepdims=True)
    o_ref[0] = (oc * lax.rsqrt(var + GN_EPS) * jax.nn.silu(g_ref[0].astype(F32))).astype(BF16)


def _retention_tables():
    L = RET_BLOCK
    log_gamma = jnp.log1p(-jnp.exp2(-5.0 - jnp.arange(RET_HEADS, dtype=F32)))
    p = jnp.arange(L, dtype=F32)
    dist = jnp.abs(p[:, None] - p[None, :])
    chunk = jnp.arange(L) // CHUNK
    allowed = (chunk[None, :] <= chunk[:, None]).astype(F32)
    dec = jnp.exp(log_gamma[:, None, None] * dist) * allowed
    lanes = lambda a: jnp.broadcast_to(a[:, :, None], (RET_HEADS, L, RET_HEAD_DIM))
    qw = lanes(jnp.exp(log_gamma[:, None] * (p[None, :] + 1.0)))
    kw = lanes(jnp.exp(log_gamma[:, None] * (L - 1.0 - p[None, :])))
    cd = jnp.broadcast_to(jnp.exp(log_gamma * L)[:, None, None], (RET_HEADS, SUBLANES, RET_HEAD_DIM))
    return dec, qw, kw, cd


def _retention(rq, rk, rv, rg):
    B, S, _ = rq.shape
    L = RET_BLOCK
    dec, qw, kw, cd = _retention_tables()
    head = pl.BlockSpec((1, L, RET_HEAD_DIM), lambda b, h, s: (b, s, h))
    per_head = lambda r: pl.BlockSpec((1, r, RET_HEAD_DIM), lambda b, h, s: (h, 0, 0))
    return pl.pallas_call(
        _retention_kernel,
        grid=(B, RET_HEADS, S // L),
        in_specs=[head, head, head, head,
                  pl.BlockSpec((1, L, L), lambda b, h, s: (h, 0, 0)),
                  per_head(L), per_head(L), per_head(SUBLANES)],
        out_specs=head,
        out_shape=jax.ShapeDtypeStruct((B, S, RET_WIDTH), BF16),
        scratch_shapes=[pltpu.VMEM((RET_HEAD_DIM, RET_HEAD_DIM), F32)],
        compiler_params=pltpu.CompilerParams(dimension_semantics=("arbitrary",) * 3),
        name="retention",
    )(rq, rk, rv, rg, dec, qw, kw, cd)


def _fox_attn_kernel(q_ref, k_ref, v_ref, o_ref):
    qi = pl.program_id(2)
    TQ, TK = FOX_TQ, FOX_TK
    d = FOX_HEAD_DIM
    n_full = (qi * TQ) // TK
    n_diag = TQ // TK

    def tile(j, carry, qt, hh, masked):
        m, l, acc = carry
        kj = k_ref[0, hh, pl.ds(pl.multiple_of(j * TK, TK), TK), :]
        st = _dot(kj, qt)
        if masked:
            key = j * TK + lax.broadcasted_iota(jnp.int32, (TK, TQ), 0)
            qry = qi * TQ + lax.broadcasted_iota(jnp.int32, (TK, TQ), 1)
            st = jnp.where(key <= qry, st, -jnp.inf)
        m_new = jnp.maximum(m, jnp.max(st, axis=0, keepdims=True))
        alpha = jnp.exp(m - m_new)
        p = jnp.exp(st - m_new)
        l = alpha * l + jnp.sum(p, axis=0, keepdims=True)
        vj = v_ref[0, j, hh * d:(hh + 1) * d, :]
        acc = alpha * acc + _dot(vj, p.astype(BF16))
        return m_new, l, acc

    outs = []
    for hh in range(2):
        qt = q_ref[0, hh]
        carry = (jnp.full((1, TQ), -jnp.inf, F32), jnp.zeros((1, TQ), F32), jnp.zeros((d, TQ), F32))
        carry = lax.fori_loop(0, n_full, functools.partial(tile, qt=qt, hh=hh, masked=False), carry)
        for jd in range(n_diag):
            carry = tile(n_full + jd, carry, qt, hh, True)
        _, l, acc = carry
        outs.append(acc / l)
    o_ref[0] = jnp.concatenate(outs, axis=0).T.astype(BF16)


def _fox_attn(qaug, kaug, fvt):
    B, H, S, _ = kaug.shape
    nk = S // FOX_TK
    return pl.pallas_call(
        _fox_attn_kernel,
        grid=(B, H // 2, S // FOX_TQ),
        in_specs=[
            pl.BlockSpec((1, 2, AUG, FOX_TQ), lambda b, p, q: (b, p, 0, q)),
            pl.BlockSpec((1, 2, S, AUG), lambda b, p, q: (b, p, 0, 0)),
            pl.BlockSpec((1, nk, 2 * FOX_HEAD_DIM, FOX_TK), lambda b, p, q: (b, 0, p, 0)),
        ],
        out_specs=pl.BlockSpec((1, FOX_TQ, 2 * FOX_HEAD_DIM), lambda b, p, q: (b, q, p)),
        out_shape=jax.ShapeDtypeStruct((B, S, FOX_WIDTH), BF16),
        compiler_params=pltpu.CompilerParams(
            dimension_semantics=("arbitrary",) * 3, vmem_limit_bytes=48 * 1024 * 1024),
        name="fox_attn",
    )(qaug, kaug, fvt)


def _out_router_kernel(x_ref, oret_ref, ofox_ref, wor_ref, wof_ref, g_ref, wrh_ref, wrl_ref, br_ref,
                       h1_ref, u2_ref, sel_ref, cnt_ref):
    TM = x_ref.shape[0]
    h1 = x_ref[...] + _dot(oret_ref[...], wor_ref[...]) + _dot(ofox_ref[...], wof_ref[...])
    h1_ref[...] = h1
    u2 = _rms(h1, g_ref[...])
    uh = u2.astype(BF16)
    u2_ref[...] = uh
    ul = (u2 - uh.astype(F32)).astype(BF16)
    logits = _dot(uh, wrh_ref[...]) + _dot(ul, wrh_ref[...]) + _dot(uh, wrl_ref[...]) + br_ref[...]
    lane = lax.broadcasted_iota(jnp.int32, (TM, LANES), 1)
    l = jnp.where(lane < N_EXPERTS, logits, -jnp.inf)
    picks, vals = [], []
    for _ in range(TOP_K):
        m = jnp.max(l, axis=-1, keepdims=True)
        idx = jnp.min(jnp.where(l == m, lane, LANES), axis=-1, keepdims=True)
        pick = lane == idx
        picks.append(pick)
        vals.append(m)
        l = jnp.where(pick, -jnp.inf, l)
    exps = [jnp.exp(v - vals[0]) for v in vals]
    den = exps[0] + exps[1] + exps[2] + exps[3]
    sel = jnp.full((TM, LANES), -1.0, F32)
    for pick, e in zip(picks, exps):
        sel = jnp.where(pick, e / den, sel)
    sel_ref[...] = sel
    cnt = jnp.sum((sel >= 0.0).astype(F32), axis=0, keepdims=True)
    cnt_ref[0] = jnp.broadcast_to(cnt, (SUBLANES, LANES))


def _out_router(x2, o_ret, o_fox, wor, wof, g, wrh, wrl, br):
    T, D = x2.shape
    TM = MOE_TILE
    nT = T // TM
    const = lambda a: pl.BlockSpec(a.shape, lambda i: (0,) * a.ndim)
    tok = lambda w: pl.BlockSpec((TM, w), lambda i: (i, 0))
    return pl.pallas_call(
        _out_router_kernel,
        grid=(nT,),
        in_specs=[tok(D), tok(RET_WIDTH), tok(FOX_WIDTH), const(wor), const(wof), const(g),
                  const(wrh), const(wrl), const(br)],
        out_specs=(tok(D), tok(D), tok(LANES), pl.BlockSpec((1, SUBLANES, LANES), lambda i: (i, 0, 0))),
        out_shape=(
            jax.ShapeDtypeStruct((T, D), F32),
            jax.ShapeDtypeStruct((T, D), BF16),
            jax.ShapeDtypeStruct((T, LANES), F32),
            jax.ShapeDtypeStruct((nT, SUBLANES, LANES), F32),
        ),
        compiler_params=pltpu.CompilerParams(dimension_semantics=("arbitrary",)),
        name="out_router",
    )(x2, o_ret, o_fox, wor, wof, g, wrh, wrl, br)


def _tile_sort(sel):
    TM = sel.shape[0]
    NS = TOP_K * TM
    maskf = (sel >= 0.0).astype(F32)
    mask = maskf.astype(BF16)
    ri = lax.broadcasted_iota(jnp.int32, (TM, TM), 0)
    ci = lax.broadcasted_iota(jnp.int32, (TM, TM), 1)
    rank = _dot((ri > ci).astype(BF16), mask)
    cnt = jnp.sum(maskf, axis=0, keepdims=True)
    ei = lax.broadcasted_iota(jnp.int32, (LANES, LANES), 0)
    ej = lax.broadcasted_iota(jnp.int32, (LANES, LANES), 1)
    cnt8 = jnp.broadcast_to(cnt, (SUBLANES, LANES)).astype(BF16)
    off = _dot(cnt8, (ei < ej).astype(BF16))[0:1, :]
    slot = lax.broadcasted_iota(jnp.int32, (NS, LANES), 0).astype(F32)
    esel = ((slot >= off) & (slot < off + cnt)).astype(BF16)
    return mask, rank.astype(BF16), esel, off, cnt


def _segment_dmas(step, slot, segdst_ref, cnt_ref, local, remote_rows, sem, to_remote, wait):
    def body(e, off):
        c = cnt_ref[step * N_EXPERTS + e]
        dst = segdst_ref[step * N_EXPERTS + e]
        bit = MOE_TILE
        while bit >= 1:
            done = c & (~(2 * bit - 1))

            @pl.when((c & bit) != 0)
            def _(bit=bit, done=done):
                loc = local.at[slot, pl.ds((off + done) * ROW_TILES, bit * ROW_TILES), :]
                rem = remote_rows.at[pl.ds((dst + done) * ROW_TILES, bit * ROW_TILES), :]
                cp = (pltpu.make_async_copy(loc, rem, sem.at[slot]) if to_remote
                      else pltpu.make_async_copy(rem, loc, sem.at[slot]))
                cp.wait() if wait else cp.start()
            bit //= 2
        return off + c

    lax.fori_loop(0, N_EXPERTS, body, 0)


def _dispatch_kernel(segdst_ref, cnt_ref, paddst_ref, padcnt_ref, nused_ref, u2_ref, sel_ref, xs_ref,
                     buf, zbuf, sems, zsem):
    i = pl.program_id(0)
    last = pl.num_programs(0) - 1
    slot = i % 2
    TM = MOE_TILE
    NS = TOP_K * TM
    mask, rank, esel, off, _ = _tile_sort(sel_ref[...])
    slot_id = lax.broadcasted_iota(jnp.int32, (NS, 1), 0).astype(F32)
    r_s = slot_id - jnp.sum(esel.astype(F32) * off, axis=1, keepdims=True)
    perm = ((_dot_nt(esel, mask) > 0.5) & (_dot_nt(esel, rank) == r_s)).astype(BF16)
    rows = _dot(perm, u2_ref[...])

    @pl.when(i >= 2)
    def _():
        _segment_dmas(i - 2, slot, segdst_ref, cnt_ref, buf, xs_ref, sems, True, True)

    for j in range(ROW_TILES):
        buf[slot, pl.ds(j, NS, stride=ROW_TILES), :] = rows[:, j * LANES:(j + 1) * LANES]
    _segment_dmas(i, slot, segdst_ref, cnt_ref, buf, xs_ref, sems, True, False)

    @pl.when(i == last)
    def _():
        @pl.when(i >= 1)
        def _():
            _segment_dmas(i - 1, 1 - slot, segdst_ref, cnt_ref, buf, xs_ref, sems, True, True)
        _segment_dmas(i, slot, segdst_ref, cnt_ref, buf, xs_ref, sems, True, True)
        zbuf[...] = jnp.zeros_like(zbuf)
        half = EXPERT_BLOCK // 2 * ROW_TILES
        n_blocks = xs_ref.shape[0] // (EXPERT_BLOCK * ROW_TILES)
        for wait in (False, True):
            def unused(hb, carry, wait=wait):
                cp = pltpu.make_async_copy(zbuf, xs_ref.at[pl.ds(hb * half, half), :], zsem.at[0])
                cp.wait() if wait else cp.start()
                return carry
            lax.fori_loop(2 * nused_ref[0], 2 * n_blocks, unused, 0)


            def body(e, carry, wait=wait):
                c = padcnt_ref[e]
                dst = paddst_ref[e]
                bit = EXPERT_BLOCK // 2
                while bit >= 1:
                    done = c & (~(2 * bit - 1))

                    @pl.when((c & bit) != 0)
                    def _(bit=bit, done=done):
                        cp = pltpu.make_async_copy(
                            zbuf.at[pl.ds(0, bit * ROW_TILES), :],
                            xs_ref.at[pl.ds((dst + done) * ROW_TILES, bit * ROW_TILES), :], zsem.at[0])
                        cp.wait() if wait else cp.start()
                    bit //= 2
                return carry
            lax.fori_loop(0, N_EXPERTS, body, 0)


def _dispatch(u2, sel, segdst, cnt, paddst, padcnt, n_used, n_rows):
    T, D = u2.shape
    TM = MOE_TILE
    NS = TOP_K * TM
    return pl.pallas_call(
        _dispatch_kernel,
        grid_spec=pltpu.PrefetchScalarGridSpec(
            num_scalar_prefetch=5,
            grid=(T // TM,),
            in_specs=[pl.BlockSpec((TM, D), lambda i, *_: (i, 0)),
                      pl.BlockSpec((TM, LANES), lambda i, *_: (i, 0))],
            out_specs=pl.BlockSpec(memory_space=pl.ANY),
            scratch_shapes=[pltpu.VMEM((2, NS * ROW_TILES, LANES), F32),
                            pltpu.VMEM((EXPERT_BLOCK // 2 * ROW_TILES, LANES), F32),
                            pltpu.SemaphoreType.DMA((2,)), pltpu.SemaphoreType.DMA((1,))],
        ),
        out_shape=jax.ShapeDtypeStruct((n_rows * ROW_TILES, LANES), F32),
        compiler_params=pltpu.CompilerParams(
            dimension_semantics=("arbitrary",), vmem_limit_bytes=48 * 1024 * 1024),
        name="dispatch",
    )(segdst, cnt, paddst, padcnt, n_used, u2, sel)


def _expert_kernel(bexp_ref, nused_ref, xs_ref, w1_ref, b1_ref, w2_ref, b2_ref, ys_ref, w1b, w2b):
    b = pl.program_id(0)
    BLK = EXPERT_BLOCK
    used = b < nused_ref[0]

    @pl.when(used)
    def _():
        e = bexp_ref[b]
        prev = bexp_ref[jnp.maximum(b - 1, 0)]

        @pl.when((b == 0) | (e != prev))
        def _():
            rows = 128

            def cast(r, carry):
                sl = pl.ds(pl.multiple_of(r * rows, rows), rows)
                w1b[sl, :] = w1_ref[0, sl, :].astype(BF16)
                w2b[sl, :] = w2_ref[0, sl, :].astype(BF16)
                return carry
            lax.fori_loop(0, D_MODEL // rows, cast, 0)

        x = jnp.concatenate([xs_ref[pl.ds(j, BLK, stride=ROW_TILES), :] for j in range(ROW_TILES)],
                            axis=1).astype(BF16)
        h = _dot(x, w1b[...]) + b1_ref[0]
        glu = jnp.minimum(h[:, :D_FF], SWIGLU_LIMIT)
        lin = jnp.clip(h[:, D_FF:], -SWIGLU_LIMIT, SWIGLU_LIMIT)
        act = glu * jax.nn.sigmoid(SWIGLU_ALPHA * glu) * (lin + 1.0)
        y = _dot(act.astype(BF16), w2b[...]) + b2_ref[0]
        for j in range(ROW_TILES):
            ys_ref[pl.ds(j, BLK, stride=ROW_TILES), :] = y[:, j * LANES:(j + 1) * LANES]

    @pl.when(jnp.logical_not(used))
    def _():
        ys_ref[...] = jnp.zeros_like(ys_ref)


def _experts(xs, block_exp, n_used, w1, b1, w2, b2):
    BLK = EXPERT_BLOCK
    NB = xs.shape[0] // (BLK * ROW_TILES)
    blk = lambda b, nused: jnp.minimum(b, nused[0] - 1)
    return pl.pallas_call(
        _expert_kernel,
        grid_spec=pltpu.PrefetchScalarGridSpec(
            num_scalar_prefetch=2,
            grid=(NB,),
            in_specs=[
                pl.BlockSpec((BLK * ROW_TILES, LANES), lambda b, bexp, nused: (blk(b, nused), 0)),
                pl.BlockSpec((1, D_MODEL, 2 * D_FF), lambda b, bexp, nused: (bexp[blk(b, nused)], 0, 0)),
                pl.BlockSpec((1, 1, 2 * D_FF), lambda b, bexp, nused: (bexp[blk(b, nused)], 0, 0)),
                pl.BlockSpec((1, D_FF, D_MODEL), lambda b, bexp, nused: (bexp[blk(b, nused)], 0, 0)),
                pl.BlockSpec((1, 1, D_MODEL), lambda b, bexp, nused: (bexp[blk(b, nused)], 0, 0)),
            ],
            out_specs=pl.BlockSpec((BLK * ROW_TILES, LANES), lambda b, bexp, nused: (b, 0)),
            scratch_shapes=[pltpu.VMEM((D_MODEL, 2 * D_FF), BF16), pltpu.VMEM((D_FF, D_MODEL), BF16)],
        ),
        out_shape=jax.ShapeDtypeStruct(xs.shape, F32),
        compiler_params=pltpu.CompilerParams(
            dimension_semantics=("arbitrary",), vmem_limit_bytes=56 * 1024 * 1024),
        name="experts",
    )(block_exp, n_used, xs, w1, b1[:, None, :], w2, b2[:, None, :])


def _combine_kernel(segdst_ref, cnt_ref, ys_ref, sel_ref, h1_ref, g_ref, out_ref, buf, sems):
    i = pl.program_id(0)
    n = pl.num_programs(0)
    slot = i % 2
    TM = MOE_TILE
    NS = TOP_K * TM

    @pl.when(i == 0)
    def _():
        _segment_dmas(i, slot, segdst_ref, cnt_ref, buf, ys_ref, sems, False, False)

    @pl.when(i + 1 < n)
    def _():
        _segment_dmas(i + 1, 1 - slot, segdst_ref, cnt_ref, buf, ys_ref, sems, False, False)

    sel = sel_ref[...]
    mask, rank, esel, off, _ = _tile_sort(sel)
    gate = jnp.maximum(sel, 0.0)
    gh = gate.astype(BF16)
    gl = (gate - gh.astype(F32)).astype(BF16)
    o3 = _split3(jnp.broadcast_to(off, (SUBLANES, LANES)))
    off_s = (_dot_nt(o3[0], esel) + _dot_nt(o3[1], esel) + _dot_nt(o3[2], esel))[0:1, :]
    r_s = lax.broadcasted_iota(jnp.int32, (1, NS), 1).astype(F32) - off_s
    hit = (_dot_nt(mask, esel) > 0.5) & (_dot_nt(rank, esel) == r_s)
    unperm = jnp.where(hit, _dot_nt(gh, esel) + _dot_nt(gl, esel), 0.0).astype(BF16)

    _segment_dmas(i, slot, segdst_ref, cnt_ref, buf, ys_ref, sems, False, True)
    y = jnp.concatenate([buf[slot, pl.ds(j, NS, stride=ROW_TILES), :] for j in range(ROW_TILES)],
                        axis=1).astype(BF16)
    h2 = h1_ref[...] + _dot(unperm, y)
    out_ref[...] = _rms(h2, g_ref[...])


def _combine(ys, sel, h1, g, segdst, cnt):
    T, D = h1.shape
    TM = MOE_TILE
    NS = TOP_K * TM
    return pl.pallas_call(
        _combine_kernel,
        grid_spec=pltpu.PrefetchScalarGridSpec(
            num_scalar_prefetch=2,
            grid=(T // TM,),
            in_specs=[pl.BlockSpec(memory_space=pl.ANY),
                      pl.BlockSpec((TM, LANES), lambda i, *_: (i, 0)),
                      pl.BlockSpec((TM, D), lambda i, *_: (i, 0)),
                      pl.BlockSpec((1, D), lambda i, *_: (0, 0))],
            out_specs=pl.BlockSpec((TM, D), lambda i, *_: (i, 0)),
            scratch_shapes=[pltpu.VMEM((2, NS * ROW_TILES, LANES), F32), pltpu.SemaphoreType.DMA((2,))],
        ),
        out_shape=jax.ShapeDtypeStruct((T, D), F32),
        compiler_params=pltpu.CompilerParams(
            dimension_semantics=("arbitrary",), vmem_limit_bytes=48 * 1024 * 1024),
        name="combine",
    )(segdst, cnt, ys, sel, h1, g)


def _routing_tables(cnt_tiles):
    BLK = EXPERT_BLOCK
    nT = cnt_tiles.shape[0]
    A = nT * MOE_TILE * TOP_K
    NB = A // BLK + N_EXPERTS
    total = jnp.sum(cnt_tiles, axis=0)
    padded = (total + BLK - 1) // BLK * BLK
    pad_ends = jnp.cumsum(padded)
    pad_starts = pad_ends - padded
    before = jnp.cumsum(cnt_tiles, axis=0) - cnt_tiles
    segdst = (pad_starts[None, :] + before).reshape(-1).astype(jnp.int32)
    block_start = jnp.arange(NB, dtype=jnp.int32) * BLK
    block_exp = jnp.minimum(jnp.searchsorted(pad_ends, block_start, side="right"), N_EXPERTS - 1).astype(jnp.int32)
    n_used = (pad_ends[-1] // BLK).astype(jnp.int32).reshape(1)
    paddst = (pad_starts + total).astype(jnp.int32)
    padcnt = (padded - total).astype(jnp.int32)
    return segdst, cnt_tiles.reshape(-1).astype(jnp.int32), paddst, padcnt, block_exp, n_used, NB * BLK


def _rotary_tables(S):
    half = RET_HEAD_DIM // 2
    inv_freq = ROPE_BASE ** (-jnp.arange(half, dtype=F32) / half)
    ang = jnp.arange(S, dtype=F32)[:, None] * inv_freq[None, :]
    cos, sin = jnp.cos(ang), jnp.sin(ang)
    return jnp.concatenate([cos, cos], axis=-1), jnp.concatenate([-sin, sin], axis=-1)


def _layer(h, norm_mix_g, w_in, b_forget, w_out, norm_ffn_g, w_router, b_router,
           w_exp_in, b_exp_in, w_exp_out, b_exp_out, final_g):
    B, S, D = h.shape
    R, Fw = RET_WIDTH, FOX_WIDTH
    cos, sin = _rotary_tables(S)
    wb = w_in.astype(BF16)
    wr = wb[:, :4 * R]
    wfq, wfk, wfv = (wb[:, 4 * R + i * Fw:4 * R + (i + 1) * Fw] for i in range(3))
    wz = jnp.zeros((D, LANES), BF16).at[:, :FOX_HEADS].set(wb[:, 4 * R + 3 * Fw:])
    wzt = jnp.zeros((16, D), BF16).at[:FOX_HEADS, :].set(wb[:, 4 * R + 3 * Fw:].T)
    rq, rk, rv, rg, fk, fqt, fvt, z, zt = _in_proj(
        h, norm_mix_g[None, :], cos, sin, wr, wfk, wfq.T, wfv.T, wz, wzt)
    kaug, qaug = _fox_prep(z, zt, fk, fqt, b_forget)
    o_ret = _retention(rq, rk, rv, rg)
    o_fox = _fox_attn(qaug, kaug, fvt)

    T = B * S
    wo = w_out.astype(BF16)
    wrt = jnp.zeros((D, LANES), F32).at[:, :N_EXPERTS].set(w_router)
    wrh = wrt.astype(BF16)
    wrl = (wrt - wrh.astype(F32)).astype(BF16)
    br = jnp.zeros((1, LANES), F32).at[0, :N_EXPERTS].set(b_router)
    h1, u2, sel, cnt = _out_router(h.reshape(T, D), o_ret.reshape(T, R), o_fox.reshape(T, Fw),
                                   wo[:R], wo[R:], norm_ffn_g[None, :], wrh, wrl, br)
    cnt_tiles = cnt[:, 0, :N_EXPERTS].astype(jnp.int32)
    segdst, cnt_flat, paddst, padcnt, block_exp, n_used, n_rows = _routing_tables(cnt_tiles)
    xs = _dispatch(u2, sel, segdst, cnt_flat, paddst, padcnt, n_used, n_rows)
    ys = _experts(xs, block_exp, n_used, w_exp_in, b_exp_in, w_exp_out, b_exp_out)
    out = _combine(ys, sel, h1, final_g[None, :], segdst, cnt_flat)
    return out.reshape(B, S, D)


def kernel(x, norm_mix_g, w_in, b_forget, w_out, norm_ffn_g, w_router, b_router,
           w_exp_in, b_exp_in, w_exp_out, b_exp_out, norm_final_g):
    depth = w_in.shape[0]
    assert depth == 1, "the fused final RMSNorm assumes a single layer"
    return _layer(x, norm_mix_g[0], w_in[0], b_forget[0], w_out[0], norm_ffn_g[0], w_router[0], b_router[0],
                  w_exp_in[0], b_exp_in[0], w_exp_out[0], b_exp_out[0], norm_final_g)
```

```python
import functools

import numpy as np
import jax
import jax.numpy as jnp
from jax import lax
from jax.experimental import pallas as pl
from jax.experimental.pallas import tpu as pltpu

F32 = jnp.float32
BF16 = jnp.bfloat16

D_MODEL = 1024
RET_HEADS, RET_HEAD_DIM = 4, 128
RET_WIDTH = RET_HEADS * RET_HEAD_DIM
FOX_HEADS, FOX_HEAD_DIM = 8, 64
FOX_WIDTH = FOX_HEADS * FOX_HEAD_DIM
CHUNK = 64
ROPE_BASE = 10000.0
N_EXPERTS = 32
TOP_K = 4
D_FF = D_MODEL
SWIGLU_ALPHA = 1.702
SWIGLU_LIMIT = 7.0
RMS_EPS = 1e-5
GN_EPS = 1e-5

LANES = 128
SUBLANES = 8
ROW_TILES = D_MODEL // LANES

PROJ_TILE = 512
RET_BLOCK = 256
FOX_TQ = 512
FOX_TK = PROJ_TILE
FOX_SUB = 256
AUG = 128
V_AUG = 80
LOG2E = 1.4426950408889634
MOE_TILE = 256
EXPERT_BLOCK = 256

NT_DIMS = (((1,), (1,)), ((), ()))


def _split3(a):
    hi = a.astype(BF16)
    r1 = a - hi.astype(F32)
    mid = r1.astype(BF16)
    lo = (r1 - mid.astype(F32)).astype(BF16)
    return hi, mid, lo


def _dot(a, b):
    return jnp.dot(a, b, preferred_element_type=F32)


def _dot_nt(a, b):
    return lax.dot_general(a, b, NT_DIMS, preferred_element_type=F32)


def _rms(x, g):
    return x * lax.rsqrt(jnp.mean(x * x, axis=-1, keepdims=True) + RMS_EPS) * g


def _in_proj_kernel(x_ref, g_ref, cos_ref, sin_ref, wr_ref, wfk_ref, wfqt_ref, wfvt_ref, vone_ref, wz_ref, wzt_ref,
                    rq_ref, rk_ref, rv_ref, rg_ref, fk_ref, fqt_ref, fvt_ref, z_ref, zt_ref):
    u = _rms(x_ref[0], g_ref[...]).astype(BF16)
    r = _dot(u, wr_ref[...])
    cos, sin = cos_ref[...], sin_ref[...]
    k_scale = RET_HEAD_DIM ** -0.5
    for h in range(RET_HEADS):
        lo = h * RET_HEAD_DIM
        q = r[:, lo:lo + RET_HEAD_DIM]
        k = r[:, RET_WIDTH + lo:RET_WIDTH + lo + RET_HEAD_DIM]
        rq_ref[0, :, lo:lo + RET_HEAD_DIM] = (q * cos + pltpu.roll(q, RET_HEAD_DIM // 2, 1) * sin).astype(BF16)
        rk_ref[0, :, lo:lo + RET_HEAD_DIM] = (
            (k * cos + pltpu.roll(k, RET_HEAD_DIM // 2, 1) * sin) * k_scale).astype(BF16)
    rv_ref[0] = r[:, 2 * RET_WIDTH:3 * RET_WIDTH].astype(BF16)
    rg_ref[0] = r[:, 3 * RET_WIDTH:4 * RET_WIDTH].astype(BF16)
    fk_ref[0] = _dot(u, wfk_ref[...]).astype(BF16)
    q_scale = FOX_HEAD_DIM ** -0.5 * LOG2E
    fqt_ref[0] = (_dot_nt(wfqt_ref[...], u) * q_scale).astype(BF16)
    fvt_ref[0, 0] = (_dot_nt(wfvt_ref[...], u) + vone_ref[...]).astype(BF16)
    z_ref[0] = _dot(u, wz_ref[...])
    zt_ref[0] = _dot_nt(wzt_ref[...], u)


def _in_proj(x, g, cos, sin, wr, wfk, wfqt, wfvt, vone, wz, wzt):
    B, S, D = x.shape
    TM = PROJ_TILE
    ns = S // TM
    const = lambda shape: pl.BlockSpec(shape, lambda b, s: (0,) * len(shape))
    tok = lambda w: pl.BlockSpec((1, TM, w), lambda b, s: (b, s, 0))
    out_shape = (
        jax.ShapeDtypeStruct((B, S, RET_WIDTH), BF16),
        jax.ShapeDtypeStruct((B, S, RET_WIDTH), BF16),
        jax.ShapeDtypeStruct((B, S, RET_WIDTH), BF16),
        jax.ShapeDtypeStruct((B, S, RET_WIDTH), BF16),
        jax.ShapeDtypeStruct((B, S, FOX_WIDTH), BF16),
        jax.ShapeDtypeStruct((B, FOX_WIDTH, S), BF16),
        jax.ShapeDtypeStruct((B, ns, FOX_HEADS * V_AUG, TM), BF16),
        jax.ShapeDtypeStruct((B, S, LANES), F32),
        jax.ShapeDtypeStruct((B, 16, S), F32),
    )
    return pl.pallas_call(
        _in_proj_kernel,
        grid=(B, ns),
        in_specs=[
            pl.BlockSpec((1, TM, D), lambda b, s: (b, s, 0)),
            const((1, D)),
            pl.BlockSpec((TM, RET_HEAD_DIM), lambda b, s: (s, 0)),
            pl.BlockSpec((TM, RET_HEAD_DIM), lambda b, s: (s, 0)),
            const(wr.shape), const(wfk.shape), const(wfqt.shape), const(wfvt.shape), const(vone.shape),
            const(wz.shape), const(wzt.shape),
        ],
        out_specs=(
            tok(RET_WIDTH), tok(RET_WIDTH), tok(RET_WIDTH), tok(RET_WIDTH), tok(FOX_WIDTH),
            pl.BlockSpec((1, FOX_WIDTH, TM), lambda b, s: (b, 0, s)),
            pl.BlockSpec((1, 1, FOX_HEADS * V_AUG, TM), lambda b, s: (b, s, 0, 0)),
            tok(LANES),
            pl.BlockSpec((1, 16, TM), lambda b, s: (b, 0, s)),
        ),
        out_shape=out_shape,
        compiler_params=pltpu.CompilerParams(
            dimension_semantics=("arbitrary", "arbitrary"), vmem_limit_bytes=48 * 1024 * 1024),
        name="in_proj",
    )(x, g, cos, sin, wr, wfk, wfqt, wfvt, vone, wz, wzt)


def _fox_prep_kernel(z_ref, zt_ref, fk_ref, fqt_ref, brow_ref, bcol_ref,
                     selk_ref, selkf_ref, constk_ref, selq_ref, selqf_ref, constq_ref,
                     kaug_ref, qaug_ref, crow, ccol):
    TS = z_ref.shape[1]

    @pl.when(pl.program_id(1) == 0)
    def _():
        crow[...] = jnp.zeros_like(crow)
        ccol[...] = jnp.zeros_like(ccol)

    ri = lax.broadcasted_iota(jnp.int32, (TS, TS), 0)
    ci = lax.broadcasted_iota(jnp.int32, (TS, TS), 1)

    lf = jax.nn.log_sigmoid(z_ref[0] + brow_ref[...])
    ltri = (ri >= ci).astype(BF16)
    h3 = _split3(lf)
    f_row = _dot(ltri, h3[0]) + _dot(ltri, h3[1]) + _dot(ltri, h3[2]) + crow[...]
    crow[...] = f_row[TS - 1:TS, :]
    n3 = _split3(f_row * -LOG2E)
    kaug = (_dot(fk_ref[0], selk_ref[...]) + _dot(n3[0], selkf_ref[0]) + _dot(n3[1], selkf_ref[1])
            + _dot(n3[2], selkf_ref[2]) + constk_ref[...])
    for h in range(FOX_HEADS):
        kaug_ref[0, h] = kaug[:, h * AUG:(h + 1) * AUG].astype(BF16)

    lft = jax.nn.log_sigmoid(zt_ref[0] + bcol_ref[...])
    utri = (ri <= ci).astype(BF16)
    t3 = _split3(lft)
    f_col = _dot(t3[0], utri) + _dot(t3[1], utri) + _dot(t3[2], utri) + ccol[:, 0:1]
    ccol[...] = jnp.broadcast_to(f_col[:, TS - 1:TS], ccol.shape)
    p3 = _split3(f_col * LOG2E)
    qaug = (_dot(selq_ref[...], fqt_ref[0]) + _dot(selqf_ref[0], p3[0]) + _dot(selqf_ref[1], p3[1])
            + _dot(selqf_ref[2], p3[2]) + constq_ref[...])
    for h in range(FOX_HEADS):
        qaug_ref[0, h] = qaug[h * AUG:(h + 1) * AUG, :].astype(BF16)


def _fox_prep_constants():
    W = FOX_HEADS * AUG
    selk = np.zeros((FOX_WIDTH, W), np.float32)
    selkf = np.zeros((3, LANES, W), np.float32)
    constk = np.zeros((1, W), np.float32)
    selq = np.zeros((W, FOX_WIDTH), np.float32)
    selqf = np.zeros((3, W, 16), np.float32)
    constq = np.zeros((W, 1), np.float32)
    d = FOX_HEAD_DIM
    for h in range(FOX_HEADS):
        for j in range(d):
            selk[h * d + j, h * AUG + j] = 1.0
            selq[h * AUG + j, h * d + j] = 1.0
        for p in range(3):
            selqf[p, h * AUG + d + p, h] = 1.0
            constq[h * AUG + d + 3 + p, 0] = 1.0
            constk[0, h * AUG + d + p] = 1.0
            selkf[p, h, h * AUG + d + 3 + p] = 1.0
    return (jnp.asarray(selk, BF16), jnp.asarray(selkf, BF16), jnp.asarray(constk, F32),
            jnp.asarray(selq, BF16), jnp.asarray(selqf, BF16), jnp.asarray(constq, F32))


def _fox_prep(z, zt, fk, fqt, b_forget):
    B, S, _ = z.shape
    TS = PROJ_TILE
    ns = S // TS
    selk, selkf, constk, selq, selqf, constq = _fox_prep_constants()
    brow = jnp.zeros((1, LANES), F32).at[0, :FOX_HEADS].set(b_forget)
    bcol = jnp.zeros((16, 1), F32).at[:FOX_HEADS, 0].set(b_forget)
    const = lambda a: pl.BlockSpec(a.shape, lambda b, s: (0,) * a.ndim)
    return pl.pallas_call(
        _fox_prep_kernel,
        grid=(B, ns),
        in_specs=[
            pl.BlockSpec((1, TS, LANES), lambda b, s: (b, s, 0)),
            pl.BlockSpec((1, 16, TS), lambda b, s: (b, 0, s)),
            pl.BlockSpec((1, TS, FOX_WIDTH), lambda b, s: (b, s, 0)),
            pl.BlockSpec((1, FOX_WIDTH, TS), lambda b, s: (b, 0, s)),
            const(brow), const(bcol), const(selk), const(selkf), const(constk),
            const(selq), const(selqf), const(constq),
        ],
        out_specs=(
            pl.BlockSpec((1, FOX_HEADS, TS, AUG), lambda b, s: (b, 0, s, 0)),
            pl.BlockSpec((1, FOX_HEADS, AUG, TS), lambda b, s: (b, 0, 0, s)),
        ),
        out_shape=(
            jax.ShapeDtypeStruct((B, FOX_HEADS, S, AUG), BF16),
            jax.ShapeDtypeStruct((B, FOX_HEADS, AUG, S), BF16),
        ),
        scratch_shapes=[pltpu.VMEM((1, LANES), F32), pltpu.VMEM((16, LANES), F32)],
        compiler_params=pltpu.CompilerParams(
            dimension_semantics=("arbitrary", "arbitrary"), vmem_limit_bytes=48 * 1024 * 1024),
        name="fox_prep",
    )(z, zt, fk, fqt, brow, bcol, selk, selkf, constk, selq, selqf, constq)


def _retention_kernel(q_ref, k_ref, v_ref, g_ref, dec_ref, qw_ref, kw_ref, cd_ref, o_ref, state):
    @pl.when(pl.program_id(2) == 0)
    def _():
        state[...] = jnp.zeros_like(state)

    q, k, v = q_ref[0], k_ref[0], v_ref[0]
    scores = (_dot_nt(q, k) * dec_ref[0]).astype(BF16)
    st = state[...]
    o = _dot(scores, v) + _dot((q.astype(F32) * qw_ref[0]).astype(BF16), st.astype(BF16))
    kk = k.astype(F32) * kw_ref[0]
    state[...] = st * cd_ref[0, 0:1, :] + _dot(kk.T.astype(BF16), v)
    mu = jnp.mean(o, axis=-1, keepdims=True)
    oc = o - mu
    var = jnp.mean(oc * oc, axis=-1, keepdims=True)
    o_ref[0] = (oc * lax.rsqrt(var + GN_EPS) * jax.nn.silu(g_ref[0].astype(F32))).astype(BF16)


def _retention_tables():
    L = RET_BLOCK
    log_gamma = jnp.log1p(-jnp.exp2(-5.0 - jnp.arange(RET_HEADS, dtype=F32)))
    p = jnp.arange(L, dtype=F32)
    dist = jnp.abs(p[:, None] - p[None, :])
    chunk = jnp.arange(L) // CHUNK
    allowed = (chunk[None, :] <= chunk[:, None]).astype(F32)
    dec = jnp.exp(log_gamma[:, None, None] * dist) * allowed
    lanes = lambda a: jnp.broadcast_to(a[:, :, None], (RET_HEADS, L, RET_HEAD_DIM))
    qw = lanes(jnp.exp(log_gamma[:, None] * (p[None, :] + 1.0)))
    kw = lanes(jnp.exp(log_gamma[:, None] * (L - 1.0 - p[None, :])))
    cd = jnp.broadcast_to(jnp.exp(log_gamma * L)[:, None, None], (RET_HEADS, SUBLANES, RET_HEAD_DIM))
    return dec, qw, kw, cd


def _retention(rq, rk, rv, rg):
    B, S, _ = rq.shape
    L = RET_BLOCK
    dec, qw, kw, cd = _retention_tables()
    head = pl.BlockSpec((1, L, RET_HEAD_DIM), lambda b, h, s: (b, s, h))
    per_head = lambda r: pl.BlockSpec((1, r, RET_HEAD_DIM), lambda b, h, s: (h, 0, 0))
    return pl.pallas_call(
        _retention_kernel,
        grid=(B, RET_HEADS, S // L),
        in_specs=[head, head, head, head,
                  pl.BlockSpec((1, L, L), lambda b, h, s: (h, 0, 0)),
                  per_head(L), per_head(L), per_head(SUBLANES)],
        out_specs=head,
        out_shape=jax.ShapeDtypeStruct((B, S, RET_WIDTH), BF16),
        scratch_shapes=[pltpu.VMEM((RET_HEAD_DIM, RET_HEAD_DIM), F32)],
        compiler_params=pltpu.CompilerParams(dimension_semantics=("arbitrary",) * 3),
        name="retention",
    )(rq, rk, rv, rg, dec, qw, kw, cd)


def _fox_attn_kernel(q_ref, k_ref, v_ref, o_ref, s_a, s_b, cm_a, cm_b, m_ref, acc_ref):
    qi = pl.program_id(2)
    T = FOX_TQ
    d = FOX_HEAD_DIM

    def scores(j, s_ref, cm_ref):
        for hh in range(2):
            kj = k_ref[0, hh, pl.ds(pl.multiple_of(j * T, T), T), :]
            st = _dot(kj, q_ref[0, hh])
            s_ref[hh] = st
            cm_ref[hh] = jnp.max(st, axis=0, keepdims=True)

    def consume(j, s_ref, cm_ref, masked):
        for hh in range(2):
            st = s_ref[hh]
            if masked:
                key = lax.broadcasted_iota(jnp.int32, (T, T), 0)
                qry = lax.broadcasted_iota(jnp.int32, (T, T), 1)
                st = jnp.where(key <= qry, st, -jnp.inf)
                cm = jnp.max(st, axis=0, keepdims=True)
            else:
                cm = cm_ref[hh]
            m = m_ref[hh]
            m_new = jnp.maximum(m, cm)
            p = jnp.exp2(st - m_new).astype(BF16)
            vj = v_ref[0, j, hh * V_AUG:(hh + 1) * V_AUG, :]
            acc_ref[hh] = jnp.exp2(m - m_new) * acc_ref[hh] + _dot(vj, p)
            m_ref[hh] = m_new

    m_ref[...] = jnp.full(m_ref.shape, -jnp.inf, F32)
    acc_ref[...] = jnp.zeros(acc_ref.shape, F32)
    scores(0, s_a, cm_a)

    def pair(jj, carry):
        j = 2 * jj
        scores(j + 1, s_b, cm_b)
        consume(j, s_a, cm_a, False)
        scores(j + 2, s_a, cm_a)
        consume(j + 1, s_b, cm_b, False)
        return carry

    lax.fori_loop(0, qi // 2, pair, 0)

    @pl.when(qi % 2 == 0)
    def _():
        consume(qi, s_a, cm_a, True)

    @pl.when(qi % 2 == 1)
    def _():
        scores(qi, s_b, cm_b)
        consume(qi - 1, s_a, cm_a, False)
        consume(qi, s_b, cm_b, True)

    outs = [acc_ref[hh, 0:d, :] / acc_ref[hh, d:d + 1, :] for hh in range(2)]
    o_ref[0] = jnp.concatenate(outs, axis=0).T.astype(BF16)


def _fox_attn(qaug, kaug, fvt):
    B, H, S, _ = kaug.shape
    nk = S // FOX_TK
    return pl.pallas_call(
        _fox_attn_kernel,
        grid=(B, H // 2, S // FOX_TQ),
        in_specs=[
            pl.BlockSpec((1, 2, AUG, FOX_TQ), lambda b, p, q: (b, p, 0, q)),
            pl.BlockSpec((1, 2, S, AUG), lambda b, p, q: (b, p, 0, 0)),
            pl.BlockSpec((1, nk, 2 * V_AUG, FOX_TK), lambda b, p, q: (b, 0, p, 0)),
        ],
        out_specs=pl.BlockSpec((1, FOX_TQ, 2 * FOX_HEAD_DIM), lambda b, p, q: (b, q, p)),
        out_shape=jax.ShapeDtypeStruct((B, S, FOX_WIDTH), BF16),
        scratch_shapes=[
            pltpu.VMEM((2, FOX_TK, FOX_TQ), F32), pltpu.VMEM((2, FOX_TK, FOX_TQ), F32),
            pltpu.VMEM((2, 1, FOX_TQ), F32), pltpu.VMEM((2, 1, FOX_TQ), F32),
            pltpu.VMEM((2, 1, FOX_TQ), F32), pltpu.VMEM((2, V_AUG, FOX_TQ), F32),
        ],
        compiler_params=pltpu.CompilerParams(
            dimension_semantics=("arbitrary",) * 3, vmem_limit_bytes=48 * 1024 * 1024),
        name="fox_attn",
    )(qaug, kaug, fvt)


def _out_router_kernel(x_ref, oret_ref, ofox_ref, wor_ref, wof_ref, g_ref, wrh_ref, wrl_ref, br_ref,
                       h1_ref, u2_ref, sel_ref, cnt_ref):
    TM = x_ref.shape[0]
    h1 = x_ref[...] + _dot(oret_ref[...], wor_ref[...]) + _dot(ofox_ref[...], wof_ref[...])
    h1_ref[...] = h1
    u2 = _rms(h1, g_ref[...])
    uh = u2.astype(BF16)
    u2_ref[...] = uh
    ul = (u2 - uh.astype(F32)).astype(BF16)
    logits = _dot(uh, wrh_ref[...]) + _dot(ul, wrh_ref[...]) + _dot(uh, wrl_ref[...]) + br_ref[...]
    lane = lax.broadcasted_iota(jnp.int32, (TM, LANES), 1)
    l = jnp.where(lane < N_EXPERTS, logits, -jnp.inf)
    picks, vals = [], []
    for _ in range(TOP_K):
        m = jnp.max(l, axis=-1, keepdims=True)
        idx = jnp.min(jnp.where(l == m, lane, LANES), axis=-1, keepdims=True)
        pick = lane == idx
        picks.append(pick)
        vals.append(m)
        l = jnp.where(pick, -jnp.inf, l)
    exps = [jnp.exp(v - vals[0]) for v in vals]
    den = exps[0] + exps[1] + exps[2] + exps[3]
    sel = jnp.full((TM, LANES), -1.0, F32)
    for pick, e in zip(picks, exps):
        sel = jnp.where(pick, e / den, sel)
    sel_ref[...] = sel
    cnt = jnp.sum((sel >= 0.0).astype(F32), axis=0, keepdims=True)
    cnt_ref[0] = jnp.broadcast_to(cnt, (SUBLANES, LANES))


def _out_router(x2, o_ret, o_fox, wor, wof, g, wrh, wrl, br):
    T, D = x2.shape
    TM = MOE_TILE
    nT = T // TM
    const = lambda a: pl.BlockSpec(a.shape, lambda i: (0,) * a.ndim)
    tok = lambda w: pl.BlockSpec((TM, w), lambda i: (i, 0))
    return pl.pallas_call(
        _out_router_kernel,
        grid=(nT,),
        in_specs=[tok(D), tok(RET_WIDTH), tok(FOX_WIDTH), const(wor), const(wof), const(g),
                  const(wrh), const(wrl), const(br)],
        out_specs=(tok(D), tok(D), tok(LANES), pl.BlockSpec((1, SUBLANES, LANES), lambda i: (i, 0, 0))),
        out_shape=(
            jax.ShapeDtypeStruct((T, D), F32),
            jax.ShapeDtypeStruct((T, D), BF16),
            jax.ShapeDtypeStruct((T, LANES), F32),
            jax.ShapeDtypeStruct((nT, SUBLANES, LANES), F32),
        ),
        compiler_params=pltpu.CompilerParams(dimension_semantics=("arbitrary",)),
        name="out_router",
    )(x2, o_ret, o_fox, wor, wof, g, wrh, wrl, br)


def _tile_sort(sel):
    TM = sel.shape[0]
    NS = TOP_K * TM
    maskf = (sel >= 0.0).astype(F32)
    mask = maskf.astype(BF16)
    ri = lax.broadcasted_iota(jnp.int32, (TM, TM), 0)
    ci = lax.broadcasted_iota(jnp.int32, (TM, TM), 1)
    rank = _dot((ri > ci).astype(BF16), mask)
    cnt = jnp.sum(maskf, axis=0, keepdims=True)
    ei = lax.broadcasted_iota(jnp.int32, (LANES, LANES), 0)
    ej = lax.broadcasted_iota(jnp.int32, (LANES, LANES), 1)
    cnt8 = jnp.broadcast_to(cnt, (SUBLANES, LANES)).astype(BF16)
    off = _dot(cnt8, (ei < ej).astype(BF16))[0:1, :]
    slot = lax.broadcasted_iota(jnp.int32, (NS, LANES), 0).astype(F32)
    esel = ((slot >= off) & (slot < off + cnt)).astype(BF16)
    return mask, rank.astype(BF16), esel, off, cnt


def _segment_dmas(step, slot, segdst_ref, cnt_ref, local, remote_rows, sem, to_remote, wait):
    def body(e, off):
        c = cnt_ref[step * N_EXPERTS + e]
        dst = segdst_ref[step * N_EXPERTS + e]
        bit = MOE_TILE
        while bit >= 1:
            done = c & (~(2 * bit - 1))

            @pl.when((c & bit) != 0)
            def _(bit=bit, done=done):
                loc = local.at[slot, pl.ds((off + done) * ROW_TILES, bit * ROW_TILES), :]
                rem = remote_rows.at[pl.ds((dst + done) * ROW_TILES, bit * ROW_TILES), :]
                cp = (pltpu.make_async_copy(loc, rem, sem.at[slot]) if to_remote
                      else pltpu.make_async_copy(rem, loc, sem.at[slot]))
                cp.wait() if wait else cp.start()
            bit //= 2
        return off + c

    lax.fori_loop(0, N_EXPERTS, body, 0)


def _dispatch_kernel(segdst_ref, cnt_ref, paddst_ref, padcnt_ref, nused_ref, u2_ref, sel_ref, xs_ref,
                     buf, zbuf, sems, zsem):
    i = pl.program_id(0)
    last = pl.num_programs(0) - 1
    slot = i % 2
    TM = MOE_TILE
    NS = TOP_K * TM
    mask, rank, esel, off, _ = _tile_sort(sel_ref[...])
    slot_id = lax.broadcasted_iota(jnp.int32, (NS, 1), 0).astype(F32)
    r_s = slot_id - jnp.sum(esel.astype(F32) * off, axis=1, keepdims=True)
    perm = ((_dot_nt(esel, mask) > 0.5) & (_dot_nt(esel, rank) == r_s)).astype(BF16)
    rows = _dot(perm, u2_ref[...])

    @pl.when(i >= 2)
    def _():
        _segment_dmas(i - 2, slot, segdst_ref, cnt_ref, buf, xs_ref, sems, True, True)

    for j in range(ROW_TILES):
        buf[slot, pl.ds(j, NS, stride=ROW_TILES), :] = rows[:, j * LANES:(j + 1) * LANES]
    _segment_dmas(i, slot, segdst_ref, cnt_ref, buf, xs_ref, sems, True, False)

    @pl.when(i == last)
    def _():
        @pl.when(i >= 1)
        def _():
            _segment_dmas(i - 1, 1 - slot, segdst_ref, cnt_ref, buf, xs_ref, sems, True, True)
        _segment_dmas(i, slot, segdst_ref, cnt_ref, buf, xs_ref, sems, True, True)
        zbuf[...] = jnp.zeros_like(zbuf)
        half = EXPERT_BLOCK // 2 * ROW_TILES
        n_blocks = xs_ref.shape[0] // (EXPERT_BLOCK * ROW_TILES)
        for wait in (False, True):
            def unused(hb, carry, wait=wait):
                cp = pltpu.make_async_copy(zbuf, xs_ref.at[pl.ds(hb * half, half), :], zsem.at[0])
                cp.wait() if wait else cp.start()
                return carry
            lax.fori_loop(2 * nused_ref[0], 2 * n_blocks, unused, 0)


            def body(e, carry, wait=wait):
                c = padcnt_ref[e]
                dst = paddst_ref[e]
                bit = EXPERT_BLOCK // 2
                while bit >= 1:
                    done = c & (~(2 * bit - 1))

                    @pl.when((c & bit) != 0)
                    def _(bit=bit, done=done):
                        cp = pltpu.make_async_copy(
                            zbuf.at[pl.ds(0, bit * ROW_TILES), :],
                            xs_ref.at[pl.ds((dst + done) * ROW_TILES, bit * ROW_TILES), :], zsem.at[0])
                        cp.wait() if wait else cp.start()
                    bit //= 2
                return carry
            lax.fori_loop(0, N_EXPERTS, body, 0)


def _dispatch(u2, sel, segdst, cnt, paddst, padcnt, n_used, n_rows):
    T, D = u2.shape
    TM = MOE_TILE
    NS = TOP_K * TM
    return pl.pallas_call(
        _dispatch_kernel,
        grid_spec=pltpu.PrefetchScalarGridSpec(
            num_scalar_prefetch=5,
            grid=(T // TM,),
            in_specs=[pl.BlockSpec((TM, D), lambda i, *_: (i, 0)),
                      pl.BlockSpec((TM, LANES), lambda i, *_: (i, 0))],
            out_specs=pl.BlockSpec(memory_space=pl.ANY),
            scratch_shapes=[pltpu.VMEM((2, NS * ROW_TILES, LANES), F32),
                            pltpu.VMEM((EXPERT_BLOCK // 2 * ROW_TILES, LANES), F32),
                            pltpu.SemaphoreType.DMA((2,)), pltpu.SemaphoreType.DMA((1,))],
        ),
        out_shape=jax.ShapeDtypeStruct((n_rows * ROW_TILES, LANES), F32),
        compiler_params=pltpu.CompilerParams(
            dimension_semantics=("arbitrary",), vmem_limit_bytes=48 * 1024 * 1024),
        name="dispatch",
    )(segdst, cnt, paddst, padcnt, n_used, u2, sel)


def _expert_kernel(bexp_ref, nused_ref, xs_ref, w1_ref, b1_ref, w2_ref, b2_ref, ys_ref, w1b, w2b):
    b = pl.program_id(0)
    BLK = EXPERT_BLOCK
    used = b < nused_ref[0]

    @pl.when(used)
    def _():
        e = bexp_ref[b]
        prev = bexp_ref[jnp.maximum(b - 1, 0)]

        @pl.when((b == 0) | (e != prev))
        def _():
            rows = 128

            def cast(r, carry):
                sl = pl.ds(pl.multiple_of(r * rows, rows), rows)
                w1b[sl, :] = w1_ref[0, sl, :].astype(BF16)
                w2b[sl, :] = w2_ref[0, sl, :].astype(BF16)
                return carry
            lax.fori_loop(0, D_MODEL // rows, cast, 0)

        x = jnp.concatenate([xs_ref[pl.ds(j, BLK, stride=ROW_TILES), :] for j in range(ROW_TILES)],
                            axis=1).astype(BF16)
        h = _dot(x, w1b[...]) + b1_ref[0]
        glu = jnp.minimum(h[:, :D_FF], SWIGLU_LIMIT)
        lin = jnp.clip(h[:, D_FF:], -SWIGLU_LIMIT, SWIGLU_LIMIT)
        act = glu * jax.nn.sigmoid(SWIGLU_ALPHA * glu) * (lin + 1.0)
        y = _dot(act.astype(BF16), w2b[...]) + b2_ref[0]
        for j in range(ROW_TILES):
            ys_ref[pl.ds(j, BLK, stride=ROW_TILES), :] = y[:, j * LANES:(j + 1) * LANES]

    @pl.when(jnp.logical_not(used))
    def _():
        ys_ref[...] = jnp.zeros_like(ys_ref)


def _experts(xs, block_exp, n_used, w1, b1, w2, b2):
    BLK = EXPERT_BLOCK
    NB = xs.shape[0] // (BLK * ROW_TILES)
    blk = lambda b, nused: jnp.minimum(b, nused[0] - 1)
    return pl.pallas_call(
        _expert_kernel,
        grid_spec=pltpu.PrefetchScalarGridSpec(
            num_scalar_prefetch=2,
            grid=(NB,),
            in_specs=[
                pl.BlockSpec((BLK * ROW_TILES, LANES), lambda b, bexp, nused: (blk(b, nused), 0)),
                pl.BlockSpec((1, D_MODEL, 2 * D_FF), lambda b, bexp, nused: (bexp[blk(b, nused)], 0, 0)),
                pl.BlockSpec((1, 1, 2 * D_FF), lambda b, bexp, nused: (bexp[blk(b, nused)], 0, 0)),
                pl.BlockSpec((1, D_FF, D_MODEL), lambda b, bexp, nused: (bexp[blk(b, nused)], 0, 0)),
                pl.BlockSpec((1, 1, D_MODEL), lambda b, bexp, nused: (bexp[blk(b, nused)], 0, 0)),
            ],
            out_specs=pl.BlockSpec((BLK * ROW_TILES, LANES), lambda b, bexp, nused: (b, 0)),
            scratch_shapes=[pltpu.VMEM((D_MODEL, 2 * D_FF), BF16), pltpu.VMEM((D_FF, D_MODEL), BF16)],
        ),
        out_shape=jax.ShapeDtypeStruct(xs.shape, F32),
        compiler_params=pltpu.CompilerParams(
            dimension_semantics=("arbitrary",), vmem_limit_bytes=56 * 1024 * 1024),
        name="experts",
    )(block_exp, n_used, xs, w1, b1[:, None, :], w2, b2[:, None, :])


def _combine_kernel(segdst_ref, cnt_ref, ys_ref, sel_ref, h1_ref, g_ref, out_ref, buf, sems):
    i = pl.program_id(0)
    n = pl.num_programs(0)
    slot = i % 2
    TM = MOE_TILE
    NS = TOP_K * TM

    @pl.when(i == 0)
    def _():
        _segment_dmas(i, slot, segdst_ref, cnt_ref, buf, ys_ref, sems, False, False)

    @pl.when(i + 1 < n)
    def _():
        _segment_dmas(i + 1, 1 - slot, segdst_ref, cnt_ref, buf, ys_ref, sems, False, False)

    sel = sel_ref[...]
    mask, rank, esel, off, _ = _tile_sort(sel)
    gate = jnp.maximum(sel, 0.0)
    gh = gate.astype(BF16)
    gl = (gate - gh.astype(F32)).astype(BF16)
    o3 = _split3(jnp.broadcast_to(off, (SUBLANES, LANES)))
    off_s = (_dot_nt(o3[0], esel) + _dot_nt(o3[1], esel) + _dot_nt(o3[2], esel))[0:1, :]
    r_s = lax.broadcasted_iota(jnp.int32, (1, NS), 1).astype(F32) - off_s
    hit = (_dot_nt(mask, esel) > 0.5) & (_dot_nt(rank, esel) == r_s)
    unperm = jnp.where(hit, _dot_nt(gh, esel) + _dot_nt(gl, esel), 0.0).astype(BF16)

    _segment_dmas(i, slot, segdst_ref, cnt_ref, buf, ys_ref, sems, False, True)
    y = jnp.concatenate([buf[slot, pl.ds(j, NS, stride=ROW_TILES), :] for j in range(ROW_TILES)],
                        axis=1).astype(BF16)
    h2 = h1_ref[...] + _dot(unperm, y)
    out_ref[...] = _rms(h2, g_ref[...])


def _combine(ys, sel, h1, g, segdst, cnt):
    T, D = h1.shape
    TM = MOE_TILE
    NS = TOP_K * TM
    return pl.pallas_call(
        _combine_kernel,
        grid_spec=pltpu.PrefetchScalarGridSpec(
            num_scalar_prefetch=2,
            grid=(T // TM,),
            in_specs=[pl.BlockSpec(memory_space=pl.ANY),
                      pl.BlockSpec((TM, LANES), lambda i, *_: (i, 0)),
                      pl.BlockSpec((TM, D), lambda i, *_: (i, 0)),
                      pl.BlockSpec((1, D), lambda i, *_: (0, 0))],
            out_specs=pl.BlockSpec((TM, D), lambda i, *_: (i, 0)),
            scratch_shapes=[pltpu.VMEM((2, NS * ROW_TILES, LANES), F32), pltpu.SemaphoreType.DMA((2,))],
        ),
        out_shape=jax.ShapeDtypeStruct((T, D), F32),
        compiler_params=pltpu.CompilerParams(
            dimension_semantics=("arbitrary",), vmem_limit_bytes=48 * 1024 * 1024),
        name="combine",
    )(segdst, cnt, ys, sel, h1, g)


def _routing_tables(cnt_tiles):
    BLK = EXPERT_BLOCK
    nT = cnt_tiles.shape[0]
    A = nT * MOE_TILE * TOP_K
    NB = A // BLK + N_EXPERTS
    total = jnp.sum(cnt_tiles, axis=0)
    padded = (total + BLK - 1) // BLK * BLK
    pad_ends = jnp.cumsum(padded)
    pad_starts = pad_ends - padded
    before = jnp.cumsum(cnt_tiles, axis=0) - cnt_tiles
    segdst = (pad_starts[None, :] + before).reshape(-1).astype(jnp.int32)
    block_start = jnp.arange(NB, dtype=jnp.int32) * BLK
    block_exp = jnp.minimum(jnp.sum(pad_ends[None, :] <= block_start[:, None], axis=1), N_EXPERTS - 1).astype(jnp.int32)
    n_used = (pad_ends[-1] // BLK).astype(jnp.int32).reshape(1)
    paddst = (pad_starts + total).astype(jnp.int32)
    padcnt = (padded - total).astype(jnp.int32)
    return segdst, cnt_tiles.reshape(-1).astype(jnp.int32), paddst, padcnt, block_exp, n_used, NB * BLK


def _rotary_tables(S):
    half = RET_HEAD_DIM // 2
    inv_freq = ROPE_BASE ** (-jnp.arange(half, dtype=F32) / half)
    ang = jnp.arange(S, dtype=F32)[:, None] * inv_freq[None, :]
    cos, sin = jnp.cos(ang), jnp.sin(ang)
    return jnp.concatenate([cos, cos], axis=-1), jnp.concatenate([-sin, sin], axis=-1)


def _layer(h, norm_mix_g, w_in, b_forget, w_out, norm_ffn_g, w_router, b_router,
           w_exp_in, b_exp_in, w_exp_out, b_exp_out, final_g):
    B, S, D = h.shape
    R, Fw = RET_WIDTH, FOX_WIDTH
    cos, sin = _rotary_tables(S)
    wb = w_in.astype(BF16)
    wr = wb[:, :4 * R]
    wfq, wfk, wfv = (wb[:, 4 * R + i * Fw:4 * R + (i + 1) * Fw] for i in range(3))
    wz = jnp.zeros((D, LANES), BF16).at[:, :FOX_HEADS].set(wb[:, 4 * R + 3 * Fw:])
    wzt = jnp.zeros((16, D), BF16).at[:FOX_HEADS, :].set(wb[:, 4 * R + 3 * Fw:].T)
    wfvt = jnp.pad(wfv.T.reshape(FOX_HEADS, FOX_HEAD_DIM, D), ((0, 0), (0, V_AUG - FOX_HEAD_DIM), (0, 0)))
    vone = jnp.zeros((FOX_HEADS, V_AUG, 1), F32).at[:, FOX_HEAD_DIM, 0].set(1.0)
    rq, rk, rv, rg, fk, fqt, fvt, z, zt = _in_proj(
        h, norm_mix_g[None, :], cos, sin, wr, wfk, wfq.T, wfvt.reshape(FOX_HEADS * V_AUG, D),
        vone.reshape(FOX_HEADS * V_AUG, 1), wz, wzt)
    kaug, qaug = _fox_prep(z, zt, fk, fqt, b_forget)
    o_ret = _retention(rq, rk, rv, rg)
    o_fox = _fox_attn(qaug, kaug, fvt)

    T = B * S
    wo = w_out.astype(BF16)
    wrt = jnp.zeros((D, LANES), F32).at[:, :N_EXPERTS].set(w_router)
    wrh = wrt.astype(BF16)
    wrl = (wrt - wrh.astype(F32)).astype(BF16)
    br = jnp.zeros((1, LANES), F32).at[0, :N_EXPERTS].set(b_router)
    h1, u2, sel, cnt = _out_router(h.reshape(T, D), o_ret.reshape(T, R), o_fox.reshape(T, Fw),
                                   wo[:R], wo[R:], norm_ffn_g[None, :], wrh, wrl, br)
    cnt_tiles = cnt[:, 0, :N_EXPERTS].astype(jnp.int32)
    segdst, cnt_flat, paddst, padcnt, block_exp, n_used, n_rows = _routing_tables(cnt_tiles)
    xs = _dispatch(u2, sel, segdst, cnt_flat, paddst, padcnt, n_used, n_rows)
    ys = _experts(xs, block_exp, n_used, w_exp_in, b_exp_in, w_exp_out, b_exp_out)
    out = _combine(ys, sel, h1, final_g[None, :], segdst, cnt_flat)
    return out.reshape(B, S, D)


def kernel(x, norm_mix_g, w_in, b_forget, w_out, norm_ffn_g, w_router, b_router,
           w_exp_in, b_exp_in, w_exp_out, b_exp_out, norm_final_g):
    depth = w_in.shape[0]
    assert depth == 1, "the fused final RMSNorm assumes a single layer"
    return _layer(x, norm_mix_g[0], w_in[0], b_forget[0], w_out[0], norm_ffn_g[0], w_router[0], b_router[0],
                  w_exp_in[0], b_exp_in[0], w_exp_out[0], b_exp_out[0], norm_final_g)
```

```python
import functools

import numpy as np
import jax
import jax.numpy as jnp
from jax import lax
from jax.experimental import pallas as pl
from jax.experimental.pallas import tpu as pltpu

F32 = jnp.float32
BF16 = jnp.bfloat16

D_MODEL = 1024
RET_HEADS, RET_HEAD_DIM = 4, 128
RET_WIDTH = RET_HEADS * RET_HEAD_DIM
FOX_HEADS, FOX_HEAD_DIM = 8, 64
FOX_WIDTH = FOX_HEADS * FOX_HEAD_DIM
CHUNK = 64
ROPE_BASE = 10000.0
N_EXPERTS = 32
TOP_K = 4
D_FF = D_MODEL
SWIGLU_ALPHA = 1.702
SWIGLU_LIMIT = 7.0
RMS_EPS = 1e-5
GN_EPS = 1e-5

LANES = 128
SUBLANES = 8
ROW_TILES = D_MODEL // LANES

PROJ_TILE = 512
RET_BLOCK = 256
FOX_TQ = 512
FOX_TK = PROJ_TILE
FOX_SUB = 256
AUG = 128
V_AUG = 80
LOG2E = 1.4426950408889634
MOE_TILE = 256
EXPERT_BLOCK = 512
EXPERT_PASS_ROWS = 256

NT_DIMS = (((1,), (1,)), ((), ()))


def _split3(a):
    hi = a.astype(BF16)
    r1 = a - hi.astype(F32)
    mid = r1.astype(BF16)
    lo = (r1 - mid.astype(F32)).astype(BF16)
    return hi, mid, lo


def _dot(a, b):
    return jnp.dot(a, b, preferred_element_type=F32)


def _dot_nt(a, b):
    return lax.dot_general(a, b, NT_DIMS, preferred_element_type=F32)


def _rms(x, g):
    return x * lax.rsqrt(jnp.mean(x * x, axis=-1, keepdims=True) + RMS_EPS) * g


def _in_proj_kernel(x_ref, g_ref, cos_ref, sin_ref, wr_ref, wfk_ref, wfqt_ref, wfvt_ref, vone_ref, wz_ref, wzt_ref,
                    rq_ref, rk_ref, rv_ref, rg_ref, fk_ref, fqt_ref, fvt_ref, z_ref, zt_ref):
    u = _rms(x_ref[0], g_ref[...]).astype(BF16)
    r = _dot(u, wr_ref[...])
    cos, sin = cos_ref[...], sin_ref[...]
    k_scale = RET_HEAD_DIM ** -0.5
    for h in range(RET_HEADS):
        lo = h * RET_HEAD_DIM
        q = r[:, lo:lo + RET_HEAD_DIM]
        k = r[:, RET_WIDTH + lo:RET_WIDTH + lo + RET_HEAD_DIM]
        rq_ref[0, :, lo:lo + RET_HEAD_DIM] = (q * cos + pltpu.roll(q, RET_HEAD_DIM // 2, 1) * sin).astype(BF16)
        rk_ref[0, :, lo:lo + RET_HEAD_DIM] = (
            (k * cos + pltpu.roll(k, RET_HEAD_DIM // 2, 1) * sin) * k_scale).astype(BF16)
    rv_ref[0] = r[:, 2 * RET_WIDTH:3 * RET_WIDTH].astype(BF16)
    rg_ref[0] = r[:, 3 * RET_WIDTH:4 * RET_WIDTH].astype(BF16)
    fk_ref[0] = _dot(u, wfk_ref[...]).astype(BF16)
    q_scale = FOX_HEAD_DIM ** -0.5 * LOG2E
    fqt_ref[0] = (_dot_nt(wfqt_ref[...], u) * q_scale).astype(BF16)
    fvt_ref[0, 0] = (_dot_nt(wfvt_ref[...], u) + vone_ref[...]).astype(BF16)
    z_ref[0] = _dot(u, wz_ref[...])
    zt_ref[0] = _dot_nt(wzt_ref[...], u)


def _in_proj(x, g, cos, sin, wr, wfk, wfqt, wfvt, vone, wz, wzt):
    B, S, D = x.shape
    TM = PROJ_TILE
    ns = S // TM
    const = lambda shape: pl.BlockSpec(shape, lambda b, s: (0,) * len(shape))
    tok = lambda w: pl.BlockSpec((1, TM, w), lambda b, s: (b, s, 0))
    out_shape = (
        jax.ShapeDtypeStruct((B, S, RET_WIDTH), BF16),
        jax.ShapeDtypeStruct((B, S, RET_WIDTH), BF16),
        jax.ShapeDtypeStruct((B, S, RET_WIDTH), BF16),
        jax.ShapeDtypeStruct((B, S, RET_WIDTH), BF16),
        jax.ShapeDtypeStruct((B, S, FOX_WIDTH), BF16),
        jax.ShapeDtypeStruct((B, FOX_WIDTH, S), BF16),
        jax.ShapeDtypeStruct((B, ns, FOX_HEADS * V_AUG, TM), BF16),
        jax.ShapeDtypeStruct((B, S, LANES), F32),
        jax.ShapeDtypeStruct((B, 16, S), F32),
    )
    return pl.pallas_call(
        _in_proj_kernel,
        grid=(B, ns),
        in_specs=[
            pl.BlockSpec((1, TM, D), lambda b, s: (b, s, 0)),
            const((1, D)),
            pl.BlockSpec((TM, RET_HEAD_DIM), lambda b, s: (s, 0)),
            pl.BlockSpec((TM, RET_HEAD_DIM), lambda b, s: (s, 0)),
            const(wr.shape), const(wfk.shape), const(wfqt.shape), const(wfvt.shape), const(vone.shape),
            const(wz.shape), const(wzt.shape),
        ],
        out_specs=(
            tok(RET_WIDTH), tok(RET_WIDTH), tok(RET_WIDTH), tok(RET_WIDTH), tok(FOX_WIDTH),
            pl.BlockSpec((1, FOX_WIDTH, TM), lambda b, s: (b, 0, s)),
            pl.BlockSpec((1, 1, FOX_HEADS * V_AUG, TM), lambda b, s: (b, s, 0, 0)),
            tok(LANES),
            pl.BlockSpec((1, 16, TM), lambda b, s: (b, 0, s)),
        ),
        out_shape=out_shape,
        compiler_params=pltpu.CompilerParams(
            dimension_semantics=("arbitrary", "arbitrary"), vmem_limit_bytes=48 * 1024 * 1024),
        name="in_proj",
    )(x, g, cos, sin, wr, wfk, wfqt, wfvt, vone, wz, wzt)


def _fox_prep_kernel(z_ref, zt_ref, fk_ref, fqt_ref, brow_ref, bcol_ref,
                     selkf_ref, constk_ref, selqf_ref, constq_ref, kaug_ref, qaug_ref, crow, ccol):
    TS = z_ref.shape[1]
    d = FOX_HEAD_DIM
    H = FOX_HEADS

    @pl.when(pl.program_id(1) == 0)
    def _():
        crow[...] = jnp.zeros_like(crow)
        ccol[...] = jnp.zeros_like(ccol)

    ri = lax.broadcasted_iota(jnp.int32, (TS, TS), 0)
    ci = lax.broadcasted_iota(jnp.int32, (TS, TS), 1)

    lane = lax.broadcasted_iota(jnp.int32, (TS, LANES), 1)
    lf = jnp.where(lane < H, jax.nn.log_sigmoid(z_ref[0] + brow_ref[...]), 0.0)
    ltri = (ri >= ci).astype(BF16)
    h3 = _split3(lf)
    f_row = _dot(ltri, h3[0]) + _dot(ltri, h3[1]) + _dot(ltri, h3[2]) + crow[...]
    crow[...] = f_row[TS - 1:TS, :]
    n3 = _split3(f_row * -LOG2E)
    pieces = (n3[0].astype(F32) + pltpu.roll(n3[1].astype(F32), H, 1)
              + pltpu.roll(n3[2].astype(F32), 2 * H, 1)).astype(BF16)
    for g in range(H // 2):
        bias = _dot(pieces, selkf_ref[g])
        kg = fk_ref[0, :, g * 2 * d:(g + 1) * 2 * d].astype(F32)
        for o in range(2):
            kh = kg if o == 0 else pltpu.roll(kg, d, 1)
            extra = bias[:, o * AUG:(o + 1) * AUG] + constk_ref[...]
            kaug_ref[0, 2 * g + o] = jnp.where(lane < d, kh, extra).astype(BF16)

    row = lax.broadcasted_iota(jnp.int32, (16, TS), 0)
    lft = jnp.where(row < H, jax.nn.log_sigmoid(zt_ref[0] + bcol_ref[...]), 0.0)
    utri = (ri <= ci).astype(BF16)
    t3 = _split3(lft)
    f_col = _dot(t3[0], utri) + _dot(t3[1], utri) + _dot(t3[2], utri) + ccol[:, 0:1]
    ccol[...] = jnp.broadcast_to(f_col[:, TS - 1:TS], ccol.shape)
    pieces_t = jnp.concatenate(_split3(f_col * LOG2E), axis=0)
    for h in range(H):
        extra = _dot(selqf_ref[h], pieces_t) + constq_ref[...]
        qaug_ref[0, h] = jnp.concatenate([fqt_ref[0, h * d:(h + 1) * d, :], extra.astype(BF16)], axis=0)


def _fox_prep_constants():
    d, H = FOX_HEAD_DIM, FOX_HEADS
    selkf = np.zeros((H // 2, LANES, 2 * AUG), np.float32)
    constk = np.zeros((1, AUG), np.float32)
    selqf = np.zeros((H, d, 48), np.float32)
    constq = np.zeros((d, 1), np.float32)
    for p in range(3):
        constk[0, d + p] = 1.0
        constq[3 + p, 0] = 1.0
        for h in range(H):
            selkf[h // 2, p * H + h, (h % 2) * AUG + d + 3 + p] = 1.0
            selqf[h, p, p * 16 + h] = 1.0
    return jnp.asarray(selkf, BF16), jnp.asarray(constk, F32), jnp.asarray(selqf, BF16), jnp.asarray(constq, F32)


def _fox_prep(z, zt, fk, fqt, b_forget):
    B, S, _ = z.shape
    TS = PROJ_TILE
    ns = S // TS
    selkf, constk, selqf, constq = _fox_prep_constants()
    brow = jnp.zeros((1, LANES), F32).at[0, :FOX_HEADS].set(b_forget)
    bcol = jnp.zeros((16, 1), F32).at[:FOX_HEADS, 0].set(b_forget)
    const = lambda a: pl.BlockSpec(a.shape, lambda b, s: (0,) * a.ndim)
    return pl.pallas_call(
        _fox_prep_kernel,
        grid=(B, ns),
        in_specs=[
            pl.BlockSpec((1, TS, LANES), lambda b, s: (b, s, 0)),
            pl.BlockSpec((1, 16, TS), lambda b, s: (b, 0, s)),
            pl.BlockSpec((1, TS, FOX_WIDTH), lambda b, s: (b, s, 0)),
            pl.BlockSpec((1, FOX_WIDTH, TS), lambda b, s: (b, 0, s)),
            const(brow), const(bcol), const(selkf), const(constk), const(selqf), const(constq),
        ],
        out_specs=(
            pl.BlockSpec((1, FOX_HEADS, TS, AUG), lambda b, s: (b, 0, s, 0)),
            pl.BlockSpec((1, FOX_HEADS, AUG, TS), lambda b, s: (b, 0, 0, s)),
        ),
        out_shape=(
            jax.ShapeDtypeStruct((B, FOX_HEADS, S, AUG), BF16),
            jax.ShapeDtypeStruct((B, FOX_HEADS, AUG, S), BF16),
        ),
        scratch_shapes=[pltpu.VMEM((1, LANES), F32), pltpu.VMEM((16, LANES), F32)],
        compiler_params=pltpu.CompilerParams(
            dimension_semantics=("arbitrary", "arbitrary"), vmem_limit_bytes=48 * 1024 * 1024),
        name="fox_prep",
    )(z, zt, fk, fqt, brow, bcol, selkf, constk, selqf, constq)


def _retention_kernel(q_ref, k_ref, v_ref, g_ref, dec_ref, qw_ref, kw_ref, cd_ref, o_ref, state):
    @pl.when(pl.program_id(1) == 0)
    def _():
        state[...] = jnp.zeros_like(state)

    for h in range(RET_HEADS):
        hs = slice(h * RET_HEAD_DIM, (h + 1) * RET_HEAD_DIM)
        q, k, v = q_ref[0, :, hs], k_ref[0, :, hs], v_ref[0, :, hs]
        scores = (_dot_nt(q, k) * dec_ref[h]).astype(BF16)
        st = state[h]
        o = _dot(scores, v) + _dot((q.astype(F32) * qw_ref[h]).astype(BF16), st.astype(BF16))
        kk = k.astype(F32) * kw_ref[h]
        state[h] = st * cd_ref[h, 0:1, :] + _dot(kk.T.astype(BF16), v)
        mu = jnp.mean(o, axis=-1, keepdims=True)
        oc = o - mu
        var = jnp.mean(oc * oc, axis=-1, keepdims=True)
        o_ref[0, :, hs] = (oc * lax.rsqrt(var + GN_EPS) * jax.nn.silu(g_ref[0, :, hs].astype(F32))).astype(BF16)


def _retention_tables():
    L = RET_BLOCK
    log_gamma = jnp.log1p(-jnp.exp2(-5.0 - jnp.arange(RET_HEADS, dtype=F32)))
    p = jnp.arange(L, dtype=F32)
    dist = jnp.abs(p[:, None] - p[None, :])
    chunk = jnp.arange(L) // CHUNK
    allowed = (chunk[None, :] <= chunk[:, None]).astype(F32)
    dec = jnp.exp(log_gamma[:, None, None] * dist) * allowed
    lanes = lambda a: jnp.broadcast_to(a[:, :, None], (RET_HEADS, L, RET_HEAD_DIM))
    qw = lanes(jnp.exp(log_gamma[:, None] * (p[None, :] + 1.0)))
    kw = lanes(jnp.exp(log_gamma[:, None] * (L - 1.0 - p[None, :])))
    cd = jnp.broadcast_to(jnp.exp(log_gamma * L)[:, None, None], (RET_HEADS, SUBLANES, RET_HEAD_DIM))
    return dec, qw, kw, cd


def _retention(rq, rk, rv, rg):
    B, S, _ = rq.shape
    L = RET_BLOCK
    dec, qw, kw, cd = _retention_tables()
    tok = pl.BlockSpec((1, L, RET_WIDTH), lambda b, s: (b, s, 0))
    const = lambda a: pl.BlockSpec(a.shape, lambda b, s: (0,) * a.ndim)
    return pl.pallas_call(
        _retention_kernel,
        grid=(B, S // L),
        in_specs=[tok, tok, tok, tok, const(dec), const(qw), const(kw), const(cd)],
        out_specs=tok,
        out_shape=jax.ShapeDtypeStruct((B, S, RET_WIDTH), BF16),
        scratch_shapes=[pltpu.VMEM((RET_HEADS, RET_HEAD_DIM, RET_HEAD_DIM), F32)],
        compiler_params=pltpu.CompilerParams(dimension_semantics=("arbitrary",) * 2),
        name="retention",
    )(rq, rk, rv, rg, dec, qw, kw, cd)


def _fox_attn_kernel(q_ref, k_ref, v_ref, o_ref, s_a, s_b, cm_a, cm_b, m_ref, acc_ref):
    qi = pl.program_id(2)
    T = FOX_TQ
    d = FOX_HEAD_DIM

    def scores(j, s_ref, cm_ref):
        for hh in range(2):
            kj = k_ref[0, hh, pl.ds(pl.multiple_of(j * T, T), T), :]
            st = _dot(kj, q_ref[0, hh])
            s_ref[hh] = st
            cm_ref[hh] = jnp.max(st, axis=0, keepdims=True)

    def consume(j, s_ref, cm_ref, masked):
        for hh in range(2):
            st = s_ref[hh]
            if masked:
                key = lax.broadcasted_iota(jnp.int32, (T, T), 0)
                qry = lax.broadcasted_iota(jnp.int32, (T, T), 1)
                st = jnp.where(key <= qry, st, -jnp.inf)
                cm = jnp.max(st, axis=0, keepdims=True)
            else:
                cm = cm_ref[hh]
            m = m_ref[hh]
            m_new = jnp.maximum(m, cm)
            p = jnp.exp2(st - m_new).astype(BF16)
            vj = v_ref[0, j, hh * V_AUG:(hh + 1) * V_AUG, :]
            acc_ref[hh] = jnp.exp2(m - m_new) * acc_ref[hh] + _dot(vj, p)
            m_ref[hh] = m_new

    m_ref[...] = jnp.full(m_ref.shape, -jnp.inf, F32)
    acc_ref[...] = jnp.zeros(acc_ref.shape, F32)
    scores(0, s_a, cm_a)

    def pair(jj, carry):
        j = 2 * jj
        scores(j + 1, s_b, cm_b)
        consume(j, s_a, cm_a, False)
        scores(j + 2, s_a, cm_a)
        consume(j + 1, s_b, cm_b, False)
        return carry

    lax.fori_loop(0, qi // 2, pair, 0)

    @pl.when(qi % 2 == 0)
    def _():
        consume(qi, s_a, cm_a, True)

    @pl.when(qi % 2 == 1)
    def _():
        scores(qi, s_b, cm_b)
        consume(qi - 1, s_a, cm_a, False)
        consume(qi, s_b, cm_b, True)

    outs = [acc_ref[hh, 0:d, :] / acc_ref[hh, d:d + 1, :] for hh in range(2)]
    o_ref[0] = jnp.concatenate(outs, axis=0).T.astype(BF16)


def _fox_attn(qaug, kaug, fvt):
    B, H, S, _ = kaug.shape
    nk = S // FOX_TK
    return pl.pallas_call(
        _fox_attn_kernel,
        grid=(B, H // 2, S // FOX_TQ),
        in_specs=[
            pl.BlockSpec((1, 2, AUG, FOX_TQ), lambda b, p, q: (b, p, 0, q)),
            pl.BlockSpec((1, 2, S, AUG), lambda b, p, q: (b, p, 0, 0)),
            pl.BlockSpec((1, nk, 2 * V_AUG, FOX_TK), lambda b, p, q: (b, 0, p, 0)),
        ],
        out_specs=pl.BlockSpec((1, FOX_TQ, 2 * FOX_HEAD_DIM), lambda b, p, q: (b, q, p)),
        out_shape=jax.ShapeDtypeStruct((B, S, FOX_WIDTH), BF16),
        scratch_shapes=[
            pltpu.VMEM((2, FOX_TK, FOX_TQ), F32), pltpu.VMEM((2, FOX_TK, FOX_TQ), F32),
            pltpu.VMEM((2, 1, FOX_TQ), F32), pltpu.VMEM((2, 1, FOX_TQ), F32),
            pltpu.VMEM((2, 1, FOX_TQ), F32), pltpu.VMEM((2, V_AUG, FOX_TQ), F32),
        ],
        compiler_params=pltpu.CompilerParams(
            dimension_semantics=("arbitrary",) * 3, vmem_limit_bytes=48 * 1024 * 1024),
        name="fox_attn",
    )(qaug, kaug, fvt)


def _out_router_kernel(x_ref, oret_ref, ofox_ref, wor_ref, wof_ref, g_ref, wrh_ref, wrl_ref, br_ref,
                       h1_ref, u2_ref, sel_ref, cnt_ref):
    TM = x_ref.shape[0]
    h1 = x_ref[...] + _dot(oret_ref[...], wor_ref[...]) + _dot(ofox_ref[...], wof_ref[...])
    h1_ref[...] = h1
    u2 = _rms(h1, g_ref[...])
    uh = u2.astype(BF16)
    u2_ref[...] = uh
    ul = (u2 - uh.astype(F32)).astype(BF16)
    logits = _dot(uh, wrh_ref[...]) + _dot(ul, wrh_ref[...]) + _dot(uh, wrl_ref[...]) + br_ref[...]
    lane = lax.broadcasted_iota(jnp.int32, (TM, LANES), 1).astype(F32)
    l = jnp.where(lane < N_EXPERTS, logits, -jnp.inf)
    picks, vals = [], []
    for _ in range(TOP_K):
        m = jnp.max(l, axis=-1, keepdims=True)
        idx = jnp.min(jnp.where(l == m, lane, float(LANES)), axis=-1, keepdims=True)
        pick = lane == idx
        picks.append(pick)
        vals.append(m)
        l = jnp.where(pick, -jnp.inf, l)
    exps = [jnp.exp(v - vals[0]) for v in vals]
    den = exps[0] + exps[1] + exps[2] + exps[3]
    sel = jnp.full((TM, LANES), -1.0, F32)
    for pick, e in zip(picks, exps):
        sel = jnp.where(pick, e / den, sel)
    sel_ref[...] = sel
    cnt = jnp.sum((sel >= 0.0).astype(F32), axis=0, keepdims=True)
    cnt_ref[0] = jnp.broadcast_to(cnt, (SUBLANES, LANES))


def _out_router(x2, o_ret, o_fox, wor, wof, g, wrh, wrl, br):
    T, D = x2.shape
    TM = MOE_TILE
    nT = T // TM
    const = lambda a: pl.BlockSpec(a.shape, lambda i: (0,) * a.ndim)
    tok = lambda w: pl.BlockSpec((TM, w), lambda i: (i, 0))
    return pl.pallas_call(
        _out_router_kernel,
        grid=(nT,),
        in_specs=[tok(D), tok(RET_WIDTH), tok(FOX_WIDTH), const(wor), const(wof), const(g),
                  const(wrh), const(wrl), const(br)],
        out_specs=(tok(D), tok(D), tok(LANES), pl.BlockSpec((1, SUBLANES, LANES), lambda i: (i, 0, 0))),
        out_shape=(
            jax.ShapeDtypeStruct((T, D), F32),
            jax.ShapeDtypeStruct((T, D), BF16),
            jax.ShapeDtypeStruct((T, LANES), F32),
            jax.ShapeDtypeStruct((nT, SUBLANES, LANES), F32),
        ),
        compiler_params=pltpu.CompilerParams(dimension_semantics=("arbitrary",)),
        name="out_router",
    )(x2, o_ret, o_fox, wor, wof, g, wrh, wrl, br)


def _tile_sort(sel):
    TM = sel.shape[0]
    NS = TOP_K * TM
    maskf = (sel >= 0.0).astype(F32)
    mask = maskf.astype(BF16)
    ri = lax.broadcasted_iota(jnp.int32, (TM, TM), 0)
    ci = lax.broadcasted_iota(jnp.int32, (TM, TM), 1)
    rank = _dot((ri > ci).astype(BF16), mask)
    cnt = jnp.sum(maskf, axis=0, keepdims=True)
    ei = lax.broadcasted_iota(jnp.int32, (LANES, LANES), 0)
    ej = lax.broadcasted_iota(jnp.int32, (LANES, LANES), 1)
    cnt8 = jnp.broadcast_to(cnt, (SUBLANES, LANES)).astype(BF16)
    off = _dot(cnt8, (ei < ej).astype(BF16))[0:1, :]
    slot = lax.broadcasted_iota(jnp.int32, (NS, LANES), 0).astype(F32)
    esel = ((slot >= off) & (slot < off + cnt)).astype(BF16)
    return mask, rank.astype(BF16), esel, off, cnt


def _segment_dmas(step, slot, segdst_ref, cnt_ref, local, remote_rows, sem, to_remote, wait):
    def body(e, off):
        c = cnt_ref[step * N_EXPERTS + e]
        dst = segdst_ref[step * N_EXPERTS + e]
        bit = MOE_TILE
        while bit >= 1:
            done = c & (~(2 * bit - 1))

            @pl.when((c & bit) != 0)
            def _(bit=bit, done=done):
                loc = local.at[slot, pl.ds((off + done) * ROW_TILES, bit * ROW_TILES), :]
                rem = remote_rows.at[pl.ds((dst + done) * ROW_TILES, bit * ROW_TILES), :]
                cp = (pltpu.make_async_copy(loc, rem, sem.at[slot]) if to_remote
                      else pltpu.make_async_copy(rem, loc, sem.at[slot]))
                cp.wait() if wait else cp.start()
            bit //= 2
        return off + c

    lax.fori_loop(0, N_EXPERTS, body, 0)


def _dispatch_kernel(segdst_ref, cnt_ref, paddst_ref, padcnt_ref, nused_ref, u2_ref, sel_ref, xs_ref,
                     buf, zbuf, sems, zsem):
    i = pl.program_id(0)
    last = pl.num_programs(0) - 1
    slot = i % 2
    TM = MOE_TILE
    NS = TOP_K * TM
    mask, rank, esel, off, _ = _tile_sort(sel_ref[...])
    slot_id = lax.broadcasted_iota(jnp.int32, (NS, 1), 0).astype(F32)
    r_s = slot_id - jnp.sum(esel.astype(F32) * off, axis=1, keepdims=True)
    perm = ((_dot_nt(esel, mask) > 0.5) & (_dot_nt(esel, rank) == r_s)).astype(BF16)

    @pl.when(i >= 2)
    def _():
        _segment_dmas(i - 2, slot, segdst_ref, cnt_ref, buf, xs_ref, sems, True, True)

    u2 = u2_ref[...]
    for c in range(NS // TM):
        rows = _dot(perm[c * TM:(c + 1) * TM], u2)
        for j in range(ROW_TILES):
            buf[slot, pl.ds(c * TM * ROW_TILES + j, TM, stride=ROW_TILES), :] = rows[:, j * LANES:(j + 1) * LANES]
    _segment_dmas(i, slot, segdst_ref, cnt_ref, buf, xs_ref, sems, True, False)

    @pl.when(i == last)
    def _():
        @pl.when(i >= 1)
        def _():
            _segment_dmas(i - 1, 1 - slot, segdst_ref, cnt_ref, buf, xs_ref, sems, True, True)
        _segment_dmas(i, slot, segdst_ref, cnt_ref, buf, xs_ref, sems, True, True)
        zbuf[...] = jnp.zeros_like(zbuf)
        half = EXPERT_BLOCK // 2 * ROW_TILES
        n_blocks = xs_ref.shape[0] // (EXPERT_BLOCK * ROW_TILES)
        for wait in (False, True):
            def unused(hb, carry, wait=wait):
                cp = pltpu.make_async_copy(zbuf, xs_ref.at[pl.ds(hb * half, half), :], zsem.at[0])
                cp.wait() if wait else cp.start()
                return carry
            lax.fori_loop(2 * nused_ref[0], 2 * n_blocks, unused, 0)


            def body(e, carry, wait=wait):
                c = padcnt_ref[e]
                dst = paddst_ref[e]
                bit = EXPERT_BLOCK // 2
                while bit >= 1:
                    done = c & (~(2 * bit - 1))

                    @pl.when((c & bit) != 0)
                    def _(bit=bit, done=done):
                        cp = pltpu.make_async_copy(
                            zbuf.at[pl.ds(0, bit * ROW_TILES), :],
                            xs_ref.at[pl.ds((dst + done) * ROW_TILES, bit * ROW_TILES), :], zsem.at[0])
                        cp.wait() if wait else cp.start()
                    bit //= 2
                return carry
            lax.fori_loop(0, N_EXPERTS, body, 0)


def _dispatch(u2, sel, segdst, cnt, paddst, padcnt, n_used, n_rows):
    T, D = u2.shape
    TM = MOE_TILE
    NS = TOP_K * TM
    return pl.pallas_call(
        _dispatch_kernel,
        grid_spec=pltpu.PrefetchScalarGridSpec(
            num_scalar_prefetch=5,
            grid=(T // TM,),
            in_specs=[pl.BlockSpec((TM, D), lambda i, *_: (i, 0)),
                      pl.BlockSpec((TM, LANES), lambda i, *_: (i, 0))],
            out_specs=pl.BlockSpec(memory_space=pl.ANY),
            scratch_shapes=[pltpu.VMEM((2, NS * ROW_TILES, LANES), F32),
                            pltpu.VMEM((EXPERT_BLOCK // 2 * ROW_TILES, LANES), F32),
                            pltpu.SemaphoreType.DMA((2,)), pltpu.SemaphoreType.DMA((1,))],
        ),
        out_shape=jax.ShapeDtypeStruct((n_rows * ROW_TILES, LANES), F32),
        compiler_params=pltpu.CompilerParams(
            dimension_semantics=("arbitrary",), vmem_limit_bytes=48 * 1024 * 1024),
        name="dispatch",
    )(segdst, cnt, paddst, padcnt, n_used, u2, sel)


def _expert_kernel(bexp_ref, nused_ref, xs_ref, w1_ref, b1_ref, w2_ref, b2_ref, ys_ref, w1b, w2b):
    b = pl.program_id(0)
    BLK = EXPERT_BLOCK
    used = b < nused_ref[0]

    @pl.when(used)
    def _():
        e = bexp_ref[b]
        prev = bexp_ref[jnp.maximum(b - 1, 0)]

        @pl.when((b == 0) | (e != prev))
        def _():
            rows = 128

            def cast(r, carry):
                sl = pl.ds(pl.multiple_of(r * rows, rows), rows)
                w1b[sl, :] = w1_ref[0, sl, :].astype(BF16)
                w2b[sl, :] = w2_ref[0, sl, :].astype(BF16)
                return carry
            lax.fori_loop(0, D_MODEL // rows, cast, 0)

        R = EXPERT_PASS_ROWS
        for rp in range(BLK // R):
            r0 = rp * R * ROW_TILES
            x = jnp.concatenate([xs_ref[pl.ds(r0 + j, R, stride=ROW_TILES), :] for j in range(ROW_TILES)],
                                axis=1).astype(BF16)
            h = _dot(x, w1b[...]) + b1_ref[0]
            glu = jnp.minimum(h[:, :D_FF], SWIGLU_LIMIT)
            lin = jnp.clip(h[:, D_FF:], -SWIGLU_LIMIT, SWIGLU_LIMIT)
            act = glu * jax.nn.sigmoid(SWIGLU_ALPHA * glu) * (lin + 1.0)
            y = _dot(act.astype(BF16), w2b[...]) + b2_ref[0]
            for j in range(ROW_TILES):
                ys_ref[pl.ds(r0 + j, R, stride=ROW_TILES), :] = y[:, j * LANES:(j + 1) * LANES]

    @pl.when(jnp.logical_not(used))
    def _():
        ys_ref[...] = jnp.zeros_like(ys_ref)


def _experts(xs, block_exp, n_used, w1, b1, w2, b2):
    BLK = EXPERT_BLOCK
    NB = xs.shape[0] // (BLK * ROW_TILES)
    blk = lambda b, nused: jnp.minimum(b, nused[0] - 1)
    return pl.pallas_call(
        _expert_kernel,
        grid_spec=pltpu.PrefetchScalarGridSpec(
            num_scalar_prefetch=2,
            grid=(NB,),
            in_specs=[
                pl.BlockSpec((BLK * ROW_TILES, LANES), lambda b, bexp, nused: (blk(b, nused), 0)),
                pl.BlockSpec((1, D_MODEL, 2 * D_FF), lambda b, bexp, nused: (bexp[blk(b, nused)], 0, 0)),
                pl.BlockSpec((1, 1, 2 * D_FF), lambda b, bexp, nused: (bexp[blk(b, nused)], 0, 0)),
                pl.BlockSpec((1, D_FF, D_MODEL), lambda b, bexp, nused: (bexp[blk(b, nused)], 0, 0)),
                pl.BlockSpec((1, 1, D_MODEL), lambda b, bexp, nused: (bexp[blk(b, nused)], 0, 0)),
            ],
            out_specs=pl.BlockSpec((BLK * ROW_TILES, LANES), lambda b, bexp, nused: (b, 0)),
            scratch_shapes=[pltpu.VMEM((D_MODEL, 2 * D_FF), BF16), pltpu.VMEM((D_FF, D_MODEL), BF16)],
        ),
        out_shape=jax.ShapeDtypeStruct(xs.shape, F32),
        compiler_params=pltpu.CompilerParams(
            dimension_semantics=("arbitrary",), vmem_limit_bytes=56 * 1024 * 1024),
        name="experts",
    )(block_exp, n_used, xs, w1, b1[:, None, :], w2, b2[:, None, :])


def _combine_kernel(segdst_ref, cnt_ref, ys_ref, sel_ref, h1_ref, g_ref, out_ref, buf, sems):
    i = pl.program_id(0)
    n = pl.num_programs(0)
    slot = i % 2
    TM = MOE_TILE
    NS = TOP_K * TM

    @pl.when(i == 0)
    def _():
        _segment_dmas(i, slot, segdst_ref, cnt_ref, buf, ys_ref, sems, False, False)

    @pl.when(i + 1 < n)
    def _():
        _segment_dmas(i + 1, 1 - slot, segdst_ref, cnt_ref, buf, ys_ref, sems, False, False)

    sel = sel_ref[...]
    mask, rank, esel, off, _ = _tile_sort(sel)
    gate = jnp.maximum(sel, 0.0)
    gh = gate.astype(BF16)
    gl = (gate - gh.astype(F32)).astype(BF16)
    o3 = _split3(jnp.broadcast_to(off, (SUBLANES, LANES)))
    off_s = (_dot_nt(o3[0], esel) + _dot_nt(o3[1], esel) + _dot_nt(o3[2], esel))[0:1, :]
    r_s = lax.broadcasted_iota(jnp.int32, (1, NS), 1).astype(F32) - off_s
    hit = (_dot_nt(mask, esel) > 0.5) & (_dot_nt(rank, esel) == r_s)
    unperm = jnp.where(hit, _dot_nt(gh, esel) + _dot_nt(gl, esel), 0.0).astype(BF16)

    _segment_dmas(i, slot, segdst_ref, cnt_ref, buf, ys_ref, sems, False, True)
    y = jnp.concatenate([buf[slot, pl.ds(j, NS, stride=ROW_TILES), :] for j in range(ROW_TILES)],
                        axis=1).astype(BF16)
    h2 = h1_ref[...] + _dot(unperm, y)
    out_ref[...] = _rms(h2, g_ref[...])


def _combine(ys, sel, h1, g, segdst, cnt):
    T, D = h1.shape
    TM = MOE_TILE
    NS = TOP_K * TM
    return pl.pallas_call(
        _combine_kernel,
        grid_spec=pltpu.PrefetchScalarGridSpec(
            num_scalar_prefetch=2,
            grid=(T // TM,),
            in_specs=[pl.BlockSpec(memory_space=pl.ANY),
                      pl.BlockSpec((TM, LANES), lambda i, *_: (i, 0)),
                      pl.BlockSpec((TM, D), lambda i, *_: (i, 0)),
                      pl.BlockSpec((1, D), lambda i, *_: (0, 0))],
            out_specs=pl.BlockSpec((TM, D), lambda i, *_: (i, 0)),
            scratch_shapes=[pltpu.VMEM((2, NS * ROW_TILES, LANES), F32), pltpu.SemaphoreType.DMA((2,))],
        ),
        out_shape=jax.ShapeDtypeStruct((T, D), F32),
        compiler_params=pltpu.CompilerParams(
            dimension_semantics=("arbitrary",), vmem_limit_bytes=48 * 1024 * 1024),
        name="combine",
    )(segdst, cnt, ys, sel, h1, g)


def _routing_tables(cnt_tiles):
    BLK = EXPERT_BLOCK
    nT = cnt_tiles.shape[0]
    A = nT * MOE_TILE * TOP_K
    NB = A // BLK + N_EXPERTS
    total = jnp.sum(cnt_tiles, axis=0)
    padded = (total + BLK - 1) // BLK * BLK
    pad_ends = jnp.cumsum(padded)
    pad_starts = pad_ends - padded
    before = jnp.cumsum(cnt_tiles, axis=0) - cnt_tiles
    segdst = (pad_starts[None, :] + before).reshape(-1).astype(jnp.int32)
    block_start = jnp.arange(NB, dtype=jnp.int32) * BLK
    block_exp = jnp.minimum(jnp.sum(pad_ends[None, :] <= block_start[:, None], axis=1), N_EXPERTS - 1).astype(jnp.int32)
    n_used = (pad_ends[-1] // BLK).astype(jnp.int32).reshape(1)
    paddst = (pad_starts + total).astype(jnp.int32)
    padcnt = (padded - total).astype(jnp.int32)
    return segdst, cnt_tiles.reshape(-1).astype(jnp.int32), paddst, padcnt, block_exp, n_used, NB * BLK


def _rotary_tables(S):
    half = RET_HEAD_DIM // 2
    inv_freq = ROPE_BASE ** (-jnp.arange(half, dtype=F32) / half)
    ang = jnp.arange(S, dtype=F32)[:, None] * inv_freq[None, :]
    cos, sin = jnp.cos(ang), jnp.sin(ang)
    return jnp.concatenate([cos, cos], axis=-1), jnp.concatenate([-sin, sin], axis=-1)


def _layer(h, norm_mix_g, w_in, b_forget, w_out, norm_ffn_g, w_router, b_router,
           w_exp_in, b_exp_in, w_exp_out, b_exp_out, final_g):
    B, S, D = h.shape
    R, Fw = RET_WIDTH, FOX_WIDTH
    cos, sin = _rotary_tables(S)
    wb = w_in.astype(BF16)
    wr = wb[:, :4 * R]
    wfq, wfk, wfv = (wb[:, 4 * R + i * Fw:4 * R + (i + 1) * Fw] for i in range(3))
    wz = jnp.zeros((D, LANES), BF16).at[:, :FOX_HEADS].set(wb[:, 4 * R + 3 * Fw:])
    wzt = jnp.zeros((16, D), BF16).at[:FOX_HEADS, :].set(wb[:, 4 * R + 3 * Fw:].T)
    wfvt = jnp.pad(wfv.T.reshape(FOX_HEADS, FOX_HEAD_DIM, D), ((0, 0), (0, V_AUG - FOX_HEAD_DIM), (0, 0)))
    vone = jnp.zeros((FOX_HEADS, V_AUG, 1), F32).at[:, FOX_HEAD_DIM, 0].set(1.0)
    rq, rk, rv, rg, fk, fqt, fvt, z, zt = _in_proj(
        h, norm_mix_g[None, :], cos, sin, wr, wfk, wfq.T, wfvt.reshape(FOX_HEADS * V_AUG, D),
        vone.reshape(FOX_HEADS * V_AUG, 1), wz, wzt)
    kaug, qaug = _fox_prep(z, zt, fk, fqt, b_forget)
    o_ret = _retention(rq, rk, rv, rg)
    o_fox = _fox_attn(qaug, kaug, fvt)

    T = B * S
    wo = w_out.astype(BF16)
    wrt = jnp.zeros((D, LANES), F32).at[:, :N_EXPERTS].set(w_router)
    wrh = wrt.astype(BF16)
    wrl = (wrt - wrh.astype(F32)).astype(BF16)
    br = jnp.zeros((1, LANES), F32).at[0, :N_EXPERTS].set(b_router)
    h1, u2, sel, cnt = _out_router(h.reshape(T, D), o_ret.reshape(T, R), o_fox.reshape(T, Fw),
                                   wo[:R], wo[R:], norm_ffn_g[None, :], wrh, wrl, br)
    cnt_tiles = cnt[:, 0, :N_EXPERTS].astype(jnp.int32)
    segdst, cnt_flat, paddst, padcnt, block_exp, n_used, n_rows = _routing_tables(cnt_tiles)
    xs = _dispatch(u2, sel, segdst, cnt_flat, paddst, padcnt, n_used, n_rows)
    ys = _experts(xs, block_exp, n_used, w_exp_in, b_exp_in, w_exp_out, b_exp_out)
    out = _combine(ys, sel, h1, final_g[None, :], segdst, cnt_flat)
    return out.reshape(B, S, D)


def kernel(x, norm_mix_g, w_in, b_forget, w_out, norm_ffn_g, w_router, b_router,
           w_exp_in, b_exp_in, w_exp_out, b_exp_out, norm_final_g):
    depth = w_in.shape[0]
    assert depth == 1, "the fused final RMSNorm assumes a single layer"
    return _layer(x, norm_mix_g[0], w_in[0], b_forget[0], w_out[0], norm_ffn_g[0], w_router[0], b_router[0],
                  w_exp_in[0], b_exp_in[0], w_exp_out[0], b_exp_out[0], norm_final_g)
```

```python
import functools

import numpy as np
import jax
import jax.numpy as jnp
from jax import lax
from jax.experimental import pallas as pl
from jax.experimental.pallas import tpu as pltpu

F32 = jnp.float32
BF16 = jnp.bfloat16

D_MODEL = 1024
RET_HEADS, RET_HEAD_DIM = 4, 128
RET_WIDTH = RET_HEADS * RET_HEAD_DIM
FOX_HEADS, FOX_HEAD_DIM = 8, 64
FOX_WIDTH = FOX_HEADS * FOX_HEAD_DIM
CHUNK = 64
ROPE_BASE = 10000.0
N_EXPERTS = 32
TOP_K = 4
D_FF = D_MODEL
SWIGLU_ALPHA = 1.702
SWIGLU_LIMIT = 7.0
RMS_EPS = 1e-5
GN_EPS = 1e-5

LANES = 128
SUBLANES = 8
ROW_TILES = D_MODEL // LANES

PROJ_TILE = 512
RET_BLOCK = 256
FOX_TQ = 512
FOX_TK = PROJ_TILE
FOX_SUB = 256
AUG = 128
V_AUG = 80
LOG2E = 1.4426950408889634
MOE_TILE = 256
ROUTER_TILES = 4
EXPERT_BLOCK = 512
EXPERT_PASS_ROWS = 256

NT_DIMS = (((1,), (1,)), ((), ()))


def _split3(a):
    hi = a.astype(BF16)
    r1 = a - hi.astype(F32)
    mid = r1.astype(BF16)
    lo = (r1 - mid.astype(F32)).astype(BF16)
    return hi, mid, lo


def _dot(a, b):
    return jnp.dot(a, b, preferred_element_type=F32)


def _dot_nt(a, b):
    return lax.dot_general(a, b, NT_DIMS, preferred_element_type=F32)


def _rms(x, g):
    return x * lax.rsqrt(jnp.mean(x * x, axis=-1, keepdims=True) + RMS_EPS) * g


def _in_proj_kernel(x_ref, g_ref, cos_ref, sin_ref, wr_ref, wfk_ref, wfqt_ref, wfvt_ref, vone_ref, wz_ref, wzt_ref,
                    rq_ref, rk_ref, rv_ref, rg_ref, fk_ref, fqt_ref, fvt_ref, z_ref, zt_ref):
    u = _rms(x_ref[0], g_ref[...]).astype(BF16)
    r = _dot(u, wr_ref[...])
    cos, sin = cos_ref[...], sin_ref[...]
    k_scale = RET_HEAD_DIM ** -0.5
    for h in range(RET_HEADS):
        lo = h * RET_HEAD_DIM
        q = r[:, lo:lo + RET_HEAD_DIM]
        k = r[:, RET_WIDTH + lo:RET_WIDTH + lo + RET_HEAD_DIM]
        rq_ref[0, :, lo:lo + RET_HEAD_DIM] = (q * cos + pltpu.roll(q, RET_HEAD_DIM // 2, 1) * sin).astype(BF16)
        rk_ref[0, :, lo:lo + RET_HEAD_DIM] = (
            (k * cos + pltpu.roll(k, RET_HEAD_DIM // 2, 1) * sin) * k_scale).astype(BF16)
    rv_ref[0] = r[:, 2 * RET_WIDTH:3 * RET_WIDTH].astype(BF16)
    rg_ref[0] = r[:, 3 * RET_WIDTH:4 * RET_WIDTH].astype(BF16)
    fk_ref[0] = _dot(u, wfk_ref[...]).astype(BF16)
    q_scale = FOX_HEAD_DIM ** -0.5 * LOG2E
    fqt_ref[0] = (_dot_nt(wfqt_ref[...], u) * q_scale).astype(BF16)
    fvt_ref[0, 0] = (_dot_nt(wfvt_ref[...], u) + vone_ref[...]).astype(BF16)
    z_ref[0] = _dot(u, wz_ref[...])
    zt_ref[0] = _dot_nt(wzt_ref[...], u)


def _in_proj(x, g, cos, sin, wr, wfk, wfqt, wfvt, vone, wz, wzt):
    B, S, D = x.shape
    TM = PROJ_TILE
    ns = S // TM
    const = lambda shape: pl.BlockSpec(shape, lambda b, s: (0,) * len(shape))
    tok = lambda w: pl.BlockSpec((1, TM, w), lambda b, s: (b, s, 0))
    out_shape = (
        jax.ShapeDtypeStruct((B, S, RET_WIDTH), BF16),
        jax.ShapeDtypeStruct((B, S, RET_WIDTH), BF16),
        jax.ShapeDtypeStruct((B, S, RET_WIDTH), BF16),
        jax.ShapeDtypeStruct((B, S, RET_WIDTH), BF16),
        jax.ShapeDtypeStruct((B, S, FOX_WIDTH), BF16),
        jax.ShapeDtypeStruct((B, FOX_WIDTH, S), BF16),
        jax.ShapeDtypeStruct((B, ns, FOX_HEADS * V_AUG, TM), BF16),
        jax.ShapeDtypeStruct((B, S, LANES), F32),
        jax.ShapeDtypeStruct((B, 16, S), F32),
    )
    return pl.pallas_call(
        _in_proj_kernel,
        grid=(B, ns),
        in_specs=[
            pl.BlockSpec((1, TM, D), lambda b, s: (b, s, 0)),
            const((1, D)),
            pl.BlockSpec((TM, RET_HEAD_DIM), lambda b, s: (s, 0)),
            pl.BlockSpec((TM, RET_HEAD_DIM), lambda b, s: (s, 0)),
            const(wr.shape), const(wfk.shape), const(wfqt.shape), const(wfvt.shape), const(vone.shape),
            const(wz.shape), const(wzt.shape),
        ],
        out_specs=(
            tok(RET_WIDTH), tok(RET_WIDTH), tok(RET_WIDTH), tok(RET_WIDTH), tok(FOX_WIDTH),
            pl.BlockSpec((1, FOX_WIDTH, TM), lambda b, s: (b, 0, s)),
            pl.BlockSpec((1, 1, FOX_HEADS * V_AUG, TM), lambda b, s: (b, s, 0, 0)),
            tok(LANES),
            pl.BlockSpec((1, 16, TM), lambda b, s: (b, 0, s)),
        ),
        out_shape=out_shape,
        compiler_params=pltpu.CompilerParams(
            dimension_semantics=("arbitrary", "arbitrary"), vmem_limit_bytes=48 * 1024 * 1024),
        name="in_proj",
    )(x, g, cos, sin, wr, wfk, wfqt, wfvt, vone, wz, wzt)


def _fox_prep_kernel(z_ref, zt_ref, fk_ref, fqt_ref, brow_ref, bcol_ref,
                     selkf_ref, constk_ref, selqf_ref, constq_ref, kaug_ref, qaug_ref, crow, ccol):
    TS = z_ref.shape[1]
    d = FOX_HEAD_DIM
    H = FOX_HEADS

    @pl.when(pl.program_id(1) == 0)
    def _():
        crow[...] = jnp.zeros_like(crow)
        ccol[...] = jnp.zeros_like(ccol)

    ri = lax.broadcasted_iota(jnp.int32, (TS, TS), 0)
    ci = lax.broadcasted_iota(jnp.int32, (TS, TS), 1)

    lane = lax.broadcasted_iota(jnp.int32, (TS, LANES), 1)
    lf = jnp.where(lane < H, jax.nn.log_sigmoid(z_ref[0] + brow_ref[...]), 0.0)
    ltri = (ri >= ci).astype(BF16)
    h3 = _split3(lf)
    f_row = _dot(ltri, h3[0]) + _dot(ltri, h3[1]) + _dot(ltri, h3[2]) + crow[...]
    crow[...] = f_row[TS - 1:TS, :]
    n3 = _split3(f_row * -LOG2E)
    pieces = (n3[0].astype(F32) + pltpu.roll(n3[1].astype(F32), H, 1)
              + pltpu.roll(n3[2].astype(F32), 2 * H, 1)).astype(BF16)
    for g in range(H // 2):
        bias = _dot(pieces, selkf_ref[g])
        kg = fk_ref[0, :, g * 2 * d:(g + 1) * 2 * d].astype(F32)
        for o in range(2):
            kh = kg if o == 0 else pltpu.roll(kg, d, 1)
            extra = bias[:, o * AUG:(o + 1) * AUG] + constk_ref[...]
            kaug_ref[0, 2 * g + o] = jnp.where(lane < d, kh, extra).astype(BF16)

    row = lax.broadcasted_iota(jnp.int32, (16, TS), 0)
    lft = jnp.where(row < H, jax.nn.log_sigmoid(zt_ref[0] + bcol_ref[...]), 0.0)
    utri = (ri <= ci).astype(BF16)
    t3 = _split3(lft)
    f_col = _dot(t3[0], utri) + _dot(t3[1], utri) + _dot(t3[2], utri) + ccol[:, 0:1]
    ccol[...] = jnp.broadcast_to(f_col[:, TS - 1:TS], ccol.shape)
    pieces_t = jnp.concatenate(_split3(f_col * LOG2E), axis=0)
    for h in range(H):
        extra = _dot(selqf_ref[h], pieces_t) + constq_ref[...]
        qaug_ref[0, h, 0] = jnp.concatenate([fqt_ref[0, h * d:(h + 1) * d, :], extra.astype(BF16)], axis=0)


def _fox_prep_constants():
    d, H = FOX_HEAD_DIM, FOX_HEADS
    selkf = np.zeros((H // 2, LANES, 2 * AUG), np.float32)
    constk = np.zeros((1, AUG), np.float32)
    selqf = np.zeros((H, d, 48), np.float32)
    constq = np.zeros((d, 1), np.float32)
    for p in range(3):
        constk[0, d + p] = 1.0
        constq[3 + p, 0] = 1.0
        for h in range(H):
            selkf[h // 2, p * H + h, (h % 2) * AUG + d + 3 + p] = 1.0
            selqf[h, p, p * 16 + h] = 1.0
    return jnp.asarray(selkf, BF16), jnp.asarray(constk, F32), jnp.asarray(selqf, BF16), jnp.asarray(constq, F32)


def _fox_prep(z, zt, fk, fqt, b_forget):
    B, S, _ = z.shape
    TS = PROJ_TILE
    ns = S // TS
    selkf, constk, selqf, constq = _fox_prep_constants()
    brow = jnp.zeros((1, LANES), F32).at[0, :FOX_HEADS].set(b_forget)
    bcol = jnp.zeros((16, 1), F32).at[:FOX_HEADS, 0].set(b_forget)
    const = lambda a: pl.BlockSpec(a.shape, lambda b, s: (0,) * a.ndim)
    return pl.pallas_call(
        _fox_prep_kernel,
        grid=(B, ns),
        in_specs=[
            pl.BlockSpec((1, TS, LANES), lambda b, s: (b, s, 0)),
            pl.BlockSpec((1, 16, TS), lambda b, s: (b, 0, s)),
            pl.BlockSpec((1, TS, FOX_WIDTH), lambda b, s: (b, s, 0)),
            pl.BlockSpec((1, FOX_WIDTH, TS), lambda b, s: (b, 0, s)),
            const(brow), const(bcol), const(selkf), const(constk), const(selqf), const(constq),
        ],
        out_specs=(
            pl.BlockSpec((1, FOX_HEADS, TS, AUG), lambda b, s: (b, 0, s, 0)),
            pl.BlockSpec((1, FOX_HEADS, 1, AUG, TS), lambda b, s: (b, 0, s, 0, 0)),
        ),
        out_shape=(
            jax.ShapeDtypeStruct((B, FOX_HEADS, S, AUG), BF16),
            jax.ShapeDtypeStruct((B, FOX_HEADS, ns, AUG, TS), BF16),
        ),
        scratch_shapes=[pltpu.VMEM((1, LANES), F32), pltpu.VMEM((16, LANES), F32)],
        compiler_params=pltpu.CompilerParams(
            dimension_semantics=("arbitrary", "arbitrary"), vmem_limit_bytes=48 * 1024 * 1024),
        name="fox_prep",
    )(z, zt, fk, fqt, brow, bcol, selkf, constk, selqf, constq)


def _retention_kernel(q_ref, k_ref, v_ref, g_ref, dec_ref, qw_ref, kw_ref, cd_ref, o_ref, state):
    @pl.when(pl.program_id(1) == 0)
    def _():
        state[...] = jnp.zeros_like(state)

    for h in range(RET_HEADS):
        hs = slice(h * RET_HEAD_DIM, (h + 1) * RET_HEAD_DIM)
        q, k, v = q_ref[0, :, hs], k_ref[0, :, hs], v_ref[0, :, hs]
        scores = (_dot_nt(q, k) * dec_ref[h]).astype(BF16)
        st = state[h]
        o = _dot(scores, v) + _dot((q.astype(F32) * qw_ref[h]).astype(BF16), st.astype(BF16))
        kk = k.astype(F32) * kw_ref[h]
        state[h] = st * cd_ref[h, 0:1, :] + _dot(kk.T.astype(BF16), v)
        mu = jnp.mean(o, axis=-1, keepdims=True)
        oc = o - mu
        var = jnp.mean(oc * oc, axis=-1, keepdims=True)
        o_ref[0, :, hs] = (oc * lax.rsqrt(var + GN_EPS) * jax.nn.silu(g_ref[0, :, hs].astype(F32))).astype(BF16)


def _retention_tables():
    L = RET_BLOCK
    log_gamma = jnp.log1p(-jnp.exp2(-5.0 - jnp.arange(RET_HEADS, dtype=F32)))
    p = jnp.arange(L, dtype=F32)
    dist = jnp.abs(p[:, None] - p[None, :])
    chunk = jnp.arange(L) // CHUNK
    allowed = (chunk[None, :] <= chunk[:, None]).astype(F32)
    dec = jnp.exp(log_gamma[:, None, None] * dist) * allowed
    lanes = lambda a: jnp.broadcast_to(a[:, :, None], (RET_HEADS, L, RET_HEAD_DIM))
    qw = lanes(jnp.exp(log_gamma[:, None] * (p[None, :] + 1.0)))
    kw = lanes(jnp.exp(log_gamma[:, None] * (L - 1.0 - p[None, :])))
    cd = jnp.broadcast_to(jnp.exp(log_gamma * L)[:, None, None], (RET_HEADS, SUBLANES, RET_HEAD_DIM))
    return dec, qw, kw, cd


def _retention(rq, rk, rv, rg):
    B, S, _ = rq.shape
    L = RET_BLOCK
    dec, qw, kw, cd = _retention_tables()
    tok = pl.BlockSpec((1, L, RET_WIDTH), lambda b, s: (b, s, 0))
    const = lambda a: pl.BlockSpec(a.shape, lambda b, s: (0,) * a.ndim)
    return pl.pallas_call(
        _retention_kernel,
        grid=(B, S // L),
        in_specs=[tok, tok, tok, tok, const(dec), const(qw), const(kw), const(cd)],
        out_specs=tok,
        out_shape=jax.ShapeDtypeStruct((B, S, RET_WIDTH), BF16),
        scratch_shapes=[pltpu.VMEM((RET_HEADS, RET_HEAD_DIM, RET_HEAD_DIM), F32)],
        compiler_params=pltpu.CompilerParams(dimension_semantics=("arbitrary",) * 2),
        name="retention",
    )(rq, rk, rv, rg, dec, qw, kw, cd)


def _fox_attn_kernel(q_ref, k_ref, v_ref, o_ref, s_a, s_b, s_c, cm_a, cm_b, cm_c, m_ref, acc_ref):
    T = FOX_TQ
    d = FOX_HEAD_DIM
    nq = q_ref.shape[2]

    def scores(qi, j, s_ref, cm_ref):
        for hh in range(2):
            kj = k_ref[0, hh, pl.ds(pl.multiple_of(j * T, T), T), :]
            st = _dot(kj, q_ref[0, hh, qi])
            s_ref[hh] = st
            cm_ref[hh] = jnp.max(st, axis=0, keepdims=True)

    def consume(j, s_ref, cm_ref, masked):
        for hh in range(2):
            st = s_ref[hh]
            if masked:
                key = lax.broadcasted_iota(jnp.int32, (T, T), 0)
                qry = lax.broadcasted_iota(jnp.int32, (T, T), 1)
                st = jnp.where(key <= qry, st, -jnp.inf)
                cm = jnp.max(st, axis=0, keepdims=True)
            else:
                cm = cm_ref[hh]
            m = m_ref[hh]
            m_new = jnp.maximum(m, cm)
            p = jnp.exp2(st - m_new).astype(BF16)
            vj = v_ref[0, j, hh * V_AUG:(hh + 1) * V_AUG, :]
            acc_ref[hh] = jnp.exp2(m - m_new) * acc_ref[hh] + _dot(vj, p)
            m_ref[hh] = m_new

    def reset():
        m_ref[...] = jnp.full(m_ref.shape, -jnp.inf, F32)
        acc_ref[...] = jnp.zeros(acc_ref.shape, F32)

    def prefetch_next(qi):
        @pl.when(qi + 1 < nq)
        def _():
            scores(qi + 1, 0, s_c, cm_c)

    def finish(qi):
        outs = [acc_ref[hh, 0:d, :] / acc_ref[hh, d:d + 1, :] for hh in range(2)]
        o_ref[0, pl.ds(pl.multiple_of(qi * T, T), T), :] = jnp.concatenate(outs, axis=0).T.astype(BF16)

    reset()
    scores(0, 0, s_a, cm_a)
    prefetch_next(0)
    consume(0, s_a, cm_a, True)
    finish(0)

    def query_tile(qi, carry):
        reset()
        scores(qi, 1, s_a, cm_a)
        consume(0, s_c, cm_c, False)

        def pair(jj, c):
            j = 1 + 2 * jj
            scores(qi, j + 1, s_b, cm_b)
            consume(j, s_a, cm_a, False)
            scores(qi, j + 2, s_a, cm_a)
            consume(j + 1, s_b, cm_b, False)
            return c

        lax.fori_loop(0, (qi - 1) // 2, pair, 0)

        @pl.when(qi % 2 == 1)
        def _():
            prefetch_next(qi)
            consume(qi, s_a, cm_a, True)

        @pl.when(qi % 2 == 0)
        def _():
            scores(qi, qi, s_b, cm_b)
            consume(qi - 1, s_a, cm_a, False)
            prefetch_next(qi)
            consume(qi, s_b, cm_b, True)

        finish(qi)
        return carry

    lax.fori_loop(1, nq, query_tile, 0)


def _fox_attn(qaug, kaug, fvt):
    B, H, S, _ = kaug.shape
    nk = S // FOX_TK
    nq = S // FOX_TQ
    score_buf = pltpu.VMEM((2, FOX_TK, FOX_TQ), F32)
    col_max = pltpu.VMEM((2, 1, FOX_TQ), F32)
    return pl.pallas_call(
        _fox_attn_kernel,
        grid=(B, H // 2),
        in_specs=[
            pl.BlockSpec((1, 2, nq, AUG, FOX_TQ), lambda b, p: (b, p, 0, 0, 0)),
            pl.BlockSpec((1, 2, S, AUG), lambda b, p: (b, p, 0, 0)),
            pl.BlockSpec((1, nk, 2 * V_AUG, FOX_TK), lambda b, p: (b, 0, p, 0)),
        ],
        out_specs=pl.BlockSpec((1, S, 2 * FOX_HEAD_DIM), lambda b, p: (b, 0, p)),
        out_shape=jax.ShapeDtypeStruct((B, S, FOX_WIDTH), BF16),
        scratch_shapes=[
            score_buf, score_buf, score_buf, col_max, col_max, col_max,
            pltpu.VMEM((2, 1, FOX_TQ), F32), pltpu.VMEM((2, V_AUG, FOX_TQ), F32),
        ],
        compiler_params=pltpu.CompilerParams(
            dimension_semantics=("arbitrary",) * 2, vmem_limit_bytes=48 * 1024 * 1024),
        name="fox_attn",
    )(qaug, kaug, fvt)


def _out_router_kernel(x_ref, oret_ref, ofox_ref, wor_ref, wof_ref, g_ref, wrh_ref, wrl_ref, br_ref,
                       h1_ref, u2_ref, sel_ref, cnt_ref):
    TM = MOE_TILE
    rows = lambda t: slice(t * TM, (t + 1) * TM)

    def out_proj(t):
        rs = rows(t)
        h1 = x_ref[rs] + _dot(oret_ref[rs], wor_ref[...]) + _dot(ofox_ref[rs], wof_ref[...])
        h1_ref[rs] = h1
        return h1

    def router_logits(t, h1):
        u2 = _rms(h1, g_ref[...])
        uh = u2.astype(BF16)
        u2_ref[rows(t)] = uh
        ul = (u2 - uh.astype(F32)).astype(BF16)
        return (_dot_nt(wrh_ref[...], uh) + _dot_nt(wrh_ref[...], ul) + _dot_nt(wrl_ref[...], uh)
                + br_ref[...])

    def top_k(t, logits):
        rs = rows(t)
        row = lax.broadcasted_iota(jnp.int32, (LANES, TM), 0).astype(F32)
        l = jnp.where(row < N_EXPERTS, logits, -jnp.inf)
        picks, vals = [], []
        for _ in range(TOP_K):
            m = jnp.max(l, axis=0, keepdims=True)
            idx = jnp.min(jnp.where(l == m, row, float(LANES)), axis=0, keepdims=True)
            pick = row == idx
            picks.append(pick)
            vals.append(m)
            l = jnp.where(pick, -jnp.inf, l)
        exps = [jnp.exp(v - vals[0]) for v in vals]
        den = exps[0] + exps[1] + exps[2] + exps[3]
        sel_t = jnp.full((LANES, TM), -1.0, F32)
        for pick, e in zip(picks, exps):
            sel_t = jnp.where(pick, e / den, sel_t)
        sel = sel_t.T
        sel_ref[rs] = sel
        cnt = jnp.sum((sel >= 0.0).astype(F32), axis=0, keepdims=True)
        cnt_ref[t] = jnp.broadcast_to(cnt, (SUBLANES, LANES))

    n = ROUTER_TILES
    h1s, lgs = {0: out_proj(0)}, {}
    for t in range(1, n + 2):
        if t < n:
            h1s[t] = out_proj(t)
        if 1 <= t <= n:
            lgs[t - 1] = router_logits(t - 1, h1s.pop(t - 1))
        if t >= 2:
            top_k(t - 2, lgs.pop(t - 2))


def _out_router(x2, o_ret, o_fox, wor, wof, g, wrh, wrl, br):
    T, D = x2.shape
    TM = MOE_TILE * ROUTER_TILES
    nT = T // MOE_TILE
    const = lambda a: pl.BlockSpec(a.shape, lambda i: (0,) * a.ndim)
    tok = lambda w: pl.BlockSpec((TM, w), lambda i: (i, 0))
    return pl.pallas_call(
        _out_router_kernel,
        grid=(T // TM,),
        in_specs=[tok(D), tok(RET_WIDTH), tok(FOX_WIDTH), const(wor), const(wof), const(g),
                  const(wrh), const(wrl), const(br)],
        out_specs=(tok(D), tok(D), tok(LANES),
                   pl.BlockSpec((ROUTER_TILES, SUBLANES, LANES), lambda i: (i, 0, 0))),
        out_shape=(
            jax.ShapeDtypeStruct((T, D), F32),
            jax.ShapeDtypeStruct((T, D), BF16),
            jax.ShapeDtypeStruct((T, LANES), F32),
            jax.ShapeDtypeStruct((nT, SUBLANES, LANES), F32),
        ),
        compiler_params=pltpu.CompilerParams(dimension_semantics=("arbitrary",)),
        name="out_router",
    )(x2, o_ret, o_fox, wor, wof, g, wrh, wrl, br)


def _tile_sort(sel):
    TM = sel.shape[0]
    NS = TOP_K * TM
    maskf = (sel >= 0.0).astype(F32)
    mask = maskf.astype(BF16)
    ri = lax.broadcasted_iota(jnp.int32, (TM, TM), 0)
    ci = lax.broadcasted_iota(jnp.int32, (TM, TM), 1)
    rank = _dot((ri > ci).astype(BF16), mask)
    cnt = jnp.sum(maskf, axis=0, keepdims=True)
    ei = lax.broadcasted_iota(jnp.int32, (LANES, LANES), 0)
    ej = lax.broadcasted_iota(jnp.int32, (LANES, LANES), 1)
    cnt8 = jnp.broadcast_to(cnt, (SUBLANES, LANES)).astype(BF16)
    off = _dot(cnt8, (ei < ej).astype(BF16))[0:1, :]
    slot = lax.broadcasted_iota(jnp.int32, (NS, LANES), 0).astype(F32)
    esel = ((slot >= off) & (slot < off + cnt)).astype(BF16)
    return mask, rank.astype(BF16), esel, off, cnt


def _segment_dmas(step, slot, segdst_ref, cnt_ref, local, remote_rows, sem, to_remote, wait):
    if wait:
        whole = local.at[slot]
        rem = remote_rows.at[pl.ds(0, whole.shape[0]), :]
        cp = (pltpu.make_async_copy(whole, rem, sem.at[slot]) if to_remote
              else pltpu.make_async_copy(rem, whole, sem.at[slot]))
        cp.wait()
        return

    def body(e, off):
        c = cnt_ref[step * N_EXPERTS + e]
        dst = segdst_ref[step * N_EXPERTS + e]
        bit = MOE_TILE
        while bit >= 1:
            done = c & (~(2 * bit - 1))

            @pl.when((c & bit) != 0)
            def _(bit=bit, done=done):
                loc = local.at[slot, pl.ds((off + done) * ROW_TILES, bit * ROW_TILES), :]
                rem = remote_rows.at[pl.ds((dst + done) * ROW_TILES, bit * ROW_TILES), :]
                cp = (pltpu.make_async_copy(loc, rem, sem.at[slot]) if to_remote
                      else pltpu.make_async_copy(rem, loc, sem.at[slot]))
                cp.start()
            bit //= 2
        return off + c

    lax.fori_loop(0, N_EXPERTS, body, 0)


def _dispatch_kernel(segdst_ref, cnt_ref, paddst_ref, padcnt_ref, nused_ref, u2_ref, sel_ref, xs_ref,
                     buf, zbuf, sems, zsem):
    i = pl.program_id(0)
    last = pl.num_programs(0) - 1
    slot = i % 2
    TM = MOE_TILE
    NS = TOP_K * TM
    mask, rank, esel, off, _ = _tile_sort(sel_ref[...])
    slot_id = lax.broadcasted_iota(jnp.int32, (NS, 1), 0).astype(F32)
    r_s = slot_id - jnp.sum(esel.astype(F32) * off, axis=1, keepdims=True)
    perm = ((_dot_nt(esel, mask) > 0.5) & (_dot_nt(esel, rank) == r_s)).astype(BF16)

    @pl.when(i >= 2)
    def _():
        _segment_dmas(i - 2, slot, segdst_ref, cnt_ref, buf, xs_ref, sems, True, True)

    u2 = u2_ref[...]
    for c in range(NS // TM):
        rows = _dot(perm[c * TM:(c + 1) * TM], u2)
        for j in range(ROW_TILES):
            buf[slot, pl.ds(c * TM * ROW_TILES + j, TM, stride=ROW_TILES), :] = rows[:, j * LANES:(j + 1) * LANES]
    _segment_dmas(i, slot, segdst_ref, cnt_ref, buf, xs_ref, sems, True, False)

    @pl.when(i == last)
    def _():
        @pl.when(i >= 1)
        def _():
            _segment_dmas(i - 1, 1 - slot, segdst_ref, cnt_ref, buf, xs_ref, sems, True, True)
        _segment_dmas(i, slot, segdst_ref, cnt_ref, buf, xs_ref, sems, True, True)
        zbuf[...] = jnp.zeros_like(zbuf)
        half = EXPERT_BLOCK // 2 * ROW_TILES
        n_blocks = xs_ref.shape[0] // (EXPERT_BLOCK * ROW_TILES)
        for wait in (False, True):
            def unused(hb, carry, wait=wait):
                cp = pltpu.make_async_copy(zbuf, xs_ref.at[pl.ds(hb * half, half), :], zsem.at[0])
                cp.wait() if wait else cp.start()
                return carry
            lax.fori_loop(2 * nused_ref[0], 2 * n_blocks, unused, 0)


            def body(e, carry, wait=wait):
                c = padcnt_ref[e]
                dst = paddst_ref[e]
                bit = EXPERT_BLOCK // 2
                while bit >= 1:
                    done = c & (~(2 * bit - 1))

                    @pl.when((c & bit) != 0)
                    def _(bit=bit, done=done):
                        cp = pltpu.make_async_copy(
                            zbuf.at[pl.ds(0, bit * ROW_TILES), :],
                            xs_ref.at[pl.ds((dst + done) * ROW_TILES, bit * ROW_TILES), :], zsem.at[0])
                        cp.wait() if wait else cp.start()
                    bit //= 2
                return carry
            lax.fori_loop(0, N_EXPERTS, body, 0)


def _dispatch(u2, sel, segdst, cnt, paddst, padcnt, n_used, n_rows):
    T, D = u2.shape
    TM = MOE_TILE
    NS = TOP_K * TM
    return pl.pallas_call(
        _dispatch_kernel,
        grid_spec=pltpu.PrefetchScalarGridSpec(
            num_scalar_prefetch=5,
            grid=(T // TM,),
            in_specs=[pl.BlockSpec((TM, D), lambda i, *_: (i, 0)),
                      pl.BlockSpec((TM, LANES), lambda i, *_: (i, 0))],
            out_specs=pl.BlockSpec(memory_space=pl.ANY),
            scratch_shapes=[pltpu.VMEM((2, NS * ROW_TILES, LANES), F32),
                            pltpu.VMEM((EXPERT_BLOCK // 2 * ROW_TILES, LANES), F32),
                            pltpu.SemaphoreType.DMA((2,)), pltpu.SemaphoreType.DMA((1,))],
        ),
        out_shape=jax.ShapeDtypeStruct((n_rows * ROW_TILES, LANES), F32),
        compiler_params=pltpu.CompilerParams(
            dimension_semantics=("arbitrary",), vmem_limit_bytes=48 * 1024 * 1024),
        name="dispatch",
    )(segdst, cnt, paddst, padcnt, n_used, u2, sel)


def _expert_kernel(bexp_ref, nused_ref, xs_ref, w1_ref, b1_ref, w2_ref, b2_ref, ys_ref, w1b, w2b):
    b = pl.program_id(0)
    BLK = EXPERT_BLOCK
    used = b < nused_ref[0]

    @pl.when(used)
    def _():
        e = bexp_ref[b]
        prev = bexp_ref[jnp.maximum(b - 1, 0)]

        @pl.when((b == 0) | (e != prev))
        def _():
            rows = 128

            def cast(r, carry):
                sl = pl.ds(pl.multiple_of(r * rows, rows), rows)
                w1b[sl, :] = w1_ref[0, sl, :].astype(BF16)
                w2b[sl, :] = w2_ref[0, sl, :].astype(BF16)
                return carry
            lax.fori_loop(0, D_MODEL // rows, cast, 0)

        R = EXPERT_PASS_ROWS
        for rp in range(BLK // R):
            r0 = rp * R * ROW_TILES
            x = jnp.concatenate([xs_ref[pl.ds(r0 + j, R, stride=ROW_TILES), :] for j in range(ROW_TILES)],
                                axis=1).astype(BF16)
            h = _dot(x, w1b[...]) + b1_ref[0]
            glu = jnp.minimum(h[:, :D_FF], SWIGLU_LIMIT)
            lin = jnp.clip(h[:, D_FF:], -SWIGLU_LIMIT, SWIGLU_LIMIT)
            act = glu * jax.nn.sigmoid(SWIGLU_ALPHA * glu) * (lin + 1.0)
            y = _dot(act.astype(BF16), w2b[...]) + b2_ref[0]
            for j in range(ROW_TILES):
                ys_ref[pl.ds(r0 + j, R, stride=ROW_TILES), :] = y[:, j * LANES:(j + 1) * LANES]

    @pl.when(jnp.logical_not(used))
    def _():
        ys_ref[...] = jnp.zeros_like(ys_ref)


def _experts(xs, block_exp, n_used, w1, b1, w2, b2):
    BLK = EXPERT_BLOCK
    NB = xs.shape[0] // (BLK * ROW_TILES)
    blk = lambda b, nused: jnp.minimum(b, nused[0] - 1)
    return pl.pallas_call(
        _expert_kernel,
        grid_spec=pltpu.PrefetchScalarGridSpec(
            num_scalar_prefetch=2,
            grid=(NB,),
            in_specs=[
                pl.BlockSpec((BLK * ROW_TILES, LANES), lambda b, bexp, nused: (blk(b, nused), 0)),
                pl.BlockSpec((1, D_MODEL, 2 * D_FF), lambda b, bexp, nused: (bexp[blk(b, nused)], 0, 0)),
                pl.BlockSpec((1, 1, 2 * D_FF), lambda b, bexp, nused: (bexp[blk(b, nused)], 0, 0)),
                pl.BlockSpec((1, D_FF, D_MODEL), lambda b, bexp, nused: (bexp[blk(b, nused)], 0, 0)),
                pl.BlockSpec((1, 1, D_MODEL), lambda b, bexp, nused: (bexp[blk(b, nused)], 0, 0)),
            ],
            out_specs=pl.BlockSpec((BLK * ROW_TILES, LANES), lambda b, bexp, nused: (b, 0)),
            scratch_shapes=[pltpu.VMEM((D_MODEL, 2 * D_FF), BF16), pltpu.VMEM((D_FF, D_MODEL), BF16)],
        ),
        out_shape=jax.ShapeDtypeStruct(xs.shape, F32),
        compiler_params=pltpu.CompilerParams(
            dimension_semantics=("arbitrary",), vmem_limit_bytes=56 * 1024 * 1024),
        name="experts",
    )(block_exp, n_used, xs, w1, b1[:, None, :], w2, b2[:, None, :])


def _combine_kernel(segdst_ref, cnt_ref, ys_ref, sel_ref, h1_ref, g_ref, out_ref, buf, sems):
    i = pl.program_id(0)
    n = pl.num_programs(0)
    slot = i % 2
    TM = MOE_TILE
    NS = TOP_K * TM

    @pl.when(i == 0)
    def _():
        _segment_dmas(i, slot, segdst_ref, cnt_ref, buf, ys_ref, sems, False, False)

    @pl.when(i + 1 < n)
    def _():
        _segment_dmas(i + 1, 1 - slot, segdst_ref, cnt_ref, buf, ys_ref, sems, False, False)

    sel = sel_ref[...]
    mask, rank, esel, off, _ = _tile_sort(sel)
    gate = jnp.maximum(sel, 0.0)
    gh = gate.astype(BF16)
    gl = (gate - gh.astype(F32)).astype(BF16)
    o3 = _split3(jnp.broadcast_to(off, (SUBLANES, LANES)))
    off_s = (_dot_nt(o3[0], esel) + _dot_nt(o3[1], esel) + _dot_nt(o3[2], esel))[0:1, :]
    r_s = lax.broadcasted_iota(jnp.int32, (1, NS), 1).astype(F32) - off_s
    hit = (_dot_nt(mask, esel) > 0.5) & (_dot_nt(rank, esel) == r_s)
    unperm = jnp.where(hit, _dot_nt(gh, esel) + _dot_nt(gl, esel), 0.0).astype(BF16)

    _segment_dmas(i, slot, segdst_ref, cnt_ref, buf, ys_ref, sems, False, True)
    y = jnp.concatenate([buf[slot, pl.ds(j, NS, stride=ROW_TILES), :] for j in range(ROW_TILES)],
                        axis=1).astype(BF16)
    h2 = h1_ref[...] + _dot(unperm, y)
    out_ref[...] = _rms(h2, g_ref[...])


def _combine(ys, sel, h1, g, segdst, cnt):
    T, D = h1.shape
    TM = MOE_TILE
    NS = TOP_K * TM
    return pl.pallas_call(
        _combine_kernel,
        grid_spec=pltpu.PrefetchScalarGridSpec(
            num_scalar_prefetch=2,
            grid=(T // TM,),
            in_specs=[pl.BlockSpec(memory_space=pl.ANY),
                      pl.BlockSpec((TM, LANES), lambda i, *_: (i, 0)),
                      pl.BlockSpec((TM, D), lambda i, *_: (i, 0)),
                      pl.BlockSpec((1, D), lambda i, *_: (0, 0))],
            out_specs=pl.BlockSpec((TM, D), lambda i, *_: (i, 0)),
            scratch_shapes=[pltpu.VMEM((2, NS * ROW_TILES, LANES), F32), pltpu.SemaphoreType.DMA((2,))],
        ),
        out_shape=jax.ShapeDtypeStruct((T, D), F32),
        compiler_params=pltpu.CompilerParams(
            dimension_semantics=("arbitrary",), vmem_limit_bytes=48 * 1024 * 1024),
        name="combine",
    )(segdst, cnt, ys, sel, h1, g)


def _routing_tables(cnt_tiles):
    BLK = EXPERT_BLOCK
    nT = cnt_tiles.shape[0]
    A = nT * MOE_TILE * TOP_K
    NB = A // BLK + N_EXPERTS
    total = jnp.sum(cnt_tiles, axis=0)
    padded = (total + BLK - 1) // BLK * BLK
    pad_ends = jnp.cumsum(padded)
    pad_starts = pad_ends - padded
    before = jnp.cumsum(cnt_tiles, axis=0) - cnt_tiles
    segdst = (pad_starts[None, :] + before).reshape(-1).astype(jnp.int32)
    block_start = jnp.arange(NB, dtype=jnp.int32) * BLK
    block_exp = jnp.minimum(jnp.sum(pad_ends[None, :] <= block_start[:, None], axis=1), N_EXPERTS - 1).astype(jnp.int32)
    n_used = (pad_ends[-1] // BLK).astype(jnp.int32).reshape(1)
    paddst = (pad_starts + total).astype(jnp.int32)
    padcnt = (padded - total).astype(jnp.int32)
    return segdst, cnt_tiles.reshape(-1).astype(jnp.int32), paddst, padcnt, block_exp, n_used, NB * BLK


def _rotary_tables(S):
    half = RET_HEAD_DIM // 2
    inv_freq = ROPE_BASE ** (-jnp.arange(half, dtype=F32) / half)
    ang = jnp.arange(S, dtype=F32)[:, None] * inv_freq[None, :]
    cos, sin = jnp.cos(ang), jnp.sin(ang)
    return jnp.concatenate([cos, cos], axis=-1), jnp.concatenate([-sin, sin], axis=-1)


def _layer(h, norm_mix_g, w_in, b_forget, w_out, norm_ffn_g, w_router, b_router,
           w_exp_in, b_exp_in, w_exp_out, b_exp_out, final_g):
    B, S, D = h.shape
    R, Fw = RET_WIDTH, FOX_WIDTH
    cos, sin = _rotary_tables(S)
    wb = w_in.astype(BF16)
    wr = wb[:, :4 * R]
    wfq, wfk, wfv = (wb[:, 4 * R + i * Fw:4 * R + (i + 1) * Fw] for i in range(3))
    wz = jnp.zeros((D, LANES), BF16).at[:, :FOX_HEADS].set(wb[:, 4 * R + 3 * Fw:])
    wzt = jnp.zeros((16, D), BF16).at[:FOX_HEADS, :].set(wb[:, 4 * R + 3 * Fw:].T)
    wfvt = jnp.pad(wfv.T.reshape(FOX_HEADS, FOX_HEAD_DIM, D), ((0, 0), (0, V_AUG - FOX_HEAD_DIM), (0, 0)))
    vone = jnp.zeros((FOX_HEADS, V_AUG, 1), F32).at[:, FOX_HEAD_DIM, 0].set(1.0)
    rq, rk, rv, rg, fk, fqt, fvt, z, zt = _in_proj(
        h, norm_mix_g[None, :], cos, sin, wr, wfk, wfq.T, wfvt.reshape(FOX_HEADS * V_AUG, D),
        vone.reshape(FOX_HEADS * V_AUG, 1), wz, wzt)
    kaug, qaug = _fox_prep(z, zt, fk, fqt, b_forget)
    o_ret = _retention(rq, rk, rv, rg)
    o_fox = _fox_attn(qaug, kaug, fvt)

    T = B * S
    wo = w_out.astype(BF16)
    wrt = jnp.zeros((LANES, D), F32).at[:N_EXPERTS, :].set(w_router.T)
    wrh = wrt.astype(BF16)
    wrl = (wrt - wrh.astype(F32)).astype(BF16)
    br = jnp.zeros((LANES, 1), F32).at[:N_EXPERTS, 0].set(b_router)
    h1, u2, sel, cnt = _out_router(h.reshape(T, D), o_ret.reshape(T, R), o_fox.reshape(T, Fw),
                                   wo[:R], wo[R:], norm_ffn_g[None, :], wrh, wrl, br)
    cnt_tiles = cnt[:, 0, :N_EXPERTS].astype(jnp.int32)
    segdst, cnt_flat, paddst, padcnt, block_exp, n_used, n_rows = _routing_tables(cnt_tiles)
    xs = _dispatch(u2, sel, segdst, cnt_flat, paddst, padcnt, n_used, n_rows)
    ys = _experts(xs, block_exp, n_used, w_exp_in, b_exp_in, w_exp_out, b_exp_out)
    out = _combine(ys, sel, h1, final_g[None, :], segdst, cnt_flat)
    return out.reshape(B, S, D)


def kernel(x, norm_mix_g, w_in, b_forget, w_out, norm_ffn_g, w_router, b_router,
           w_exp_in, b_exp_in, w_exp_out, b_exp_out, norm_final_g):
    depth = w_in.shape[0]
    assert depth == 1, "the fused final RMSNorm assumes a single layer"
    return _layer(x, norm_mix_g[0], w_in[0], b_forget[0], w_out[0], norm_ffn_g[0], w_router[0], b_router[0],
                  w_exp_in[0], b_exp_in[0], w_exp_out[0], b_exp_out[0], norm_final_g)
```

```python
import functools

import numpy as np
import jax
import jax.numpy as jnp
from jax import lax
from jax.experimental import pallas as pl
from jax.experimental.pallas import tpu as pltpu

F32 = jnp.float32
BF16 = jnp.bfloat16

D_MODEL = 1024
RET_HEADS, RET_HEAD_DIM = 4, 128
RET_WIDTH = RET_HEADS * RET_HEAD_DIM
FOX_HEADS, FOX_HEAD_DIM = 8, 64
FOX_WIDTH = FOX_HEADS * FOX_HEAD_DIM
CHUNK = 64
ROPE_BASE = 10000.0
N_EXPERTS = 32
TOP_K = 4
D_FF = D_MODEL
SWIGLU_ALPHA = 1.702
SWIGLU_LIMIT = 7.0
RMS_EPS = 1e-5
GN_EPS = 1e-5

LANES = 128
SUBLANES = 8
ROW_TILES = D_MODEL // LANES

PROJ_TILE = 512
RET_BLOCK = 256
FOX_TQ = 512
FOX_TK = PROJ_TILE
FOX_SUB = 256
AUG = 128
V_AUG = 80
LOG2E = 1.4426950408889634
MOE_TILE = 256
ROUTER_TILES = 4
EXPERT_BLOCK = 512
EXPERT_PASS_ROWS = 256

NT_DIMS = (((1,), (1,)), ((), ()))


def _split3(a):
    hi = a.astype(BF16)
    r1 = a - hi.astype(F32)
    mid = r1.astype(BF16)
    lo = (r1 - mid.astype(F32)).astype(BF16)
    return hi, mid, lo


def _dot(a, b):
    return jnp.dot(a, b, preferred_element_type=F32)


def _dot_nt(a, b):
    return lax.dot_general(a, b, NT_DIMS, preferred_element_type=F32)


def _rms(x, g):
    return x * lax.rsqrt(jnp.mean(x * x, axis=-1, keepdims=True) + RMS_EPS) * g


def _in_proj_kernel(x_ref, g_ref, cos_ref, sin_ref, wr_ref, wfk_ref, wfqt_ref, wfvt_ref, vone_ref, wz_ref, wzt_ref,
                    rq_ref, rk_ref, rv_ref, rg_ref, fk_ref, fqt_ref, fvt_ref, z_ref, zt_ref):
    u = _rms(x_ref[0], g_ref[...]).astype(BF16)
    r = _dot(u, wr_ref[...])
    cos, sin = cos_ref[...], sin_ref[...]
    k_scale = RET_HEAD_DIM ** -0.5
    for h in range(RET_HEADS):
        lo = h * RET_HEAD_DIM
        q = r[:, lo:lo + RET_HEAD_DIM]
        k = r[:, RET_WIDTH + lo:RET_WIDTH + lo + RET_HEAD_DIM]
        rq_ref[0, :, lo:lo + RET_HEAD_DIM] = (q * cos + pltpu.roll(q, RET_HEAD_DIM // 2, 1) * sin).astype(BF16)
        rk_ref[0, :, lo:lo + RET_HEAD_DIM] = (
            (k * cos + pltpu.roll(k, RET_HEAD_DIM // 2, 1) * sin) * k_scale).astype(BF16)
    rv_ref[0] = r[:, 2 * RET_WIDTH:3 * RET_WIDTH].astype(BF16)
    rg_ref[0] = r[:, 3 * RET_WIDTH:4 * RET_WIDTH].astype(BF16)
    fk_ref[0] = _dot(u, wfk_ref[...]).astype(BF16)
    q_scale = FOX_HEAD_DIM ** -0.5 * LOG2E
    fqt_ref[0] = (_dot_nt(wfqt_ref[...], u) * q_scale).astype(BF16)
    fvt_ref[0, 0] = (_dot_nt(wfvt_ref[...], u) + vone_ref[...]).astype(BF16)
    z_ref[0] = _dot(u, wz_ref[...])
    zt_ref[0] = _dot_nt(wzt_ref[...], u)


def _in_proj(x, g, cos, sin, wr, wfk, wfqt, wfvt, vone, wz, wzt):
    B, S, D = x.shape
    TM = PROJ_TILE
    ns = S // TM
    const = lambda shape: pl.BlockSpec(shape, lambda b, s: (0,) * len(shape))
    tok = lambda w: pl.BlockSpec((1, TM, w), lambda b, s: (b, s, 0))
    out_shape = (
        jax.ShapeDtypeStruct((B, S, RET_WIDTH), BF16),
        jax.ShapeDtypeStruct((B, S, RET_WIDTH), BF16),
        jax.ShapeDtypeStruct((B, S, RET_WIDTH), BF16),
        jax.ShapeDtypeStruct((B, S, RET_WIDTH), BF16),
        jax.ShapeDtypeStruct((B, S, FOX_WIDTH), BF16),
        jax.ShapeDtypeStruct((B, FOX_WIDTH, S), BF16),
        jax.ShapeDtypeStruct((B, ns, FOX_HEADS * V_AUG, TM), BF16),
        jax.ShapeDtypeStruct((B, S, LANES), F32),
        jax.ShapeDtypeStruct((B, 16, S), F32),
    )
    return pl.pallas_call(
        _in_proj_kernel,
        grid=(B, ns),
        in_specs=[
            pl.BlockSpec((1, TM, D), lambda b, s: (b, s, 0)),
            const((1, D)),
            pl.BlockSpec((TM, RET_HEAD_DIM), lambda b, s: (s, 0)),
            pl.BlockSpec((TM, RET_HEAD_DIM), lambda b, s: (s, 0)),
            const(wr.shape), const(wfk.shape), const(wfqt.shape), const(wfvt.shape), const(vone.shape),
            const(wz.shape), const(wzt.shape),
        ],
        out_specs=(
            tok(RET_WIDTH), tok(RET_WIDTH), tok(RET_WIDTH), tok(RET_WIDTH), tok(FOX_WIDTH),
            pl.BlockSpec((1, FOX_WIDTH, TM), lambda b, s: (b, 0, s)),
            pl.BlockSpec((1, 1, FOX_HEADS * V_AUG, TM), lambda b, s: (b, s, 0, 0)),
            tok(LANES),
            pl.BlockSpec((1, 16, TM), lambda b, s: (b, 0, s)),
        ),
        out_shape=out_shape,
        compiler_params=pltpu.CompilerParams(
            dimension_semantics=("arbitrary", "arbitrary"), vmem_limit_bytes=48 * 1024 * 1024),
        name="in_proj",
    )(x, g, cos, sin, wr, wfk, wfqt, wfvt, vone, wz, wzt)


def _fox_prep_kernel(z_ref, zt_ref, fk_ref, fqt_ref, brow_ref, bcol_ref,
                     selkf_ref, constk_ref, selqf_ref, constq_ref, kaug_ref, qaug_ref, crow, ccol):
    TS = z_ref.shape[1]
    d = FOX_HEAD_DIM
    H = FOX_HEADS

    @pl.when(pl.program_id(1) == 0)
    def _():
        crow[...] = jnp.zeros_like(crow)
        ccol[...] = jnp.zeros_like(ccol)

    ri = lax.broadcasted_iota(jnp.int32, (TS, TS), 0)
    ci = lax.broadcasted_iota(jnp.int32, (TS, TS), 1)

    lane = lax.broadcasted_iota(jnp.int32, (TS, LANES), 1)
    lf = jnp.where(lane < H, jax.nn.log_sigmoid(z_ref[0] + brow_ref[...]), 0.0)
    ltri = (ri >= ci).astype(BF16)
    h3 = _split3(lf)
    f_row = _dot(ltri, h3[0]) + _dot(ltri, h3[1]) + _dot(ltri, h3[2]) + crow[...]
    crow[...] = f_row[TS - 1:TS, :]
    n3 = _split3(f_row * -LOG2E)
    pieces = (n3[0].astype(F32) + pltpu.roll(n3[1].astype(F32), H, 1)
              + pltpu.roll(n3[2].astype(F32), 2 * H, 1)).astype(BF16)
    for g in range(H // 2):
        bias = _dot(pieces, selkf_ref[g])
        kg = fk_ref[0, :, g * 2 * d:(g + 1) * 2 * d].astype(F32)
        for o in range(2):
            kh = kg if o == 0 else pltpu.roll(kg, d, 1)
            extra = bias[:, o * AUG:(o + 1) * AUG] + constk_ref[...]
            kaug_ref[0, 2 * g + o] = jnp.where(lane < d, kh, extra).astype(BF16)

    row = lax.broadcasted_iota(jnp.int32, (16, TS), 0)
    lft = jnp.where(row < H, jax.nn.log_sigmoid(zt_ref[0] + bcol_ref[...]), 0.0)
    utri = (ri <= ci).astype(BF16)
    t3 = _split3(lft)
    f_col = _dot(t3[0], utri) + _dot(t3[1], utri) + _dot(t3[2], utri) + ccol[:, 0:1]
    ccol[...] = jnp.broadcast_to(f_col[:, TS - 1:TS], ccol.shape)
    pieces_t = jnp.concatenate(_split3(f_col * LOG2E), axis=0)
    for h in range(H):
        extra = _dot(selqf_ref[h], pieces_t) + constq_ref[...]
        qaug_ref[0, h, 0] = jnp.concatenate([fqt_ref[0, h * d:(h + 1) * d, :], extra.astype(BF16)], axis=0)


def _fox_prep_constants():
    d, H = FOX_HEAD_DIM, FOX_HEADS
    selkf = np.zeros((H // 2, LANES, 2 * AUG), np.float32)
    constk = np.zeros((1, AUG), np.float32)
    selqf = np.zeros((H, d, 48), np.float32)
    constq = np.zeros((d, 1), np.float32)
    for p in range(3):
        constk[0, d + p] = 1.0
        constq[3 + p, 0] = 1.0
        for h in range(H):
            selkf[h // 2, p * H + h, (h % 2) * AUG + d + 3 + p] = 1.0
            selqf[h, p, p * 16 + h] = 1.0
    return jnp.asarray(selkf, BF16), jnp.asarray(constk, F32), jnp.asarray(selqf, BF16), jnp.asarray(constq, F32)


def _fox_prep(z, zt, fk, fqt, b_forget):
    B, S, _ = z.shape
    TS = PROJ_TILE
    ns = S // TS
    selkf, constk, selqf, constq = _fox_prep_constants()
    brow = jnp.zeros((1, LANES), F32).at[0, :FOX_HEADS].set(b_forget)
    bcol = jnp.zeros((16, 1), F32).at[:FOX_HEADS, 0].set(b_forget)
    const = lambda a: pl.BlockSpec(a.shape, lambda b, s: (0,) * a.ndim)
    return pl.pallas_call(
        _fox_prep_kernel,
        grid=(B, ns),
        in_specs=[
            pl.BlockSpec((1, TS, LANES), lambda b, s: (b, s, 0)),
            pl.BlockSpec((1, 16, TS), lambda b, s: (b, 0, s)),
            pl.BlockSpec((1, TS, FOX_WIDTH), lambda b, s: (b, s, 0)),
            pl.BlockSpec((1, FOX_WIDTH, TS), lambda b, s: (b, 0, s)),
            const(brow), const(bcol), const(selkf), const(constk), const(selqf), const(constq),
        ],
        out_specs=(
            pl.BlockSpec((1, FOX_HEADS, TS, AUG), lambda b, s: (b, 0, s, 0)),
            pl.BlockSpec((1, FOX_HEADS, 1, AUG, TS), lambda b, s: (b, 0, s, 0, 0)),
        ),
        out_shape=(
            jax.ShapeDtypeStruct((B, FOX_HEADS, S, AUG), BF16),
            jax.ShapeDtypeStruct((B, FOX_HEADS, ns, AUG, TS), BF16),
        ),
        scratch_shapes=[pltpu.VMEM((1, LANES), F32), pltpu.VMEM((16, LANES), F32)],
        compiler_params=pltpu.CompilerParams(
            dimension_semantics=("arbitrary", "arbitrary"), vmem_limit_bytes=48 * 1024 * 1024),
        name="fox_prep",
    )(z, zt, fk, fqt, brow, bcol, selkf, constk, selqf, constq)


def _retention_kernel(q_ref, k_ref, v_ref, g_ref, dec_ref, qw_ref, kw_ref, cd_ref, o_ref, state):
    @pl.when(pl.program_id(1) == 0)
    def _():
        state[...] = jnp.zeros_like(state)

    for h in range(RET_HEADS):
        hs = slice(h * RET_HEAD_DIM, (h + 1) * RET_HEAD_DIM)
        q, k, v = q_ref[0, :, hs], k_ref[0, :, hs], v_ref[0, :, hs]
        scores = (_dot_nt(q, k) * dec_ref[h]).astype(BF16)
        st = state[h]
        o = _dot(scores, v) + _dot((q.astype(F32) * qw_ref[h]).astype(BF16), st.astype(BF16))
        kk = k.astype(F32) * kw_ref[h]
        state[h] = st * cd_ref[h, 0:1, :] + _dot(kk.T.astype(BF16), v)
        mu = jnp.mean(o, axis=-1, keepdims=True)
        oc = o - mu
        var = jnp.mean(oc * oc, axis=-1, keepdims=True)
        o_ref[0, :, hs] = (oc * lax.rsqrt(var + GN_EPS) * jax.nn.silu(g_ref[0, :, hs].astype(F32))).astype(BF16)


def _retention_tables():
    L = RET_BLOCK
    log_gamma = jnp.log1p(-jnp.exp2(-5.0 - jnp.arange(RET_HEADS, dtype=F32)))
    p = jnp.arange(L, dtype=F32)
    dist = jnp.abs(p[:, None] - p[None, :])
    chunk = jnp.arange(L) // CHUNK
    allowed = (chunk[None, :] <= chunk[:, None]).astype(F32)
    dec = jnp.exp(log_gamma[:, None, None] * dist) * allowed
    lanes = lambda a: jnp.broadcast_to(a[:, :, None], (RET_HEADS, L, RET_HEAD_DIM))
    qw = lanes(jnp.exp(log_gamma[:, None] * (p[None, :] + 1.0)))
    kw = lanes(jnp.exp(log_gamma[:, None] * (L - 1.0 - p[None, :])))
    cd = jnp.broadcast_to(jnp.exp(log_gamma * L)[:, None, None], (RET_HEADS, SUBLANES, RET_HEAD_DIM))
    return dec, qw, kw, cd


def _retention(rq, rk, rv, rg):
    B, S, _ = rq.shape
    L = RET_BLOCK
    dec, qw, kw, cd = _retention_tables()
    tok = pl.BlockSpec((1, L, RET_WIDTH), lambda b, s: (b, s, 0))
    const = lambda a: pl.BlockSpec(a.shape, lambda b, s: (0,) * a.ndim)
    return pl.pallas_call(
        _retention_kernel,
        grid=(B, S // L),
        in_specs=[tok, tok, tok, tok, const(dec), const(qw), const(kw), const(cd)],
        out_specs=tok,
        out_shape=jax.ShapeDtypeStruct((B, S, RET_WIDTH), BF16),
        scratch_shapes=[pltpu.VMEM((RET_HEADS, RET_HEAD_DIM, RET_HEAD_DIM), F32)],
        compiler_params=pltpu.CompilerParams(dimension_semantics=("arbitrary",) * 2),
        name="retention",
    )(rq, rk, rv, rg, dec, qw, kw, cd)


def _fox_attn_kernel(q_ref, k_ref, v_ref, o_ref, s_a, s_b, s_c, cm_a, cm_b, cm_c, m_ref, acc_ref):
    T = FOX_TQ
    d = FOX_HEAD_DIM
    nq = q_ref.shape[2]

    def scores(qi, j, s_ref, cm_ref):
        for hh in range(2):
            kj = k_ref[0, hh, pl.ds(pl.multiple_of(j * T, T), T), :]
            st = _dot(kj, q_ref[0, hh, qi])
            s_ref[hh] = st
            cm_ref[hh] = jnp.max(st, axis=0, keepdims=True)

    def consume(j, s_ref, cm_ref, masked):
        for hh in range(2):
            st = s_ref[hh]
            if masked:
                key = lax.broadcasted_iota(jnp.int32, (T, T), 0)
                qry = lax.broadcasted_iota(jnp.int32, (T, T), 1)
                st = jnp.where(key <= qry, st, -jnp.inf)
                cm = jnp.max(st, axis=0, keepdims=True)
            else:
                cm = cm_ref[hh]
            m = m_ref[hh]
            m_new = jnp.maximum(m, cm)
            p = jnp.exp2(st - m_new).astype(BF16)
            vj = v_ref[0, j, hh * V_AUG:(hh + 1) * V_AUG, :]
            acc_ref[hh] = jnp.exp2(m - m_new) * acc_ref[hh] + _dot(vj, p)
            m_ref[hh] = m_new

    def reset():
        m_ref[...] = jnp.full(m_ref.shape, -jnp.inf, F32)
        acc_ref[...] = jnp.zeros(acc_ref.shape, F32)

    def prefetch_next(qi):
        @pl.when(qi + 1 < nq)
        def _():
            scores(qi + 1, 0, s_c, cm_c)

    def finish(qi):
        outs = [acc_ref[hh, 0:d, :] / acc_ref[hh, d:d + 1, :] for hh in range(2)]
        o_ref[0, pl.ds(pl.multiple_of(qi * T, T), T), :] = jnp.concatenate(outs, axis=0).T.astype(BF16)

    reset()
    scores(0, 0, s_a, cm_a)
    prefetch_next(0)
    consume(0, s_a, cm_a, True)
    finish(0)

    def query_tile(qi, carry):
        reset()
        scores(qi, 1, s_a, cm_a)
        consume(0, s_c, cm_c, False)

        def pair(j):
            scores(qi, j + 1, s_b, cm_b)
            consume(j, s_a, cm_a, False)
            scores(qi, j + 2, s_a, cm_a)
            consume(j + 1, s_b, cm_b, False)

        def two_pairs(jj, c):
            pair(1 + 4 * jj)
            pair(3 + 4 * jj)
            return c

        def one_pair(jj, c):
            pair(1 + 4 * (n_pairs // 2) + 2 * jj)
            return c

        n_pairs = (qi - 1) // 2
        lax.fori_loop(0, n_pairs // 2, two_pairs, 0)
        lax.fori_loop(0, n_pairs % 2, one_pair, 0)

        @pl.when(qi % 2 == 1)
        def _():
            prefetch_next(qi)
            consume(qi, s_a, cm_a, True)

        @pl.when(qi % 2 == 0)
        def _():
            scores(qi, qi, s_b, cm_b)
            consume(qi - 1, s_a, cm_a, False)
            prefetch_next(qi)
            consume(qi, s_b, cm_b, True)

        finish(qi)
        return carry

    lax.fori_loop(1, nq, query_tile, 0)


def _fox_attn(qaug, kaug, fvt):
    B, H, S, _ = kaug.shape
    nk = S // FOX_TK
    nq = S // FOX_TQ
    score_buf = pltpu.VMEM((2, FOX_TK, FOX_TQ), F32)
    col_max = pltpu.VMEM((2, 1, FOX_TQ), F32)
    return pl.pallas_call(
        _fox_attn_kernel,
        grid=(B, H // 2),
        in_specs=[
            pl.BlockSpec((1, 2, nq, AUG, FOX_TQ), lambda b, p: (b, p, 0, 0, 0)),
            pl.BlockSpec((1, 2, S, AUG), lambda b, p: (b, p, 0, 0)),
            pl.BlockSpec((1, nk, 2 * V_AUG, FOX_TK), lambda b, p: (b, 0, p, 0)),
        ],
        out_specs=pl.BlockSpec((1, S, 2 * FOX_HEAD_DIM), lambda b, p: (b, 0, p)),
        out_shape=jax.ShapeDtypeStruct((B, S, FOX_WIDTH), BF16),
        scratch_shapes=[
            score_buf, score_buf, score_buf, col_max, col_max, col_max,
            pltpu.VMEM((2, 1, FOX_TQ), F32), pltpu.VMEM((2, V_AUG, FOX_TQ), F32),
        ],
        compiler_params=pltpu.CompilerParams(
            dimension_semantics=("arbitrary",) * 2, vmem_limit_bytes=48 * 1024 * 1024),
        name="fox_attn",
    )(qaug, kaug, fvt)


def _out_router_kernel(x_ref, oret_ref, ofox_ref, wor_ref, wof_ref, g_ref, wrh_ref, wrl_ref, br_ref,
                       h1_ref, u2_ref, sel_ref, cnt_ref):
    TM = MOE_TILE
    rows = lambda t: slice(t * TM, (t + 1) * TM)

    def out_proj(t):
        rs = rows(t)
        h1 = x_ref[rs] + _dot(oret_ref[rs], wor_ref[...]) + _dot(ofox_ref[rs], wof_ref[...])
        h1_ref[rs] = h1
        return h1

    def router_logits(t, h1):
        u2 = _rms(h1, g_ref[...])
        uh = u2.astype(BF16)
        u2_ref[rows(t)] = uh
        ul = (u2 - uh.astype(F32)).astype(BF16)
        return (_dot_nt(wrh_ref[...], uh) + _dot_nt(wrh_ref[...], ul) + _dot_nt(wrl_ref[...], uh)
                + br_ref[...])

    def top_k(t, logits):
        rs = rows(t)
        row = lax.broadcasted_iota(jnp.int32, (LANES, TM), 0).astype(F32)
        l = jnp.where(row < N_EXPERTS, logits, -jnp.inf)
        picks, vals = [], []
        for _ in range(TOP_K):
            m = jnp.max(l, axis=0, keepdims=True)
            idx = jnp.min(jnp.where(l == m, row, float(LANES)), axis=0, keepdims=True)
            pick = row == idx
            picks.append(pick)
            vals.append(m)
            l = jnp.where(pick, -jnp.inf, l)
        exps = [jnp.exp(v - vals[0]) for v in vals]
        den = exps[0] + exps[1] + exps[2] + exps[3]
        sel_t = jnp.full((LANES, TM), -1.0, F32)
        for pick, e in zip(picks, exps):
            sel_t = jnp.where(pick, e / den, sel_t)
        sel = sel_t.T
        sel_ref[rs] = sel
        cnt = jnp.sum((sel >= 0.0).astype(F32), axis=0, keepdims=True)
        cnt_ref[t] = jnp.broadcast_to(cnt, (SUBLANES, LANES))

    n = ROUTER_TILES
    h1s, lgs = {0: out_proj(0)}, {}
    for t in range(1, n + 2):
        if t < n:
            h1s[t] = out_proj(t)
        if 1 <= t <= n:
            lgs[t - 1] = router_logits(t - 1, h1s.pop(t - 1))
        if t >= 2:
            top_k(t - 2, lgs.pop(t - 2))


def _out_router(x2, o_ret, o_fox, wor, wof, g, wrh, wrl, br):
    T, D = x2.shape
    TM = MOE_TILE * ROUTER_TILES
    nT = T // MOE_TILE
    const = lambda a: pl.BlockSpec(a.shape, lambda i: (0,) * a.ndim)
    tok = lambda w: pl.BlockSpec((TM, w), lambda i: (i, 0))
    return pl.pallas_call(
        _out_router_kernel,
        grid=(T // TM,),
        in_specs=[tok(D), tok(RET_WIDTH), tok(FOX_WIDTH), const(wor), const(wof), const(g),
                  const(wrh), const(wrl), const(br)],
        out_specs=(tok(D), tok(D), tok(LANES),
                   pl.BlockSpec((ROUTER_TILES, SUBLANES, LANES), lambda i: (i, 0, 0))),
        out_shape=(
            jax.ShapeDtypeStruct((T, D), F32),
            jax.ShapeDtypeStruct((T, D), BF16),
            jax.ShapeDtypeStruct((T, LANES), F32),
            jax.ShapeDtypeStruct((nT, SUBLANES, LANES), F32),
        ),
        compiler_params=pltpu.CompilerParams(dimension_semantics=("arbitrary",)),
        name="out_router",
    )(x2, o_ret, o_fox, wor, wof, g, wrh, wrl, br)


def _tile_sort(sel):
    TM = sel.shape[0]
    NS = TOP_K * TM
    maskf = (sel >= 0.0).astype(F32)
    mask = maskf.astype(BF16)
    ri = lax.broadcasted_iota(jnp.int32, (TM, TM), 0)
    ci = lax.broadcasted_iota(jnp.int32, (TM, TM), 1)
    rank = _dot((ri > ci).astype(BF16), mask)
    cnt = jnp.sum(maskf, axis=0, keepdims=True)
    ei = lax.broadcasted_iota(jnp.int32, (LANES, LANES), 0)
    ej = lax.broadcasted_iota(jnp.int32, (LANES, LANES), 1)
    cnt8 = jnp.broadcast_to(cnt, (SUBLANES, LANES)).astype(BF16)
    off = _dot(cnt8, (ei < ej).astype(BF16))[0:1, :]
    slot = lax.broadcasted_iota(jnp.int32, (NS, LANES), 0).astype(F32)
    esel = ((slot >= off) & (slot < off + cnt)).astype(BF16)
    return mask, rank.astype(BF16), esel, off, cnt


def _segment_dmas(step, slot, segdst_ref, cnt_ref, local, remote_rows, sem, to_remote, wait):
    if wait:
        whole = local.at[slot]
        rem = remote_rows.at[pl.ds(0, whole.shape[0]), :]
        cp = (pltpu.make_async_copy(whole, rem, sem.at[slot]) if to_remote
              else pltpu.make_async_copy(rem, whole, sem.at[slot]))
        cp.wait()
        return

    def body(e, off):
        c = cnt_ref[step * N_EXPERTS + e]
        dst = segdst_ref[step * N_EXPERTS + e]
        bit = MOE_TILE
        while bit >= 1:
            done = c & (~(2 * bit - 1))

            @pl.when((c & bit) != 0)
            def _(bit=bit, done=done):
                loc = local.at[slot, pl.ds((off + done) * ROW_TILES, bit * ROW_TILES), :]
                rem = remote_rows.at[pl.ds((dst + done) * ROW_TILES, bit * ROW_TILES), :]
                cp = (pltpu.make_async_copy(loc, rem, sem.at[slot]) if to_remote
                      else pltpu.make_async_copy(rem, loc, sem.at[slot]))
                cp.start()
            bit //= 2
        return off + c

    off = 0
    for e in range(N_EXPERTS):
        off = body(e, off)


def _dispatch_kernel(segdst_ref, cnt_ref, paddst_ref, padcnt_ref, nused_ref, u2_ref, sel_ref, xs_ref,
                     buf, zbuf, sems, zsem):
    i = pl.program_id(0)
    last = pl.num_programs(0) - 1
    slot = i % 2
    TM = MOE_TILE
    NS = TOP_K * TM
    mask, rank, esel, off, _ = _tile_sort(sel_ref[...])
    slot_id = lax.broadcasted_iota(jnp.int32, (NS, 1), 0).astype(F32)
    r_s = slot_id - jnp.sum(esel.astype(F32) * off, axis=1, keepdims=True)
    perm = ((_dot_nt(esel, mask) > 0.5) & (_dot_nt(esel, rank) == r_s)).astype(BF16)

    @pl.when(i >= 2)
    def _():
        _segment_dmas(i - 2, slot, segdst_ref, cnt_ref, buf, xs_ref, sems, True, True)

    u2 = u2_ref[...]
    for c in range(NS // TM):
        rows = _dot(perm[c * TM:(c + 1) * TM], u2)
        for j in range(ROW_TILES):
            buf[slot, pl.ds(c * TM * ROW_TILES + j, TM, stride=ROW_TILES), :] = rows[:, j * LANES:(j + 1) * LANES]
    _segment_dmas(i, slot, segdst_ref, cnt_ref, buf, xs_ref, sems, True, False)

    @pl.when(i == last)
    def _():
        @pl.when(i >= 1)
        def _():
            _segment_dmas(i - 1, 1 - slot, segdst_ref, cnt_ref, buf, xs_ref, sems, True, True)
        _segment_dmas(i, slot, segdst_ref, cnt_ref, buf, xs_ref, sems, True, True)
        zbuf[...] = jnp.zeros_like(zbuf)
        half = EXPERT_BLOCK // 2 * ROW_TILES
        n_blocks = xs_ref.shape[0] // (EXPERT_BLOCK * ROW_TILES)
        for wait in (False, True):
            def unused(hb, carry, wait=wait):
                cp = pltpu.make_async_copy(zbuf, xs_ref.at[pl.ds(hb * half, half), :], zsem.at[0])
                cp.wait() if wait else cp.start()
                return carry
            lax.fori_loop(2 * nused_ref[0], 2 * n_blocks, unused, 0)


            def body(e, carry, wait=wait):
                c = padcnt_ref[e]
                dst = paddst_ref[e]
                bit = EXPERT_BLOCK // 2
                while bit >= 1:
                    done = c & (~(2 * bit - 1))

                    @pl.when((c & bit) != 0)
                    def _(bit=bit, done=done):
                        cp = pltpu.make_async_copy(
                            zbuf.at[pl.ds(0, bit * ROW_TILES), :],
                            xs_ref.at[pl.ds((dst + done) * ROW_TILES, bit * ROW_TILES), :], zsem.at[0])
                        cp.wait() if wait else cp.start()
                    bit //= 2
                return carry
            lax.fori_loop(0, N_EXPERTS, body, 0)


def _dispatch(u2, sel, segdst, cnt, paddst, padcnt, n_used, n_rows):
    T, D = u2.shape
    TM = MOE_TILE
    NS = TOP_K * TM
    return pl.pallas_call(
        _dispatch_kernel,
        grid_spec=pltpu.PrefetchScalarGridSpec(
            num_scalar_prefetch=5,
            grid=(T // TM,),
            in_specs=[pl.BlockSpec((TM, D), lambda i, *_: (i, 0)),
                      pl.BlockSpec((TM, LANES), lambda i, *_: (i, 0))],
            out_specs=pl.BlockSpec(memory_space=pl.ANY),
            scratch_shapes=[pltpu.VMEM((2, NS * ROW_TILES, LANES), F32),
                            pltpu.VMEM((EXPERT_BLOCK // 2 * ROW_TILES, LANES), F32),
                            pltpu.SemaphoreType.DMA((2,)), pltpu.SemaphoreType.DMA((1,))],
        ),
        out_shape=jax.ShapeDtypeStruct((n_rows * ROW_TILES, LANES), F32),
        compiler_params=pltpu.CompilerParams(
            dimension_semantics=("arbitrary",), vmem_limit_bytes=48 * 1024 * 1024),
        name="dispatch",
    )(segdst, cnt, paddst, padcnt, n_used, u2, sel)


def _expert_kernel(bexp_ref, nused_ref, xs_ref, w1_ref, b1_ref, w2_ref, b2_ref, ys_ref, w1b, w2b):
    b = pl.program_id(0)
    BLK = EXPERT_BLOCK
    used = b < nused_ref[0]

    @pl.when(used)
    def _():
        e = bexp_ref[b]
        prev = bexp_ref[jnp.maximum(b - 1, 0)]

        @pl.when((b == 0) | (e != prev))
        def _():
            rows = 128

            def cast(r, carry):
                sl = pl.ds(pl.multiple_of(r * rows, rows), rows)
                w1b[sl, :] = w1_ref[0, sl, :].astype(BF16)
                w2b[sl, :] = w2_ref[0, sl, :].astype(BF16)
                return carry
            lax.fori_loop(0, D_MODEL // rows, cast, 0)

        R = EXPERT_PASS_ROWS
        for rp in range(BLK // R):
            r0 = rp * R * ROW_TILES
            x = jnp.concatenate([xs_ref[pl.ds(r0 + j, R, stride=ROW_TILES), :] for j in range(ROW_TILES)],
                                axis=1).astype(BF16)
            h = _dot(x, w1b[...]) + b1_ref[0]
            glu = jnp.minimum(h[:, :D_FF], SWIGLU_LIMIT)
            lin = jnp.clip(h[:, D_FF:], -SWIGLU_LIMIT, SWIGLU_LIMIT)
            act = glu * jax.nn.sigmoid(SWIGLU_ALPHA * glu) * (lin + 1.0)
            y = _dot(act.astype(BF16), w2b[...]) + b2_ref[0]
            for j in range(ROW_TILES):
                ys_ref[pl.ds(r0 + j, R, stride=ROW_TILES), :] = y[:, j * LANES:(j + 1) * LANES]

    @pl.when(jnp.logical_not(used))
    def _():
        ys_ref[...] = jnp.zeros_like(ys_ref)


def _experts(xs, block_exp, n_used, w1, b1, w2, b2):
    BLK = EXPERT_BLOCK
    NB = xs.shape[0] // (BLK * ROW_TILES)
    blk = lambda b, nused: jnp.minimum(b, nused[0] - 1)
    return pl.pallas_call(
        _expert_kernel,
        grid_spec=pltpu.PrefetchScalarGridSpec(
            num_scalar_prefetch=2,
            grid=(NB,),
            in_specs=[
                pl.BlockSpec((BLK * ROW_TILES, LANES), lambda b, bexp, nused: (blk(b, nused), 0)),
                pl.BlockSpec((1, D_MODEL, 2 * D_FF), lambda b, bexp, nused: (bexp[blk(b, nused)], 0, 0)),
                pl.BlockSpec((1, 1, 2 * D_FF), lambda b, bexp, nused: (bexp[blk(b, nused)], 0, 0)),
                pl.BlockSpec((1, D_FF, D_MODEL), lambda b, bexp, nused: (bexp[blk(b, nused)], 0, 0)),
                pl.BlockSpec((1, 1, D_MODEL), lambda b, bexp, nused: (bexp[blk(b, nused)], 0, 0)),
            ],
            out_specs=pl.BlockSpec((BLK * ROW_TILES, LANES), lambda b, bexp, nused: (b, 0)),
            scratch_shapes=[pltpu.VMEM((D_MODEL, 2 * D_FF), BF16), pltpu.VMEM((D_FF, D_MODEL), BF16)],
        ),
        out_shape=jax.ShapeDtypeStruct(xs.shape, F32),
        compiler_params=pltpu.CompilerParams(
            dimension_semantics=("arbitrary",), vmem_limit_bytes=56 * 1024 * 1024),
        name="experts",
    )(block_exp, n_used, xs, w1, b1[:, None, :], w2, b2[:, None, :])


def _combine_kernel(segdst_ref, cnt_ref, ys_ref, sel_ref, h1_ref, g_ref, out_ref, buf, sems):
    i = pl.program_id(0)
    n = pl.num_programs(0)
    slot = i % 2
    TM = MOE_TILE
    NS = TOP_K * TM

    @pl.when(i == 0)
    def _():
        _segment_dmas(i, slot, segdst_ref, cnt_ref, buf, ys_ref, sems, False, False)

    @pl.when(i + 1 < n)
    def _():
        _segment_dmas(i + 1, 1 - slot, segdst_ref, cnt_ref, buf, ys_ref, sems, False, False)

    sel = sel_ref[...]
    mask, rank, esel, off, _ = _tile_sort(sel)
    gate = jnp.maximum(sel, 0.0)
    gh = gate.astype(BF16)
    gl = (gate - gh.astype(F32)).astype(BF16)
    o3 = _split3(jnp.broadcast_to(off, (SUBLANES, LANES)))
    off_s = (_dot_nt(o3[0], esel) + _dot_nt(o3[1], esel) + _dot_nt(o3[2], esel))[0:1, :]
    r_s = lax.broadcasted_iota(jnp.int32, (1, NS), 1).astype(F32) - off_s
    hit = (_dot_nt(mask, esel) > 0.5) & (_dot_nt(rank, esel) == r_s)
    unperm = jnp.where(hit, _dot_nt(gh, esel) + _dot_nt(gl, esel), 0.0).astype(BF16)

    _segment_dmas(i, slot, segdst_ref, cnt_ref, buf, ys_ref, sems, False, True)
    y = jnp.concatenate([buf[slot, pl.ds(j, NS, stride=ROW_TILES), :] for j in range(ROW_TILES)],
                        axis=1).astype(BF16)
    h2 = h1_ref[...] + _dot(unperm, y)
    out_ref[...] = _rms(h2, g_ref[...])


def _combine(ys, sel, h1, g, segdst, cnt):
    T, D = h1.shape
    TM = MOE_TILE
    NS = TOP_K * TM
    return pl.pallas_call(
        _combine_kernel,
        grid_spec=pltpu.PrefetchScalarGridSpec(
            num_scalar_prefetch=2,
            grid=(T // TM,),
            in_specs=[pl.BlockSpec(memory_space=pl.ANY),
                      pl.BlockSpec((TM, LANES), lambda i, *_: (i, 0)),
                      pl.BlockSpec((TM, D), lambda i, *_: (i, 0)),
                      pl.BlockSpec((1, D), lambda i, *_: (0, 0))],
            out_specs=pl.BlockSpec((TM, D), lambda i, *_: (i, 0)),
            scratch_shapes=[pltpu.VMEM((2, NS * ROW_TILES, LANES), F32), pltpu.SemaphoreType.DMA((2,))],
        ),
        out_shape=jax.ShapeDtypeStruct((T, D), F32),
        compiler_params=pltpu.CompilerParams(
            dimension_semantics=("arbitrary",), vmem_limit_bytes=48 * 1024 * 1024),
        name="combine",
    )(segdst, cnt, ys, sel, h1, g)


def _routing_tables(cnt_tiles):
    BLK = EXPERT_BLOCK
    nT = cnt_tiles.shape[0]
    A = nT * MOE_TILE * TOP_K
    NB = A // BLK + N_EXPERTS
    total = jnp.sum(cnt_tiles, axis=0)
    padded = (total + BLK - 1) // BLK * BLK
    pad_ends = jnp.cumsum(padded)
    pad_starts = pad_ends - padded
    before = jnp.cumsum(cnt_tiles, axis=0) - cnt_tiles
    segdst = (pad_starts[None, :] + before).reshape(-1).astype(jnp.int32)
    block_start = jnp.arange(NB, dtype=jnp.int32) * BLK
    block_exp = jnp.minimum(jnp.sum(pad_ends[None, :] <= block_start[:, None], axis=1), N_EXPERTS - 1).astype(jnp.int32)
    n_used = (pad_ends[-1] // BLK).astype(jnp.int32).reshape(1)
    paddst = (pad_starts + total).astype(jnp.int32)
    padcnt = (padded - total).astype(jnp.int32)
    return segdst, cnt_tiles.reshape(-1).astype(jnp.int32), paddst, padcnt, block_exp, n_used, NB * BLK


def _rotary_tables(S):
    half = RET_HEAD_DIM // 2
    inv_freq = ROPE_BASE ** (-jnp.arange(half, dtype=F32) / half)
    ang = jnp.arange(S, dtype=F32)[:, None] * inv_freq[None, :]
    cos, sin = jnp.cos(ang), jnp.sin(ang)
    return jnp.concatenate([cos, cos], axis=-1), jnp.concatenate([-sin, sin], axis=-1)


def _layer(h, norm_mix_g, w_in, b_forget, w_out, norm_ffn_g, w_router, b_router,
           w_exp_in, b_exp_in, w_exp_out, b_exp_out, final_g):
    B, S, D = h.shape
    R, Fw = RET_WIDTH, FOX_WIDTH
    cos, sin = _rotary_tables(S)
    wb = w_in.astype(BF16)
    wr = wb[:, :4 * R]
    wfq, wfk, wfv = (wb[:, 4 * R + i * Fw:4 * R + (i + 1) * Fw] for i in range(3))
    wz = jnp.zeros((D, LANES), BF16).at[:, :FOX_HEADS].set(wb[:, 4 * R + 3 * Fw:])
    wzt = jnp.zeros((16, D), BF16).at[:FOX_HEADS, :].set(wb[:, 4 * R + 3 * Fw:].T)
    wfvt = jnp.pad(wfv.T.reshape(FOX_HEADS, FOX_HEAD_DIM, D), ((0, 0), (0, V_AUG - FOX_HEAD_DIM), (0, 0)))
    vone = jnp.zeros((FOX_HEADS, V_AUG, 1), F32).at[:, FOX_HEAD_DIM, 0].set(1.0)
    rq, rk, rv, rg, fk, fqt, fvt, z, zt = _in_proj(
        h, norm_mix_g[None, :], cos, sin, wr, wfk, wfq.T, wfvt.reshape(FOX_HEADS * V_AUG, D),
        vone.reshape(FOX_HEADS * V_AUG, 1), wz, wzt)
    kaug, qaug = _fox_prep(z, zt, fk, fqt, b_forget)
    o_ret = _retention(rq, rk, rv, rg)
    o_fox = _fox_attn(qaug, kaug, fvt)

    T = B * S
    wo = w_out.astype(BF16)
    wrt = jnp.zeros((LANES, D), F32).at[:N_EXPERTS, :].set(w_router.T)
    wrh = wrt.astype(BF16)
    wrl = (wrt - wrh.astype(F32)).astype(BF16)
    br = jnp.zeros((LANES, 1), F32).at[:N_EXPERTS, 0].set(b_router)
    h1, u2, sel, cnt = _out_router(h.reshape(T, D), o_ret.reshape(T, R), o_fox.reshape(T, Fw),
                                   wo[:R], wo[R:], norm_ffn_g[None, :], wrh, wrl, br)
    cnt_tiles = cnt[:, 0, :N_EXPERTS].astype(jnp.int32)
    segdst, cnt_flat, paddst, padcnt, block_exp, n_used, n_rows = _routing_tables(cnt_tiles)
    xs = _dispatch(u2, sel, segdst, cnt_flat, paddst, padcnt, n_used, n_rows)
    ys = _experts(xs, block_exp, n_used, w_exp_in, b_exp_in, w_exp_out, b_exp_out)
    out = _combine(ys, sel, h1, final_g[None, :], segdst, cnt_flat)
    return out.reshape(B, S, D)


def kernel(x, norm_mix_g, w_in, b_forget, w_out, norm_ffn_g, w_router, b_router,
           w_exp_in, b_exp_in, w_exp_out, b_exp_out, norm_final_g):
    depth = w_in.shape[0]
    assert depth == 1, "the fused final RMSNorm assumes a single layer"
    return _layer(x, norm_mix_g[0], w_in[0], b_forget[0], w_out[0], norm_ffn_g[0], w_router[0], b_router[0],
                  w_exp_in[0], b_exp_in[0], w_exp_out[0], b_exp_out[0], norm_final_g)
```

```python
import functools

import numpy as np
import jax
import jax.numpy as jnp
from jax import lax
from jax.experimental import pallas as pl
from jax.experimental.pallas import tpu as pltpu

F32 = jnp.float32
BF16 = jnp.bfloat16

D_MODEL = 1024
RET_HEADS, RET_HEAD_DIM = 4, 128
RET_WIDTH = RET_HEADS * RET_HEAD_DIM
FOX_HEADS, FOX_HEAD_DIM = 8, 64
FOX_WIDTH = FOX_HEADS * FOX_HEAD_DIM
CHUNK = 64
ROPE_BASE = 10000.0
N_EXPERTS = 32
TOP_K = 4
D_FF = D_MODEL
SWIGLU_ALPHA = 1.702
SWIGLU_LIMIT = 7.0
RMS_EPS = 1e-5
GN_EPS = 1e-5

LANES = 128
SUBLANES = 8
ROW_TILES = D_MODEL // LANES

PROJ_TILE = 512
RET_BLOCK = 256
FOX_TQ = 512
FOX_TK = PROJ_TILE
FOX_SUB = 256
AUG = 128
V_AUG = 80
LOG2E = 1.4426950408889634
MOE_TILE = 256
ROUTER_TILES = 4
EXPERT_BLOCK = 512
EXPERT_PASS_ROWS = 256

NT_DIMS = (((1,), (1,)), ((), ()))


def _split3(a):
    hi = a.astype(BF16)
    r1 = a - hi.astype(F32)
    mid = r1.astype(BF16)
    lo = (r1 - mid.astype(F32)).astype(BF16)
    return hi, mid, lo


def _dot(a, b):
    return jnp.dot(a, b, preferred_element_type=F32)


def _dot_nt(a, b):
    return lax.dot_general(a, b, NT_DIMS, preferred_element_type=F32)


def _rms(x, g):
    return x * lax.rsqrt(jnp.mean(x * x, axis=-1, keepdims=True) + RMS_EPS) * g


def _in_proj_kernel(x_ref, g_ref, cos_ref, sin_ref, wr_ref, wfk_ref, wfqt_ref, wfvt_ref, wzt_ref, bcol_ref,
                    selkf_ref, constk_ref, selqf_ref, constq_ref,
                    rq_ref, rk_ref, rv_ref, rg_ref, kaug_ref, qaug_ref, fvt_ref, ccol):
    TM = x_ref.shape[1]
    d, H = FOX_HEAD_DIM, FOX_HEADS
    u = _rms(x_ref[0], g_ref[...]).astype(BF16)
    r = _dot(u, wr_ref[...])
    cos, sin = cos_ref[...], sin_ref[...]
    k_scale = RET_HEAD_DIM ** -0.5
    for h in range(RET_HEADS):
        lo = h * RET_HEAD_DIM
        q = r[:, lo:lo + RET_HEAD_DIM]
        k = r[:, RET_WIDTH + lo:RET_WIDTH + lo + RET_HEAD_DIM]
        rq_ref[0, :, lo:lo + RET_HEAD_DIM] = (q * cos + pltpu.roll(q, RET_HEAD_DIM // 2, 1) * sin).astype(BF16)
        rk_ref[0, :, lo:lo + RET_HEAD_DIM] = (
            (k * cos + pltpu.roll(k, RET_HEAD_DIM // 2, 1) * sin) * k_scale).astype(BF16)
    rv_ref[0] = r[:, 2 * RET_WIDTH:3 * RET_WIDTH].astype(BF16)
    rg_ref[0] = r[:, 3 * RET_WIDTH:4 * RET_WIDTH].astype(BF16)
    fvt_ref[0, 0] = _dot_nt(wfvt_ref[...], u).astype(BF16)
    fk = _dot(u, wfk_ref[...])
    fqt = (_dot_nt(wfqt_ref[...], u) * (d ** -0.5 * LOG2E)).astype(BF16)
    zt = _dot_nt(wzt_ref[...], u)

    @pl.when(pl.program_id(1) == 0)
    def _():
        ccol[...] = jnp.zeros_like(ccol)

    row = lax.broadcasted_iota(jnp.int32, (16, TM), 0)
    lft = jnp.where(row < H, jax.nn.log_sigmoid(zt + bcol_ref[...]), 0.0)
    utri = (lax.broadcasted_iota(jnp.int32, (TM, TM), 0) <= lax.broadcasted_iota(jnp.int32, (TM, TM), 1)).astype(BF16)
    t3 = _split3(lft)
    f_col = _dot(t3[0], utri) + _dot(t3[1], utri) + _dot(t3[2], utri) + ccol[:, 0:1]
    ccol[...] = jnp.broadcast_to(f_col[:, TM - 1:TM], ccol.shape)

    pieces_t = jnp.concatenate(_split3(f_col * LOG2E), axis=0)
    for h in range(H):
        extra = _dot(selqf_ref[h], pieces_t) + constq_ref[...]
        qaug_ref[0, h, 0] = jnp.concatenate([fqt[h * d:(h + 1) * d, :], extra.astype(BF16)], axis=0)

    f_row = jnp.concatenate([f_col, jnp.zeros((LANES - 16, TM), F32)], axis=0).T
    n3 = _split3(f_row * -LOG2E)
    pieces = (n3[0].astype(F32) + pltpu.roll(n3[1].astype(F32), H, 1)
              + pltpu.roll(n3[2].astype(F32), 2 * H, 1)).astype(BF16)
    lane = lax.broadcasted_iota(jnp.int32, (TM, LANES), 1)
    for g in range(H // 2):
        bias = _dot(pieces, selkf_ref[g])
        kg = fk[:, g * 2 * d:(g + 1) * 2 * d]
        for o in range(2):
            kh = kg if o == 0 else pltpu.roll(kg, d, 1)
            extra = bias[:, o * AUG:(o + 1) * AUG] + constk_ref[...]
            kaug_ref[0, 2 * g + o] = jnp.where(lane < d, kh, extra).astype(BF16)


def _in_proj(x, g, cos, sin, wr, wfk, wfqt, wfvt, wzt, b_forget):
    B, S, D = x.shape
    TM = PROJ_TILE
    ns = S // TM
    selkf, constk, selqf, constq = _fox_prep_constants()
    bcol = jnp.zeros((16, 1), F32).at[:FOX_HEADS, 0].set(b_forget)
    consts = (wr, wfk, wfqt, wfvt, wzt, bcol, selkf, constk, selqf, constq)
    const = lambda a: pl.BlockSpec(a.shape, lambda b, s: (0,) * a.ndim)
    tok = lambda w: pl.BlockSpec((1, TM, w), lambda b, s: (b, s, 0))
    out_shape = (
        jax.ShapeDtypeStruct((B, S, RET_WIDTH), BF16),
        jax.ShapeDtypeStruct((B, S, RET_WIDTH), BF16),
        jax.ShapeDtypeStruct((B, S, RET_WIDTH), BF16),
        jax.ShapeDtypeStruct((B, S, RET_WIDTH), BF16),
        jax.ShapeDtypeStruct((B, FOX_HEADS, S, AUG), BF16),
        jax.ShapeDtypeStruct((B, FOX_HEADS, ns, AUG, TM), BF16),
        jax.ShapeDtypeStruct((B, ns, FOX_WIDTH, TM), BF16),
    )
    return pl.pallas_call(
        _in_proj_kernel,
        grid=(B, ns),
        in_specs=[
            pl.BlockSpec((1, TM, D), lambda b, s: (b, s, 0)),
            pl.BlockSpec((1, D), lambda b, s: (0, 0)),
            pl.BlockSpec((TM, RET_HEAD_DIM), lambda b, s: (s, 0)),
            pl.BlockSpec((TM, RET_HEAD_DIM), lambda b, s: (s, 0)),
        ] + [const(a) for a in consts],
        out_specs=(
            tok(RET_WIDTH), tok(RET_WIDTH), tok(RET_WIDTH), tok(RET_WIDTH),
            pl.BlockSpec((1, FOX_HEADS, TM, AUG), lambda b, s: (b, 0, s, 0)),
            pl.BlockSpec((1, FOX_HEADS, 1, AUG, TM), lambda b, s: (b, 0, s, 0, 0)),
            pl.BlockSpec((1, 1, FOX_WIDTH, TM), lambda b, s: (b, s, 0, 0)),
        ),
        out_shape=out_shape,
        scratch_shapes=[pltpu.VMEM((16, LANES), F32)],
        compiler_params=pltpu.CompilerParams(
            dimension_semantics=("arbitrary", "arbitrary"), vmem_limit_bytes=48 * 1024 * 1024),
        name="in_proj",
    )(x, g, cos, sin, *consts)


def _fox_prep_constants():
    d, H = FOX_HEAD_DIM, FOX_HEADS
    selkf = np.zeros((H // 2, LANES, 2 * AUG), np.float32)
    constk = np.zeros((1, AUG), np.float32)
    selqf = np.zeros((H, d, 48), np.float32)
    constq = np.zeros((d, 1), np.float32)
    for p in range(3):
        constk[0, d + p] = 1.0
        constq[3 + p, 0] = 1.0
        for h in range(H):
            selkf[h // 2, p * H + h, (h % 2) * AUG + d + 3 + p] = 1.0
            selqf[h, p, p * 16 + h] = 1.0
    return jnp.asarray(selkf, BF16), jnp.asarray(constk, F32), jnp.asarray(selqf, BF16), jnp.asarray(constq, F32)


def _retention_kernel(q_ref, k_ref, v_ref, g_ref, dec_ref, qw_ref, kw_ref, cd_ref, o_ref, state):
    @pl.when(pl.program_id(1) == 0)
    def _():
        state[...] = jnp.zeros_like(state)

    for h in range(RET_HEADS):
        hs = slice(h * RET_HEAD_DIM, (h + 1) * RET_HEAD_DIM)
        q, k, v = q_ref[0, :, hs], k_ref[0, :, hs], v_ref[0, :, hs]
        scores = (_dot_nt(q, k) * dec_ref[h]).astype(BF16)
        st = state[h]
        o = _dot(scores, v) + _dot((q.astype(F32) * qw_ref[h]).astype(BF16), st.astype(BF16))
        kk = k.astype(F32) * kw_ref[h]
        state[h] = st * cd_ref[h, 0:1, :] + _dot(kk.T.astype(BF16), v)
        mu = jnp.mean(o, axis=-1, keepdims=True)
        oc = o - mu
        var = jnp.mean(oc * oc, axis=-1, keepdims=True)
        o_ref[0, :, hs] = (oc * lax.rsqrt(var + GN_EPS) * jax.nn.silu(g_ref[0, :, hs].astype(F32))).astype(BF16)


def _retention_tables():
    L = RET_BLOCK
    log_gamma = jnp.log1p(-jnp.exp2(-5.0 - jnp.arange(RET_HEADS, dtype=F32)))
    p = jnp.arange(L, dtype=F32)
    dist = jnp.abs(p[:, None] - p[None, :])
    chunk = jnp.arange(L) // CHUNK
    allowed = (chunk[None, :] <= chunk[:, None]).astype(F32)
    dec = jnp.exp(log_gamma[:, None, None] * dist) * allowed
    lanes = lambda a: jnp.broadcast_to(a[:, :, None], (RET_HEADS, L, RET_HEAD_DIM))
    qw = lanes(jnp.exp(log_gamma[:, None] * (p[None, :] + 1.0)))
    kw = lanes(jnp.exp(log_gamma[:, None] * (L - 1.0 - p[None, :])))
    cd = jnp.broadcast_to(jnp.exp(log_gamma * L)[:, None, None], (RET_HEADS, SUBLANES, RET_HEAD_DIM))
    return dec, qw, kw, cd


def _retention(rq, rk, rv, rg):
    B, S, _ = rq.shape
    L = RET_BLOCK
    dec, qw, kw, cd = _retention_tables()
    tok = pl.BlockSpec((1, L, RET_WIDTH), lambda b, s: (b, s, 0))
    const = lambda a: pl.BlockSpec(a.shape, lambda b, s: (0,) * a.ndim)
    return pl.pallas_call(
        _retention_kernel,
        grid=(B, S // L),
        in_specs=[tok, tok, tok, tok, const(dec), const(qw), const(kw), const(cd)],
        out_specs=tok,
        out_shape=jax.ShapeDtypeStruct((B, S, RET_WIDTH), BF16),
        scratch_shapes=[pltpu.VMEM((RET_HEADS, RET_HEAD_DIM, RET_HEAD_DIM), F32)],
        compiler_params=pltpu.CompilerParams(dimension_semantics=("arbitrary",) * 2),
        name="retention",
    )(rq, rk, rv, rg, dec, qw, kw, cd)


def _fox_attn_kernel(q_ref, k_ref, v_ref, o_ref, s_a, s_b, s_c, cm_a, cm_b, cm_c, m_ref, acc_ref):
    T = FOX_TQ
    d = FOX_HEAD_DIM
    nq = q_ref.shape[2]
    ones_rows = (lax.broadcasted_iota(jnp.int32, (V_AUG - d, T), 0) == 0).astype(BF16)

    def scores(qi, j, s_ref, cm_ref):
        for hh in range(2):
            kj = k_ref[0, hh, pl.ds(pl.multiple_of(j * T, T), T), :]
            st = _dot(kj, q_ref[0, hh, qi])
            s_ref[hh] = st
            cm_ref[hh] = jnp.max(st, axis=0, keepdims=True)

    def consume(j, s_ref, cm_ref, masked):
        for hh in range(2):
            st = s_ref[hh]
            if masked:
                key = lax.broadcasted_iota(jnp.int32, (T, T), 0)
                qry = lax.broadcasted_iota(jnp.int32, (T, T), 1)
                st = jnp.where(key <= qry, st, -jnp.inf)
                cm = jnp.max(st, axis=0, keepdims=True)
            else:
                cm = cm_ref[hh]
            m = m_ref[hh]
            m_new = jnp.maximum(m, cm)
            p = jnp.exp2(st - m_new).astype(BF16)
            vj = jnp.concatenate([v_ref[0, j, hh * d:(hh + 1) * d, :], ones_rows], axis=0)
            acc_ref[hh] = jnp.exp2(m - m_new) * acc_ref[hh] + _dot(vj, p)
            m_ref[hh] = m_new

    def reset():
        m_ref[...] = jnp.full(m_ref.shape, -jnp.inf, F32)
        acc_ref[...] = jnp.zeros(acc_ref.shape, F32)

    def prefetch_next(qi):
        @pl.when(qi + 1 < nq)
        def _():
            scores(qi + 1, 0, s_c, cm_c)

    def finish(qi):
        outs = [acc_ref[hh, 0:d, :] / acc_ref[hh, d:d + 1, :] for hh in range(2)]
        o_ref[0, pl.ds(pl.multiple_of(qi * T, T), T), :] = jnp.concatenate(outs, axis=0).T.astype(BF16)

    reset()
    scores(0, 0, s_a, cm_a)
    prefetch_next(0)
    consume(0, s_a, cm_a, True)
    finish(0)

    def query_tile(qi, carry):
        reset()
        scores(qi, 1, s_a, cm_a)
        consume(0, s_c, cm_c, False)

        def pair(j):
            scores(qi, j + 1, s_b, cm_b)
            consume(j, s_a, cm_a, False)
            scores(qi, j + 2, s_a, cm_a)
            consume(j + 1, s_b, cm_b, False)

        def two_pairs(jj, c):
            pair(1 + 4 * jj)
            pair(3 + 4 * jj)
            return c

        def one_pair(jj, c):
            pair(1 + 4 * (n_pairs // 2) + 2 * jj)
            return c

        n_pairs = (qi - 1) // 2
        lax.fori_loop(0, n_pairs // 2, two_pairs, 0)
        lax.fori_loop(0, n_pairs % 2, one_pair, 0)

        @pl.when(qi % 2 == 1)
        def _():
            prefetch_next(qi)
            consume(qi, s_a, cm_a, True)

        @pl.when(qi % 2 == 0)
        def _():
            scores(qi, qi, s_b, cm_b)
            consume(qi - 1, s_a, cm_a, False)
            prefetch_next(qi)
            consume(qi, s_b, cm_b, True)

        finish(qi)
        return carry

    lax.fori_loop(1, nq, query_tile, 0)


def _fox_attn(qaug, kaug, fvt):
    B, H, S, _ = kaug.shape
    nk = S // FOX_TK
    nq = S // FOX_TQ
    score_buf = pltpu.VMEM((2, FOX_TK, FOX_TQ), F32)
    col_max = pltpu.VMEM((2, 1, FOX_TQ), F32)
    return pl.pallas_call(
        _fox_attn_kernel,
        grid=(B, H // 2),
        in_specs=[
            pl.BlockSpec((1, 2, nq, AUG, FOX_TQ), lambda b, p: (b, p, 0, 0, 0)),
            pl.BlockSpec((1, 2, S, AUG), lambda b, p: (b, p, 0, 0)),
            pl.BlockSpec((1, nk, 2 * FOX_HEAD_DIM, FOX_TK), lambda b, p: (b, 0, p, 0)),
        ],
        out_specs=pl.BlockSpec((1, S, 2 * FOX_HEAD_DIM), lambda b, p: (b, 0, p)),
        out_shape=jax.ShapeDtypeStruct((B, S, FOX_WIDTH), BF16),
        scratch_shapes=[
            score_buf, score_buf, score_buf, col_max, col_max, col_max,
            pltpu.VMEM((2, 1, FOX_TQ), F32), pltpu.VMEM((2, V_AUG, FOX_TQ), F32),
        ],
        compiler_params=pltpu.CompilerParams(
            dimension_semantics=("arbitrary",) * 2, vmem_limit_bytes=48 * 1024 * 1024),
        name="fox_attn",
    )(qaug, kaug, fvt)


def _out_router_kernel(x_ref, oret_ref, ofox_ref, wor_ref, wof_ref, g_ref, wrh_ref, wrl_ref, br_ref,
                       h1_ref, u2_ref, sel_ref, cnt_ref):
    TM = MOE_TILE
    rows = lambda t: slice(t * TM, (t + 1) * TM)

    def out_proj(t):
        rs = rows(t)
        h1 = x_ref[rs] + _dot(oret_ref[rs], wor_ref[...]) + _dot(ofox_ref[rs], wof_ref[...])
        h1_ref[rs] = h1
        return h1

    def router_logits(t, h1):
        u2 = _rms(h1, g_ref[...])
        uh = u2.astype(BF16)
        u2_ref[rows(t)] = uh
        ul = (u2 - uh.astype(F32)).astype(BF16)
        return (_dot_nt(wrh_ref[...], uh) + _dot_nt(wrh_ref[...], ul) + _dot_nt(wrl_ref[...], uh)
                + br_ref[...])

    def top_k(t, logits):
        rs = rows(t)
        row = lax.broadcasted_iota(jnp.int32, (LANES, TM), 0).astype(F32)
        l = jnp.where(row < N_EXPERTS, logits, -jnp.inf)
        picks, vals = [], []
        for _ in range(TOP_K):
            m = jnp.max(l, axis=0, keepdims=True)
            idx = jnp.min(jnp.where(l == m, row, float(LANES)), axis=0, keepdims=True)
            pick = row == idx
            picks.append(pick)
            vals.append(m)
            l = jnp.where(pick, -jnp.inf, l)
        exps = [jnp.exp(v - vals[0]) for v in vals]
        den = exps[0] + exps[1] + exps[2] + exps[3]
        sel_t = jnp.full((LANES, TM), -1.0, F32)
        for pick, e in zip(picks, exps):
            sel_t = jnp.where(pick, e / den, sel_t)
        sel = sel_t.T
        sel_ref[rs] = sel
        cnt = jnp.sum((sel >= 0.0).astype(F32), axis=0, keepdims=True)
        cnt_ref[t] = jnp.broadcast_to(cnt, (SUBLANES, LANES))

    n = ROUTER_TILES
    h1s, lgs = {0: out_proj(0)}, {}
    for t in range(1, n + 2):
        if t < n:
            h1s[t] = out_proj(t)
        if 1 <= t <= n:
            lgs[t - 1] = router_logits(t - 1, h1s.pop(t - 1))
        if t >= 2:
            top_k(t - 2, lgs.pop(t - 2))


def _out_router(x2, o_ret, o_fox, wor, wof, g, wrh, wrl, br):
    T, D = x2.shape
    TM = MOE_TILE * ROUTER_TILES
    nT = T // MOE_TILE
    const = lambda a: pl.BlockSpec(a.shape, lambda i: (0,) * a.ndim)
    tok = lambda w: pl.BlockSpec((TM, w), lambda i: (i, 0))
    return pl.pallas_call(
        _out_router_kernel,
        grid=(T // TM,),
        in_specs=[tok(D), tok(RET_WIDTH), tok(FOX_WIDTH), const(wor), const(wof), const(g),
                  const(wrh), const(wrl), const(br)],
        out_specs=(tok(D), tok(D), tok(LANES),
                   pl.BlockSpec((ROUTER_TILES, SUBLANES, LANES), lambda i: (i, 0, 0))),
        out_shape=(
            jax.ShapeDtypeStruct((T, D), F32),
            jax.ShapeDtypeStruct((T, D), BF16),
            jax.ShapeDtypeStruct((T, LANES), F32),
            jax.ShapeDtypeStruct((nT, SUBLANES, LANES), F32),
        ),
        compiler_params=pltpu.CompilerParams(dimension_semantics=("arbitrary",)),
        name="out_router",
    )(x2, o_ret, o_fox, wor, wof, g, wrh, wrl, br)


def _tile_sort(sel):
    TM = sel.shape[0]
    NS = TOP_K * TM
    maskf = (sel >= 0.0).astype(F32)
    mask = maskf.astype(BF16)
    ri = lax.broadcasted_iota(jnp.int32, (TM, TM), 0)
    ci = lax.broadcasted_iota(jnp.int32, (TM, TM), 1)
    rank = _dot((ri > ci).astype(BF16), mask)
    cnt = jnp.sum(maskf, axis=0, keepdims=True)
    ei = lax.broadcasted_iota(jnp.int32, (LANES, LANES), 0)
    ej = lax.broadcasted_iota(jnp.int32, (LANES, LANES), 1)
    cnt8 = jnp.broadcast_to(cnt, (SUBLANES, LANES)).astype(BF16)
    off = _dot(cnt8, (ei < ej).astype(BF16))[0:1, :]
    slot = lax.broadcasted_iota(jnp.int32, (NS, LANES), 0).astype(F32)
    esel = ((slot >= off) & (slot < off + cnt)).astype(BF16)
    return mask, rank.astype(BF16), esel, off, cnt


def _segment_dmas(step, slot, segdst_ref, cnt_ref, local, remote_rows, sem, to_remote, wait):
    if wait:
        whole = local.at[slot]
        rem = remote_rows.at[pl.ds(0, whole.shape[0]), :]
        cp = (pltpu.make_async_copy(whole, rem, sem.at[slot]) if to_remote
              else pltpu.make_async_copy(rem, whole, sem.at[slot]))
        cp.wait()
        return

    def body(e, off):
        c = cnt_ref[step * N_EXPERTS + e]
        dst = segdst_ref[step * N_EXPERTS + e]
        bit = MOE_TILE
        while bit >= 1:
            done = c & (~(2 * bit - 1))

            @pl.when((c & bit) != 0)
            def _(bit=bit, done=done):
                loc = local.at[slot, pl.ds((off + done) * ROW_TILES, bit * ROW_TILES), :]
                rem = remote_rows.at[pl.ds((dst + done) * ROW_TILES, bit * ROW_TILES), :]
                cp = (pltpu.make_async_copy(loc, rem, sem.at[slot]) if to_remote
                      else pltpu.make_async_copy(rem, loc, sem.at[slot]))
                cp.start()
            bit //= 2
        return off + c

    off = 0
    for e in range(N_EXPERTS):
        off = body(e, off)


def _dispatch_kernel(segdst_ref, cnt_ref, paddst_ref, padcnt_ref, nused_ref, u2_ref, sel_ref, xs_ref,
                     buf, zbuf, sems, zsem):
    i = pl.program_id(0)
    last = pl.num_programs(0) - 1
    slot = i % 2
    TM = MOE_TILE
    NS = TOP_K * TM
    mask, rank, esel, off, _ = _tile_sort(sel_ref[...])
    slot_id = lax.broadcasted_iota(jnp.int32, (NS, 1), 0).astype(F32)
    r_s = slot_id - jnp.sum(esel.astype(F32) * off, axis=1, keepdims=True)
    perm = ((_dot_nt(esel, mask) > 0.5) & (_dot_nt(esel, rank) == r_s)).astype(BF16)

    @pl.when(i >= 2)
    def _():
        _segment_dmas(i - 2, slot, segdst_ref, cnt_ref, buf, xs_ref, sems, True, True)

    u2 = u2_ref[...]
    for c in range(NS // TM):
        rows = _dot(perm[c * TM:(c + 1) * TM], u2)
        for j in range(ROW_TILES):
            buf[slot, pl.ds(c * TM * ROW_TILES + j, TM, stride=ROW_TILES), :] = rows[:, j * LANES:(j + 1) * LANES]
    _segment_dmas(i, slot, segdst_ref, cnt_ref, buf, xs_ref, sems, True, False)

    @pl.when(i == last)
    def _():
        @pl.when(i >= 1)
        def _():
            _segment_dmas(i - 1, 1 - slot, segdst_ref, cnt_ref, buf, xs_ref, sems, True, True)
        _segment_dmas(i, slot, segdst_ref, cnt_ref, buf, xs_ref, sems, True, True)
        zbuf[...] = jnp.zeros_like(zbuf)
        half = EXPERT_BLOCK // 2 * ROW_TILES
        n_blocks = xs_ref.shape[0] // (EXPERT_BLOCK * ROW_TILES)
        for wait in (False, True):
            def unused(hb, carry, wait=wait):
                cp = pltpu.make_async_copy(zbuf, xs_ref.at[pl.ds(hb * half, half), :], zsem.at[0])
                cp.wait() if wait else cp.start()
                return carry
            lax.fori_loop(2 * nused_ref[0], 2 * n_blocks, unused, 0)


            def body(e, carry, wait=wait):
                c = padcnt_ref[e]
                dst = paddst_ref[e]
                bit = EXPERT_BLOCK // 2
                while bit >= 1:
                    done = c & (~(2 * bit - 1))

                    @pl.when((c & bit) != 0)
                    def _(bit=bit, done=done):
                        cp = pltpu.make_async_copy(
                            zbuf.at[pl.ds(0, bit * ROW_TILES), :],
                            xs_ref.at[pl.ds((dst + done) * ROW_TILES, bit * ROW_TILES), :], zsem.at[0])
                        cp.wait() if wait else cp.start()
                    bit //= 2
                return carry
            lax.fori_loop(0, N_EXPERTS, body, 0)


def _dispatch(u2, sel, segdst, cnt, paddst, padcnt, n_used, n_rows):
    T, D = u2.shape
    TM = MOE_TILE
    NS = TOP_K * TM
    return pl.pallas_call(
        _dispatch_kernel,
        grid_spec=pltpu.PrefetchScalarGridSpec(
            num_scalar_prefetch=5,
            grid=(T // TM,),
            in_specs=[pl.BlockSpec((TM, D), lambda i, *_: (i, 0)),
                      pl.BlockSpec((TM, LANES), lambda i, *_: (i, 0))],
            out_specs=pl.BlockSpec(memory_space=pl.ANY),
            scratch_shapes=[pltpu.VMEM((2, NS * ROW_TILES, LANES), F32),
                            pltpu.VMEM((EXPERT_BLOCK // 2 * ROW_TILES, LANES), F32),
                            pltpu.SemaphoreType.DMA((2,)), pltpu.SemaphoreType.DMA((1,))],
        ),
        out_shape=jax.ShapeDtypeStruct((n_rows * ROW_TILES, LANES), F32),
        compiler_params=pltpu.CompilerParams(
            dimension_semantics=("arbitrary",), vmem_limit_bytes=48 * 1024 * 1024),
        name="dispatch",
    )(segdst, cnt, paddst, padcnt, n_used, u2, sel)


def _expert_kernel(bexp_ref, nused_ref, xs_ref, w1_ref, b1_ref, w2_ref, b2_ref, ys_ref, w1b, w2b):
    b = pl.program_id(0)
    BLK = EXPERT_BLOCK
    used = b < nused_ref[0]

    @pl.when(used)
    def _():
        e = bexp_ref[b]
        prev = bexp_ref[jnp.maximum(b - 1, 0)]

        @pl.when((b == 0) | (e != prev))
        def _():
            rows = 128

            def cast(r, carry):
                sl = pl.ds(pl.multiple_of(r * rows, rows), rows)
                w1b[sl, :] = w1_ref[0, sl, :].astype(BF16)
                w2b[sl, :] = w2_ref[0, sl, :].astype(BF16)
                return carry
            lax.fori_loop(0, D_MODEL // rows, cast, 0)

        R = EXPERT_PASS_ROWS
        for rp in range(BLK // R):
            r0 = rp * R * ROW_TILES
            x = jnp.concatenate([xs_ref[pl.ds(r0 + j, R, stride=ROW_TILES), :] for j in range(ROW_TILES)],
                                axis=1).astype(BF16)
            h = _dot(x, w1b[...]) + b1_ref[0]
            glu = jnp.minimum(h[:, :D_FF], SWIGLU_LIMIT)
            lin = jnp.clip(h[:, D_FF:], -SWIGLU_LIMIT, SWIGLU_LIMIT)
            act = glu * jax.nn.sigmoid(SWIGLU_ALPHA * glu) * (lin + 1.0)
            y = _dot(act.astype(BF16), w2b[...]) + b2_ref[0]
            for j in range(ROW_TILES):
                ys_ref[pl.ds(r0 + j, R, stride=ROW_TILES), :] = y[:, j * LANES:(j + 1) * LANES]

    @pl.when(jnp.logical_not(used))
    def _():
        ys_ref[...] = jnp.zeros_like(ys_ref)


def _experts(xs, block_exp, n_used, w1, b1, w2, b2):
    BLK = EXPERT_BLOCK
    NB = xs.shape[0] // (BLK * ROW_TILES)
    blk = lambda b, nused: jnp.minimum(b, nused[0] - 1)
    return pl.pallas_call(
        _expert_kernel,
        grid_spec=pltpu.PrefetchScalarGridSpec(
            num_scalar_prefetch=2,
            grid=(NB,),
            in_specs=[
                pl.BlockSpec((BLK * ROW_TILES, LANES), lambda b, bexp, nused: (blk(b, nused), 0)),
                pl.BlockSpec((1, D_MODEL, 2 * D_FF), lambda b, bexp, nused: (bexp[blk(b, nused)], 0, 0)),
                pl.BlockSpec((1, 1, 2 * D_FF), lambda b, bexp, nused: (bexp[blk(b, nused)], 0, 0)),
                pl.BlockSpec((1, D_FF, D_MODEL), lambda b, bexp, nused: (bexp[blk(b, nused)], 0, 0)),
                pl.BlockSpec((1, 1, D_MODEL), lambda b, bexp, nused: (bexp[blk(b, nused)], 0, 0)),
            ],
            out_specs=pl.BlockSpec((BLK * ROW_TILES, LANES), lambda b, bexp, nused: (b, 0)),
            scratch_shapes=[pltpu.VMEM((D_MODEL, 2 * D_FF), BF16), pltpu.VMEM((D_FF, D_MODEL), BF16)],
        ),
        out_shape=jax.ShapeDtypeStruct(xs.shape, F32),
        compiler_params=pltpu.CompilerParams(
            dimension_semantics=("arbitrary",), vmem_limit_bytes=56 * 1024 * 1024),
        name="experts",
    )(block_exp, n_used, xs, w1, b1[:, None, :], w2, b2[:, None, :])


def _combine_kernel(segdst_ref, cnt_ref, ys_ref, sel_ref, h1_ref, g_ref, out_ref, buf, sems):
    i = pl.program_id(0)
    n = pl.num_programs(0)
    slot = i % 2
    TM = MOE_TILE
    NS = TOP_K * TM

    @pl.when(i == 0)
    def _():
        _segment_dmas(i, slot, segdst_ref, cnt_ref, buf, ys_ref, sems, False, False)

    @pl.when(i + 1 < n)
    def _():
        _segment_dmas(i + 1, 1 - slot, segdst_ref, cnt_ref, buf, ys_ref, sems, False, False)

    sel = sel_ref[...]
    mask, rank, esel, off, _ = _tile_sort(sel)
    gate = jnp.maximum(sel, 0.0)
    gh = gate.astype(BF16)
    gl = (gate - gh.astype(F32)).astype(BF16)
    o3 = _split3(jnp.broadcast_to(off, (SUBLANES, LANES)))
    off_s = (_dot_nt(o3[0], esel) + _dot_nt(o3[1], esel) + _dot_nt(o3[2], esel))[0:1, :]
    r_s = lax.broadcasted_iota(jnp.int32, (1, NS), 1).astype(F32) - off_s
    hit = (_dot_nt(mask, esel) > 0.5) & (_dot_nt(rank, esel) == r_s)
    unperm = jnp.where(hit, _dot_nt(gh, esel) + _dot_nt(gl, esel), 0.0).astype(BF16)

    _segment_dmas(i, slot, segdst_ref, cnt_ref, buf, ys_ref, sems, False, True)
    y = jnp.concatenate([buf[slot, pl.ds(j, NS, stride=ROW_TILES), :] for j in range(ROW_TILES)],
                        axis=1).astype(BF16)
    h2 = h1_ref[...] + _dot(unperm, y)
    out_ref[...] = _rms(h2, g_ref[...])


def _combine(ys, sel, h1, g, segdst, cnt):
    T, D = h1.shape
    TM = MOE_TILE
    NS = TOP_K * TM
    return pl.pallas_call(
        _combine_kernel,
        grid_spec=pltpu.PrefetchScalarGridSpec(
            num_scalar_prefetch=2,
            grid=(T // TM,),
            in_specs=[pl.BlockSpec(memory_space=pl.ANY),
                      pl.BlockSpec((TM, LANES), lambda i, *_: (i, 0)),
                      pl.BlockSpec((TM, D), lambda i, *_: (i, 0)),
                      pl.BlockSpec((1, D), lambda i, *_: (0, 0))],
            out_specs=pl.BlockSpec((TM, D), lambda i, *_: (i, 0)),
            scratch_shapes=[pltpu.VMEM((2, NS * ROW_TILES, LANES), F32), pltpu.SemaphoreType.DMA((2,))],
        ),
        out_shape=jax.ShapeDtypeStruct((T, D), F32),
        compiler_params=pltpu.CompilerParams(
            dimension_semantics=("arbitrary",), vmem_limit_bytes=48 * 1024 * 1024),
        name="combine",
    )(segdst, cnt, ys, sel, h1, g)


def _routing_tables(cnt_tiles):
    BLK = EXPERT_BLOCK
    nT = cnt_tiles.shape[0]
    A = nT * MOE_TILE * TOP_K
    NB = A // BLK + N_EXPERTS
    total = jnp.sum(cnt_tiles, axis=0)
    padded = (total + BLK - 1) // BLK * BLK
    pad_ends = jnp.cumsum(padded)
    pad_starts = pad_ends - padded
    before = jnp.cumsum(cnt_tiles, axis=0) - cnt_tiles
    segdst = (pad_starts[None, :] + before).reshape(-1).astype(jnp.int32)
    block_start = jnp.arange(NB, dtype=jnp.int32) * BLK
    block_exp = jnp.minimum(jnp.sum(pad_ends[None, :] <= block_start[:, None], axis=1), N_EXPERTS - 1).astype(jnp.int32)
    n_used = (pad_ends[-1] // BLK).astype(jnp.int32).reshape(1)
    paddst = (pad_starts + total).astype(jnp.int32)
    padcnt = (padded - total).astype(jnp.int32)
    return segdst, cnt_tiles.reshape(-1).astype(jnp.int32), paddst, padcnt, block_exp, n_used, NB * BLK


def _rotary_tables(S):
    half = RET_HEAD_DIM // 2
    inv_freq = ROPE_BASE ** (-jnp.arange(half, dtype=F32) / half)
    ang = jnp.arange(S, dtype=F32)[:, None] * inv_freq[None, :]
    cos, sin = jnp.cos(ang), jnp.sin(ang)
    return jnp.concatenate([cos, cos], axis=-1), jnp.concatenate([-sin, sin], axis=-1)


def _layer(h, norm_mix_g, w_in, b_forget, w_out, norm_ffn_g, w_router, b_router,
           w_exp_in, b_exp_in, w_exp_out, b_exp_out, final_g):
    B, S, D = h.shape
    R, Fw = RET_WIDTH, FOX_WIDTH
    cos, sin = _rotary_tables(S)
    wb = w_in.astype(BF16)
    wr = wb[:, :4 * R]
    wfq, wfk, wfv = (wb[:, 4 * R + i * Fw:4 * R + (i + 1) * Fw] for i in range(3))
    wzt = jnp.zeros((16, D), BF16).at[:FOX_HEADS, :].set(wb[:, 4 * R + 3 * Fw:].T)
    rq, rk, rv, rg, kaug, qaug, fvt = _in_proj(
        h, norm_mix_g[None, :], cos, sin, wr, wfk, wfq.T, wfv.T, wzt, b_forget)
    o_ret = _retention(rq, rk, rv, rg)
    o_fox = _fox_attn(qaug, kaug, fvt)

    T = B * S
    wo = w_out.astype(BF16)
    wrt = jnp.zeros((LANES, D), F32).at[:N_EXPERTS, :].set(w_router.T)
    wrh = wrt.astype(BF16)
    wrl = (wrt - wrh.astype(F32)).astype(BF16)
    br = jnp.zeros((LANES, 1), F32).at[:N_EXPERTS, 0].set(b_router)
    h1, u2, sel, cnt = _out_router(h.reshape(T, D), o_ret.reshape(T, R), o_fox.reshape(T, Fw),
                                   wo[:R], wo[R:], norm_ffn_g[None, :], wrh, wrl, br)
    cnt_tiles = cnt[:, 0, :N_EXPERTS].astype(jnp.int32)
    segdst, cnt_flat, paddst, padcnt, block_exp, n_used, n_rows = _routing_tables(cnt_tiles)
    xs = _dispatch(u2, sel, segdst, cnt_flat, paddst, padcnt, n_used, n_rows)
    ys = _experts(xs, block_exp, n_used, w_exp_in, b_exp_in, w_exp_out, b_exp_out)
    out = _combine(ys, sel, h1, final_g[None, :], segdst, cnt_flat)
    return out.reshape(B, S, D)


def kernel(x, norm_mix_g, w_in, b_forget, w_out, norm_ffn_g, w_router, b_router,
           w_exp_in, b_exp_in, w_exp_out, b_exp_out, norm_final_g):
    depth = w_in.shape[0]
    assert depth == 1, "the fused final RMSNorm assumes a single layer"
    return _layer(x, norm_mix_g[0], w_in[0], b_forget[0], w_out[0], norm_ffn_g[0], w_router[0], b_router[0],
                  w_exp_in[0], b_exp_in[0], w_exp_out[0], b_exp_out[0], norm_final_g)
```

```python
import functools

import numpy as np
import jax
import jax.numpy as jnp
from jax import lax
from jax.experimental import pallas as pl
from jax.experimental.pallas import tpu as pltpu

F32 = jnp.float32
BF16 = jnp.bfloat16

D_MODEL = 1024
RET_HEADS, RET_HEAD_DIM = 4, 128
RET_WIDTH = RET_HEADS * RET_HEAD_DIM
FOX_HEADS, FOX_HEAD_DIM = 8, 64
FOX_WIDTH = FOX_HEADS * FOX_HEAD_DIM
CHUNK = 64
ROPE_BASE = 10000.0
N_EXPERTS = 32
TOP_K = 4
D_FF = D_MODEL
SWIGLU_ALPHA = 1.702
SWIGLU_LIMIT = 7.0
RMS_EPS = 1e-5
GN_EPS = 1e-5

LANES = 128
SUBLANES = 8
ROW_TILES = D_MODEL // LANES

PROJ_TILE = 512
RET_BLOCK = 256
FOX_TQ = 512
FOX_TK = PROJ_TILE
FOX_SUB = 256
AUG = 128
V_AUG = 80
LOG2E = 1.4426950408889634
MOE_TILE = 256
ROUTER_TILES = 4
SMALL_RUN = 64
EXPERT_BLOCK = 512
EXPERT_PASS_ROWS = 256

NT_DIMS = (((1,), (1,)), ((), ()))


def _split3(a):
    hi = a.astype(BF16)
    r1 = a - hi.astype(F32)
    mid = r1.astype(BF16)
    lo = (r1 - mid.astype(F32)).astype(BF16)
    return hi, mid, lo


def _dot(a, b):
    return jnp.dot(a, b, preferred_element_type=F32)


def _dot_nt(a, b):
    return lax.dot_general(a, b, NT_DIMS, preferred_element_type=F32)


def _rms(x, g):
    return x * lax.rsqrt(jnp.mean(x * x, axis=-1, keepdims=True) + RMS_EPS) * g


def _in_proj_kernel(x_ref, g_ref, cos_ref, sin_ref, wr_ref, wfk_ref, wfqt_ref, wfvt_ref, wzt_ref, bcol_ref,
                    selkf_ref, constk_ref, selqf_ref, constq_ref,
                    rq_ref, rk_ref, rv_ref, rg_ref, kaug_ref, qaug_ref, fvt_ref, ccol):
    TM = x_ref.shape[1]
    d, H = FOX_HEAD_DIM, FOX_HEADS
    u = _rms(x_ref[0], g_ref[...]).astype(BF16)
    r = _dot(u, wr_ref[...])
    cos, sin = cos_ref[...], sin_ref[...]
    k_scale = RET_HEAD_DIM ** -0.5
    for h in range(RET_HEADS):
        lo = h * RET_HEAD_DIM
        q = r[:, lo:lo + RET_HEAD_DIM]
        k = r[:, RET_WIDTH + lo:RET_WIDTH + lo + RET_HEAD_DIM]
        rq_ref[0, :, lo:lo + RET_HEAD_DIM] = (q * cos + pltpu.roll(q, RET_HEAD_DIM // 2, 1) * sin).astype(BF16)
        rk_ref[0, :, lo:lo + RET_HEAD_DIM] = (
            (k * cos + pltpu.roll(k, RET_HEAD_DIM // 2, 1) * sin) * k_scale).astype(BF16)
    rv_ref[0] = r[:, 2 * RET_WIDTH:3 * RET_WIDTH].astype(BF16)
    rg_ref[0] = r[:, 3 * RET_WIDTH:4 * RET_WIDTH].astype(BF16)
    fvt_ref[0, 0] = _dot_nt(wfvt_ref[...], u).astype(BF16)
    fk = _dot(u, wfk_ref[...])
    fqt = (_dot_nt(wfqt_ref[...], u) * (d ** -0.5 * LOG2E)).astype(BF16)
    zt = _dot_nt(wzt_ref[...], u)

    @pl.when(pl.program_id(1) == 0)
    def _():
        ccol[...] = jnp.zeros_like(ccol)

    row = lax.broadcasted_iota(jnp.int32, (16, TM), 0)
    lft = jnp.where(row < H, jax.nn.log_sigmoid(zt + bcol_ref[...]), 0.0)
    utri = (lax.broadcasted_iota(jnp.int32, (TM, TM), 0) <= lax.broadcasted_iota(jnp.int32, (TM, TM), 1)).astype(BF16)
    t3 = _split3(lft)
    f_col = _dot(t3[0], utri) + _dot(t3[1], utri) + _dot(t3[2], utri) + ccol[:, 0:1]
    ccol[...] = jnp.broadcast_to(f_col[:, TM - 1:TM], ccol.shape)

    pieces_t = jnp.concatenate(_split3(f_col * LOG2E), axis=0)
    for h in range(H):
        extra = _dot(selqf_ref[h], pieces_t) + constq_ref[...]
        qaug_ref[0, h, 0] = jnp.concatenate([fqt[h * d:(h + 1) * d, :], extra.astype(BF16)], axis=0)

    f_row = jnp.concatenate([f_col, jnp.zeros((LANES - 16, TM), F32)], axis=0).T
    n3 = _split3(f_row * -LOG2E)
    pieces = (n3[0].astype(F32) + pltpu.roll(n3[1].astype(F32), H, 1)
              + pltpu.roll(n3[2].astype(F32), 2 * H, 1)).astype(BF16)
    lane = lax.broadcasted_iota(jnp.int32, (TM, LANES), 1)
    for g in range(H // 2):
        bias = _dot(pieces, selkf_ref[g])
        kg = fk[:, g * 2 * d:(g + 1) * 2 * d]
        for o in range(2):
            kh = kg if o == 0 else pltpu.roll(kg, d, 1)
            extra = bias[:, o * AUG:(o + 1) * AUG] + constk_ref[...]
            kaug_ref[0, 2 * g + o] = jnp.where(lane < d, kh, extra).astype(BF16)


def _in_proj(x, g, cos, sin, wr, wfk, wfqt, wfvt, wzt, b_forget):
    B, S, D = x.shape
    TM = PROJ_TILE
    ns = S // TM
    selkf, constk, selqf, constq = _fox_prep_constants()
    bcol = jnp.zeros((16, 1), F32).at[:FOX_HEADS, 0].set(b_forget)
    consts = (wr, wfk, wfqt, wfvt, wzt, bcol, selkf, constk, selqf, constq)
    const = lambda a: pl.BlockSpec(a.shape, lambda b, s: (0,) * a.ndim)
    tok = lambda w: pl.BlockSpec((1, TM, w), lambda b, s: (b, s, 0))
    out_shape = (
        jax.ShapeDtypeStruct((B, S, RET_WIDTH), BF16),
        jax.ShapeDtypeStruct((B, S, RET_WIDTH), BF16),
        jax.ShapeDtypeStruct((B, S, RET_WIDTH), BF16),
        jax.ShapeDtypeStruct((B, S, RET_WIDTH), BF16),
        jax.ShapeDtypeStruct((B, FOX_HEADS, S, AUG), BF16),
        jax.ShapeDtypeStruct((B, FOX_HEADS, ns, AUG, TM), BF16),
        jax.ShapeDtypeStruct((B, ns, FOX_WIDTH, TM), BF16),
    )
    return pl.pallas_call(
        _in_proj_kernel,
        grid=(B, ns),
        in_specs=[
            pl.BlockSpec((1, TM, D), lambda b, s: (b, s, 0)),
            pl.BlockSpec((1, D), lambda b, s: (0, 0)),
            pl.BlockSpec((TM, RET_HEAD_DIM), lambda b, s: (s, 0)),
            pl.BlockSpec((TM, RET_HEAD_DIM), lambda b, s: (s, 0)),
        ] + [const(a) for a in consts],
        out_specs=(
            tok(RET_WIDTH), tok(RET_WIDTH), tok(RET_WIDTH), tok(RET_WIDTH),
            pl.BlockSpec((1, FOX_HEADS, TM, AUG), lambda b, s: (b, 0, s, 0)),
            pl.BlockSpec((1, FOX_HEADS, 1, AUG, TM), lambda b, s: (b, 0, s, 0, 0)),
            pl.BlockSpec((1, 1, FOX_WIDTH, TM), lambda b, s: (b, s, 0, 0)),
        ),
        out_shape=out_shape,
        scratch_shapes=[pltpu.VMEM((16, LANES), F32)],
        compiler_params=pltpu.CompilerParams(
            dimension_semantics=("arbitrary", "arbitrary"), vmem_limit_bytes=48 * 1024 * 1024),
        name="in_proj",
    )(x, g, cos, sin, *consts)


def _fox_prep_constants():
    d, H = FOX_HEAD_DIM, FOX_HEADS
    selkf = np.zeros((H // 2, LANES, 2 * AUG), np.float32)
    constk = np.zeros((1, AUG), np.float32)
    selqf = np.zeros((H, d, 48), np.float32)
    constq = np.zeros((d, 1), np.float32)
    for p in range(3):
        constk[0, d + p] = 1.0
        constq[3 + p, 0] = 1.0
        for h in range(H):
            selkf[h // 2, p * H + h, (h % 2) * AUG + d + 3 + p] = 1.0
            selqf[h, p, p * 16 + h] = 1.0
    return jnp.asarray(selkf, BF16), jnp.asarray(constk, F32), jnp.asarray(selqf, BF16), jnp.asarray(constq, F32)


def _retention_kernel(q_ref, k_ref, v_ref, g_ref, dec_ref, qw_ref, kw_ref, cd_ref, o_ref, state):
    @pl.when(pl.program_id(1) == 0)
    def _():
        state[...] = jnp.zeros_like(state)

    for bb, h in [(bb, h) for bb in range(q_ref.shape[0]) for h in range(RET_HEADS)]:
        hs = slice(h * RET_HEAD_DIM, (h + 1) * RET_HEAD_DIM)
        q, k, v = q_ref[bb, :, hs], k_ref[bb, :, hs], v_ref[bb, :, hs]
        scores = (_dot_nt(q, k) * dec_ref[h]).astype(BF16)
        st = state[bb, h]
        o = _dot(scores, v) + _dot((q.astype(F32) * qw_ref[h]).astype(BF16), st.astype(BF16))
        kk = k.astype(F32) * kw_ref[h]
        state[bb, h] = st * cd_ref[h, 0:1, :] + _dot(kk.T.astype(BF16), v)
        mu = jnp.mean(o, axis=-1, keepdims=True)
        oc = o - mu
        var = jnp.mean(oc * oc, axis=-1, keepdims=True)
        o_ref[bb, :, hs] = (oc * lax.rsqrt(var + GN_EPS) * jax.nn.silu(g_ref[bb, :, hs].astype(F32))).astype(BF16)


def _retention_tables():
    L = RET_BLOCK
    log_gamma = jnp.log1p(-jnp.exp2(-5.0 - jnp.arange(RET_HEADS, dtype=F32)))
    p = jnp.arange(L, dtype=F32)
    dist = jnp.abs(p[:, None] - p[None, :])
    chunk = jnp.arange(L) // CHUNK
    allowed = (chunk[None, :] <= chunk[:, None]).astype(F32)
    dec = jnp.exp(log_gamma[:, None, None] * dist) * allowed
    lanes = lambda a: jnp.broadcast_to(a[:, :, None], (RET_HEADS, L, RET_HEAD_DIM))
    qw = lanes(jnp.exp(log_gamma[:, None] * (p[None, :] + 1.0)))
    kw = lanes(jnp.exp(log_gamma[:, None] * (L - 1.0 - p[None, :])))
    cd = jnp.broadcast_to(jnp.exp(log_gamma * L)[:, None, None], (RET_HEADS, SUBLANES, RET_HEAD_DIM))
    return dec, qw, kw, cd


def _retention(rq, rk, rv, rg):
    B, S, _ = rq.shape
    L = RET_BLOCK
    dec, qw, kw, cd = _retention_tables()
    nb = 2 if B % 2 == 0 else 1
    tok = pl.BlockSpec((nb, L, RET_WIDTH), lambda b, s: (b, s, 0))
    const = lambda a: pl.BlockSpec(a.shape, lambda b, s: (0,) * a.ndim)
    return pl.pallas_call(
        _retention_kernel,
        grid=(B // nb, S // L),
        in_specs=[tok, tok, tok, tok, const(dec), const(qw), const(kw), const(cd)],
        out_specs=tok,
        out_shape=jax.ShapeDtypeStruct((B, S, RET_WIDTH), BF16),
        scratch_shapes=[pltpu.VMEM((nb, RET_HEADS, RET_HEAD_DIM, RET_HEAD_DIM), F32)],
        compiler_params=pltpu.CompilerParams(dimension_semantics=("arbitrary",) * 2),
        name="retention",
    )(rq, rk, rv, rg, dec, qw, kw, cd)


def _fox_attn_kernel(q_ref, k_ref, v_ref, o_ref, s_a, s_b, s_c, cm_a, cm_b, cm_c, m_ref, acc_ref):
    T = FOX_TQ
    d = FOX_HEAD_DIM
    nq = q_ref.shape[2]
    ones_rows = (lax.broadcasted_iota(jnp.int32, (V_AUG - d, T), 0) == 0).astype(BF16)

    def scores(qi, j, s_ref, cm_ref):
        for hh in range(2):
            kj = k_ref[0, hh, pl.ds(pl.multiple_of(j * T, T), T), :]
            st = _dot(kj, q_ref[0, hh, qi])
            s_ref[hh] = st
            cm_ref[hh] = jnp.max(st, axis=0, keepdims=True)

    def consume(j, s_ref, cm_ref, masked):
        for hh in range(2):
            st = s_ref[hh]
            if masked:
                key = lax.broadcasted_iota(jnp.int32, (T, T), 0)
                qry = lax.broadcasted_iota(jnp.int32, (T, T), 1)
                st = jnp.where(key <= qry, st, -jnp.inf)
                cm = jnp.max(st, axis=0, keepdims=True)
            else:
                cm = cm_ref[hh]
            m = m_ref[hh]
            m_new = jnp.maximum(m, cm)
            p = jnp.exp2(st - m_new).astype(BF16)
            vj = jnp.concatenate([v_ref[0, j, hh * d:(hh + 1) * d, :], ones_rows], axis=0)
            acc_ref[hh] = jnp.exp2(m - m_new) * acc_ref[hh] + _dot(vj, p)
            m_ref[hh] = m_new

    def reset():
        m_ref[...] = jnp.full(m_ref.shape, -jnp.inf, F32)
        acc_ref[...] = jnp.zeros(acc_ref.shape, F32)

    def prefetch_next(qi):
        @pl.when(qi + 1 < nq)
        def _():
            scores(qi + 1, 0, s_c, cm_c)

    def finish(qi):
        outs = [acc_ref[hh, 0:d, :] / acc_ref[hh, d:d + 1, :] for hh in range(2)]
        o_ref[0, pl.ds(pl.multiple_of(qi * T, T), T), :] = jnp.concatenate(outs, axis=0).T.astype(BF16)

    reset()
    scores(0, 0, s_a, cm_a)
    prefetch_next(0)
    consume(0, s_a, cm_a, True)
    finish(0)

    def query_tile(qi, carry):
        reset()
        scores(qi, 1, s_a, cm_a)
        consume(0, s_c, cm_c, False)

        def pair(j):
            scores(qi, j + 1, s_b, cm_b)
            consume(j, s_a, cm_a, False)
            scores(qi, j + 2, s_a, cm_a)
            consume(j + 1, s_b, cm_b, False)

        def two_pairs(jj, c):
            pair(1 + 4 * jj)
            pair(3 + 4 * jj)
            return c

        def one_pair(jj, c):
            pair(1 + 4 * (n_pairs // 2) + 2 * jj)
            return c

        n_pairs = (qi - 1) // 2
        lax.fori_loop(0, n_pairs // 2, two_pairs, 0)
        lax.fori_loop(0, n_pairs % 2, one_pair, 0)

        @pl.when(qi % 2 == 1)
        def _():
            prefetch_next(qi)
            consume(qi, s_a, cm_a, True)

        @pl.when(qi % 2 == 0)
        def _():
            scores(qi, qi, s_b, cm_b)
            consume(qi - 1, s_a, cm_a, False)
            prefetch_next(qi)
            consume(qi, s_b, cm_b, True)

        finish(qi)
        return carry

    lax.fori_loop(1, nq, query_tile, 0)


def _fox_attn(qaug, kaug, fvt):
    B, H, S, _ = kaug.shape
    nk = S // FOX_TK
    nq = S // FOX_TQ
    score_buf = pltpu.VMEM((2, FOX_TK, FOX_TQ), F32)
    col_max = pltpu.VMEM((2, 1, FOX_TQ), F32)
    return pl.pallas_call(
        _fox_attn_kernel,
        grid=(B, H // 2),
        in_specs=[
            pl.BlockSpec((1, 2, nq, AUG, FOX_TQ), lambda b, p: (b, p, 0, 0, 0)),
            pl.BlockSpec((1, 2, S, AUG), lambda b, p: (b, p, 0, 0)),
            pl.BlockSpec((1, nk, 2 * FOX_HEAD_DIM, FOX_TK), lambda b, p: (b, 0, p, 0)),
        ],
        out_specs=pl.BlockSpec((1, S, 2 * FOX_HEAD_DIM), lambda b, p: (b, 0, p)),
        out_shape=jax.ShapeDtypeStruct((B, S, FOX_WIDTH), BF16),
        scratch_shapes=[
            score_buf, score_buf, score_buf, col_max, col_max, col_max,
            pltpu.VMEM((2, 1, FOX_TQ), F32), pltpu.VMEM((2, V_AUG, FOX_TQ), F32),
        ],
        compiler_params=pltpu.CompilerParams(
            dimension_semantics=("arbitrary",) * 2, vmem_limit_bytes=48 * 1024 * 1024),
        name="fox_attn",
    )(qaug, kaug, fvt)


def _out_router_kernel(x_ref, oret_ref, ofox_ref, wor_ref, wof_ref, g_ref, wrh_ref, wrl_ref, br_ref,
                       h1_ref, u2_ref, sel_ref, cnt_ref):
    TM = MOE_TILE
    rows = lambda t: slice(t * TM, (t + 1) * TM)

    def out_proj(t):
        rs = rows(t)
        h1 = x_ref[rs] + _dot(oret_ref[rs], wor_ref[...]) + _dot(ofox_ref[rs], wof_ref[...])
        h1_ref[rs] = h1
        return h1

    def router_logits(t, h1):
        u2 = _rms(h1, g_ref[...])
        uh = u2.astype(BF16)
        u2_ref[rows(t)] = uh
        ul = (u2 - uh.astype(F32)).astype(BF16)
        return (_dot_nt(wrh_ref[...], uh) + _dot_nt(wrh_ref[...], ul) + _dot_nt(wrl_ref[...], uh)
                + br_ref[...])

    def top_k(t, logits):
        rs = rows(t)
        row = lax.broadcasted_iota(jnp.int32, (LANES, TM), 0).astype(F32)
        l = jnp.where(row < N_EXPERTS, logits, -jnp.inf)
        picks, vals = [], []
        for _ in range(TOP_K):
            m = jnp.max(l, axis=0, keepdims=True)
            idx = jnp.min(jnp.where(l == m, row, float(LANES)), axis=0, keepdims=True)
            pick = row == idx
            picks.append(pick)
            vals.append(m)
            l = jnp.where(pick, -jnp.inf, l)
        exps = [jnp.exp(v - vals[0]) for v in vals]
        den = exps[0] + exps[1] + exps[2] + exps[3]
        sel_t = jnp.full((LANES, TM), -1.0, F32)
        for pick, e in zip(picks, exps):
            sel_t = jnp.where(pick, e / den, sel_t)
        sel = sel_t.T
        sel_ref[rs] = sel
        cnt = jnp.sum((sel >= 0.0).astype(F32), axis=0, keepdims=True)
        cnt_ref[t] = jnp.broadcast_to(cnt, (SUBLANES, LANES))

    n = ROUTER_TILES
    h1s, lgs = {0: out_proj(0)}, {}
    for t in range(1, n + 2):
        if t < n:
            h1s[t] = out_proj(t)
        if 1 <= t <= n:
            lgs[t - 1] = router_logits(t - 1, h1s.pop(t - 1))
        if t >= 2:
            top_k(t - 2, lgs.pop(t - 2))


def _out_router(x2, o_ret, o_fox, wor, wof, g, wrh, wrl, br):
    T, D = x2.shape
    TM = MOE_TILE * ROUTER_TILES
    nT = T // MOE_TILE
    const = lambda a: pl.BlockSpec(a.shape, lambda i: (0,) * a.ndim)
    tok = lambda w: pl.BlockSpec((TM, w), lambda i: (i, 0))
    return pl.pallas_call(
        _out_router_kernel,
        grid=(T // TM,),
        in_specs=[tok(D), tok(RET_WIDTH), tok(FOX_WIDTH), const(wor), const(wof), const(g),
                  const(wrh), const(wrl), const(br)],
        out_specs=(tok(D), tok(D), tok(LANES),
                   pl.BlockSpec((ROUTER_TILES, SUBLANES, LANES), lambda i: (i, 0, 0))),
        out_shape=(
            jax.ShapeDtypeStruct((T, D), F32),
            jax.ShapeDtypeStruct((T, D), BF16),
            jax.ShapeDtypeStruct((T, LANES), F32),
            jax.ShapeDtypeStruct((nT, SUBLANES, LANES), F32),
        ),
        compiler_params=pltpu.CompilerParams(dimension_semantics=("arbitrary",)),
        name="out_router",
    )(x2, o_ret, o_fox, wor, wof, g, wrh, wrl, br)


def _tile_sort(sel):
    TM = sel.shape[0]
    NS = TOP_K * TM
    maskf = (sel >= 0.0).astype(F32)
    mask = maskf.astype(BF16)
    ri = lax.broadcasted_iota(jnp.int32, (TM, TM), 0)
    ci = lax.broadcasted_iota(jnp.int32, (TM, TM), 1)
    rank = _dot((ri > ci).astype(BF16), mask)
    cnt = jnp.sum(maskf, axis=0, keepdims=True)
    ei = lax.broadcasted_iota(jnp.int32, (LANES, LANES), 0)
    ej = lax.broadcasted_iota(jnp.int32, (LANES, LANES), 1)
    cnt8 = jnp.broadcast_to(cnt, (SUBLANES, LANES)).astype(BF16)
    off = _dot(cnt8, (ei < ej).astype(BF16))[0:1, :]
    slot = lax.broadcasted_iota(jnp.int32, (NS, LANES), 0).astype(F32)
    esel = ((slot >= off) & (slot < off + cnt)).astype(BF16)
    return mask, rank.astype(BF16), esel, off, cnt


def _segment_wait(slot, local, remote_rows, sem, to_remote):
    whole = local.at[slot]
    rem = remote_rows.at[pl.ds(0, whole.shape[0]), :]
    cp = (pltpu.make_async_copy(whole, rem, sem.at[slot]) if to_remote
          else pltpu.make_async_copy(rem, whole, sem.at[slot]))
    cp.wait()


def _segment_dmas(step, slot, segdst_ref, cnt_ref, big_ref, local, remote_rows, sem, to_remote):
    big = big_ref[step] != 0
    for cond, top_bit in ((big, MOE_TILE), (jnp.logical_not(big), SMALL_RUN // 2)):
        pl.when(cond)(functools.partial(
            _segment_dma_path, step, slot, segdst_ref, cnt_ref, local, remote_rows, sem, to_remote, top_bit))


def _segment_dma_path(step, slot, segdst_ref, cnt_ref, local, remote_rows, sem, to_remote, top_bit):
    def body(e, off):
        c = cnt_ref[step * N_EXPERTS + e]
        dst = segdst_ref[step * N_EXPERTS + e]
        bit = top_bit
        while bit >= 1:
            done = c & (~(2 * bit - 1))

            @pl.when((c & bit) != 0)
            def _(bit=bit, done=done):
                loc = local.at[slot, pl.ds((off + done) * ROW_TILES, bit * ROW_TILES), :]
                rem = remote_rows.at[pl.ds((dst + done) * ROW_TILES, bit * ROW_TILES), :]
                cp = (pltpu.make_async_copy(loc, rem, sem.at[slot]) if to_remote
                      else pltpu.make_async_copy(rem, loc, sem.at[slot]))
                cp.start()
            bit //= 2
        return off + c

    off = 0
    for e in range(N_EXPERTS):
        off = body(e, off)


def _dispatch_kernel(segdst_ref, cnt_ref, big_ref, paddst_ref, padcnt_ref, nused_ref, u2_ref, sel_ref, xs_ref,
                     buf, zbuf, sems, zsem):
    i = pl.program_id(0)
    last = pl.num_programs(0) - 1
    slot = i % 2
    TM = MOE_TILE
    NS = TOP_K * TM
    mask, rank, esel, off, _ = _tile_sort(sel_ref[...])
    slot_id = lax.broadcasted_iota(jnp.int32, (NS, 1), 0).astype(F32)
    r_s = slot_id - jnp.sum(esel.astype(F32) * off, axis=1, keepdims=True)
    perm = ((_dot_nt(esel, mask) > 0.5) & (_dot_nt(esel, rank) == r_s)).astype(BF16)

    @pl.when(i >= 2)
    def _():
        _segment_wait(slot, buf, xs_ref, sems, True)

    u2 = u2_ref[...]
    for c in range(NS // TM):
        rows = _dot(perm[c * TM:(c + 1) * TM], u2)
        for j in range(ROW_TILES):
            buf[slot, pl.ds(c * TM * ROW_TILES + j, TM, stride=ROW_TILES), :] = rows[:, j * LANES:(j + 1) * LANES]
    _segment_dmas(i, slot, segdst_ref, cnt_ref, big_ref, buf, xs_ref, sems, True)

    @pl.when(i == last)
    def _():
        @pl.when(i >= 1)
        def _():
            _segment_wait(1 - slot, buf, xs_ref, sems, True)
        _segment_wait(slot, buf, xs_ref, sems, True)
        zbuf[...] = jnp.zeros_like(zbuf)
        half = EXPERT_BLOCK // 2 * ROW_TILES
        n_blocks = xs_ref.shape[0] // (EXPERT_BLOCK * ROW_TILES)
        for wait in (False, True):
            def unused(hb, carry, wait=wait):
                cp = pltpu.make_async_copy(zbuf, xs_ref.at[pl.ds(hb * half, half), :], zsem.at[0])
                cp.wait() if wait else cp.start()
                return carry
            lax.fori_loop(2 * nused_ref[0], 2 * n_blocks, unused, 0)


            def body(e, carry, wait=wait):
                c = padcnt_ref[e]
                dst = paddst_ref[e]
                bit = EXPERT_BLOCK // 2
                while bit >= 1:
                    done = c & (~(2 * bit - 1))

                    @pl.when((c & bit) != 0)
                    def _(bit=bit, done=done):
                        cp = pltpu.make_async_copy(
                            zbuf.at[pl.ds(0, bit * ROW_TILES), :],
                            xs_ref.at[pl.ds((dst + done) * ROW_TILES, bit * ROW_TILES), :], zsem.at[0])
                        cp.wait() if wait else cp.start()
                    bit //= 2
                return carry
            lax.fori_loop(0, N_EXPERTS, body, 0)


def _dispatch(u2, sel, segdst, cnt, big, paddst, padcnt, n_used, n_rows):
    T, D = u2.shape
    TM = MOE_TILE
    NS = TOP_K * TM
    return pl.pallas_call(
        _dispatch_kernel,
        grid_spec=pltpu.PrefetchScalarGridSpec(
            num_scalar_prefetch=6,
            grid=(T // TM,),
            in_specs=[pl.BlockSpec((TM, D), lambda i, *_: (i, 0)),
                      pl.BlockSpec((TM, LANES), lambda i, *_: (i, 0))],
            out_specs=pl.BlockSpec(memory_space=pl.ANY),
            scratch_shapes=[pltpu.VMEM((2, NS * ROW_TILES, LANES), F32),
                            pltpu.VMEM((EXPERT_BLOCK // 2 * ROW_TILES, LANES), F32),
                            pltpu.SemaphoreType.DMA((2,)), pltpu.SemaphoreType.DMA((1,))],
        ),
        out_shape=jax.ShapeDtypeStruct((n_rows * ROW_TILES, LANES), F32),
        compiler_params=pltpu.CompilerParams(
            dimension_semantics=("arbitrary",), vmem_limit_bytes=48 * 1024 * 1024),
        name="dispatch",
    )(segdst, cnt, big, paddst, padcnt, n_used, u2, sel)


def _expert_kernel(bexp_ref, nused_ref, xs_ref, w1_ref, b1_ref, w2_ref, b2_ref, ys_ref, w1b, w2b):
    b = pl.program_id(0)
    BLK = EXPERT_BLOCK
    used = b < nused_ref[0]

    @pl.when(used)
    def _():
        e = bexp_ref[b]
        prev = bexp_ref[jnp.maximum(b - 1, 0)]

        @pl.when((b == 0) | (e != prev))
        def _():
            rows = 128

            def cast(r, carry):
                sl = pl.ds(pl.multiple_of(r * rows, rows), rows)
                w1b[sl, :] = w1_ref[0, sl, :].astype(BF16)
                w2b[sl, :] = w2_ref[0, sl, :].astype(BF16)
                return carry
            lax.fori_loop(0, D_MODEL // rows, cast, 0)

        R = EXPERT_PASS_ROWS
        for rp in range(BLK // R):
            r0 = rp * R * ROW_TILES
            x = jnp.concatenate([xs_ref[pl.ds(r0 + j, R, stride=ROW_TILES), :] for j in range(ROW_TILES)],
                                axis=1).astype(BF16)
            h = _dot(x, w1b[...]) + b1_ref[0]
            glu = jnp.minimum(h[:, :D_FF], SWIGLU_LIMIT)
            lin = jnp.clip(h[:, D_FF:], -SWIGLU_LIMIT, SWIGLU_LIMIT)
            act = glu * jax.nn.sigmoid(SWIGLU_ALPHA * glu) * (lin + 1.0)
            y = _dot(act.astype(BF16), w2b[...]) + b2_ref[0]
            for j in range(ROW_TILES):
                ys_ref[pl.ds(r0 + j, R, stride=ROW_TILES), :] = y[:, j * LANES:(j + 1) * LANES]

    @pl.when(jnp.logical_not(used))
    def _():
        ys_ref[...] = jnp.zeros_like(ys_ref)


def _experts(xs, block_exp, n_used, w1, b1, w2, b2):
    BLK = EXPERT_BLOCK
    NB = xs.shape[0] // (BLK * ROW_TILES)
    blk = lambda b, nused: jnp.minimum(b, nused[0] - 1)
    return pl.pallas_call(
        _expert_kernel,
        grid_spec=pltpu.PrefetchScalarGridSpec(
            num_scalar_prefetch=2,
            grid=(NB,),
            in_specs=[
                pl.BlockSpec((BLK * ROW_TILES, LANES), lambda b, bexp, nused: (blk(b, nused), 0)),
                pl.BlockSpec((1, D_MODEL, 2 * D_FF), lambda b, bexp, nused: (bexp[blk(b, nused)], 0, 0)),
                pl.BlockSpec((1, 1, 2 * D_FF), lambda b, bexp, nused: (bexp[blk(b, nused)], 0, 0)),
                pl.BlockSpec((1, D_FF, D_MODEL), lambda b, bexp, nused: (bexp[blk(b, nused)], 0, 0)),
                pl.BlockSpec((1, 1, D_MODEL), lambda b, bexp, nused: (bexp[blk(b, nused)], 0, 0)),
            ],
            out_specs=pl.BlockSpec((BLK * ROW_TILES, LANES), lambda b, bexp, nused: (b, 0)),
            scratch_shapes=[pltpu.VMEM((D_MODEL, 2 * D_FF), BF16), pltpu.VMEM((D_FF, D_MODEL), BF16)],
        ),
        out_shape=jax.ShapeDtypeStruct(xs.shape, F32),
        compiler_params=pltpu.CompilerParams(
            dimension_semantics=("arbitrary",), vmem_limit_bytes=56 * 1024 * 1024),
        name="experts",
    )(block_exp, n_used, xs, w1, b1[:, None, :], w2, b2[:, None, :])


def _combine_kernel(segdst_ref, cnt_ref, big_ref, ys_ref, sel_ref, h1_ref, g_ref, out_ref, buf, sems):
    i = pl.program_id(0)
    n = pl.num_programs(0)
    slot = i % 2
    TM = MOE_TILE
    NS = TOP_K * TM

    @pl.when(i == 0)
    def _():
        _segment_dmas(i, slot, segdst_ref, cnt_ref, big_ref, buf, ys_ref, sems, False)

    @pl.when(i + 1 < n)
    def _():
        _segment_dmas(i + 1, 1 - slot, segdst_ref, cnt_ref, big_ref, buf, ys_ref, sems, False)

    sel = sel_ref[...]
    mask, rank, esel, off, _ = _tile_sort(sel)
    gate = jnp.maximum(sel, 0.0)
    gh = gate.astype(BF16)
    gl = (gate - gh.astype(F32)).astype(BF16)
    o3 = _split3(jnp.broadcast_to(off, (SUBLANES, LANES)))
    off_s = (_dot_nt(o3[0], esel) + _dot_nt(o3[1], esel) + _dot_nt(o3[2], esel))[0:1, :]
    r_s = lax.broadcasted_iota(jnp.int32, (1, NS), 1).astype(F32) - off_s
    hit = (_dot_nt(mask, esel) > 0.5) & (_dot_nt(rank, esel) == r_s)
    unperm = jnp.where(hit, _dot_nt(gh, esel) + _dot_nt(gl, esel), 0.0).astype(BF16)

    _segment_wait(slot, buf, ys_ref, sems, False)
    y = jnp.concatenate([buf[slot, pl.ds(j, NS, stride=ROW_TILES), :] for j in range(ROW_TILES)],
                        axis=1).astype(BF16)
    h2 = h1_ref[...] + _dot(unperm, y)
    out_ref[...] = _rms(h2, g_ref[...])


def _combine(ys, sel, h1, g, segdst, cnt, big):
    T, D = h1.shape
    TM = MOE_TILE
    NS = TOP_K * TM
    return pl.pallas_call(
        _combine_kernel,
        grid_spec=pltpu.PrefetchScalarGridSpec(
            num_scalar_prefetch=3,
            grid=(T // TM,),
            in_specs=[pl.BlockSpec(memory_space=pl.ANY),
                      pl.BlockSpec((TM, LANES), lambda i, *_: (i, 0)),
                      pl.BlockSpec((TM, D), lambda i, *_: (i, 0)),
                      pl.BlockSpec((1, D), lambda i, *_: (0, 0))],
            out_specs=pl.BlockSpec((TM, D), lambda i, *_: (i, 0)),
            scratch_shapes=[pltpu.VMEM((2, NS * ROW_TILES, LANES), F32), pltpu.SemaphoreType.DMA((2,))],
        ),
        out_shape=jax.ShapeDtypeStruct((T, D), F32),
        compiler_params=pltpu.CompilerParams(
            dimension_semantics=("arbitrary",), vmem_limit_bytes=48 * 1024 * 1024),
        name="combine",
    )(segdst, cnt, big, ys, sel, h1, g)


def _routing_tables(cnt_tiles):
    BLK = EXPERT_BLOCK
    nT = cnt_tiles.shape[0]
    A = nT * MOE_TILE * TOP_K
    NB = A // BLK + N_EXPERTS
    total = jnp.sum(cnt_tiles, axis=0)
    padded = (total + BLK - 1) // BLK * BLK
    pad_ends = jnp.cumsum(padded)
    pad_starts = pad_ends - padded
    before = jnp.cumsum(cnt_tiles, axis=0) - cnt_tiles
    segdst = (pad_starts[None, :] + before).reshape(-1).astype(jnp.int32)
    block_start = jnp.arange(NB, dtype=jnp.int32) * BLK
    block_exp = jnp.minimum(jnp.sum(pad_ends[None, :] <= block_start[:, None], axis=1), N_EXPERTS - 1).astype(jnp.int32)
    n_used = (pad_ends[-1] // BLK).astype(jnp.int32).reshape(1)
    paddst = (pad_starts + total).astype(jnp.int32)
    padcnt = (padded - total).astype(jnp.int32)
    big = jnp.any(cnt_tiles >= SMALL_RUN, axis=1).astype(jnp.int32)
    return segdst, cnt_tiles.reshape(-1).astype(jnp.int32), big, paddst, padcnt, block_exp, n_used, NB * BLK


def _rotary_tables(S):
    half = RET_HEAD_DIM // 2
    inv_freq = ROPE_BASE ** (-jnp.arange(half, dtype=F32) / half)
    ang = jnp.arange(S, dtype=F32)[:, None] * inv_freq[None, :]
    cos, sin = jnp.cos(ang), jnp.sin(ang)
    return jnp.concatenate([cos, cos], axis=-1), jnp.concatenate([-sin, sin], axis=-1)


def _layer(h, norm_mix_g, w_in, b_forget, w_out, norm_ffn_g, w_router, b_router,
           w_exp_in, b_exp_in, w_exp_out, b_exp_out, final_g):
    B, S, D = h.shape
    R, Fw = RET_WIDTH, FOX_WIDTH
    cos, sin = _rotary_tables(S)
    wb = w_in.astype(BF16)
    wr = wb[:, :4 * R]
    wfq, wfk, wfv = (wb[:, 4 * R + i * Fw:4 * R + (i + 1) * Fw] for i in range(3))
    wzt = jnp.zeros((16, D), BF16).at[:FOX_HEADS, :].set(wb[:, 4 * R + 3 * Fw:].T)
    rq, rk, rv, rg, kaug, qaug, fvt = _in_proj(
        h, norm_mix_g[None, :], cos, sin, wr, wfk, wfq.T, wfv.T, wzt, b_forget)
    o_ret = _retention(rq, rk, rv, rg)
    o_fox = _fox_attn(qaug, kaug, fvt)

    T = B * S
    wo = w_out.astype(BF16)
    wrt = jnp.zeros((LANES, D), F32).at[:N_EXPERTS, :].set(w_router.T)
    wrh = wrt.astype(BF16)
    wrl = (wrt - wrh.astype(F32)).astype(BF16)
    br = jnp.zeros((LANES, 1), F32).at[:N_EXPERTS, 0].set(b_router)
    h1, u2, sel, cnt = _out_router(h.reshape(T, D), o_ret.reshape(T, R), o_fox.reshape(T, Fw),
                                   wo[:R], wo[R:], norm_ffn_g[None, :], wrh, wrl, br)
    cnt_tiles = cnt[:, 0, :N_EXPERTS].astype(jnp.int32)
    segdst, cnt_flat, big, paddst, padcnt, block_exp, n_used, n_rows = _routing_tables(cnt_tiles)
    xs = _dispatch(u2, sel, segdst, cnt_flat, big, paddst, padcnt, n_used, n_rows)
    ys = _experts(xs, block_exp, n_used, w_exp_in, b_exp_in, w_exp_out, b_exp_out)
    out = _combine(ys, sel, h1, final_g[None, :], segdst, cnt_flat, big)
    return out.reshape(B, S, D)


def kernel(x, norm_mix_g, w_in, b_forget, w_out, norm_ffn_g, w_router, b_router,
           w_exp_in, b_exp_in, w_exp_out, b_exp_out, norm_final_g):
    depth = w_in.shape[0]
    assert depth == 1, "the fused final RMSNorm assumes a single layer"
    return _layer(x, norm_mix_g[0], w_in[0], b_forget[0], w_out[0], norm_ffn_g[0], w_router[0], b_router[0],
                  w_exp_in[0], b_exp_in[0], w_exp_out[0], b_exp_out[0], norm_final_g)
```

```python
import functools

import numpy as np
import jax
import jax.numpy as jnp
from jax import lax
from jax.experimental import pallas as pl
from jax.experimental.pallas import tpu as pltpu

F32 = jnp.float32
BF16 = jnp.bfloat16

D_MODEL = 1024
RET_HEADS, RET_HEAD_DIM = 4, 128
RET_WIDTH = RET_HEADS * RET_HEAD_DIM
FOX_HEADS, FOX_HEAD_DIM = 8, 64
FOX_WIDTH = FOX_HEADS * FOX_HEAD_DIM
CHUNK = 64
ROPE_BASE = 10000.0
N_EXPERTS = 32
TOP_K = 4
D_FF = D_MODEL
SWIGLU_ALPHA = 1.702
SWIGLU_LIMIT = 7.0
RMS_EPS = 1e-5
GN_EPS = 1e-5

LANES = 128
SUBLANES = 8
ROW_TILES = D_MODEL // LANES

PROJ_TILE = 512
RET_BLOCK = 256
FOX_TQ = 512
FOX_TK = PROJ_TILE
FOX_SUB = 256
AUG = 128
V_AUG = 80
LOG2E = 1.4426950408889634
MOE_TILE = 256
ROUTER_TILES = 4
SMALL_RUN = 64
EXPERT_BLOCK = 512
EXPERT_PASS_ROWS = 256

NT_DIMS = (((1,), (1,)), ((), ()))


def _split3(a):
    hi = a.astype(BF16)
    r1 = a - hi.astype(F32)
    mid = r1.astype(BF16)
    lo = (r1 - mid.astype(F32)).astype(BF16)
    return hi, mid, lo


def _dot(a, b):
    return jnp.dot(a, b, preferred_element_type=F32)


def _dot_nt(a, b):
    return lax.dot_general(a, b, NT_DIMS, preferred_element_type=F32)


def _rms(x, g):
    return x * lax.rsqrt(jnp.mean(x * x, axis=-1, keepdims=True) + RMS_EPS) * g


def _in_proj_kernel(x_ref, g_ref, cos_ref, sin_ref, wr_ref, wfk_ref, wfqt_ref, wfvt_ref, wzt_ref, bcol_ref,
                    selkf_ref, constk_ref, selqf_ref, constq_ref,
                    rq_ref, rk_ref, rv_ref, rg_ref, kaug_ref, qaug_ref, fvt_ref, ccol):
    TM = x_ref.shape[1]
    d, H = FOX_HEAD_DIM, FOX_HEADS
    u = _rms(x_ref[0], g_ref[...]).astype(BF16)
    r = _dot(u, wr_ref[...])
    cos, sin = cos_ref[...], sin_ref[...]
    k_scale = RET_HEAD_DIM ** -0.5
    for h in range(RET_HEADS):
        lo = h * RET_HEAD_DIM
        q = r[:, lo:lo + RET_HEAD_DIM]
        k = r[:, RET_WIDTH + lo:RET_WIDTH + lo + RET_HEAD_DIM]
        rq_ref[0, :, lo:lo + RET_HEAD_DIM] = (q * cos + pltpu.roll(q, RET_HEAD_DIM // 2, 1) * sin).astype(BF16)
        rk_ref[0, :, lo:lo + RET_HEAD_DIM] = (
            (k * cos + pltpu.roll(k, RET_HEAD_DIM // 2, 1) * sin) * k_scale).astype(BF16)
    rv_ref[0] = r[:, 2 * RET_WIDTH:3 * RET_WIDTH].astype(BF16)
    rg_ref[0] = r[:, 3 * RET_WIDTH:4 * RET_WIDTH].astype(BF16)
    fvt_ref[0, 0] = _dot_nt(wfvt_ref[...], u).astype(BF16)
    fk = _dot(u, wfk_ref[...])
    fqt = (_dot_nt(wfqt_ref[...], u) * (d ** -0.5 * LOG2E)).astype(BF16)
    zt = _dot_nt(wzt_ref[...], u)

    @pl.when(pl.program_id(1) == 0)
    def _():
        ccol[...] = jnp.zeros_like(ccol)

    row = lax.broadcasted_iota(jnp.int32, (16, TM), 0)
    lft = jnp.where(row < H, jax.nn.log_sigmoid(zt + bcol_ref[...]), 0.0)
    utri = (lax.broadcasted_iota(jnp.int32, (TM, TM), 0) <= lax.broadcasted_iota(jnp.int32, (TM, TM), 1)).astype(BF16)
    t3 = _split3(lft)
    f_col = _dot(t3[0], utri) + _dot(t3[1], utri) + _dot(t3[2], utri) + ccol[:, 0:1]
    ccol[...] = jnp.broadcast_to(f_col[:, TM - 1:TM], ccol.shape)

    pieces_t = jnp.concatenate(_split3(f_col * LOG2E), axis=0)
    for h in range(H):
        extra = _dot(selqf_ref[h], pieces_t) + constq_ref[...]
        qaug_ref[0, h, 0] = jnp.concatenate([fqt[h * d:(h + 1) * d, :], extra.astype(BF16)], axis=0)

    f_row = jnp.concatenate([f_col, jnp.zeros((LANES - 16, TM), F32)], axis=0).T
    n3 = _split3(f_row * -LOG2E)
    pieces = (n3[0].astype(F32) + pltpu.roll(n3[1].astype(F32), H, 1)
              + pltpu.roll(n3[2].astype(F32), 2 * H, 1)).astype(BF16)
    lane = lax.broadcasted_iota(jnp.int32, (TM, LANES), 1)
    for g in range(H // 2):
        bias = _dot(pieces, selkf_ref[g])
        kg = fk[:, g * 2 * d:(g + 1) * 2 * d]
        for o in range(2):
            kh = kg if o == 0 else pltpu.roll(kg, d, 1)
            extra = bias[:, o * AUG:(o + 1) * AUG] + constk_ref[...]
            kaug_ref[0, 2 * g + o] = jnp.where(lane < d, kh, extra).astype(BF16)


def _in_proj(x, g, cos, sin, wr, wfk, wfqt, wfvt, wzt, b_forget):
    B, S, D = x.shape
    TM = PROJ_TILE
    ns = S // TM
    selkf, constk, selqf, constq = _fox_prep_constants()
    bcol = jnp.zeros((16, 1), F32).at[:FOX_HEADS, 0].set(b_forget)
    consts = (wr, wfk, wfqt, wfvt, wzt, bcol, selkf, constk, selqf, constq)
    const = lambda a: pl.BlockSpec(a.shape, lambda b, s: (0,) * a.ndim)
    tok = lambda w: pl.BlockSpec((1, TM, w), lambda b, s: (b, s, 0))
    out_shape = (
        jax.ShapeDtypeStruct((B, S, RET_WIDTH), BF16),
        jax.ShapeDtypeStruct((B, S, RET_WIDTH), BF16),
        jax.ShapeDtypeStruct((B, S, RET_WIDTH), BF16),
        jax.ShapeDtypeStruct((B, S, RET_WIDTH), BF16),
        jax.ShapeDtypeStruct((B, FOX_HEADS, S, AUG), BF16),
        jax.ShapeDtypeStruct((B, FOX_HEADS, ns, AUG, TM), BF16),
        jax.ShapeDtypeStruct((B, ns, FOX_WIDTH, TM), BF16),
    )
    return pl.pallas_call(
        _in_proj_kernel,
        grid=(B, ns),
        in_specs=[
            pl.BlockSpec((1, TM, D), lambda b, s: (b, s, 0)),
            pl.BlockSpec((1, D), lambda b, s: (0, 0)),
            pl.BlockSpec((TM, RET_HEAD_DIM), lambda b, s: (s, 0)),
            pl.BlockSpec((TM, RET_HEAD_DIM), lambda b, s: (s, 0)),
        ] + [const(a) for a in consts],
        out_specs=(
            tok(RET_WIDTH), tok(RET_WIDTH), tok(RET_WIDTH), tok(RET_WIDTH),
            pl.BlockSpec((1, FOX_HEADS, TM, AUG), lambda b, s: (b, 0, s, 0)),
            pl.BlockSpec((1, FOX_HEADS, 1, AUG, TM), lambda b, s: (b, 0, s, 0, 0)),
            pl.BlockSpec((1, 1, FOX_WIDTH, TM), lambda b, s: (b, s, 0, 0)),
        ),
        out_shape=out_shape,
        scratch_shapes=[pltpu.VMEM((16, LANES), F32)],
        compiler_params=pltpu.CompilerParams(
            dimension_semantics=("arbitrary", "arbitrary"), vmem_limit_bytes=48 * 1024 * 1024),
        name="in_proj",
    )(x, g, cos, sin, *consts)


def _fox_prep_constants():
    d, H = FOX_HEAD_DIM, FOX_HEADS
    selkf = np.zeros((H // 2, LANES, 2 * AUG), np.float32)
    constk = np.zeros((1, AUG), np.float32)
    selqf = np.zeros((H, d, 48), np.float32)
    constq = np.zeros((d, 1), np.float32)
    for p in range(3):
        constk[0, d + p] = 1.0
        constq[3 + p, 0] = 1.0
        for h in range(H):
            selkf[h // 2, p * H + h, (h % 2) * AUG + d + 3 + p] = 1.0
            selqf[h, p, p * 16 + h] = 1.0
    return jnp.asarray(selkf, BF16), jnp.asarray(constk, F32), jnp.asarray(selqf, BF16), jnp.asarray(constq, F32)


def _retention_kernel(q_ref, k_ref, v_ref, g_ref, dec_ref, qw_ref, kw_ref, cd_ref, o_ref, state):
    @pl.when(pl.program_id(1) == 0)
    def _():
        state[...] = jnp.zeros_like(state)

    for bb, h in [(bb, h) for bb in range(q_ref.shape[0]) for h in range(RET_HEADS)]:
        hs = slice(h * RET_HEAD_DIM, (h + 1) * RET_HEAD_DIM)
        q, k, v = q_ref[bb, :, hs], k_ref[bb, :, hs], v_ref[bb, :, hs]
        scores = (_dot_nt(q, k) * dec_ref[h]).astype(BF16)
        st = state[bb, h]
        o = _dot(scores, v) + _dot((q.astype(F32) * qw_ref[h]).astype(BF16), st.astype(BF16))
        kk = k.astype(F32) * kw_ref[h]
        state[bb, h] = st * cd_ref[h, 0:1, :] + _dot(kk.T.astype(BF16), v)
        mu = jnp.mean(o, axis=-1, keepdims=True)
        oc = o - mu
        var = jnp.mean(oc * oc, axis=-1, keepdims=True)
        o_ref[bb, :, hs] = (oc * lax.rsqrt(var + GN_EPS) * jax.nn.silu(g_ref[bb, :, hs].astype(F32))).astype(BF16)


def _retention_tables():
    L = RET_BLOCK
    log_gamma = jnp.log1p(-jnp.exp2(-5.0 - jnp.arange(RET_HEADS, dtype=F32)))
    p = jnp.arange(L, dtype=F32)
    dist = jnp.abs(p[:, None] - p[None, :])
    chunk = jnp.arange(L) // CHUNK
    allowed = (chunk[None, :] <= chunk[:, None]).astype(F32)
    dec = jnp.exp(log_gamma[:, None, None] * dist) * allowed
    lanes = lambda a: jnp.broadcast_to(a[:, :, None], (RET_HEADS, L, RET_HEAD_DIM))
    qw = lanes(jnp.exp(log_gamma[:, None] * (p[None, :] + 1.0)))
    kw = lanes(jnp.exp(log_gamma[:, None] * (L - 1.0 - p[None, :])))
    cd = jnp.broadcast_to(jnp.exp(log_gamma * L)[:, None, None], (RET_HEADS, SUBLANES, RET_HEAD_DIM))
    return dec, qw, kw, cd


def _retention(rq, rk, rv, rg):
    B, S, _ = rq.shape
    L = RET_BLOCK
    dec, qw, kw, cd = _retention_tables()
    nb = 2 if B % 2 == 0 else 1
    tok = pl.BlockSpec((nb, L, RET_WIDTH), lambda b, s: (b, s, 0))
    const = lambda a: pl.BlockSpec(a.shape, lambda b, s: (0,) * a.ndim)
    return pl.pallas_call(
        _retention_kernel,
        grid=(B // nb, S // L),
        in_specs=[tok, tok, tok, tok, const(dec), const(qw), const(kw), const(cd)],
        out_specs=tok,
        out_shape=jax.ShapeDtypeStruct((B, S, RET_WIDTH), BF16),
        scratch_shapes=[pltpu.VMEM((nb, RET_HEADS, RET_HEAD_DIM, RET_HEAD_DIM), F32)],
        compiler_params=pltpu.CompilerParams(dimension_semantics=("arbitrary",) * 2),
        name="retention",
    )(rq, rk, rv, rg, dec, qw, kw, cd)


def _fox_attn_kernel(q_ref, k_ref, v_ref, o_ref, s_a, s_b, s_c, cm_a, cm_b, cm_c, m_ref, acc_ref):
    T = FOX_TQ
    d = FOX_HEAD_DIM
    nq = q_ref.shape[2]
    ones_rows = (lax.broadcasted_iota(jnp.int32, (V_AUG - d, T), 0) == 0).astype(BF16)

    def scores(qi, j, s_ref, cm_ref):
        for hh in range(2):
            kj = k_ref[0, hh, pl.ds(pl.multiple_of(j * T, T), T), :]
            st = _dot(kj, q_ref[0, hh, qi])
            s_ref[hh] = st
            cm_ref[hh] = jnp.max(st, axis=0, keepdims=True)

    def consume(j, s_ref, cm_ref, masked):
        for hh in range(2):
            st = s_ref[hh]
            if masked:
                key = lax.broadcasted_iota(jnp.int32, (T, T), 0)
                qry = lax.broadcasted_iota(jnp.int32, (T, T), 1)
                st = jnp.where(key <= qry, st, -jnp.inf)
                cm = jnp.max(st, axis=0, keepdims=True)
            else:
                cm = cm_ref[hh]
            m = m_ref[hh]
            m_new = jnp.maximum(m, cm)
            p = jnp.exp2(st - m_new).astype(BF16)
            vj = jnp.concatenate([v_ref[0, j, hh * d:(hh + 1) * d, :], ones_rows], axis=0)
            acc_ref[hh] = jnp.exp2(m - m_new) * acc_ref[hh] + _dot(vj, p)
            m_ref[hh] = m_new

    def reset():
        m_ref[...] = jnp.full(m_ref.shape, -jnp.inf, F32)
        acc_ref[...] = jnp.zeros(acc_ref.shape, F32)

    def prefetch_next(qi):
        @pl.when(qi + 1 < nq)
        def _():
            scores(qi + 1, 0, s_c, cm_c)

    def finish(qi):
        outs = [acc_ref[hh, 0:d, :] / acc_ref[hh, d:d + 1, :] for hh in range(2)]
        o_ref[0, pl.ds(pl.multiple_of(qi * T, T), T), :] = jnp.concatenate(outs, axis=0).T.astype(BF16)

    reset()
    scores(0, 0, s_a, cm_a)
    prefetch_next(0)
    consume(0, s_a, cm_a, True)
    finish(0)

    def query_tile(qi, carry):
        reset()
        scores(qi, 1, s_a, cm_a)
        consume(0, s_c, cm_c, False)

        def pair(j):
            scores(qi, j + 1, s_b, cm_b)
            consume(j, s_a, cm_a, False)
            scores(qi, j + 2, s_a, cm_a)
            consume(j + 1, s_b, cm_b, False)

        def two_pairs(jj, c):
            pair(1 + 4 * jj)
            pair(3 + 4 * jj)
            return c

        def one_pair(jj, c):
            pair(1 + 4 * (n_pairs // 2) + 2 * jj)
            return c

        n_pairs = (qi - 1) // 2
        lax.fori_loop(0, n_pairs // 2, two_pairs, 0)
        lax.fori_loop(0, n_pairs % 2, one_pair, 0)

        @pl.when(qi % 2 == 1)
        def _():
            prefetch_next(qi)
            consume(qi, s_a, cm_a, True)

        @pl.when(qi % 2 == 0)
        def _():
            scores(qi, qi, s_b, cm_b)
            consume(qi - 1, s_a, cm_a, False)
            prefetch_next(qi)
            consume(qi, s_b, cm_b, True)

        finish(qi)
        return carry

    lax.fori_loop(1, nq, query_tile, 0)


def _fox_attn(qaug, kaug, fvt):
    B, H, S, _ = kaug.shape
    nk = S // FOX_TK
    nq = S // FOX_TQ
    score_buf = pltpu.VMEM((2, FOX_TK, FOX_TQ), F32)
    col_max = pltpu.VMEM((2, 1, FOX_TQ), F32)
    return pl.pallas_call(
        _fox_attn_kernel,
        grid=(B, H // 2),
        in_specs=[
            pl.BlockSpec((1, 2, nq, AUG, FOX_TQ), lambda b, p: (b, p, 0, 0, 0)),
            pl.BlockSpec((1, 2, S, AUG), lambda b, p: (b, p, 0, 0)),
            pl.BlockSpec((1, nk, 2 * FOX_HEAD_DIM, FOX_TK), lambda b, p: (b, 0, p, 0)),
        ],
        out_specs=pl.BlockSpec((1, S, 2 * FOX_HEAD_DIM), lambda b, p: (b, 0, p)),
        out_shape=jax.ShapeDtypeStruct((B, S, FOX_WIDTH), BF16),
        scratch_shapes=[
            score_buf, score_buf, score_buf, col_max, col_max, col_max,
            pltpu.VMEM((2, 1, FOX_TQ), F32), pltpu.VMEM((2, V_AUG, FOX_TQ), F32),
        ],
        compiler_params=pltpu.CompilerParams(
            dimension_semantics=("arbitrary",) * 2, vmem_limit_bytes=48 * 1024 * 1024),
        name="fox_attn",
    )(qaug, kaug, fvt)


def _out_router_kernel(x_ref, oret_ref, ofox_ref, wor_ref, wof_ref, g_ref, wrh_ref, wrl_ref, br_ref,
                       h1_ref, u2_ref, sel_ref, cnt_ref):
    TM = MOE_TILE
    rows = lambda t: slice(t * TM, (t + 1) * TM)

    def out_proj(t):
        rs = rows(t)
        h1 = x_ref[rs] + _dot(oret_ref[rs], wor_ref[...]) + _dot(ofox_ref[rs], wof_ref[...])
        h1_ref[rs] = h1
        return h1

    def router_logits(t, h1):
        u2 = _rms(h1, g_ref[...])
        uh = u2.astype(BF16)
        u2_ref[rows(t)] = uh
        ul = (u2 - uh.astype(F32)).astype(BF16)
        return (_dot_nt(wrh_ref[...], uh) + _dot_nt(wrh_ref[...], ul) + _dot_nt(wrl_ref[...], uh)
                + br_ref[...])

    def top_k(t, logits):
        rs = rows(t)
        row = lax.broadcasted_iota(jnp.int32, (LANES, TM), 0).astype(F32)
        l = jnp.where(row < N_EXPERTS, logits, -jnp.inf)
        picks, vals = [], []
        for _ in range(TOP_K):
            m = jnp.max(l, axis=0, keepdims=True)
            idx = jnp.min(jnp.where(l == m, row, float(LANES)), axis=0, keepdims=True)
            pick = row == idx
            picks.append(pick)
            vals.append(m)
            l = jnp.where(pick, -jnp.inf, l)
        exps = [jnp.exp(v - vals[0]) for v in vals]
        den = exps[0] + exps[1] + exps[2] + exps[3]
        sel_t = jnp.full((LANES, TM), -1.0, F32)
        for pick, e in zip(picks, exps):
            sel_t = jnp.where(pick, e / den, sel_t)
        sel = sel_t.T
        sel_ref[rs] = sel
        cnt = jnp.sum((sel >= 0.0).astype(F32), axis=0, keepdims=True)
        cnt_ref[t] = jnp.broadcast_to(cnt, (SUBLANES, LANES))

    n = ROUTER_TILES
    h1s, lgs = {0: out_proj(0)}, {}
    for t in range(1, n + 2):
        if t < n:
            h1s[t] = out_proj(t)
        if 1 <= t <= n:
            lgs[t - 1] = router_logits(t - 1, h1s.pop(t - 1))
        if t >= 2:
            top_k(t - 2, lgs.pop(t - 2))


def _out_router(x2, o_ret, o_fox, wor, wof, g, wrh, wrl, br):
    T, D = x2.shape
    TM = MOE_TILE * ROUTER_TILES
    nT = T // MOE_TILE
    const = lambda a: pl.BlockSpec(a.shape, lambda i: (0,) * a.ndim)
    tok = lambda w: pl.BlockSpec((TM, w), lambda i: (i, 0))
    return pl.pallas_call(
        _out_router_kernel,
        grid=(T // TM,),
        in_specs=[tok(D), tok(RET_WIDTH), tok(FOX_WIDTH), const(wor), const(wof), const(g),
                  const(wrh), const(wrl), const(br)],
        out_specs=(tok(D), tok(D), tok(LANES),
                   pl.BlockSpec((ROUTER_TILES, SUBLANES, LANES), lambda i: (i, 0, 0))),
        out_shape=(
            jax.ShapeDtypeStruct((T, D), F32),
            jax.ShapeDtypeStruct((T, D), BF16),
            jax.ShapeDtypeStruct((T, LANES), F32),
            jax.ShapeDtypeStruct((nT, SUBLANES, LANES), F32),
        ),
        compiler_params=pltpu.CompilerParams(dimension_semantics=("arbitrary",)),
        name="out_router",
    )(x2, o_ret, o_fox, wor, wof, g, wrh, wrl, br)


def _tile_sort(sel):
    TM = sel.shape[0]
    NS = TOP_K * TM
    maskf = (sel >= 0.0).astype(F32)
    mask = maskf.astype(BF16)
    ri = lax.broadcasted_iota(jnp.int32, (TM, TM), 0)
    ci = lax.broadcasted_iota(jnp.int32, (TM, TM), 1)
    rank1 = maskf * _dot((ri >= ci).astype(BF16), mask)
    cnt = jnp.sum(maskf, axis=0, keepdims=True)
    ei = lax.broadcasted_iota(jnp.int32, (LANES, LANES), 0)
    ej = lax.broadcasted_iota(jnp.int32, (LANES, LANES), 1)
    cnt8 = jnp.broadcast_to(cnt, (SUBLANES, LANES)).astype(BF16)
    off = _dot(cnt8, (ei < ej).astype(BF16))[0:1, :]
    slot = lax.broadcasted_iota(jnp.int32, (NS, LANES), 0).astype(F32)
    esel = ((slot >= off) & (slot < off + cnt)).astype(BF16)
    return rank1.astype(BF16), esel, off, cnt


def _segment_wait(slot, local, remote_rows, sem, to_remote):
    whole = local.at[slot]
    rem = remote_rows.at[pl.ds(0, whole.shape[0]), :]
    cp = (pltpu.make_async_copy(whole, rem, sem.at[slot]) if to_remote
          else pltpu.make_async_copy(rem, whole, sem.at[slot]))
    cp.wait()


def _segment_dmas(step, slot, segdst_ref, cnt_ref, big_ref, local, remote_rows, sem, to_remote):
    big = big_ref[step] != 0
    for cond, top_bit in ((big, MOE_TILE), (jnp.logical_not(big), SMALL_RUN // 2)):
        pl.when(cond)(functools.partial(
            _segment_dma_path, step, slot, segdst_ref, cnt_ref, local, remote_rows, sem, to_remote, top_bit))


def _segment_dma_path(step, slot, segdst_ref, cnt_ref, local, remote_rows, sem, to_remote, top_bit):
    def body(e, off):
        c = cnt_ref[step * N_EXPERTS + e]
        dst = segdst_ref[step * N_EXPERTS + e]
        bit = top_bit
        while bit >= 1:
            done = c & (~(2 * bit - 1))

            @pl.when((c & bit) != 0)
            def _(bit=bit, done=done):
                loc = local.at[slot, pl.ds((off + done) * ROW_TILES, bit * ROW_TILES), :]
                rem = remote_rows.at[pl.ds((dst + done) * ROW_TILES, bit * ROW_TILES), :]
                cp = (pltpu.make_async_copy(loc, rem, sem.at[slot]) if to_remote
                      else pltpu.make_async_copy(rem, loc, sem.at[slot]))
                cp.start()
            bit //= 2
        return off + c

    off = 0
    for e in range(N_EXPERTS):
        off = body(e, off)


def _dispatch_kernel(segdst_ref, cnt_ref, big_ref, paddst_ref, padcnt_ref, nused_ref, u2_ref, sel_ref, xs_ref,
                     buf, zbuf, sems, zsem):
    i = pl.program_id(0)
    last = pl.num_programs(0) - 1
    slot = i % 2
    TM = MOE_TILE
    NS = TOP_K * TM
    rank1, esel, off, _ = _tile_sort(sel_ref[...])
    slot_id = lax.broadcasted_iota(jnp.int32, (NS, 1), 0).astype(F32)
    r_s = slot_id - jnp.sum(esel.astype(F32) * off, axis=1, keepdims=True)
    perm = (_dot_nt(esel, rank1) == r_s + 1.0).astype(BF16)

    @pl.when(i >= 2)
    def _():
        _segment_wait(slot, buf, xs_ref, sems, True)

    u2 = u2_ref[...]
    for c in range(NS // TM):
        rows = _dot(perm[c * TM:(c + 1) * TM], u2)
        for j in range(ROW_TILES):
            buf[slot, pl.ds(c * TM * ROW_TILES + j, TM, stride=ROW_TILES), :] = rows[:, j * LANES:(j + 1) * LANES]
    _segment_dmas(i, slot, segdst_ref, cnt_ref, big_ref, buf, xs_ref, sems, True)

    @pl.when(i == last)
    def _():
        @pl.when(i >= 1)
        def _():
            _segment_wait(1 - slot, buf, xs_ref, sems, True)
        _segment_wait(slot, buf, xs_ref, sems, True)
        zbuf[...] = jnp.zeros_like(zbuf)
        half = EXPERT_BLOCK // 2 * ROW_TILES
        n_blocks = xs_ref.shape[0] // (EXPERT_BLOCK * ROW_TILES)
        for wait in (False, True):
            def unused(hb, carry, wait=wait):
                cp = pltpu.make_async_copy(zbuf, xs_ref.at[pl.ds(hb * half, half), :], zsem.at[0])
                cp.wait() if wait else cp.start()
                return carry
            lax.fori_loop(2 * nused_ref[0], 2 * n_blocks, unused, 0)


            def body(e, carry, wait=wait):
                c = padcnt_ref[e]
                dst = paddst_ref[e]
                bit = EXPERT_BLOCK // 2
                while bit >= 1:
                    done = c & (~(2 * bit - 1))

                    @pl.when((c & bit) != 0)
                    def _(bit=bit, done=done):
                        cp = pltpu.make_async_copy(
                            zbuf.at[pl.ds(0, bit * ROW_TILES), :],
                            xs_ref.at[pl.ds((dst + done) * ROW_TILES, bit * ROW_TILES), :], zsem.at[0])
                        cp.wait() if wait else cp.start()
                    bit //= 2
                return carry
            lax.fori_loop(0, N_EXPERTS, body, 0)


def _dispatch(u2, sel, segdst, cnt, big, paddst, padcnt, n_used, n_rows):
    T, D = u2.shape
    TM = MOE_TILE
    NS = TOP_K * TM
    return pl.pallas_call(
        _dispatch_kernel,
        grid_spec=pltpu.PrefetchScalarGridSpec(
            num_scalar_prefetch=6,
            grid=(T // TM,),
            in_specs=[pl.BlockSpec((TM, D), lambda i, *_: (i, 0)),
                      pl.BlockSpec((TM, LANES), lambda i, *_: (i, 0))],
            out_specs=pl.BlockSpec(memory_space=pl.ANY),
            scratch_shapes=[pltpu.VMEM((2, NS * ROW_TILES, LANES), F32),
                            pltpu.VMEM((EXPERT_BLOCK // 2 * ROW_TILES, LANES), F32),
                            pltpu.SemaphoreType.DMA((2,)), pltpu.SemaphoreType.DMA((1,))],
        ),
        out_shape=jax.ShapeDtypeStruct((n_rows * ROW_TILES, LANES), F32),
        compiler_params=pltpu.CompilerParams(
            dimension_semantics=("arbitrary",), vmem_limit_bytes=48 * 1024 * 1024),
        name="dispatch",
    )(segdst, cnt, big, paddst, padcnt, n_used, u2, sel)


def _expert_kernel(bexp_ref, nused_ref, xs_ref, w1_ref, b1_ref, w2_ref, b2_ref, ys_ref, w1b, w2b):
    b = pl.program_id(0)
    BLK = EXPERT_BLOCK
    used = b < nused_ref[0]

    @pl.when(used)
    def _():
        e = bexp_ref[b]
        prev = bexp_ref[jnp.maximum(b - 1, 0)]

        @pl.when((b == 0) | (e != prev))
        def _():
            rows = 128

            def cast(r, carry):
                sl = pl.ds(pl.multiple_of(r * rows, rows), rows)
                w1b[sl, :] = w1_ref[0, sl, :].astype(BF16)
                w2b[sl, :] = w2_ref[0, sl, :].astype(BF16)
                return carry
            lax.fori_loop(0, D_MODEL // rows, cast, 0)

        R = EXPERT_PASS_ROWS
        for rp in range(BLK // R):
            r0 = rp * R * ROW_TILES
            x = jnp.concatenate([xs_ref[pl.ds(r0 + j, R, stride=ROW_TILES), :] for j in range(ROW_TILES)],
                                axis=1).astype(BF16)
            h = _dot(x, w1b[...]) + b1_ref[0]
            glu = jnp.minimum(h[:, :D_FF], SWIGLU_LIMIT)
            lin = jnp.clip(h[:, D_FF:], -SWIGLU_LIMIT, SWIGLU_LIMIT)
            act = glu * jax.nn.sigmoid(SWIGLU_ALPHA * glu) * (lin + 1.0)
            y = _dot(act.astype(BF16), w2b[...]) + b2_ref[0]
            for j in range(ROW_TILES):
                ys_ref[pl.ds(r0 + j, R, stride=ROW_TILES), :] = y[:, j * LANES:(j + 1) * LANES]

    @pl.when(jnp.logical_not(used))
    def _():
        ys_ref[...] = jnp.zeros_like(ys_ref)


def _experts(xs, block_exp, n_used, w1, b1, w2, b2):
    BLK = EXPERT_BLOCK
    NB = xs.shape[0] // (BLK * ROW_TILES)
    blk = lambda b, nused: jnp.minimum(b, nused[0] - 1)
    return pl.pallas_call(
        _expert_kernel,
        grid_spec=pltpu.PrefetchScalarGridSpec(
            num_scalar_prefetch=2,
            grid=(NB,),
            in_specs=[
                pl.BlockSpec((BLK * ROW_TILES, LANES), lambda b, bexp, nused: (blk(b, nused), 0)),
                pl.BlockSpec((1, D_MODEL, 2 * D_FF), lambda b, bexp, nused: (bexp[blk(b, nused)], 0, 0)),
                pl.BlockSpec((1, 1, 2 * D_FF), lambda b, bexp, nused: (bexp[blk(b, nused)], 0, 0)),
                pl.BlockSpec((1, D_FF, D_MODEL), lambda b, bexp, nused: (bexp[blk(b, nused)], 0, 0)),
                pl.BlockSpec((1, 1, D_MODEL), lambda b, bexp, nused: (bexp[blk(b, nused)], 0, 0)),
            ],
            out_specs=pl.BlockSpec((BLK * ROW_TILES, LANES), lambda b, bexp, nused: (b, 0)),
            scratch_shapes=[pltpu.VMEM((D_MODEL, 2 * D_FF), BF16), pltpu.VMEM((D_FF, D_MODEL), BF16)],
        ),
        out_shape=jax.ShapeDtypeStruct(xs.shape, F32),
        compiler_params=pltpu.CompilerParams(
            dimension_semantics=("arbitrary",), vmem_limit_bytes=56 * 1024 * 1024),
        name="experts",
    )(block_exp, n_used, xs, w1, b1[:, None, :], w2, b2[:, None, :])


def _combine_kernel(segdst_ref, cnt_ref, big_ref, ys_ref, sel_ref, h1_ref, g_ref, out_ref, buf, sems):
    i = pl.program_id(0)
    n = pl.num_programs(0)
    slot = i % 2
    TM = MOE_TILE
    NS = TOP_K * TM

    @pl.when(i == 0)
    def _():
        _segment_dmas(i, slot, segdst_ref, cnt_ref, big_ref, buf, ys_ref, sems, False)

    @pl.when(i + 1 < n)
    def _():
        _segment_dmas(i + 1, 1 - slot, segdst_ref, cnt_ref, big_ref, buf, ys_ref, sems, False)

    sel = sel_ref[...]
    rank1, esel, off, _ = _tile_sort(sel)
    gate = jnp.maximum(sel, 0.0).astype(BF16)
    o3 = _split3(jnp.broadcast_to(off, (SUBLANES, LANES)))
    off_s = (_dot_nt(o3[0], esel) + _dot_nt(o3[1], esel) + _dot_nt(o3[2], esel))[0:1, :]
    r_s = lax.broadcasted_iota(jnp.int32, (1, NS), 1).astype(F32) - off_s
    hit = _dot_nt(rank1, esel) == r_s + 1.0
    unperm = jnp.where(hit, _dot_nt(gate, esel), 0.0).astype(BF16)

    _segment_wait(slot, buf, ys_ref, sems, False)
    y = jnp.concatenate([buf[slot, pl.ds(j, NS, stride=ROW_TILES), :] for j in range(ROW_TILES)],
                        axis=1).astype(BF16)
    h2 = h1_ref[...] + _dot(unperm, y)
    out_ref[...] = _rms(h2, g_ref[...])


def _combine(ys, sel, h1, g, segdst, cnt, big):
    T, D = h1.shape
    TM = MOE_TILE
    NS = TOP_K * TM
    return pl.pallas_call(
        _combine_kernel,
        grid_spec=pltpu.PrefetchScalarGridSpec(
            num_scalar_prefetch=3,
            grid=(T // TM,),
            in_specs=[pl.BlockSpec(memory_space=pl.ANY),
                      pl.BlockSpec((TM, LANES), lambda i, *_: (i, 0)),
                      pl.BlockSpec((TM, D), lambda i, *_: (i, 0)),
                      pl.BlockSpec((1, D), lambda i, *_: (0, 0))],
            out_specs=pl.BlockSpec((TM, D), lambda i, *_: (i, 0)),
            scratch_shapes=[pltpu.VMEM((2, NS * ROW_TILES, LANES), F32), pltpu.SemaphoreType.DMA((2,))],
        ),
        out_shape=jax.ShapeDtypeStruct((T, D), F32),
        compiler_params=pltpu.CompilerParams(
            dimension_semantics=("arbitrary",), vmem_limit_bytes=48 * 1024 * 1024),
        name="combine",
    )(segdst, cnt, big, ys, sel, h1, g)


def _routing_tables(cnt_tiles):
    BLK = EXPERT_BLOCK
    nT = cnt_tiles.shape[0]
    A = nT * MOE_TILE * TOP_K
    NB = A // BLK + N_EXPERTS
    total = jnp.sum(cnt_tiles, axis=0)
    padded = (total + BLK - 1) // BLK * BLK
    pad_ends = jnp.cumsum(padded)
    pad_starts = pad_ends - padded
    before = jnp.cumsum(cnt_tiles, axis=0) - cnt_tiles
    segdst = (pad_starts[None, :] + before).reshape(-1).astype(jnp.int32)
    block_start = jnp.arange(NB, dtype=jnp.int32) * BLK
    block_exp = jnp.minimum(jnp.sum(pad_ends[None, :] <= block_start[:, None], axis=1), N_EXPERTS - 1).astype(jnp.int32)
    n_used = (pad_ends[-1] // BLK).astype(jnp.int32).reshape(1)
    paddst = (pad_starts + total).astype(jnp.int32)
    padcnt = (padded - total).astype(jnp.int32)
    big = jnp.any(cnt_tiles >= SMALL_RUN, axis=1).astype(jnp.int32)
    return segdst, cnt_tiles.reshape(-1).astype(jnp.int32), big, paddst, padcnt, block_exp, n_used, NB * BLK


def _rotary_tables(S):
    half = RET_HEAD_DIM // 2
    inv_freq = ROPE_BASE ** (-jnp.arange(half, dtype=F32) / half)
    ang = jnp.arange(S, dtype=F32)[:, None] * inv_freq[None, :]
    cos, sin = jnp.cos(ang), jnp.sin(ang)
    return jnp.concatenate([cos, cos], axis=-1), jnp.concatenate([-sin, sin], axis=-1)


def _layer(h, norm_mix_g, w_in, b_forget, w_out, norm_ffn_g, w_router, b_router,
           w_exp_in, b_exp_in, w_exp_out, b_exp_out, final_g):
    B, S, D = h.shape
    R, Fw = RET_WIDTH, FOX_WIDTH
    cos, sin = _rotary_tables(S)
    wb = w_in.astype(BF16)
    wr = wb[:, :4 * R]
    wfq, wfk, wfv = (wb[:, 4 * R + i * Fw:4 * R + (i + 1) * Fw] for i in range(3))
    wzt = jnp.zeros((16, D), BF16).at[:FOX_HEADS, :].set(wb[:, 4 * R + 3 * Fw:].T)
    rq, rk, rv, rg, kaug, qaug, fvt = _in_proj(
        h, norm_mix_g[None, :], cos, sin, wr, wfk, wfq.T, wfv.T, wzt, b_forget)
    o_ret = _retention(rq, rk, rv, rg)
    o_fox = _fox_attn(qaug, kaug, fvt)

    T = B * S
    wo = w_out.astype(BF16)
    wrt = jnp.zeros((LANES, D), F32).at[:N_EXPERTS, :].set(w_router.T)
    wrh = wrt.astype(BF16)
    wrl = (wrt - wrh.astype(F32)).astype(BF16)
    br = jnp.zeros((LANES, 1), F32).at[:N_EXPERTS, 0].set(b_router)
    h1, u2, sel, cnt = _out_router(h.reshape(T, D), o_ret.reshape(T, R), o_fox.reshape(T, Fw),
                                   wo[:R], wo[R:], norm_ffn_g[None, :], wrh, wrl, br)
    cnt_tiles = cnt[:, 0, :N_EXPERTS].astype(jnp.int32)
    segdst, cnt_flat, big, paddst, padcnt, block_exp, n_used, n_rows = _routing_tables(cnt_tiles)
    xs = _dispatch(u2, sel, segdst, cnt_flat, big, paddst, padcnt, n_used, n_rows)
    ys = _experts(xs, block_exp, n_used, w_exp_in, b_exp_in, w_exp_out, b_exp_out)
    out = _combine(ys, sel, h1, final_g[None, :], segdst, cnt_flat, big)
    return out.reshape(B, S, D)


def kernel(x, norm_mix_g, w_in, b_forget, w_out, norm_ffn_g, w_router, b_router,
           w_exp_in, b_exp_in, w_exp_out, b_exp_out, norm_final_g):
    depth = w_in.shape[0]
    assert depth == 1, "the fused final RMSNorm assumes a single layer"
    return _layer(x, norm_mix_g[0], w_in[0], b_forget[0], w_out[0], norm_ffn_g[0], w_router[0], b_router[0],
                  w_exp_in[0], b_exp_in[0], w_exp_out[0], b_exp_out[0], norm_final_g)
```

```python
import functools

import numpy as np
import jax
import jax.numpy as jnp
from jax import lax
from jax.experimental import pallas as pl
from jax.experimental.pallas import tpu as pltpu

F32 = jnp.float32
BF16 = jnp.bfloat16

D_MODEL = 1024
RET_HEADS, RET_HEAD_DIM = 4, 128
RET_WIDTH = RET_HEADS * RET_HEAD_DIM
FOX_HEADS, FOX_HEAD_DIM = 8, 64
FOX_WIDTH = FOX_HEADS * FOX_HEAD_DIM
CHUNK = 64
ROPE_BASE = 10000.0
N_EXPERTS = 32
TOP_K = 4
D_FF = D_MODEL
SWIGLU_ALPHA = 1.702
SWIGLU_LIMIT = 7.0
RMS_EPS = 1e-5
GN_EPS = 1e-5

LANES = 128
SUBLANES = 8
ROW_TILES = D_MODEL // LANES

PROJ_TILE = 512
RET_BLOCK = 256
FOX_TQ = 512
FOX_TK = PROJ_TILE
AUG = 128
V_AUG = 80
LOG2E = 1.4426950408889634
MOE_TILE = 256
ROUTER_TILES = 4
SMALL_RUN = 64
EXPERT_BLOCK = 512
EXPERT_PASS_ROWS = 256

NT_DIMS = (((1,), (1,)), ((), ()))


def _split3(a):
    hi = a.astype(BF16)
    r1 = a - hi.astype(F32)
    mid = r1.astype(BF16)
    lo = (r1 - mid.astype(F32)).astype(BF16)
    return hi, mid, lo


def _dot(a, b):
    return jnp.dot(a, b, preferred_element_type=F32)


def _dot_nt(a, b):
    return lax.dot_general(a, b, NT_DIMS, preferred_element_type=F32)


def _rms(x, g):
    return x * lax.rsqrt(jnp.mean(x * x, axis=-1, keepdims=True) + RMS_EPS) * g


def _in_proj_kernel(x_ref, g_ref, cos_ref, sin_ref, wr_ref, wfk_ref, wfqt_ref, wfvt_ref, wzt_ref, bcol_ref,
                    selkf_ref, constk_ref, selqf_ref, constq_ref,
                    rq_ref, rk_ref, rv_ref, rg_ref, kaug_ref, qaug_ref, fvt_ref, ccol):
    TM = x_ref.shape[1]
    d, H = FOX_HEAD_DIM, FOX_HEADS
    u = _rms(x_ref[0], g_ref[...]).astype(BF16)
    r = _dot(u, wr_ref[...])
    cos, sin = cos_ref[...], sin_ref[...]
    k_scale = RET_HEAD_DIM ** -0.5
    for h in range(RET_HEADS):
        lo = h * RET_HEAD_DIM
        q = r[:, lo:lo + RET_HEAD_DIM]
        k = r[:, RET_WIDTH + lo:RET_WIDTH + lo + RET_HEAD_DIM]
        rq_ref[0, :, lo:lo + RET_HEAD_DIM] = (q * cos + pltpu.roll(q, RET_HEAD_DIM // 2, 1) * sin).astype(BF16)
        rk_ref[0, :, lo:lo + RET_HEAD_DIM] = (
            (k * cos + pltpu.roll(k, RET_HEAD_DIM // 2, 1) * sin) * k_scale).astype(BF16)
    rv_ref[0] = r[:, 2 * RET_WIDTH:3 * RET_WIDTH].astype(BF16)
    rg_ref[0] = r[:, 3 * RET_WIDTH:4 * RET_WIDTH].astype(BF16)
    fvt_ref[0, 0] = _dot_nt(wfvt_ref[...], u).astype(BF16)
    fk = _dot(u, wfk_ref[...])
    fqt = (_dot_nt(wfqt_ref[...], u) * (d ** -0.5 * LOG2E)).astype(BF16)
    zt = _dot_nt(wzt_ref[...], u)

    @pl.when(pl.program_id(1) == 0)
    def _():
        ccol[...] = jnp.zeros_like(ccol)

    row = lax.broadcasted_iota(jnp.int32, (16, TM), 0)
    lft = jnp.where(row < H, jax.nn.log_sigmoid(zt + bcol_ref[...]), 0.0)
    utri = (lax.broadcasted_iota(jnp.int32, (TM, TM), 0) <= lax.broadcasted_iota(jnp.int32, (TM, TM), 1)).astype(BF16)
    t3 = _split3(lft)
    f_col = _dot(t3[0], utri) + _dot(t3[1], utri) + _dot(t3[2], utri) + ccol[:, 0:1]
    ccol[...] = jnp.broadcast_to(f_col[:, TM - 1:TM], ccol.shape)

    pieces_t = jnp.concatenate(_split3(f_col * LOG2E), axis=0)
    for h in range(H):
        extra = _dot(selqf_ref[h], pieces_t) + constq_ref[...]
        qaug_ref[0, h, 0] = jnp.concatenate([fqt[h * d:(h + 1) * d, :], extra.astype(BF16)], axis=0)

    f_row = jnp.concatenate([f_col, jnp.zeros((LANES - 16, TM), F32)], axis=0).T
    n3 = _split3(f_row * -LOG2E)
    pieces = (n3[0].astype(F32) + pltpu.roll(n3[1].astype(F32), H, 1)
              + pltpu.roll(n3[2].astype(F32), 2 * H, 1)).astype(BF16)
    lane = lax.broadcasted_iota(jnp.int32, (TM, LANES), 1)
    for g in range(H // 2):
        bias = _dot(pieces, selkf_ref[g])
        kg = fk[:, g * 2 * d:(g + 1) * 2 * d]
        for o in range(2):
            kh = kg if o == 0 else pltpu.roll(kg, d, 1)
            extra = bias[:, o * AUG:(o + 1) * AUG] + constk_ref[...]
            kaug_ref[0, 2 * g + o] = jnp.where(lane < d, kh, extra).astype(BF16)


def _in_proj(x, g, cos, sin, wr, wfk, wfqt, wfvt, wzt, b_forget):
    B, S, D = x.shape
    TM = PROJ_TILE
    ns = S // TM
    selkf, constk, selqf, constq = _fox_prep_constants()
    bcol = jnp.zeros((16, 1), F32).at[:FOX_HEADS, 0].set(b_forget)
    consts = (wr, wfk, wfqt, wfvt, wzt, bcol, selkf, constk, selqf, constq)
    const = lambda a: pl.BlockSpec(a.shape, lambda b, s: (0,) * a.ndim)
    tok = lambda w: pl.BlockSpec((1, TM, w), lambda b, s: (b, s, 0))
    out_shape = (
        jax.ShapeDtypeStruct((B, S, RET_WIDTH), BF16),
        jax.ShapeDtypeStruct((B, S, RET_WIDTH), BF16),
        jax.ShapeDtypeStruct((B, S, RET_WIDTH), BF16),
        jax.ShapeDtypeStruct((B, S, RET_WIDTH), BF16),
        jax.ShapeDtypeStruct((B, FOX_HEADS, S, AUG), BF16),
        jax.ShapeDtypeStruct((B, FOX_HEADS, ns, AUG, TM), BF16),
        jax.ShapeDtypeStruct((B, ns, FOX_WIDTH, TM), BF16),
    )
    return pl.pallas_call(
        _in_proj_kernel,
        grid=(B, ns),
        in_specs=[
            pl.BlockSpec((1, TM, D), lambda b, s: (b, s, 0)),
            pl.BlockSpec((1, D), lambda b, s: (0, 0)),
            pl.BlockSpec((TM, RET_HEAD_DIM), lambda b, s: (s, 0)),
            pl.BlockSpec((TM, RET_HEAD_DIM), lambda b, s: (s, 0)),
        ] + [const(a) for a in consts],
        out_specs=(
            tok(RET_WIDTH), tok(RET_WIDTH), tok(RET_WIDTH), tok(RET_WIDTH),
            pl.BlockSpec((1, FOX_HEADS, TM, AUG), lambda b, s: (b, 0, s, 0)),
            pl.BlockSpec((1, FOX_HEADS, 1, AUG, TM), lambda b, s: (b, 0, s, 0, 0)),
            pl.BlockSpec((1, 1, FOX_WIDTH, TM), lambda b, s: (b, s, 0, 0)),
        ),
        out_shape=out_shape,
        scratch_shapes=[pltpu.VMEM((16, LANES), F32)],
        compiler_params=pltpu.CompilerParams(
            dimension_semantics=("arbitrary", "arbitrary"), vmem_limit_bytes=48 * 1024 * 1024),
        name="in_proj",
    )(x, g, cos, sin, *consts)


def _fox_prep_constants():
    d, H = FOX_HEAD_DIM, FOX_HEADS
    selkf = np.zeros((H // 2, LANES, 2 * AUG), np.float32)
    constk = np.zeros((1, AUG), np.float32)
    selqf = np.zeros((H, d, 48), np.float32)
    constq = np.zeros((d, 1), np.float32)
    for p in range(3):
        constk[0, d + p] = 1.0
        constq[3 + p, 0] = 1.0
        for h in range(H):
            selkf[h // 2, p * H + h, (h % 2) * AUG + d + 3 + p] = 1.0
            selqf[h, p, p * 16 + h] = 1.0
    return jnp.asarray(selkf, BF16), jnp.asarray(constk, F32), jnp.asarray(selqf, BF16), jnp.asarray(constq, F32)


def _retention_kernel(q_ref, k_ref, v_ref, g_ref, dec_ref, qw_ref, kw_ref, cd_ref, o_ref, state):
    @pl.when(pl.program_id(1) == 0)
    def _():
        state[...] = jnp.zeros_like(state)

    for bb, h in [(bb, h) for bb in range(q_ref.shape[0]) for h in range(RET_HEADS)]:
        hs = slice(h * RET_HEAD_DIM, (h + 1) * RET_HEAD_DIM)
        q, k, v = q_ref[bb, :, hs], k_ref[bb, :, hs], v_ref[bb, :, hs]
        scores = (_dot_nt(q, k) * dec_ref[h]).astype(BF16)
        st = state[bb, h]
        o = _dot(scores, v) + _dot((q.astype(F32) * qw_ref[h]).astype(BF16), st.astype(BF16))
        kk = k.astype(F32) * kw_ref[h]
        state[bb, h] = st * cd_ref[h, 0:1, :] + _dot(kk.T.astype(BF16), v)
        mu = jnp.mean(o, axis=-1, keepdims=True)
        oc = o - mu
        var = jnp.mean(oc * oc, axis=-1, keepdims=True)
        o_ref[bb, :, hs] = (oc * lax.rsqrt(var + GN_EPS) * jax.nn.silu(g_ref[bb, :, hs].astype(F32))).astype(BF16)


def _retention_tables():
    L = RET_BLOCK
    log_gamma = jnp.log1p(-jnp.exp2(-5.0 - jnp.arange(RET_HEADS, dtype=F32)))
    p = jnp.arange(L, dtype=F32)
    dist = jnp.abs(p[:, None] - p[None, :])
    chunk = jnp.arange(L) // CHUNK
    allowed = (chunk[None, :] <= chunk[:, None]).astype(F32)
    dec = jnp.exp(log_gamma[:, None, None] * dist) * allowed
    lanes = lambda a: jnp.broadcast_to(a[:, :, None], (RET_HEADS, L, RET_HEAD_DIM))
    qw = lanes(jnp.exp(log_gamma[:, None] * (p[None, :] + 1.0)))
    kw = lanes(jnp.exp(log_gamma[:, None] * (L - 1.0 - p[None, :])))
    cd = jnp.broadcast_to(jnp.exp(log_gamma * L)[:, None, None], (RET_HEADS, SUBLANES, RET_HEAD_DIM))
    return dec, qw, kw, cd


def _retention(rq, rk, rv, rg):
    B, S, _ = rq.shape
    L = RET_BLOCK
    dec, qw, kw, cd = _retention_tables()
    nb = 2 if B % 2 == 0 else 1
    tok = pl.BlockSpec((nb, L, RET_WIDTH), lambda b, s: (b, s, 0))
    const = lambda a: pl.BlockSpec(a.shape, lambda b, s: (0,) * a.ndim)
    return pl.pallas_call(
        _retention_kernel,
        grid=(B // nb, S // L),
        in_specs=[tok, tok, tok, tok, const(dec), const(qw), const(kw), const(cd)],
        out_specs=tok,
        out_shape=jax.ShapeDtypeStruct((B, S, RET_WIDTH), BF16),
        scratch_shapes=[pltpu.VMEM((nb, RET_HEADS, RET_HEAD_DIM, RET_HEAD_DIM), F32)],
        compiler_params=pltpu.CompilerParams(dimension_semantics=("arbitrary",) * 2),
        name="retention",
    )(rq, rk, rv, rg, dec, qw, kw, cd)


def _fox_attn_kernel(q_ref, k_ref, v_ref, o_ref, s_a, s_b, s_c, cm_a, cm_b, cm_c, m_ref, acc_ref):
    T = FOX_TQ
    d = FOX_HEAD_DIM
    nq = q_ref.shape[2]
    ones_rows = (lax.broadcasted_iota(jnp.int32, (V_AUG - d, T), 0) == 0).astype(BF16)

    def scores(qi, j, s_ref, cm_ref):
        for hh in range(2):
            kj = k_ref[0, hh, pl.ds(pl.multiple_of(j * T, T), T), :]
            st = _dot(kj, q_ref[0, hh, qi])
            s_ref[hh] = st
            cm_ref[hh] = jnp.max(st, axis=0, keepdims=True)

    def consume(j, s_ref, cm_ref, masked):
        for hh in range(2):
            st = s_ref[hh]
            if masked:
                key = lax.broadcasted_iota(jnp.int32, (T, T), 0)
                qry = lax.broadcasted_iota(jnp.int32, (T, T), 1)
                st = jnp.where(key <= qry, st, -jnp.inf)
                cm = jnp.max(st, axis=0, keepdims=True)
            else:
                cm = cm_ref[hh]
            m = m_ref[hh]
            m_new = jnp.maximum(m, cm)
            p = jnp.exp2(st - m_new).astype(BF16)
            vj = jnp.concatenate([v_ref[0, j, hh * d:(hh + 1) * d, :], ones_rows], axis=0)
            acc_ref[hh] = jnp.exp2(m - m_new) * acc_ref[hh] + _dot(vj, p)
            m_ref[hh] = m_new

    def reset():
        m_ref[...] = jnp.full(m_ref.shape, -jnp.inf, F32)
        acc_ref[...] = jnp.zeros(acc_ref.shape, F32)

    def prefetch_next(qi):
        @pl.when(qi + 1 < nq)
        def _():
            scores(qi + 1, 0, s_c, cm_c)

    def finish(qi):
        outs = [acc_ref[hh, 0:d, :] / acc_ref[hh, d:d + 1, :] for hh in range(2)]
        o_ref[0, pl.ds(pl.multiple_of(qi * T, T), T), :] = jnp.concatenate(outs, axis=0).T.astype(BF16)

    reset()
    scores(0, 0, s_a, cm_a)
    prefetch_next(0)
    consume(0, s_a, cm_a, True)
    finish(0)

    def query_tile(qi, carry):
        reset()
        scores(qi, 1, s_a, cm_a)
        consume(0, s_c, cm_c, False)

        def pair(j):
            scores(qi, j + 1, s_b, cm_b)
            consume(j, s_a, cm_a, False)
            scores(qi, j + 2, s_a, cm_a)
            consume(j + 1, s_b, cm_b, False)

        def two_pairs(jj, c):
            pair(1 + 4 * jj)
            pair(3 + 4 * jj)
            return c

        def one_pair(jj, c):
            pair(1 + 4 * (n_pairs // 2) + 2 * jj)
            return c

        n_pairs = (qi - 1) // 2
        lax.fori_loop(0, n_pairs // 2, two_pairs, 0)
        lax.fori_loop(0, n_pairs % 2, one_pair, 0)

        @pl.when(qi % 2 == 1)
        def _():
            prefetch_next(qi)
            consume(qi, s_a, cm_a, True)

        @pl.when(qi % 2 == 0)
        def _():
            scores(qi, qi, s_b, cm_b)
            consume(qi - 1, s_a, cm_a, False)
            prefetch_next(qi)
            consume(qi, s_b, cm_b, True)

        finish(qi)
        return carry

    lax.fori_loop(1, nq, query_tile, 0)


def _fox_attn(qaug, kaug, fvt):
    B, H, S, _ = kaug.shape
    nk = S // FOX_TK
    nq = S // FOX_TQ
    score_buf = pltpu.VMEM((2, FOX_TK, FOX_TQ), F32)
    col_max = pltpu.VMEM((2, 1, FOX_TQ), F32)
    return pl.pallas_call(
        _fox_attn_kernel,
        grid=(B, H // 2),
        in_specs=[
            pl.BlockSpec((1, 2, nq, AUG, FOX_TQ), lambda b, p: (b, p, 0, 0, 0)),
            pl.BlockSpec((1, 2, S, AUG), lambda b, p: (b, p, 0, 0)),
            pl.BlockSpec((1, nk, 2 * FOX_HEAD_DIM, FOX_TK), lambda b, p: (b, 0, p, 0)),
        ],
        out_specs=pl.BlockSpec((1, S, 2 * FOX_HEAD_DIM), lambda b, p: (b, 0, p)),
        out_shape=jax.ShapeDtypeStruct((B, S, FOX_WIDTH), BF16),
        scratch_shapes=[
            score_buf, score_buf, score_buf, col_max, col_max, col_max,
            pltpu.VMEM((2, 1, FOX_TQ), F32), pltpu.VMEM((2, V_AUG, FOX_TQ), F32),
        ],
        compiler_params=pltpu.CompilerParams(
            dimension_semantics=("arbitrary",) * 2, vmem_limit_bytes=48 * 1024 * 1024),
        name="fox_attn",
    )(qaug, kaug, fvt)


def _out_router_kernel(x_ref, oret_ref, ofox_ref, wor_ref, wof_ref, g_ref, wrh_ref, wrl_ref, br_ref,
                       h1_ref, u2_ref, sel_ref, cnt_ref):
    TM = MOE_TILE
    rows = lambda t: slice(t * TM, (t + 1) * TM)

    def out_proj(t):
        rs = rows(t)
        h1 = x_ref[rs] + _dot(oret_ref[rs], wor_ref[...]) + _dot(ofox_ref[rs], wof_ref[...])
        h1_ref[rs] = h1
        return h1

    def router_logits(t, h1):
        u2 = _rms(h1, g_ref[...])
        uh = u2.astype(BF16)
        u2_ref[rows(t)] = uh
        ul = (u2 - uh.astype(F32)).astype(BF16)
        return (_dot_nt(wrh_ref[...], uh) + _dot_nt(wrh_ref[...], ul) + _dot_nt(wrl_ref[...], uh)
                + br_ref[...])

    def top_k(t, logits):
        rs = rows(t)
        row = lax.broadcasted_iota(jnp.int32, (LANES, TM), 0).astype(F32)
        l = jnp.where(row < N_EXPERTS, logits, -jnp.inf)
        picks, vals = [], []
        for _ in range(TOP_K):
            m = jnp.max(l, axis=0, keepdims=True)
            idx = jnp.min(jnp.where(l == m, row, float(LANES)), axis=0, keepdims=True)
            pick = row == idx
            picks.append(pick)
            vals.append(m)
            l = jnp.where(pick, -jnp.inf, l)
        exps = [jnp.exp(v - vals[0]) for v in vals]
        den = exps[0] + exps[1] + exps[2] + exps[3]
        sel_t = jnp.full((LANES, TM), -1.0, F32)
        for pick, e in zip(picks, exps):
            sel_t = jnp.where(pick, e / den, sel_t)
        sel = sel_t.T
        sel_ref[rs] = sel
        cnt = jnp.sum((sel >= 0.0).astype(F32), axis=0, keepdims=True)
        cnt_ref[t] = jnp.broadcast_to(cnt, (SUBLANES, LANES))

    n = ROUTER_TILES
    h1s, lgs = {0: out_proj(0)}, {}
    for t in range(1, n + 2):
        if t < n:
            h1s[t] = out_proj(t)
        if 1 <= t <= n:
            lgs[t - 1] = router_logits(t - 1, h1s.pop(t - 1))
        if t >= 2:
            top_k(t - 2, lgs.pop(t - 2))


def _out_router(x2, o_ret, o_fox, wor, wof, g, wrh, wrl, br):
    T, D = x2.shape
    TM = MOE_TILE * ROUTER_TILES
    nT = T // MOE_TILE
    const = lambda a: pl.BlockSpec(a.shape, lambda i: (0,) * a.ndim)
    tok = lambda w: pl.BlockSpec((TM, w), lambda i: (i, 0))
    return pl.pallas_call(
        _out_router_kernel,
        grid=(T // TM,),
        in_specs=[tok(D), tok(RET_WIDTH), tok(FOX_WIDTH), const(wor), const(wof), const(g),
                  const(wrh), const(wrl), const(br)],
        out_specs=(tok(D), tok(D), tok(LANES),
                   pl.BlockSpec((ROUTER_TILES, SUBLANES, LANES), lambda i: (i, 0, 0))),
        out_shape=(
            jax.ShapeDtypeStruct((T, D), F32),
            jax.ShapeDtypeStruct((T, D), BF16),
            jax.ShapeDtypeStruct((T, LANES), F32),
            jax.ShapeDtypeStruct((nT, SUBLANES, LANES), F32),
        ),
        compiler_params=pltpu.CompilerParams(dimension_semantics=("arbitrary",)),
        name="out_router",
    )(x2, o_ret, o_fox, wor, wof, g, wrh, wrl, br)


def _tile_sort(sel):
    TM = sel.shape[0]
    NS = TOP_K * TM
    maskf = (sel >= 0.0).astype(F32)
    mask = maskf.astype(BF16)
    ri = lax.broadcasted_iota(jnp.int32, (TM, TM), 0)
    ci = lax.broadcasted_iota(jnp.int32, (TM, TM), 1)
    rank1 = maskf * _dot((ri >= ci).astype(BF16), mask)
    cnt = jnp.sum(maskf, axis=0, keepdims=True)
    ei = lax.broadcasted_iota(jnp.int32, (LANES, LANES), 0)
    ej = lax.broadcasted_iota(jnp.int32, (LANES, LANES), 1)
    cnt8 = jnp.broadcast_to(cnt, (SUBLANES, LANES)).astype(BF16)
    off = _dot(cnt8, (ei < ej).astype(BF16))[0:1, :]
    slot = lax.broadcasted_iota(jnp.int32, (NS, LANES), 0).astype(F32)
    esel = ((slot >= off) & (slot < off + cnt)).astype(BF16)
    return rank1.astype(BF16), esel, off, cnt


def _segment_wait(slot, local, remote_rows, sem, to_remote):
    whole = local.at[slot]
    rem = remote_rows.at[pl.ds(0, whole.shape[0]), :]
    cp = (pltpu.make_async_copy(whole, rem, sem.at[slot]) if to_remote
          else pltpu.make_async_copy(rem, whole, sem.at[slot]))
    cp.wait()


def _segment_dmas(step, slot, segdst_ref, cnt_ref, big_ref, local, remote_rows, sem, to_remote):
    big = big_ref[step] != 0
    for cond, top_bit in ((big, MOE_TILE), (jnp.logical_not(big), SMALL_RUN // 2)):
        pl.when(cond)(functools.partial(
            _segment_dma_path, step, slot, segdst_ref, cnt_ref, local, remote_rows, sem, to_remote, top_bit))


def _segment_dma_path(step, slot, segdst_ref, cnt_ref, local, remote_rows, sem, to_remote, top_bit):
    def body(e, off):
        c = cnt_ref[step * N_EXPERTS + e]
        dst = segdst_ref[step * N_EXPERTS + e]
        bit = top_bit
        while bit >= 1:
            done = c & (~(2 * bit - 1))

            @pl.when((c & bit) != 0)
            def _(bit=bit, done=done):
                loc = local.at[slot, pl.ds((off + done) * ROW_TILES, bit * ROW_TILES), :]
                rem = remote_rows.at[pl.ds((dst + done) * ROW_TILES, bit * ROW_TILES), :]
                cp = (pltpu.make_async_copy(loc, rem, sem.at[slot]) if to_remote
                      else pltpu.make_async_copy(rem, loc, sem.at[slot]))
                cp.start()
            bit //= 2
        return off + c

    off = 0
    for e in range(N_EXPERTS):
        off = body(e, off)


def _dispatch_kernel(segdst_ref, cnt_ref, big_ref, paddst_ref, padcnt_ref, nused_ref, u2_ref, sel_ref, xs_ref,
                     buf, zbuf, sems, zsem):
    i = pl.program_id(0)
    last = pl.num_programs(0) - 1
    slot = i % 2
    TM = MOE_TILE
    NS = TOP_K * TM
    rank1, esel, off, _ = _tile_sort(sel_ref[...])
    slot_id = lax.broadcasted_iota(jnp.int32, (NS, 1), 0).astype(F32)
    r_s = slot_id - jnp.sum(esel.astype(F32) * off, axis=1, keepdims=True)
    perm = (_dot_nt(esel, rank1) == r_s + 1.0).astype(BF16)

    @pl.when(i >= 2)
    def _():
        _segment_wait(slot, buf, xs_ref, sems, True)

    u2 = u2_ref[...]
    for c in range(NS // TM):
        rows = _dot(perm[c * TM:(c + 1) * TM], u2)
        for j in range(ROW_TILES):
            buf[slot, pl.ds(c * TM * ROW_TILES + j, TM, stride=ROW_TILES), :] = rows[:, j * LANES:(j + 1) * LANES]
    _segment_dmas(i, slot, segdst_ref, cnt_ref, big_ref, buf, xs_ref, sems, True)

    @pl.when(i == last)
    def _():
        @pl.when(i >= 1)
        def _():
            _segment_wait(1 - slot, buf, xs_ref, sems, True)
        _segment_wait(slot, buf, xs_ref, sems, True)
        zbuf[...] = jnp.zeros_like(zbuf)
        half = EXPERT_BLOCK // 2 * ROW_TILES
        n_blocks = xs_ref.shape[0] // (EXPERT_BLOCK * ROW_TILES)
        for wait in (False, True):
            def unused(hb, carry, wait=wait):
                cp = pltpu.make_async_copy(zbuf, xs_ref.at[pl.ds(hb * half, half), :], zsem.at[0])
                cp.wait() if wait else cp.start()
                return carry
            lax.fori_loop(2 * nused_ref[0], 2 * n_blocks, unused, 0)


            def body(e, carry, wait=wait):
                c = padcnt_ref[e]
                dst = paddst_ref[e]
                bit = EXPERT_BLOCK // 2
                while bit >= 1:
                    done = c & (~(2 * bit - 1))

                    @pl.when((c & bit) != 0)
                    def _(bit=bit, done=done):
                        cp = pltpu.make_async_copy(
                            zbuf.at[pl.ds(0, bit * ROW_TILES), :],
                            xs_ref.at[pl.ds((dst + done) * ROW_TILES, bit * ROW_TILES), :], zsem.at[0])
                        cp.wait() if wait else cp.start()
                    bit //= 2
                return carry
            lax.fori_loop(0, N_EXPERTS, body, 0)


def _dispatch(u2, sel, segdst, cnt, big, paddst, padcnt, n_used, n_rows):
    T, D = u2.shape
    TM = MOE_TILE
    NS = TOP_K * TM
    return pl.pallas_call(
        _dispatch_kernel,
        grid_spec=pltpu.PrefetchScalarGridSpec(
            num_scalar_prefetch=6,
            grid=(T // TM,),
            in_specs=[pl.BlockSpec((TM, D), lambda i, *_: (i, 0)),
                      pl.BlockSpec((TM, LANES), lambda i, *_: (i, 0))],
            out_specs=pl.BlockSpec(memory_space=pl.ANY),
            scratch_shapes=[pltpu.VMEM((2, NS * ROW_TILES, LANES), F32),
                            pltpu.VMEM((EXPERT_BLOCK // 2 * ROW_TILES, LANES), F32),
                            pltpu.SemaphoreType.DMA((2,)), pltpu.SemaphoreType.DMA((1,))],
        ),
        out_shape=jax.ShapeDtypeStruct((n_rows * ROW_TILES, LANES), F32),
        compiler_params=pltpu.CompilerParams(
            dimension_semantics=("arbitrary",), vmem_limit_bytes=48 * 1024 * 1024),
        name="dispatch",
    )(segdst, cnt, big, paddst, padcnt, n_used, u2, sel)


def _expert_kernel(bexp_ref, nused_ref, epos_ref, enext_ref, xs_ref, w1_hbm, b1_ref, w2_hbm, b2_ref, ys_ref,
                   w1f, w2f, w1b, w2b, wsem):
    b = pl.program_id(0)
    BLK = EXPERT_BLOCK
    used = b < nused_ref[0]

    def weight_copies(e, slot):
        return (pltpu.make_async_copy(w1_hbm.at[e], w1f.at[slot], wsem.at[0, slot]),
                pltpu.make_async_copy(w2_hbm.at[e], w2f.at[slot], wsem.at[1, slot]))

    @pl.when(used)
    def _():
        e = bexp_ref[b]
        prev = bexp_ref[jnp.maximum(b - 1, 0)]
        slot = epos_ref[b] % 2

        @pl.when(b == 0)
        def _():
            for cp in weight_copies(e, slot):
                cp.start()

        @pl.when((b == 0) | (e != prev))
        def _():
            nxt = enext_ref[b]

            @pl.when(nxt >= 0)
            def _():
                for cp in weight_copies(nxt, 1 - slot):
                    cp.start()

            for cp in weight_copies(e, slot):
                cp.wait()
            rows = 128

            def cast(r, carry):
                sl = pl.ds(pl.multiple_of(r * rows, rows), rows)
                w1b[sl, :] = w1f[slot, sl, :].astype(BF16)
                w2b[sl, :] = w2f[slot, sl, :].astype(BF16)
                return carry
            lax.fori_loop(0, D_MODEL // rows, cast, 0)

        R = EXPERT_PASS_ROWS
        for rp in range(BLK // R):
            r0 = rp * R * ROW_TILES
            x = jnp.concatenate([xs_ref[pl.ds(r0 + j, R, stride=ROW_TILES), :] for j in range(ROW_TILES)],
                                axis=1).astype(BF16)
            h = _dot(x, w1b[...]) + b1_ref[0]
            glu = jnp.minimum(h[:, :D_FF], SWIGLU_LIMIT)
            lin = jnp.clip(h[:, D_FF:], -SWIGLU_LIMIT, SWIGLU_LIMIT)
            act = glu * jax.nn.sigmoid(SWIGLU_ALPHA * glu) * (lin + 1.0)
            y = _dot(act.astype(BF16), w2b[...]) + b2_ref[0]
            for j in range(ROW_TILES):
                ys_ref[pl.ds(r0 + j, R, stride=ROW_TILES), :] = y[:, j * LANES:(j + 1) * LANES]

    @pl.when(jnp.logical_not(used))
    def _():
        ys_ref[...] = jnp.zeros_like(ys_ref)


def _experts(xs, block_exp, n_used, exp_pos, exp_next, w1, b1, w2, b2):
    BLK = EXPERT_BLOCK
    NB = xs.shape[0] // (BLK * ROW_TILES)
    blk = lambda b, nused: jnp.minimum(b, nused[0] - 1)
    return pl.pallas_call(
        _expert_kernel,
        grid_spec=pltpu.PrefetchScalarGridSpec(
            num_scalar_prefetch=4,
            grid=(NB,),
            in_specs=[
                pl.BlockSpec((BLK * ROW_TILES, LANES), lambda b, bexp, nused, *_: (blk(b, nused), 0)),
                pl.BlockSpec(memory_space=pl.ANY),
                pl.BlockSpec((1, 1, 2 * D_FF), lambda b, bexp, nused, *_: (bexp[blk(b, nused)], 0, 0)),
                pl.BlockSpec(memory_space=pl.ANY),
                pl.BlockSpec((1, 1, D_MODEL), lambda b, bexp, nused, *_: (bexp[blk(b, nused)], 0, 0)),
            ],
            out_specs=pl.BlockSpec((BLK * ROW_TILES, LANES), lambda b, *_: (b, 0)),
            scratch_shapes=[pltpu.VMEM((2, D_MODEL, 2 * D_FF), F32), pltpu.VMEM((2, D_FF, D_MODEL), F32),
                            pltpu.VMEM((D_MODEL, 2 * D_FF), BF16), pltpu.VMEM((D_FF, D_MODEL), BF16),
                            pltpu.SemaphoreType.DMA((2, 2))],
        ),
        out_shape=jax.ShapeDtypeStruct(xs.shape, F32),
        compiler_params=pltpu.CompilerParams(
            dimension_semantics=("arbitrary",), vmem_limit_bytes=56 * 1024 * 1024),
        name="experts",
    )(block_exp, n_used, exp_pos, exp_next, xs, w1, b1[:, None, :], w2, b2[:, None, :])


def _combine_kernel(segdst_ref, cnt_ref, big_ref, ys_ref, sel_ref, h1_ref, g_ref, out_ref, buf, sems):
    i = pl.program_id(0)
    n = pl.num_programs(0)
    slot = i % 2
    TM = MOE_TILE
    NS = TOP_K * TM

    @pl.when(i == 0)
    def _():
        _segment_dmas(i, slot, segdst_ref, cnt_ref, big_ref, buf, ys_ref, sems, False)

    @pl.when(i + 1 < n)
    def _():
        _segment_dmas(i + 1, 1 - slot, segdst_ref, cnt_ref, big_ref, buf, ys_ref, sems, False)

    sel = sel_ref[...]
    rank1, esel, off, _ = _tile_sort(sel)
    gate = jnp.maximum(sel, 0.0).astype(BF16)
    o3 = _split3(jnp.broadcast_to(off, (SUBLANES, LANES)))
    off_s = (_dot_nt(o3[0], esel) + _dot_nt(o3[1], esel) + _dot_nt(o3[2], esel))[0:1, :]
    r_s = lax.broadcasted_iota(jnp.int32, (1, NS), 1).astype(F32) - off_s
    hit = _dot_nt(rank1, esel) == r_s + 1.0
    unperm = jnp.where(hit, _dot_nt(gate, esel), 0.0).astype(BF16)

    _segment_wait(slot, buf, ys_ref, sems, False)
    y = jnp.concatenate([buf[slot, pl.ds(j, NS, stride=ROW_TILES), :] for j in range(ROW_TILES)],
                        axis=1).astype(BF16)
    h2 = h1_ref[...] + _dot(unperm, y)
    out_ref[...] = _rms(h2, g_ref[...])


def _combine(ys, sel, h1, g, segdst, cnt, big):
    T, D = h1.shape
    TM = MOE_TILE
    NS = TOP_K * TM
    return pl.pallas_call(
        _combine_kernel,
        grid_spec=pltpu.PrefetchScalarGridSpec(
            num_scalar_prefetch=3,
            grid=(T // TM,),
            in_specs=[pl.BlockSpec(memory_space=pl.ANY),
                      pl.BlockSpec((TM, LANES), lambda i, *_: (i, 0)),
                      pl.BlockSpec((TM, D), lambda i, *_: (i, 0)),
                      pl.BlockSpec((1, D), lambda i, *_: (0, 0))],
            out_specs=pl.BlockSpec((TM, D), lambda i, *_: (i, 0)),
            scratch_shapes=[pltpu.VMEM((2, NS * ROW_TILES, LANES), F32), pltpu.SemaphoreType.DMA((2,))],
        ),
        out_shape=jax.ShapeDtypeStruct((T, D), F32),
        compiler_params=pltpu.CompilerParams(
            dimension_semantics=("arbitrary",), vmem_limit_bytes=48 * 1024 * 1024),
        name="combine",
    )(segdst, cnt, big, ys, sel, h1, g)


def _routing_tables(cnt_tiles):
    BLK = EXPERT_BLOCK
    nT = cnt_tiles.shape[0]
    A = nT * MOE_TILE * TOP_K
    NB = A // BLK + N_EXPERTS
    total = jnp.sum(cnt_tiles, axis=0)
    padded = (total + BLK - 1) // BLK * BLK
    pad_ends = jnp.cumsum(padded)
    pad_starts = pad_ends - padded
    before = jnp.cumsum(cnt_tiles, axis=0) - cnt_tiles
    segdst = (pad_starts[None, :] + before).reshape(-1).astype(jnp.int32)
    block_start = jnp.arange(NB, dtype=jnp.int32) * BLK
    block_exp = jnp.minimum(jnp.sum(pad_ends[None, :] <= block_start[:, None], axis=1), N_EXPERTS - 1).astype(jnp.int32)
    n_used = (pad_ends[-1] // BLK).astype(jnp.int32).reshape(1)
    paddst = (pad_starts + total).astype(jnp.int32)
    padcnt = (padded - total).astype(jnp.int32)
    big = jnp.any(cnt_tiles >= SMALL_RUN, axis=1).astype(jnp.int32)
    has_rows = total > 0
    ids = jnp.arange(N_EXPERTS, dtype=jnp.int32)
    pos = (jnp.cumsum(has_rows) - has_rows).astype(jnp.int32)
    later = jnp.where(has_rows[None, :] & (ids[None, :] > ids[:, None]), ids[None, :], N_EXPERTS)
    nxt = jnp.min(later, axis=1)
    nxt = jnp.where(nxt < N_EXPERTS, nxt, -1).astype(jnp.int32)
    return (segdst, cnt_tiles.reshape(-1).astype(jnp.int32), big, paddst, padcnt, block_exp, n_used,
            pos[block_exp], nxt[block_exp], NB * BLK)


def _rotary_tables(S):
    half = RET_HEAD_DIM // 2
    inv_freq = ROPE_BASE ** (-jnp.arange(half, dtype=F32) / half)
    ang = jnp.arange(S, dtype=F32)[:, None] * inv_freq[None, :]
    cos, sin = jnp.cos(ang), jnp.sin(ang)
    return jnp.concatenate([cos, cos], axis=-1), jnp.concatenate([-sin, sin], axis=-1)


def _layer(h, norm_mix_g, w_in, b_forget, w_out, norm_ffn_g, w_router, b_router,
           w_exp_in, b_exp_in, w_exp_out, b_exp_out, final_g):
    B, S, D = h.shape
    R, Fw = RET_WIDTH, FOX_WIDTH
    cos, sin = _rotary_tables(S)
    wb = w_in.astype(BF16)
    wr = wb[:, :4 * R]
    wfq, wfk, wfv = (wb[:, 4 * R + i * Fw:4 * R + (i + 1) * Fw] for i in range(3))
    wzt = jnp.zeros((16, D), BF16).at[:FOX_HEADS, :].set(wb[:, 4 * R + 3 * Fw:].T)
    rq, rk, rv, rg, kaug, qaug, fvt = _in_proj(
        h, norm_mix_g[None, :], cos, sin, wr, wfk, wfq.T, wfv.T, wzt, b_forget)
    o_ret = _retention(rq, rk, rv, rg)
    o_fox = _fox_attn(qaug, kaug, fvt)

    T = B * S
    wo = w_out.astype(BF16)
    wrt = jnp.zeros((LANES, D), F32).at[:N_EXPERTS, :].set(w_router.T)
    wrh = wrt.astype(BF16)
    wrl = (wrt - wrh.astype(F32)).astype(BF16)
    br = jnp.zeros((LANES, 1), F32).at[:N_EXPERTS, 0].set(b_router)
    h1, u2, sel, cnt = _out_router(h.reshape(T, D), o_ret.reshape(T, R), o_fox.reshape(T, Fw),
                                   wo[:R], wo[R:], norm_ffn_g[None, :], wrh, wrl, br)
    cnt_tiles = cnt[:, 0, :N_EXPERTS].astype(jnp.int32)
    segdst, cnt_flat, big, paddst, padcnt, block_exp, n_used, exp_pos, exp_next, n_rows = _routing_tables(cnt_tiles)
    xs = _dispatch(u2, sel, segdst, cnt_flat, big, paddst, padcnt, n_used, n_rows)
    ys = _experts(xs, block_exp, n_used, exp_pos, exp_next, w_exp_in, b_exp_in, w_exp_out, b_exp_out)
    out = _combine(ys, sel, h1, final_g[None, :], segdst, cnt_flat, big)
    return out.reshape(B, S, D)


def kernel(x, norm_mix_g, w_in, b_forget, w_out, norm_ffn_g, w_router, b_router,
           w_exp_in, b_exp_in, w_exp_out, b_exp_out, norm_final_g):
    depth = w_in.shape[0]
    assert depth == 1, "the fused final RMSNorm assumes a single layer"
    return _layer(x, norm_mix_g[0], w_in[0], b_forget[0], w_out[0], norm_ffn_g[0], w_router[0], b_router[0],
                  w_exp_in[0], b_exp_in[0], w_exp_out[0], b_exp_out[0], norm_final_g)
```

```python
import functools

import numpy as np
import jax
import jax.numpy as jnp
from jax import lax
from jax.experimental import pallas as pl
from jax.experimental.pallas import tpu as pltpu

F32 = jnp.float32
BF16 = jnp.bfloat16

D_MODEL = 1024
RET_HEADS, RET_HEAD_DIM = 4, 128
RET_WIDTH = RET_HEADS * RET_HEAD_DIM
FOX_HEADS, FOX_HEAD_DIM = 8, 64
FOX_WIDTH = FOX_HEADS * FOX_HEAD_DIM
CHUNK = 64
ROPE_BASE = 10000.0
N_EXPERTS = 32
TOP_K = 4
D_FF = D_MODEL
SWIGLU_ALPHA = 1.702
SWIGLU_LIMIT = 7.0
RMS_EPS = 1e-5
GN_EPS = 1e-5

LANES = 128
SUBLANES = 8
ROW_TILES = D_MODEL // LANES

PROJ_TILE = 512
RET_BLOCK = 256
FOX_TQ = 512
FOX_TK = PROJ_TILE
AUG = 128
V_AUG = 80
LOG2E = 1.4426950408889634
MOE_TILE = 256
ROUTER_TILES = 4
SMALL_RUN = 64
EXPERT_BLOCK = 512
EXPERT_PASS_ROWS = 256

NT_DIMS = (((1,), (1,)), ((), ()))


def _split3(a):
    hi = a.astype(BF16)
    r1 = a - hi.astype(F32)
    mid = r1.astype(BF16)
    lo = (r1 - mid.astype(F32)).astype(BF16)
    return hi, mid, lo


def _dot(a, b):
    return jnp.dot(a, b, preferred_element_type=F32)


def _dot_nt(a, b):
    return lax.dot_general(a, b, NT_DIMS, preferred_element_type=F32)


def _rms(x, g):
    return x * lax.rsqrt(jnp.mean(x * x, axis=-1, keepdims=True) + RMS_EPS) * g


def _in_proj_kernel(x_ref, g_ref, cos_ref, sin_ref, wr_ref, wfk_ref, wfqt_ref, wfvt_ref, wzt_ref, bcol_ref,
                    selkf_ref, constk_ref, selqf_ref, constq_ref,
                    rq_ref, rk_ref, rv_ref, rg_ref, kaug_ref, qaug_ref, fvt_ref, ccol):
    TM = x_ref.shape[1]
    d, H = FOX_HEAD_DIM, FOX_HEADS
    u = _rms(x_ref[0], g_ref[...]).astype(BF16)
    r = _dot(u, wr_ref[...])
    cos, sin = cos_ref[...], sin_ref[...]
    k_scale = RET_HEAD_DIM ** -0.5
    for h in range(RET_HEADS):
        lo = h * RET_HEAD_DIM
        q = r[:, lo:lo + RET_HEAD_DIM]
        k = r[:, RET_WIDTH + lo:RET_WIDTH + lo + RET_HEAD_DIM]
        rq_ref[0, :, lo:lo + RET_HEAD_DIM] = (q * cos + pltpu.roll(q, RET_HEAD_DIM // 2, 1) * sin).astype(BF16)
        rk_ref[0, :, lo:lo + RET_HEAD_DIM] = (
            (k * cos + pltpu.roll(k, RET_HEAD_DIM // 2, 1) * sin) * k_scale).astype(BF16)
    rv_ref[0] = r[:, 2 * RET_WIDTH:3 * RET_WIDTH].astype(BF16)
    rg_ref[0] = r[:, 3 * RET_WIDTH:4 * RET_WIDTH].astype(BF16)
    fvt_ref[0, 0] = _dot_nt(wfvt_ref[...], u).astype(BF16)
    fk = _dot(u, wfk_ref[...])
    fqt = (_dot_nt(wfqt_ref[...], u) * (d ** -0.5 * LOG2E)).astype(BF16)
    zt = _dot_nt(wzt_ref[...], u)

    @pl.when(pl.program_id(1) == 0)
    def _():
        ccol[...] = jnp.zeros_like(ccol)

    row = lax.broadcasted_iota(jnp.int32, (16, TM), 0)
    lft = jnp.where(row < H, jax.nn.log_sigmoid(zt + bcol_ref[...]), 0.0)
    utri = (lax.broadcasted_iota(jnp.int32, (TM, TM), 0) <= lax.broadcasted_iota(jnp.int32, (TM, TM), 1)).astype(BF16)
    t3 = _split3(lft)
    f_col = _dot(t3[0], utri) + _dot(t3[1], utri) + _dot(t3[2], utri) + ccol[:, 0:1]
    ccol[...] = jnp.broadcast_to(f_col[:, TM - 1:TM], ccol.shape)

    pieces_t = jnp.concatenate(_split3(f_col * LOG2E), axis=0)
    for h in range(H):
        extra = _dot(selqf_ref[h], pieces_t) + constq_ref[...]
        qaug_ref[0, h, 0] = jnp.concatenate([fqt[h * d:(h + 1) * d, :], extra.astype(BF16)], axis=0)

    f_row = jnp.concatenate([f_col, jnp.zeros((LANES - 16, TM), F32)], axis=0).T
    n3 = _split3(f_row * -LOG2E)
    pieces = (n3[0].astype(F32) + pltpu.roll(n3[1].astype(F32), H, 1)
              + pltpu.roll(n3[2].astype(F32), 2 * H, 1)).astype(BF16)
    lane = lax.broadcasted_iota(jnp.int32, (TM, LANES), 1)
    for g in range(H // 2):
        bias = _dot(pieces, selkf_ref[g])
        kg = fk[:, g * 2 * d:(g + 1) * 2 * d]
        for o in range(2):
            kh = kg if o == 0 else pltpu.roll(kg, d, 1)
            extra = bias[:, o * AUG:(o + 1) * AUG] + constk_ref[...]
            kaug_ref[0, 2 * g + o] = jnp.where(lane < d, kh, extra).astype(BF16)


def _in_proj(x, g, cos, sin, wr, wfk, wfqt, wfvt, wzt, b_forget):
    B, S, D = x.shape
    TM = PROJ_TILE
    ns = S // TM
    selkf, constk, selqf, constq = _fox_prep_constants()
    bcol = jnp.zeros((16, 1), F32).at[:FOX_HEADS, 0].set(b_forget)
    consts = (wr, wfk, wfqt, wfvt, wzt, bcol, selkf, constk, selqf, constq)
    const = lambda a: pl.BlockSpec(a.shape, lambda b, s: (0,) * a.ndim)
    tok = lambda w: pl.BlockSpec((1, TM, w), lambda b, s: (b, s, 0))
    out_shape = (
        jax.ShapeDtypeStruct((B, S, RET_WIDTH), BF16),
        jax.ShapeDtypeStruct((B, S, RET_WIDTH), BF16),
        jax.ShapeDtypeStruct((B, S, RET_WIDTH), BF16),
        jax.ShapeDtypeStruct((B, S, RET_WIDTH), BF16),
        jax.ShapeDtypeStruct((B, FOX_HEADS, S, AUG), BF16),
        jax.ShapeDtypeStruct((B, FOX_HEADS, ns, AUG, TM), BF16),
        jax.ShapeDtypeStruct((B, ns, FOX_WIDTH, TM), BF16),
    )
    return pl.pallas_call(
        _in_proj_kernel,
        grid=(B, ns),
        in_specs=[
            pl.BlockSpec((1, TM, D), lambda b, s: (b, s, 0)),
            pl.BlockSpec((1, D), lambda b, s: (0, 0)),
            pl.BlockSpec((TM, RET_HEAD_DIM), lambda b, s: (s, 0)),
            pl.BlockSpec((TM, RET_HEAD_DIM), lambda b, s: (s, 0)),
        ] + [const(a) for a in consts],
        out_specs=(
            tok(RET_WIDTH), tok(RET_WIDTH), tok(RET_WIDTH), tok(RET_WIDTH),
            pl.BlockSpec((1, FOX_HEADS, TM, AUG), lambda b, s: (b, 0, s, 0)),
            pl.BlockSpec((1, FOX_HEADS, 1, AUG, TM), lambda b, s: (b, 0, s, 0, 0)),
            pl.BlockSpec((1, 1, FOX_WIDTH, TM), lambda b, s: (b, s, 0, 0)),
        ),
        out_shape=out_shape,
        scratch_shapes=[pltpu.VMEM((16, LANES), F32)],
        compiler_params=pltpu.CompilerParams(
            dimension_semantics=("arbitrary", "arbitrary"), vmem_limit_bytes=48 * 1024 * 1024),
        name="in_proj",
    )(x, g, cos, sin, *consts)


def _fox_prep_constants():
    d, H = FOX_HEAD_DIM, FOX_HEADS
    selkf = np.zeros((H // 2, LANES, 2 * AUG), np.float32)
    constk = np.zeros((1, AUG), np.float32)
    selqf = np.zeros((H, d, 48), np.float32)
    constq = np.zeros((d, 1), np.float32)
    for p in range(3):
        constk[0, d + p] = 1.0
        constq[3 + p, 0] = 1.0
        for h in range(H):
            selkf[h // 2, p * H + h, (h % 2) * AUG + d + 3 + p] = 1.0
            selqf[h, p, p * 16 + h] = 1.0
    return jnp.asarray(selkf, BF16), jnp.asarray(constk, F32), jnp.asarray(selqf, BF16), jnp.asarray(constq, F32)


def _retention_kernel(q_ref, k_ref, v_ref, g_ref, dec_ref, qw_ref, kw_ref, cd_ref, o_ref, state):
    @pl.when(pl.program_id(1) == 0)
    def _():
        state[...] = jnp.zeros_like(state)

    for bb, h in [(bb, h) for bb in range(q_ref.shape[0]) for h in range(RET_HEADS)]:
        hs = slice(h * RET_HEAD_DIM, (h + 1) * RET_HEAD_DIM)
        q, k, v = q_ref[bb, :, hs], k_ref[bb, :, hs], v_ref[bb, :, hs]
        scores = (_dot_nt(q, k) * dec_ref[h]).astype(BF16)
        st = state[bb, h]
        o = _dot(scores, v) + _dot((q.astype(F32) * qw_ref[h]).astype(BF16), st.astype(BF16))
        kk = k.astype(F32) * kw_ref[h]
        state[bb, h] = st * cd_ref[h, 0:1, :] + _dot(kk.T.astype(BF16), v)
        mu = jnp.mean(o, axis=-1, keepdims=True)
        oc = o - mu
        var = jnp.mean(oc * oc, axis=-1, keepdims=True)
        o_ref[bb, :, hs] = (oc * lax.rsqrt(var + GN_EPS) * jax.nn.silu(g_ref[bb, :, hs].astype(F32))).astype(BF16)


def _retention_tables():
    L = RET_BLOCK
    log_gamma = jnp.log1p(-jnp.exp2(-5.0 - jnp.arange(RET_HEADS, dtype=F32)))
    p = jnp.arange(L, dtype=F32)
    dist = jnp.abs(p[:, None] - p[None, :])
    chunk = jnp.arange(L) // CHUNK
    allowed = (chunk[None, :] <= chunk[:, None]).astype(F32)
    dec = jnp.exp(log_gamma[:, None, None] * dist) * allowed
    lanes = lambda a: jnp.broadcast_to(a[:, :, None], (RET_HEADS, L, RET_HEAD_DIM))
    qw = lanes(jnp.exp(log_gamma[:, None] * (p[None, :] + 1.0)))
    kw = lanes(jnp.exp(log_gamma[:, None] * (L - 1.0 - p[None, :])))
    cd = jnp.broadcast_to(jnp.exp(log_gamma * L)[:, None, None], (RET_HEADS, SUBLANES, RET_HEAD_DIM))
    return dec, qw, kw, cd


def _retention(rq, rk, rv, rg):
    B, S, _ = rq.shape
    L = RET_BLOCK
    dec, qw, kw, cd = _retention_tables()
    nb = 2 if B % 2 == 0 else 1
    tok = pl.BlockSpec((nb, L, RET_WIDTH), lambda b, s: (b, s, 0))
    const = lambda a: pl.BlockSpec(a.shape, lambda b, s: (0,) * a.ndim)
    return pl.pallas_call(
        _retention_kernel,
        grid=(B // nb, S // L),
        in_specs=[tok, tok, tok, tok, const(dec), const(qw), const(kw), const(cd)],
        out_specs=tok,
        out_shape=jax.ShapeDtypeStruct((B, S, RET_WIDTH), BF16),
        scratch_shapes=[pltpu.VMEM((nb, RET_HEADS, RET_HEAD_DIM, RET_HEAD_DIM), F32)],
        compiler_params=pltpu.CompilerParams(dimension_semantics=("arbitrary",) * 2),
        name="retention",
    )(rq, rk, rv, rg, dec, qw, kw, cd)


def _fox_attn_kernel(q_ref, k_ref, v_ref, o_ref, s_a, s_b, s_c, cm_a, cm_b, cm_c, m_ref, acc_ref):
    T = FOX_TQ
    d = FOX_HEAD_DIM
    nq = q_ref.shape[2]
    ones_rows = (lax.broadcasted_iota(jnp.int32, (V_AUG - d, T), 0) == 0).astype(BF16)

    def scores(qi, j, s_ref, cm_ref):
        for hh in range(2):
            kj = k_ref[0, hh, pl.ds(pl.multiple_of(j * T, T), T), :]
            st = _dot(kj, q_ref[0, hh, qi])
            s_ref[hh] = st
            cm_ref[hh] = jnp.max(st, axis=0, keepdims=True)

    def consume(j, s_ref, cm_ref, masked):
        for hh in range(2):
            st = s_ref[hh]
            if masked:
                key = lax.broadcasted_iota(jnp.int32, (T, T), 0)
                qry = lax.broadcasted_iota(jnp.int32, (T, T), 1)
                st = jnp.where(key <= qry, st, -jnp.inf)
                cm = jnp.max(st, axis=0, keepdims=True)
            else:
                cm = cm_ref[hh]
            m = m_ref[hh]
            m_new = jnp.maximum(m, cm)
            p = jnp.exp2((st - m_new).astype(BF16))
            vj = jnp.concatenate([v_ref[0, j, hh * d:(hh + 1) * d, :], ones_rows], axis=0)
            acc_ref[hh] = jnp.exp2(m - m_new) * acc_ref[hh] + _dot(vj, p)
            m_ref[hh] = m_new

    def reset():
        m_ref[...] = jnp.full(m_ref.shape, -jnp.inf, F32)
        acc_ref[...] = jnp.zeros(acc_ref.shape, F32)

    def prefetch_next(qi):
        @pl.when(qi + 1 < nq)
        def _():
            scores(qi + 1, 0, s_c, cm_c)

    def finish(qi):
        outs = [acc_ref[hh, 0:d, :] / acc_ref[hh, d:d + 1, :] for hh in range(2)]
        o_ref[0, pl.ds(pl.multiple_of(qi * T, T), T), :] = jnp.concatenate(outs, axis=0).T.astype(BF16)

    reset()
    scores(0, 0, s_a, cm_a)
    prefetch_next(0)
    consume(0, s_a, cm_a, True)
    finish(0)

    def query_tile(qi, carry):
        reset()
        scores(qi, 1, s_a, cm_a)
        consume(0, s_c, cm_c, False)

        def pair(j):
            scores(qi, j + 1, s_b, cm_b)
            consume(j, s_a, cm_a, False)
            scores(qi, j + 2, s_a, cm_a)
            consume(j + 1, s_b, cm_b, False)

        def two_pairs(jj, c):
            pair(1 + 4 * jj)
            pair(3 + 4 * jj)
            return c

        def one_pair(jj, c):
            pair(1 + 4 * (n_pairs // 2) + 2 * jj)
            return c

        n_pairs = (qi - 1) // 2
        lax.fori_loop(0, n_pairs // 2, two_pairs, 0)
        lax.fori_loop(0, n_pairs % 2, one_pair, 0)

        @pl.when(qi % 2 == 1)
        def _():
            prefetch_next(qi)
            consume(qi, s_a, cm_a, True)

        @pl.when(qi % 2 == 0)
        def _():
            scores(qi, qi, s_b, cm_b)
            consume(qi - 1, s_a, cm_a, False)
            prefetch_next(qi)
            consume(qi, s_b, cm_b, True)

        finish(qi)
        return carry

    lax.fori_loop(1, nq, query_tile, 0)


def _fox_attn(qaug, kaug, fvt):
    B, H, S, _ = kaug.shape
    nk = S // FOX_TK
    nq = S // FOX_TQ
    score_buf = pltpu.VMEM((2, FOX_TK, FOX_TQ), F32)
    col_max = pltpu.VMEM((2, 1, FOX_TQ), F32)
    return pl.pallas_call(
        _fox_attn_kernel,
        grid=(B, H // 2),
        in_specs=[
            pl.BlockSpec((1, 2, nq, AUG, FOX_TQ), lambda b, p: (b, p, 0, 0, 0)),
            pl.BlockSpec((1, 2, S, AUG), lambda b, p: (b, p, 0, 0)),
            pl.BlockSpec((1, nk, 2 * FOX_HEAD_DIM, FOX_TK), lambda b, p: (b, 0, p, 0)),
        ],
        out_specs=pl.BlockSpec((1, S, 2 * FOX_HEAD_DIM), lambda b, p: (b, 0, p)),
        out_shape=jax.ShapeDtypeStruct((B, S, FOX_WIDTH), BF16),
        scratch_shapes=[
            score_buf, score_buf, score_buf, col_max, col_max, col_max,
            pltpu.VMEM((2, 1, FOX_TQ), F32), pltpu.VMEM((2, V_AUG, FOX_TQ), F32),
        ],
        compiler_params=pltpu.CompilerParams(
            dimension_semantics=("arbitrary",) * 2, vmem_limit_bytes=48 * 1024 * 1024),
        name="fox_attn",
    )(qaug, kaug, fvt)


def _out_router_kernel(x_ref, oret_ref, ofox_ref, wor_ref, wof_ref, g_ref, wrh_ref, wrl_ref, br_ref,
                       h1_ref, u2_ref, sel_ref, cnt_ref):
    TM = MOE_TILE
    rows = lambda t: slice(t * TM, (t + 1) * TM)

    def out_proj(t):
        rs = rows(t)
        h1 = x_ref[rs] + _dot(oret_ref[rs], wor_ref[...]) + _dot(ofox_ref[rs], wof_ref[...])
        h1_ref[rs] = h1
        return h1

    def router_logits(t, h1):
        u2 = _rms(h1, g_ref[...])
        uh = u2.astype(BF16)
        u2_ref[rows(t)] = uh
        ul = (u2 - uh.astype(F32)).astype(BF16)
        return (_dot_nt(wrh_ref[...], uh) + _dot_nt(wrh_ref[...], ul) + _dot_nt(wrl_ref[...], uh)
                + br_ref[...])

    def top_k(t, logits):
        rs = rows(t)
        row = lax.broadcasted_iota(jnp.int32, (LANES, TM), 0).astype(F32)
        l = jnp.where(row < N_EXPERTS, logits, -jnp.inf)
        picks, vals = [], []
        for _ in range(TOP_K):
            m = jnp.max(l, axis=0, keepdims=True)
            idx = jnp.min(jnp.where(l == m, row, float(LANES)), axis=0, keepdims=True)
            pick = row == idx
            picks.append(pick)
            vals.append(m)
            l = jnp.where(pick, -jnp.inf, l)
        exps = [jnp.exp(v - vals[0]) for v in vals]
        den = exps[0] + exps[1] + exps[2] + exps[3]
        sel_t = jnp.full((LANES, TM), -1.0, F32)
        for pick, e in zip(picks, exps):
            sel_t = jnp.where(pick, e / den, sel_t)
        sel = sel_t.T
        sel_ref[rs] = sel
        cnt = jnp.sum((sel >= 0.0).astype(F32), axis=0, keepdims=True)
        cnt_ref[t] = jnp.broadcast_to(cnt, (SUBLANES, LANES))

    n = ROUTER_TILES
    h1s, lgs = {0: out_proj(0)}, {}
    for t in range(1, n + 2):
        if t < n:
            h1s[t] = out_proj(t)
        if 1 <= t <= n:
            lgs[t - 1] = router_logits(t - 1, h1s.pop(t - 1))
        if t >= 2:
            top_k(t - 2, lgs.pop(t - 2))


def _out_router(x2, o_ret, o_fox, wor, wof, g, wrh, wrl, br):
    T, D = x2.shape
    TM = MOE_TILE * ROUTER_TILES
    nT = T // MOE_TILE
    const = lambda a: pl.BlockSpec(a.shape, lambda i: (0,) * a.ndim)
    tok = lambda w: pl.BlockSpec((TM, w), lambda i: (i, 0))
    return pl.pallas_call(
        _out_router_kernel,
        grid=(T // TM,),
        in_specs=[tok(D), tok(RET_WIDTH), tok(FOX_WIDTH), const(wor), const(wof), const(g),
                  const(wrh), const(wrl), const(br)],
        out_specs=(tok(D), tok(D), tok(LANES),
                   pl.BlockSpec((ROUTER_TILES, SUBLANES, LANES), lambda i: (i, 0, 0))),
        out_shape=(
            jax.ShapeDtypeStruct((T, D), F32),
            jax.ShapeDtypeStruct((T, D), BF16),
            jax.ShapeDtypeStruct((T, LANES), F32),
            jax.ShapeDtypeStruct((nT, SUBLANES, LANES), F32),
        ),
        compiler_params=pltpu.CompilerParams(dimension_semantics=("arbitrary",)),
        name="out_router",
    )(x2, o_ret, o_fox, wor, wof, g, wrh, wrl, br)


def _tile_sort(sel):
    TM = sel.shape[0]
    NS = TOP_K * TM
    maskf = (sel >= 0.0).astype(F32)
    mask = maskf.astype(BF16)
    ri = lax.broadcasted_iota(jnp.int32, (TM, TM), 0)
    ci = lax.broadcasted_iota(jnp.int32, (TM, TM), 1)
    rank1 = maskf * _dot((ri >= ci).astype(BF16), mask)
    cnt = jnp.sum(maskf, axis=0, keepdims=True)
    ei = lax.broadcasted_iota(jnp.int32, (LANES, LANES), 0)
    ej = lax.broadcasted_iota(jnp.int32, (LANES, LANES), 1)
    cnt8 = jnp.broadcast_to(cnt, (SUBLANES, LANES)).astype(BF16)
    off = _dot(cnt8, (ei < ej).astype(BF16))[0:1, :]
    slot = lax.broadcasted_iota(jnp.int32, (NS, LANES), 0).astype(F32)
    esel = ((slot >= off) & (slot < off + cnt)).astype(BF16)
    return rank1.astype(BF16), esel, off, cnt


def _segment_wait(slot, local, remote_rows, sem, to_remote):
    whole = local.at[slot]
    rem = remote_rows.at[pl.ds(0, whole.shape[0]), :]
    cp = (pltpu.make_async_copy(whole, rem, sem.at[slot]) if to_remote
          else pltpu.make_async_copy(rem, whole, sem.at[slot]))
    cp.wait()


def _segment_dmas(step, slot, segdst_ref, cnt_ref, big_ref, local, remote_rows, sem, to_remote):
    big = big_ref[step] != 0
    for cond, top_bit in ((big, MOE_TILE), (jnp.logical_not(big), SMALL_RUN // 2)):
        pl.when(cond)(functools.partial(
            _segment_dma_path, step, slot, segdst_ref, cnt_ref, local, remote_rows, sem, to_remote, top_bit))


def _segment_dma_path(step, slot, segdst_ref, cnt_ref, local, remote_rows, sem, to_remote, top_bit):
    def body(e, off):
        c = cnt_ref[step * N_EXPERTS + e]
        dst = segdst_ref[step * N_EXPERTS + e]
        bit = top_bit
        while bit >= 1:
            done = c & (~(2 * bit - 1))

            @pl.when((c & bit) != 0)
            def _(bit=bit, done=done):
                loc = local.at[slot, pl.ds((off + done) * ROW_TILES, bit * ROW_TILES), :]
                rem = remote_rows.at[pl.ds((dst + done) * ROW_TILES, bit * ROW_TILES), :]
                cp = (pltpu.make_async_copy(loc, rem, sem.at[slot]) if to_remote
                      else pltpu.make_async_copy(rem, loc, sem.at[slot]))
                cp.start()
            bit //= 2
        return off + c

    off = 0
    for e in range(N_EXPERTS):
        off = body(e, off)


def _dispatch_kernel(segdst_ref, cnt_ref, big_ref, paddst_ref, padcnt_ref, nused_ref, u2_ref, sel_ref, xs_ref,
                     buf, zbuf, sems, zsem):
    i = pl.program_id(0)
    last = pl.num_programs(0) - 1
    slot = i % 2
    TM = MOE_TILE
    NS = TOP_K * TM
    rank1, esel, off, _ = _tile_sort(sel_ref[...])
    slot_id = lax.broadcasted_iota(jnp.int32, (NS, 1), 0).astype(F32)
    r_s = slot_id - jnp.sum(esel.astype(F32) * off, axis=1, keepdims=True)
    perm = (_dot_nt(esel, rank1) == r_s + 1.0).astype(BF16)

    @pl.when(i >= 2)
    def _():
        _segment_wait(slot, buf, xs_ref, sems, True)

    u2 = u2_ref[...]
    for c in range(NS // TM):
        rows = _dot(perm[c * TM:(c + 1) * TM], u2)
        for j in range(ROW_TILES):
            buf[slot, pl.ds(c * TM * ROW_TILES + j, TM, stride=ROW_TILES), :] = rows[:, j * LANES:(j + 1) * LANES]
    _segment_dmas(i, slot, segdst_ref, cnt_ref, big_ref, buf, xs_ref, sems, True)

    @pl.when(i == last)
    def _():
        @pl.when(i >= 1)
        def _():
            _segment_wait(1 - slot, buf, xs_ref, sems, True)
        _segment_wait(slot, buf, xs_ref, sems, True)
        zbuf[...] = jnp.zeros_like(zbuf)
        half = EXPERT_BLOCK // 2 * ROW_TILES
        n_blocks = xs_ref.shape[0] // (EXPERT_BLOCK * ROW_TILES)
        for wait in (False, True):
            def unused(hb, carry, wait=wait):
                cp = pltpu.make_async_copy(zbuf, xs_ref.at[pl.ds(hb * half, half), :], zsem.at[0])
                cp.wait() if wait else cp.start()
                return carry
            lax.fori_loop(2 * nused_ref[0], 2 * n_blocks, unused, 0)


            def body(e, carry, wait=wait):
                c = padcnt_ref[e]
                dst = paddst_ref[e]
                bit = EXPERT_BLOCK // 2
                while bit >= 1:
                    done = c & (~(2 * bit - 1))

                    @pl.when((c & bit) != 0)
                    def _(bit=bit, done=done):
                        cp = pltpu.make_async_copy(
                            zbuf.at[pl.ds(0, bit * ROW_TILES), :],
                            xs_ref.at[pl.ds((dst + done) * ROW_TILES, bit * ROW_TILES), :], zsem.at[0])
                        cp.wait() if wait else cp.start()
                    bit //= 2
                return carry
            lax.fori_loop(0, N_EXPERTS, body, 0)


def _dispatch(u2, sel, segdst, cnt, big, paddst, padcnt, n_used, n_rows):
    T, D = u2.shape
    TM = MOE_TILE
    NS = TOP_K * TM
    return pl.pallas_call(
        _dispatch_kernel,
        grid_spec=pltpu.PrefetchScalarGridSpec(
            num_scalar_prefetch=6,
            grid=(T // TM,),
            in_specs=[pl.BlockSpec((TM, D), lambda i, *_: (i, 0)),
                      pl.BlockSpec((TM, LANES), lambda i, *_: (i, 0))],
            out_specs=pl.BlockSpec(memory_space=pl.ANY),
            scratch_shapes=[pltpu.VMEM((2, NS * ROW_TILES, LANES), F32),
                            pltpu.VMEM((EXPERT_BLOCK // 2 * ROW_TILES, LANES), F32),
                            pltpu.SemaphoreType.DMA((2,)), pltpu.SemaphoreType.DMA((1,))],
        ),
        out_shape=jax.ShapeDtypeStruct((n_rows * ROW_TILES, LANES), F32),
        compiler_params=pltpu.CompilerParams(
            dimension_semantics=("arbitrary",), vmem_limit_bytes=48 * 1024 * 1024),
        name="dispatch",
    )(segdst, cnt, big, paddst, padcnt, n_used, u2, sel)


def _expert_kernel(bexp_ref, nused_ref, epos_ref, enext_ref, xs_ref, w1_hbm, b1_ref, w2_hbm, b2_ref, ys_ref,
                   w1f, w2f, w1b, w2b, wsem):
    b = pl.program_id(0)
    BLK = EXPERT_BLOCK
    used = b < nused_ref[0]

    def weight_copies(e, slot):
        return (pltpu.make_async_copy(w1_hbm.at[e], w1f.at[slot], wsem.at[0, slot]),
                pltpu.make_async_copy(w2_hbm.at[e], w2f.at[slot], wsem.at[1, slot]))

    @pl.when(used)
    def _():
        e = bexp_ref[b]
        prev = bexp_ref[jnp.maximum(b - 1, 0)]
        slot = epos_ref[b] % 2

        @pl.when(b == 0)
        def _():
            for cp in weight_copies(e, slot):
                cp.start()

        @pl.when((b == 0) | (e != prev))
        def _():
            nxt = enext_ref[b]

            @pl.when(nxt >= 0)
            def _():
                for cp in weight_copies(nxt, 1 - slot):
                    cp.start()

            for cp in weight_copies(e, slot):
                cp.wait()
            rows = 128

            def cast(r, carry):
                sl = pl.ds(pl.multiple_of(r * rows, rows), rows)
                w1b[sl, :] = w1f[slot, sl, :].astype(BF16)
                w2b[sl, :] = w2f[slot, sl, :].astype(BF16)
                return carry
            lax.fori_loop(0, D_MODEL // rows, cast, 0)

        R = EXPERT_PASS_ROWS
        for rp in range(BLK // R):
            r0 = rp * R * ROW_TILES
            x = jnp.concatenate([xs_ref[pl.ds(r0 + j, R, stride=ROW_TILES), :] for j in range(ROW_TILES)],
                                axis=1).astype(BF16)
            h = _dot(x, w1b[...]) + b1_ref[0]
            glu = jnp.minimum(h[:, :D_FF], SWIGLU_LIMIT)
            lin = jnp.clip(h[:, D_FF:], -SWIGLU_LIMIT, SWIGLU_LIMIT)
            act = glu * jax.nn.sigmoid(SWIGLU_ALPHA * glu) * (lin + 1.0)
            y = _dot(act.astype(BF16), w2b[...]) + b2_ref[0]
            for j in range(ROW_TILES):
                ys_ref[pl.ds(r0 + j, R, stride=ROW_TILES), :] = y[:, j * LANES:(j + 1) * LANES]

    @pl.when(jnp.logical_not(used))
    def _():
        ys_ref[...] = jnp.zeros_like(ys_ref)


def _experts(xs, block_exp, n_used, exp_pos, exp_next, w1, b1, w2, b2):
    BLK = EXPERT_BLOCK
    NB = xs.shape[0] // (BLK * ROW_TILES)
    blk = lambda b, nused: jnp.minimum(b, nused[0] - 1)
    return pl.pallas_call(
        _expert_kernel,
        grid_spec=pltpu.PrefetchScalarGridSpec(
            num_scalar_prefetch=4,
            grid=(NB,),
            in_specs=[
                pl.BlockSpec((BLK * ROW_TILES, LANES), lambda b, bexp, nused, *_: (blk(b, nused), 0)),
                pl.BlockSpec(memory_space=pl.ANY),
                pl.BlockSpec((1, 1, 2 * D_FF), lambda b, bexp, nused, *_: (bexp[blk(b, nused)], 0, 0)),
                pl.BlockSpec(memory_space=pl.ANY),
                pl.BlockSpec((1, 1, D_MODEL), lambda b, bexp, nused, *_: (bexp[blk(b, nused)], 0, 0)),
            ],
            out_specs=pl.BlockSpec((BLK * ROW_TILES, LANES), lambda b, *_: (b, 0)),
            scratch_shapes=[pltpu.VMEM((2, D_MODEL, 2 * D_FF), F32), pltpu.VMEM((2, D_FF, D_MODEL), F32),
                            pltpu.VMEM((D_MODEL, 2 * D_FF), BF16), pltpu.VMEM((D_FF, D_MODEL), BF16),
                            pltpu.SemaphoreType.DMA((2, 2))],
        ),
        out_shape=jax.ShapeDtypeStruct(xs.shape, F32),
        compiler_params=pltpu.CompilerParams(
            dimension_semantics=("arbitrary",), vmem_limit_bytes=56 * 1024 * 1024),
        name="experts",
    )(block_exp, n_used, exp_pos, exp_next, xs, w1, b1[:, None, :], w2, b2[:, None, :])


def _combine_kernel(segdst_ref, cnt_ref, big_ref, ys_ref, sel_ref, h1_ref, g_ref, out_ref, buf, sems):
    i = pl.program_id(0)
    n = pl.num_programs(0)
    slot = i % 2
    TM = MOE_TILE
    NS = TOP_K * TM

    @pl.when(i == 0)
    def _():
        _segment_dmas(i, slot, segdst_ref, cnt_ref, big_ref, buf, ys_ref, sems, False)

    @pl.when(i + 1 < n)
    def _():
        _segment_dmas(i + 1, 1 - slot, segdst_ref, cnt_ref, big_ref, buf, ys_ref, sems, False)

    sel = sel_ref[...]
    rank1, esel, off, _ = _tile_sort(sel)
    gate = jnp.maximum(sel, 0.0).astype(BF16)
    o3 = _split3(jnp.broadcast_to(off, (SUBLANES, LANES)))
    off_s = (_dot_nt(o3[0], esel) + _dot_nt(o3[1], esel) + _dot_nt(o3[2], esel))[0:1, :]
    r_s = lax.broadcasted_iota(jnp.int32, (1, NS), 1).astype(F32) - off_s
    hit = _dot_nt(rank1, esel) == r_s + 1.0
    unperm = jnp.where(hit, _dot_nt(gate, esel), 0.0).astype(BF16)

    _segment_wait(slot, buf, ys_ref, sems, False)
    y = jnp.concatenate([buf[slot, pl.ds(j, NS, stride=ROW_TILES), :] for j in range(ROW_TILES)],
                        axis=1).astype(BF16)
    h2 = h1_ref[...] + _dot(unperm, y)
    out_ref[...] = _rms(h2, g_ref[...])


def _combine(ys, sel, h1, g, segdst, cnt, big):
    T, D = h1.shape
    TM = MOE_TILE
    NS = TOP_K * TM
    return pl.pallas_call(
        _combine_kernel,
        grid_spec=pltpu.PrefetchScalarGridSpec(
            num_scalar_prefetch=3,
            grid=(T // TM,),
            in_specs=[pl.BlockSpec(memory_space=pl.ANY),
                      pl.BlockSpec((TM, LANES), lambda i, *_: (i, 0)),
                      pl.BlockSpec((TM, D), lambda i, *_: (i, 0)),
                      pl.BlockSpec((1, D), lambda i, *_: (0, 0))],
            out_specs=pl.BlockSpec((TM, D), lambda i, *_: (i, 0)),
            scratch_shapes=[pltpu.VMEM((2, NS * ROW_TILES, LANES), F32), pltpu.SemaphoreType.DMA((2,))],
        ),
        out_shape=jax.ShapeDtypeStruct((T, D), F32),
        compiler_params=pltpu.CompilerParams(
            dimension_semantics=("arbitrary",), vmem_limit_bytes=48 * 1024 * 1024),
        name="combine",
    )(segdst, cnt, big, ys, sel, h1, g)


def _routing_tables(cnt_tiles):
    BLK = EXPERT_BLOCK
    nT = cnt_tiles.shape[0]
    A = nT * MOE_TILE * TOP_K
    NB = A // BLK + N_EXPERTS
    total = jnp.sum(cnt_tiles, axis=0)
    padded = (total + BLK - 1) // BLK * BLK
    pad_ends = jnp.cumsum(padded)
    pad_starts = pad_ends - padded
    before = jnp.cumsum(cnt_tiles, axis=0) - cnt_tiles
    segdst = (pad_starts[None, :] + before).reshape(-1).astype(jnp.int32)
    block_start = jnp.arange(NB, dtype=jnp.int32) * BLK
    block_exp = jnp.minimum(jnp.sum(pad_ends[None, :] <= block_start[:, None], axis=1), N_EXPERTS - 1).astype(jnp.int32)
    n_used = (pad_ends[-1] // BLK).astype(jnp.int32).reshape(1)
    paddst = (pad_starts + total).astype(jnp.int32)
    padcnt = (padded - total).astype(jnp.int32)
    big = jnp.any(cnt_tiles >= SMALL_RUN, axis=1).astype(jnp.int32)
    has_rows = total > 0
    ids = jnp.arange(N_EXPERTS, dtype=jnp.int32)
    pos = (jnp.cumsum(has_rows) - has_rows).astype(jnp.int32)
    later = jnp.where(has_rows[None, :] & (ids[None, :] > ids[:, None]), ids[None, :], N_EXPERTS)
    nxt = jnp.min(later, axis=1)
    nxt = jnp.where(nxt < N_EXPERTS, nxt, -1).astype(jnp.int32)
    return (segdst, cnt_tiles.reshape(-1).astype(jnp.int32), big, paddst, padcnt, block_exp, n_used,
            pos[block_exp], nxt[block_exp], NB * BLK)


def _rotary_tables(S):
    half = RET_HEAD_DIM // 2
    inv_freq = ROPE_BASE ** (-jnp.arange(half, dtype=F32) / half)
    ang = jnp.arange(S, dtype=F32)[:, None] * inv_freq[None, :]
    cos, sin = jnp.cos(ang), jnp.sin(ang)
    return jnp.concatenate([cos, cos], axis=-1), jnp.concatenate([-sin, sin], axis=-1)


def _layer(h, norm_mix_g, w_in, b_forget, w_out, norm_ffn_g, w_router, b_router,
           w_exp_in, b_exp_in, w_exp_out, b_exp_out, final_g):
    B, S, D = h.shape
    R, Fw = RET_WIDTH, FOX_WIDTH
    cos, sin = _rotary_tables(S)
    wb = w_in.astype(BF16)
    wr = wb[:, :4 * R]
    wfq, wfk, wfv = (wb[:, 4 * R + i * Fw:4 * R + (i + 1) * Fw] for i in range(3))
    wzt = jnp.zeros((16, D), BF16).at[:FOX_HEADS, :].set(wb[:, 4 * R + 3 * Fw:].T)
    rq, rk, rv, rg, kaug, qaug, fvt = _in_proj(
        h, norm_mix_g[None, :], cos, sin, wr, wfk, wfq.T, wfv.T, wzt, b_forget)
    o_ret = _retention(rq, rk, rv, rg)
    o_fox = _fox_attn(qaug, kaug, fvt)

    T = B * S
    wo = w_out.astype(BF16)
    wrt = jnp.zeros((LANES, D), F32).at[:N_EXPERTS, :].set(w_router.T)
    wrh = wrt.astype(BF16)
    wrl = (wrt - wrh.astype(F32)).astype(BF16)
    br = jnp.zeros((LANES, 1), F32).at[:N_EXPERTS, 0].set(b_router)
    h1, u2, sel, cnt = _out_router(h.reshape(T, D), o_ret.reshape(T, R), o_fox.reshape(T, Fw),
                                   wo[:R], wo[R:], norm_ffn_g[None, :], wrh, wrl, br)
    cnt_tiles = cnt[:, 0, :N_EXPERTS].astype(jnp.int32)
    segdst, cnt_flat, big, paddst, padcnt, block_exp, n_used, exp_pos, exp_next, n_rows = _routing_tables(cnt_tiles)
    xs = _dispatch(u2, sel, segdst, cnt_flat, big, paddst, padcnt, n_used, n_rows)
    ys = _experts(xs, block_exp, n_used, exp_pos, exp_next, w_exp_in, b_exp_in, w_exp_out, b_exp_out)
    out = _combine(ys, sel, h1, final_g[None, :], segdst, cnt_flat, big)
    return out.reshape(B, S, D)


def kernel(x, norm_mix_g, w_in, b_forget, w_out, norm_ffn_g, w_router, b_router,
           w_exp_in, b_exp_in, w_exp_out, b_exp_out, norm_final_g):
    depth = w_in.shape[0]
    assert depth == 1, "the fused final RMSNorm assumes a single layer"
    return _layer(x, norm_mix_g[0], w_in[0], b_forget[0], w_out[0], norm_ffn_g[0], w_router[0], b_router[0],
                  w_exp_in[0], b_exp_in[0], w_exp_out[0], b_exp_out[0], norm_final_g)
```

```python
import functools

import numpy as np
import jax
import jax.numpy as jnp
from jax import lax
from jax.experimental import pallas as pl
from jax.experimental.pallas import tpu as pltpu

F32 = jnp.float32
BF16 = jnp.bfloat16

D_MODEL = 1024
RET_HEADS, RET_HEAD_DIM = 4, 128
RET_WIDTH = RET_HEADS * RET_HEAD_DIM
FOX_HEADS, FOX_HEAD_DIM = 8, 64
FOX_WIDTH = FOX_HEADS * FOX_HEAD_DIM
CHUNK = 64
ROPE_BASE = 10000.0
N_EXPERTS = 32
TOP_K = 4
D_FF = D_MODEL
SWIGLU_ALPHA = 1.702
SWIGLU_LIMIT = 7.0
RMS_EPS = 1e-5
GN_EPS = 1e-5

LANES = 128
SUBLANES = 8
ROW_TILES = D_MODEL // LANES

PROJ_TILE = 512
RET_BLOCK = 256
FOX_TQ = 512
FOX_TK = PROJ_TILE
AUG = 128
V_AUG = 80
LOG2E = 1.4426950408889634
MOE_TILE = 256
ROUTER_TILES = 4
SMALL_RUN = 64
COMBINE_BUFFERS = 3
EXPERT_BLOCK = 512
EXPERT_PASS_ROWS = 256

NT_DIMS = (((1,), (1,)), ((), ()))


def _split3(a):
    hi = a.astype(BF16)
    r1 = a - hi.astype(F32)
    mid = r1.astype(BF16)
    lo = (r1 - mid.astype(F32)).astype(BF16)
    return hi, mid, lo


def _dot(a, b):
    return jnp.dot(a, b, preferred_element_type=F32)


def _dot_nt(a, b):
    return lax.dot_general(a, b, NT_DIMS, preferred_element_type=F32)


def _rms(x, g):
    return x * lax.rsqrt(jnp.mean(x * x, axis=-1, keepdims=True) + RMS_EPS) * g


def _in_proj_kernel(x_ref, g_ref, cos_ref, sin_ref, wr_ref, wfk_ref, wfqt_ref, wfvt_ref, wzt_ref, bcol_ref,
                    selkf_ref, constk_ref, selqf_ref, constq_ref,
                    rq_ref, rk_ref, rv_ref, rg_ref, kaug_ref, qaug_ref, fvt_ref, ccol):
    TM = x_ref.shape[1]
    d, H = FOX_HEAD_DIM, FOX_HEADS
    u = _rms(x_ref[0], g_ref[...]).astype(BF16)
    r = _dot(u, wr_ref[...])
    cos, sin = cos_ref[...], sin_ref[...]
    k_scale = RET_HEAD_DIM ** -0.5
    for h in range(RET_HEADS):
        lo = h * RET_HEAD_DIM
        q = r[:, lo:lo + RET_HEAD_DIM]
        k = r[:, RET_WIDTH + lo:RET_WIDTH + lo + RET_HEAD_DIM]
        rq_ref[0, :, lo:lo + RET_HEAD_DIM] = (q * cos + pltpu.roll(q, RET_HEAD_DIM // 2, 1) * sin).astype(BF16)
        rk_ref[0, :, lo:lo + RET_HEAD_DIM] = (
            (k * cos + pltpu.roll(k, RET_HEAD_DIM // 2, 1) * sin) * k_scale).astype(BF16)
    rv_ref[0] = r[:, 2 * RET_WIDTH:3 * RET_WIDTH].astype(BF16)
    rg_ref[0] = r[:, 3 * RET_WIDTH:4 * RET_WIDTH].astype(BF16)
    fvt_ref[0, 0] = _dot_nt(wfvt_ref[...], u).astype(BF16)
    fk = _dot(u, wfk_ref[...])
    fqt = (_dot_nt(wfqt_ref[...], u) * (d ** -0.5 * LOG2E)).astype(BF16)
    zt = _dot_nt(wzt_ref[...], u)

    @pl.when(pl.program_id(1) == 0)
    def _():
        ccol[...] = jnp.zeros_like(ccol)

    row = lax.broadcasted_iota(jnp.int32, (16, TM), 0)
    lft = jnp.where(row < H, jax.nn.log_sigmoid(zt + bcol_ref[...]), 0.0)
    utri = (lax.broadcasted_iota(jnp.int32, (TM, TM), 0) <= lax.broadcasted_iota(jnp.int32, (TM, TM), 1)).astype(BF16)
    t3 = _split3(lft)
    f_col = _dot(t3[0], utri) + _dot(t3[1], utri) + _dot(t3[2], utri) + ccol[:, 0:1]
    ccol[...] = jnp.broadcast_to(f_col[:, TM - 1:TM], ccol.shape)

    pieces_t = jnp.concatenate(_split3(f_col * LOG2E), axis=0)
    for h in range(H):
        extra = _dot(selqf_ref[h], pieces_t) + constq_ref[...]
        qaug_ref[0, h, 0] = jnp.concatenate([fqt[h * d:(h + 1) * d, :], extra.astype(BF16)], axis=0)

    f_row = jnp.concatenate([f_col, jnp.zeros((LANES - 16, TM), F32)], axis=0).T
    n3 = _split3(f_row * -LOG2E)
    pieces = (n3[0].astype(F32) + pltpu.roll(n3[1].astype(F32), H, 1)
              + pltpu.roll(n3[2].astype(F32), 2 * H, 1)).astype(BF16)
    lane = lax.broadcasted_iota(jnp.int32, (TM, LANES), 1)
    for g in range(H // 2):
        bias = _dot(pieces, selkf_ref[g])
        kg = fk[:, g * 2 * d:(g + 1) * 2 * d]
        for o in range(2):
            kh = kg if o == 0 else pltpu.roll(kg, d, 1)
            extra = bias[:, o * AUG:(o + 1) * AUG] + constk_ref[...]
            kaug_ref[0, 2 * g + o] = jnp.where(lane < d, kh, extra).astype(BF16)


def _in_proj(x, g, cos, sin, wr, wfk, wfqt, wfvt, wzt, b_forget):
    B, S, D = x.shape
    TM = PROJ_TILE
    ns = S // TM
    selkf, constk, selqf, constq = _fox_prep_constants()
    bcol = jnp.zeros((16, 1), F32).at[:FOX_HEADS, 0].set(b_forget)
    consts = (wr, wfk, wfqt, wfvt, wzt, bcol, selkf, constk, selqf, constq)
    const = lambda a: pl.BlockSpec(a.shape, lambda b, s: (0,) * a.ndim)
    tok = lambda w: pl.BlockSpec((1, TM, w), lambda b, s: (b, s, 0))
    out_shape = (
        jax.ShapeDtypeStruct((B, S, RET_WIDTH), BF16),
        jax.ShapeDtypeStruct((B, S, RET_WIDTH), BF16),
        jax.ShapeDtypeStruct((B, S, RET_WIDTH), BF16),
        jax.ShapeDtypeStruct((B, S, RET_WIDTH), BF16),
        jax.ShapeDtypeStruct((B, FOX_HEADS, S, AUG), BF16),
        jax.ShapeDtypeStruct((B, FOX_HEADS, ns, AUG, TM), BF16),
        jax.ShapeDtypeStruct((B, ns, FOX_WIDTH, TM), BF16),
    )
    return pl.pallas_call(
        _in_proj_kernel,
        grid=(B, ns),
        in_specs=[
            pl.BlockSpec((1, TM, D), lambda b, s: (b, s, 0)),
            pl.BlockSpec((1, D), lambda b, s: (0, 0)),
            pl.BlockSpec((TM, RET_HEAD_DIM), lambda b, s: (s, 0)),
            pl.BlockSpec((TM, RET_HEAD_DIM), lambda b, s: (s, 0)),
        ] + [const(a) for a in consts],
        out_specs=(
            tok(RET_WIDTH), tok(RET_WIDTH), tok(RET_WIDTH), tok(RET_WIDTH),
            pl.BlockSpec((1, FOX_HEADS, TM, AUG), lambda b, s: (b, 0, s, 0)),
            pl.BlockSpec((1, FOX_HEADS, 1, AUG, TM), lambda b, s: (b, 0, s, 0, 0)),
            pl.BlockSpec((1, 1, FOX_WIDTH, TM), lambda b, s: (b, s, 0, 0)),
        ),
        out_shape=out_shape,
        scratch_shapes=[pltpu.VMEM((16, LANES), F32)],
        compiler_params=pltpu.CompilerParams(
            dimension_semantics=("arbitrary", "arbitrary"), vmem_limit_bytes=48 * 1024 * 1024),
        name="in_proj",
    )(x, g, cos, sin, *consts)


def _fox_prep_constants():
    d, H = FOX_HEAD_DIM, FOX_HEADS
    selkf = np.zeros((H // 2, LANES, 2 * AUG), np.float32)
    constk = np.zeros((1, AUG), np.float32)
    selqf = np.zeros((H, d, 48), np.float32)
    constq = np.zeros((d, 1), np.float32)
    for p in range(3):
        constk[0, d + p] = 1.0
        constq[3 + p, 0] = 1.0
        for h in range(H):
            selkf[h // 2, p * H + h, (h % 2) * AUG + d + 3 + p] = 1.0
            selqf[h, p, p * 16 + h] = 1.0
    return jnp.asarray(selkf, BF16), jnp.asarray(constk, F32), jnp.asarray(selqf, BF16), jnp.asarray(constq, F32)


def _retention_kernel(q_ref, k_ref, v_ref, g_ref, dec_ref, qw_ref, kw_ref, cd_ref, o_ref, state):
    @pl.when(pl.program_id(1) == 0)
    def _():
        state[...] = jnp.zeros_like(state)

    for bb, h in [(bb, h) for bb in range(q_ref.shape[0]) for h in range(RET_HEADS)]:
        hs = slice(h * RET_HEAD_DIM, (h + 1) * RET_HEAD_DIM)
        q, k, v = q_ref[bb, :, hs], k_ref[bb, :, hs], v_ref[bb, :, hs]
        scores = (_dot_nt(q, k) * dec_ref[h]).astype(BF16)
        st = state[bb, h]
        o = _dot(scores, v) + _dot((q.astype(F32) * qw_ref[h]).astype(BF16), st.astype(BF16))
        kk = k.astype(F32) * kw_ref[h]
        state[bb, h] = st * cd_ref[h, 0:1, :] + _dot(kk.T.astype(BF16), v)
        mu = jnp.mean(o, axis=-1, keepdims=True)
        oc = o - mu
        var = jnp.mean(oc * oc, axis=-1, keepdims=True)
        o_ref[bb, :, hs] = (oc * lax.rsqrt(var + GN_EPS) * jax.nn.silu(g_ref[bb, :, hs].astype(F32))).astype(BF16)


def _retention_tables():
    L = RET_BLOCK
    log_gamma = jnp.log1p(-jnp.exp2(-5.0 - jnp.arange(RET_HEADS, dtype=F32)))
    p = jnp.arange(L, dtype=F32)
    dist = jnp.abs(p[:, None] - p[None, :])
    chunk = jnp.arange(L) // CHUNK
    allowed = (chunk[None, :] <= chunk[:, None]).astype(F32)
    dec = jnp.exp(log_gamma[:, None, None] * dist) * allowed
    lanes = lambda a: jnp.broadcast_to(a[:, :, None], (RET_HEADS, L, RET_HEAD_DIM))
    qw = lanes(jnp.exp(log_gamma[:, None] * (p[None, :] + 1.0)))
    kw = lanes(jnp.exp(log_gamma[:, None] * (L - 1.0 - p[None, :])))
    cd = jnp.broadcast_to(jnp.exp(log_gamma * L)[:, None, None], (RET_HEADS, SUBLANES, RET_HEAD_DIM))
    return dec, qw, kw, cd


def _retention(rq, rk, rv, rg):
    B, S, _ = rq.shape
    L = RET_BLOCK
    dec, qw, kw, cd = _retention_tables()
    nb = 2 if B % 2 == 0 else 1
    tok = pl.BlockSpec((nb, L, RET_WIDTH), lambda b, s: (b, s, 0))
    const = lambda a: pl.BlockSpec(a.shape, lambda b, s: (0,) * a.ndim)
    return pl.pallas_call(
        _retention_kernel,
        grid=(B // nb, S // L),
        in_specs=[tok, tok, tok, tok, const(dec), const(qw), const(kw), const(cd)],
        out_specs=tok,
        out_shape=jax.ShapeDtypeStruct((B, S, RET_WIDTH), BF16),
        scratch_shapes=[pltpu.VMEM((nb, RET_HEADS, RET_HEAD_DIM, RET_HEAD_DIM), F32)],
        compiler_params=pltpu.CompilerParams(dimension_semantics=("arbitrary",) * 2),
        name="retention",
    )(rq, rk, rv, rg, dec, qw, kw, cd)


def _fox_attn_kernel(q_ref, k_ref, v_ref, o_ref, s_a, s_b, s_c, cm_a, cm_b, cm_c, m_ref, acc_ref):
    T = FOX_TQ
    d = FOX_HEAD_DIM
    nq = q_ref.shape[2]
    ones_rows = (lax.broadcasted_iota(jnp.int32, (V_AUG - d, T), 0) == 0).astype(BF16)

    def scores(qi, j, s_ref, cm_ref):
        for hh in range(2):
            kj = k_ref[0, hh, pl.ds(pl.multiple_of(j * T, T), T), :]
            st = _dot(kj, q_ref[0, hh, qi])
            s_ref[hh] = st
            cm_ref[hh] = jnp.max(st, axis=0, keepdims=True)

    def consume(j, s_ref, cm_ref, masked):
        for hh in range(2):
            st = s_ref[hh]
            if masked:
                key = lax.broadcasted_iota(jnp.int32, (T, T), 0)
                qry = lax.broadcasted_iota(jnp.int32, (T, T), 1)
                st = jnp.where(key <= qry, st, -jnp.inf)
                cm = jnp.max(st, axis=0, keepdims=True)
            else:
                cm = cm_ref[hh]
            m = m_ref[hh]
            m_new = jnp.maximum(m, cm)
            p = jnp.exp2(st - m_new).astype(BF16)
            vj = jnp.concatenate([v_ref[0, j, hh * d:(hh + 1) * d, :], ones_rows], axis=0)
            acc_ref[hh] = jnp.exp2(m - m_new) * acc_ref[hh] + _dot(vj, p)
            m_ref[hh] = m_new

    def reset():
        m_ref[...] = jnp.full(m_ref.shape, -jnp.inf, F32)
        acc_ref[...] = jnp.zeros(acc_ref.shape, F32)

    def prefetch_next(qi):
        @pl.when(qi + 1 < nq)
        def _():
            scores(qi + 1, 0, s_c, cm_c)

    def finish(qi):
        outs = [acc_ref[hh, 0:d, :] / acc_ref[hh, d:d + 1, :] for hh in range(2)]
        o_ref[0, pl.ds(pl.multiple_of(qi * T, T), T), :] = jnp.concatenate(outs, axis=0).T.astype(BF16)

    reset()
    scores(0, 0, s_a, cm_a)
    prefetch_next(0)
    consume(0, s_a, cm_a, True)
    finish(0)

    def query_tile(qi, carry):
        reset()
        scores(qi, 1, s_a, cm_a)
        consume(0, s_c, cm_c, False)

        def pair(j):
            scores(qi, j + 1, s_b, cm_b)
            consume(j, s_a, cm_a, False)
            scores(qi, j + 2, s_a, cm_a)
            consume(j + 1, s_b, cm_b, False)

        def two_pairs(jj, c):
            pair(1 + 4 * jj)
            pair(3 + 4 * jj)
            return c

        def one_pair(jj, c):
            pair(1 + 4 * (n_pairs // 2) + 2 * jj)
            return c

        n_pairs = (qi - 1) // 2
        lax.fori_loop(0, n_pairs // 2, two_pairs, 0)
        lax.fori_loop(0, n_pairs % 2, one_pair, 0)

        @pl.when(qi % 2 == 1)
        def _():
            prefetch_next(qi)
            consume(qi, s_a, cm_a, True)

        @pl.when(qi % 2 == 0)
        def _():
            scores(qi, qi, s_b, cm_b)
            consume(qi - 1, s_a, cm_a, False)
            prefetch_next(qi)
            consume(qi, s_b, cm_b, True)

        finish(qi)
        return carry

    lax.fori_loop(1, nq, query_tile, 0)


def _fox_attn(qaug, kaug, fvt):
    B, H, S, _ = kaug.shape
    nk = S // FOX_TK
    nq = S // FOX_TQ
    score_buf = pltpu.VMEM((2, FOX_TK, FOX_TQ), F32)
    col_max = pltpu.VMEM((2, 1, FOX_TQ), F32)
    return pl.pallas_call(
        _fox_attn_kernel,
        grid=(B, H // 2),
        in_specs=[
            pl.BlockSpec((1, 2, nq, AUG, FOX_TQ), lambda b, p: (b, p, 0, 0, 0)),
            pl.BlockSpec((1, 2, S, AUG), lambda b, p: (b, p, 0, 0)),
            pl.BlockSpec((1, nk, 2 * FOX_HEAD_DIM, FOX_TK), lambda b, p: (b, 0, p, 0)),
        ],
        out_specs=pl.BlockSpec((1, S, 2 * FOX_HEAD_DIM), lambda b, p: (b, 0, p)),
        out_shape=jax.ShapeDtypeStruct((B, S, FOX_WIDTH), BF16),
        scratch_shapes=[
            score_buf, score_buf, score_buf, col_max, col_max, col_max,
            pltpu.VMEM((2, 1, FOX_TQ), F32), pltpu.VMEM((2, V_AUG, FOX_TQ), F32),
        ],
        compiler_params=pltpu.CompilerParams(
            dimension_semantics=("arbitrary",) * 2, vmem_limit_bytes=48 * 1024 * 1024),
        name="fox_attn",
    )(qaug, kaug, fvt)


def _out_router_kernel(x_ref, oret_ref, ofox_ref, wor_ref, wof_ref, g_ref, wrh_ref, wrl_ref, br_ref,
                       h1_ref, u2_ref, sel_ref, cnt_ref):
    TM = MOE_TILE
    rows = lambda t: slice(t * TM, (t + 1) * TM)

    def out_proj(t):
        rs = rows(t)
        h1 = x_ref[rs] + _dot(oret_ref[rs], wor_ref[...]) + _dot(ofox_ref[rs], wof_ref[...])
        h1_ref[rs] = h1
        return h1

    def router_logits(t, h1):
        u2 = _rms(h1, g_ref[...])
        uh = u2.astype(BF16)
        u2_ref[rows(t)] = uh
        ul = (u2 - uh.astype(F32)).astype(BF16)
        return (_dot_nt(wrh_ref[...], uh) + _dot_nt(wrh_ref[...], ul) + _dot_nt(wrl_ref[...], uh)
                + br_ref[...])

    def top_k(t, logits):
        rs = rows(t)
        row = lax.broadcasted_iota(jnp.int32, (LANES, TM), 0).astype(F32)
        l = jnp.where(row < N_EXPERTS, logits, -jnp.inf)
        picks, vals = [], []
        for _ in range(TOP_K):
            m = jnp.max(l, axis=0, keepdims=True)
            idx = jnp.min(jnp.where(l == m, row, float(LANES)), axis=0, keepdims=True)
            pick = row == idx
            picks.append(pick)
            vals.append(m)
            l = jnp.where(pick, -jnp.inf, l)
        exps = [jnp.exp(v - vals[0]) for v in vals]
        den = exps[0] + exps[1] + exps[2] + exps[3]
        sel_t = jnp.full((LANES, TM), -1.0, F32)
        for pick, e in zip(picks, exps):
            sel_t = jnp.where(pick, e / den, sel_t)
        sel = sel_t.T
        sel_ref[rs] = sel
        cnt = jnp.sum((sel >= 0.0).astype(F32), axis=0, keepdims=True)
        cnt_ref[t] = jnp.broadcast_to(cnt, (SUBLANES, LANES))

    n = ROUTER_TILES
    h1s, lgs = {0: out_proj(0)}, {}
    for t in range(1, n + 2):
        if t < n:
            h1s[t] = out_proj(t)
        if 1 <= t <= n:
            lgs[t - 1] = router_logits(t - 1, h1s.pop(t - 1))
        if t >= 2:
            top_k(t - 2, lgs.pop(t - 2))


def _out_router(x2, o_ret, o_fox, wor, wof, g, wrh, wrl, br):
    T, D = x2.shape
    TM = MOE_TILE * ROUTER_TILES
    nT = T // MOE_TILE
    const = lambda a: pl.BlockSpec(a.shape, lambda i: (0,) * a.ndim)
    tok = lambda w: pl.BlockSpec((TM, w), lambda i: (i, 0))
    return pl.pallas_call(
        _out_router_kernel,
        grid=(T // TM,),
        in_specs=[tok(D), tok(RET_WIDTH), tok(FOX_WIDTH), const(wor), const(wof), const(g),
                  const(wrh), const(wrl), const(br)],
        out_specs=(tok(D), tok(D), tok(LANES),
                   pl.BlockSpec((ROUTER_TILES, SUBLANES, LANES), lambda i: (i, 0, 0))),
        out_shape=(
            jax.ShapeDtypeStruct((T, D), F32),
            jax.ShapeDtypeStruct((T, D), BF16),
            jax.ShapeDtypeStruct((T, LANES), F32),
            jax.ShapeDtypeStruct((nT, SUBLANES, LANES), F32),
        ),
        compiler_params=pltpu.CompilerParams(dimension_semantics=("arbitrary",)),
        name="out_router",
    )(x2, o_ret, o_fox, wor, wof, g, wrh, wrl, br)


def _tile_sort(sel):
    TM = sel.shape[0]
    NS = TOP_K * TM
    maskf = (sel >= 0.0).astype(F32)
    mask = maskf.astype(BF16)
    ri = lax.broadcasted_iota(jnp.int32, (TM, TM), 0)
    ci = lax.broadcasted_iota(jnp.int32, (TM, TM), 1)
    rank1 = maskf * _dot((ri >= ci).astype(BF16), mask)
    cnt = jnp.sum(maskf, axis=0, keepdims=True)
    ei = lax.broadcasted_iota(jnp.int32, (LANES, LANES), 0)
    ej = lax.broadcasted_iota(jnp.int32, (LANES, LANES), 1)
    cnt8 = jnp.broadcast_to(cnt, (SUBLANES, LANES)).astype(BF16)
    off = _dot(cnt8, (ei < ej).astype(BF16))[0:1, :]
    slot = lax.broadcasted_iota(jnp.int32, (NS, LANES), 0).astype(F32)
    esel = ((slot >= off) & (slot < off + cnt)).astype(BF16)
    return rank1.astype(BF16), esel, off, cnt


def _segment_wait(slot, local, remote_rows, sem, to_remote):
    whole = local.at[slot]
    rem = remote_rows.at[pl.ds(0, whole.shape[0]), :]
    cp = (pltpu.make_async_copy(whole, rem, sem.at[slot]) if to_remote
          else pltpu.make_async_copy(rem, whole, sem.at[slot]))
    cp.wait()


def _segment_dmas(step, slot, segdst_ref, cnt_ref, big_ref, local, remote_rows, sem, to_remote):
    big = big_ref[step] != 0
    for cond, top_bit in ((big, MOE_TILE), (jnp.logical_not(big), SMALL_RUN // 2)):
        pl.when(cond)(functools.partial(
            _segment_dma_path, step, slot, segdst_ref, cnt_ref, local, remote_rows, sem, to_remote, top_bit))


def _segment_dma_path(step, slot, segdst_ref, cnt_ref, local, remote_rows, sem, to_remote, top_bit):
    def body(e, off):
        c = cnt_ref[step * N_EXPERTS + e]
        dst = segdst_ref[step * N_EXPERTS + e]
        bit = top_bit
        while bit >= 1:
            done = c & (~(2 * bit - 1))

            @pl.when((c & bit) != 0)
            def _(bit=bit, done=done):
                loc = local.at[slot, pl.ds((off + done) * ROW_TILES, bit * ROW_TILES), :]
                rem = remote_rows.at[pl.ds((dst + done) * ROW_TILES, bit * ROW_TILES), :]
                cp = (pltpu.make_async_copy(loc, rem, sem.at[slot]) if to_remote
                      else pltpu.make_async_copy(rem, loc, sem.at[slot]))
                cp.start(priority=e % 2)
            bit //= 2
        return off + c

    off = 0
    for e in range(N_EXPERTS):
        off = body(e, off)


def _dispatch_kernel(segdst_ref, cnt_ref, big_ref, paddst_ref, padcnt_ref, nused_ref, u2_ref, sel_ref, xs_ref,
                     buf, zbuf, sems, zsem):
    i = pl.program_id(0)
    last = pl.num_programs(0) - 1
    slot = i % 2
    TM = MOE_TILE
    NS = TOP_K * TM
    rank1, esel, off, _ = _tile_sort(sel_ref[...])
    slot_id = lax.broadcasted_iota(jnp.int32, (NS, 1), 0).astype(F32)
    r_s = slot_id - jnp.sum(esel.astype(F32) * off, axis=1, keepdims=True)
    perm = (_dot_nt(esel, rank1) == r_s + 1.0).astype(BF16)

    @pl.when(i >= 2)
    def _():
        _segment_wait(slot, buf, xs_ref, sems, True)

    u2 = u2_ref[...]
    for c in range(NS // TM):
        rows = _dot(perm[c * TM:(c + 1) * TM], u2)
        for j in range(ROW_TILES):
            buf[slot, pl.ds(c * TM * ROW_TILES + j, TM, stride=ROW_TILES), :] = rows[:, j * LANES:(j + 1) * LANES]
    _segment_dmas(i, slot, segdst_ref, cnt_ref, big_ref, buf, xs_ref, sems, True)

    @pl.when(i == last)
    def _():
        @pl.when(i >= 1)
        def _():
            _segment_wait(1 - slot, buf, xs_ref, sems, True)
        _segment_wait(slot, buf, xs_ref, sems, True)
        zbuf[...] = jnp.zeros_like(zbuf)
        half = EXPERT_BLOCK // 2 * ROW_TILES
        n_blocks = xs_ref.shape[0] // (EXPERT_BLOCK * ROW_TILES)
        for wait in (False, True):
            def unused(hb, carry, wait=wait):
                cp = pltpu.make_async_copy(zbuf, xs_ref.at[pl.ds(hb * half, half), :], zsem.at[0])
                cp.wait() if wait else cp.start()
                return carry
            lax.fori_loop(2 * nused_ref[0], 2 * n_blocks, unused, 0)


            def body(e, carry, wait=wait):
                c = padcnt_ref[e]
                dst = paddst_ref[e]
                bit = EXPERT_BLOCK // 2
                while bit >= 1:
                    done = c & (~(2 * bit - 1))

                    @pl.when((c & bit) != 0)
                    def _(bit=bit, done=done):
                        cp = pltpu.make_async_copy(
                            zbuf.at[pl.ds(0, bit * ROW_TILES), :],
                            xs_ref.at[pl.ds((dst + done) * ROW_TILES, bit * ROW_TILES), :], zsem.at[0])
                        cp.wait() if wait else cp.start()
                    bit //= 2
                return carry
            lax.fori_loop(0, N_EXPERTS, body, 0)


def _dispatch(u2, sel, segdst, cnt, big, paddst, padcnt, n_used, n_rows):
    T, D = u2.shape
    TM = MOE_TILE
    NS = TOP_K * TM
    return pl.pallas_call(
        _dispatch_kernel,
        grid_spec=pltpu.PrefetchScalarGridSpec(
            num_scalar_prefetch=6,
            grid=(T // TM,),
            in_specs=[pl.BlockSpec((TM, D), lambda i, *_: (i, 0)),
                      pl.BlockSpec((TM, LANES), lambda i, *_: (i, 0))],
            out_specs=pl.BlockSpec(memory_space=pl.ANY),
            scratch_shapes=[pltpu.VMEM((2, NS * ROW_TILES, LANES), F32),
                            pltpu.VMEM((EXPERT_BLOCK // 2 * ROW_TILES, LANES), F32),
                            pltpu.SemaphoreType.DMA((2,)), pltpu.SemaphoreType.DMA((1,))],
        ),
        out_shape=jax.ShapeDtypeStruct((n_rows * ROW_TILES, LANES), F32),
        compiler_params=pltpu.CompilerParams(
            dimension_semantics=("arbitrary",), vmem_limit_bytes=48 * 1024 * 1024),
        name="dispatch",
    )(segdst, cnt, big, paddst, padcnt, n_used, u2, sel)


def _expert_kernel(bexp_ref, nused_ref, epos_ref, enext_ref, xs_ref, w1_hbm, b1_ref, w2_hbm, b2_ref, ys_ref,
                   w1f, w2f, w1b, w2b, wsem):
    b = pl.program_id(0)
    BLK = EXPERT_BLOCK
    used = b < nused_ref[0]

    def weight_copies(e, slot):
        return (pltpu.make_async_copy(w1_hbm.at[e], w1f.at[slot], wsem.at[0, slot]),
                pltpu.make_async_copy(w2_hbm.at[e], w2f.at[slot], wsem.at[1, slot]))

    @pl.when(used)
    def _():
        e = bexp_ref[b]
        prev = bexp_ref[jnp.maximum(b - 1, 0)]
        slot = epos_ref[b] % 2

        @pl.when(b == 0)
        def _():
            for cp in weight_copies(e, slot):
                cp.start()

        @pl.when((b == 0) | (e != prev))
        def _():
            nxt = enext_ref[b]

            @pl.when(nxt >= 0)
            def _():
                for cp in weight_copies(nxt, 1 - slot):
                    cp.start()

            for cp in weight_copies(e, slot):
                cp.wait()
            rows = 128

            def cast(r, carry):
                sl = pl.ds(pl.multiple_of(r * rows, rows), rows)
                w1b[sl, :] = w1f[slot, sl, :].astype(BF16)
                w2b[sl, :] = w2f[slot, sl, :].astype(BF16)
                return carry
            lax.fori_loop(0, D_MODEL // rows, cast, 0)

        R = EXPERT_PASS_ROWS
        for rp in range(BLK // R):
            r0 = rp * R * ROW_TILES
            x = jnp.concatenate([xs_ref[pl.ds(r0 + j, R, stride=ROW_TILES), :] for j in range(ROW_TILES)],
                                axis=1).astype(BF16)
            h = _dot(x, w1b[...]) + b1_ref[0]
            glu = jnp.minimum(h[:, :D_FF], SWIGLU_LIMIT)
            lin = jnp.clip(h[:, D_FF:], -SWIGLU_LIMIT, SWIGLU_LIMIT)
            act = glu * jax.nn.sigmoid(SWIGLU_ALPHA * glu) * (lin + 1.0)
            y = _dot(act.astype(BF16), w2b[...]) + b2_ref[0]
            for j in range(ROW_TILES):
                ys_ref[pl.ds(r0 + j, R, stride=ROW_TILES), :] = y[:, j * LANES:(j + 1) * LANES]

    @pl.when(jnp.logical_not(used))
    def _():
        ys_ref[...] = jnp.zeros_like(ys_ref)


def _experts(xs, block_exp, n_used, exp_pos, exp_next, w1, b1, w2, b2):
    BLK = EXPERT_BLOCK
    NB = xs.shape[0] // (BLK * ROW_TILES)
    blk = lambda b, nused: jnp.minimum(b, nused[0] - 1)
    return pl.pallas_call(
        _expert_kernel,
        grid_spec=pltpu.PrefetchScalarGridSpec(
            num_scalar_prefetch=4,
            grid=(NB,),
            in_specs=[
                pl.BlockSpec((BLK * ROW_TILES, LANES), lambda b, bexp, nused, *_: (blk(b, nused), 0)),
                pl.BlockSpec(memory_space=pl.ANY),
                pl.BlockSpec((1, 1, 2 * D_FF), lambda b, bexp, nused, *_: (bexp[blk(b, nused)], 0, 0)),
                pl.BlockSpec(memory_space=pl.ANY),
                pl.BlockSpec((1, 1, D_MODEL), lambda b, bexp, nused, *_: (bexp[blk(b, nused)], 0, 0)),
            ],
            out_specs=pl.BlockSpec((BLK * ROW_TILES, LANES), lambda b, *_: (b, 0)),
            scratch_shapes=[pltpu.VMEM((2, D_MODEL, 2 * D_FF), F32), pltpu.VMEM((2, D_FF, D_MODEL), F32),
                            pltpu.VMEM((D_MODEL, 2 * D_FF), BF16), pltpu.VMEM((D_FF, D_MODEL), BF16),
                            pltpu.SemaphoreType.DMA((2, 2))],
        ),
        out_shape=jax.ShapeDtypeStruct(xs.shape, F32),
        compiler_params=pltpu.CompilerParams(
            dimension_semantics=("arbitrary",), vmem_limit_bytes=56 * 1024 * 1024),
        name="experts",
    )(block_exp, n_used, exp_pos, exp_next, xs, w1, b1[:, None, :], w2, b2[:, None, :])


def _combine_kernel(segdst_ref, cnt_ref, big_ref, ys_ref, sel_ref, h1_ref, g_ref, out_ref, buf, sems):
    i = pl.program_id(0)
    n = pl.num_programs(0)
    nbuf = buf.shape[0]
    slot = i % nbuf
    TM = MOE_TILE
    NS = TOP_K * TM

    @pl.when(i == 0)
    def _():
        for ahead in range(nbuf - 1):
            @pl.when(ahead < n)
            def _(ahead=ahead):
                _segment_dmas(ahead, ahead, segdst_ref, cnt_ref, big_ref, buf, ys_ref, sems, False)

    @pl.when(i + nbuf - 1 < n)
    def _():
        _segment_dmas(i + nbuf - 1, (i + nbuf - 1) % nbuf, segdst_ref, cnt_ref, big_ref, buf, ys_ref, sems, False)

    sel = sel_ref[...]
    rank1, esel, off, _ = _tile_sort(sel)
    gate = jnp.maximum(sel, 0.0).astype(BF16)
    o3 = _split3(jnp.broadcast_to(off, (SUBLANES, LANES)))
    off_s = (_dot_nt(o3[0], esel) + _dot_nt(o3[1], esel) + _dot_nt(o3[2], esel))[0:1, :]
    r_s = lax.broadcasted_iota(jnp.int32, (1, NS), 1).astype(F32) - off_s
    hit = _dot_nt(rank1, esel) == r_s + 1.0
    unperm = jnp.where(hit, _dot_nt(gate, esel), 0.0).astype(BF16)

    _segment_wait(slot, buf, ys_ref, sems, False)
    y = jnp.concatenate([buf[slot, pl.ds(j, NS, stride=ROW_TILES), :] for j in range(ROW_TILES)],
                        axis=1).astype(BF16)
    h2 = h1_ref[...] + _dot(unperm, y)
    out_ref[...] = _rms(h2, g_ref[...])


def _combine(ys, sel, h1, g, segdst, cnt, big):
    T, D = h1.shape
    TM = MOE_TILE
    NS = TOP_K * TM
    return pl.pallas_call(
        _combine_kernel,
        grid_spec=pltpu.PrefetchScalarGridSpec(
            num_scalar_prefetch=3,
            grid=(T // TM,),
            in_specs=[pl.BlockSpec(memory_space=pl.ANY),
                      pl.BlockSpec((TM, LANES), lambda i, *_: (i, 0)),
                      pl.BlockSpec((TM, D), lambda i, *_: (i, 0)),
                      pl.BlockSpec((1, D), lambda i, *_: (0, 0))],
            out_specs=pl.BlockSpec((TM, D), lambda i, *_: (i, 0)),
            scratch_shapes=[pltpu.VMEM((COMBINE_BUFFERS, NS * ROW_TILES, LANES), F32),
                            pltpu.SemaphoreType.DMA((COMBINE_BUFFERS,))],
        ),
        out_shape=jax.ShapeDtypeStruct((T, D), F32),
        compiler_params=pltpu.CompilerParams(
            dimension_semantics=("arbitrary",), vmem_limit_bytes=48 * 1024 * 1024),
        name="combine",
    )(segdst, cnt, big, ys, sel, h1, g)


def _routing_tables(cnt_tiles):
    BLK = EXPERT_BLOCK
    nT = cnt_tiles.shape[0]
    A = nT * MOE_TILE * TOP_K
    NB = A // BLK + N_EXPERTS
    total = jnp.sum(cnt_tiles, axis=0)
    padded = (total + BLK - 1) // BLK * BLK
    pad_ends = jnp.cumsum(padded)
    pad_starts = pad_ends - padded
    before = jnp.cumsum(cnt_tiles, axis=0) - cnt_tiles
    segdst = (pad_starts[None, :] + before).reshape(-1).astype(jnp.int32)
    block_start = jnp.arange(NB, dtype=jnp.int32) * BLK
    block_exp = jnp.minimum(jnp.sum(pad_ends[None, :] <= block_start[:, None], axis=1), N_EXPERTS - 1).astype(jnp.int32)
    n_used = (pad_ends[-1] // BLK).astype(jnp.int32).reshape(1)
    paddst = (pad_starts + total).astype(jnp.int32)
    padcnt = (padded - total).astype(jnp.int32)
    big = jnp.any(cnt_tiles >= SMALL_RUN, axis=1).astype(jnp.int32)
    has_rows = total > 0
    ids = jnp.arange(N_EXPERTS, dtype=jnp.int32)
    pos = (jnp.cumsum(has_rows) - has_rows).astype(jnp.int32)
    later = jnp.where(has_rows[None, :] & (ids[None, :] > ids[:, None]), ids[None, :], N_EXPERTS)
    nxt = jnp.min(later, axis=1)
    nxt = jnp.where(nxt < N_EXPERTS, nxt, -1).astype(jnp.int32)
    return (segdst, cnt_tiles.reshape(-1).astype(jnp.int32), big, paddst, padcnt, block_exp, n_used,
            pos[block_exp], nxt[block_exp], NB * BLK)


def _rotary_tables(S):
    half = RET_HEAD_DIM // 2
    inv_freq = ROPE_BASE ** (-jnp.arange(half, dtype=F32) / half)
    ang = jnp.arange(S, dtype=F32)[:, None] * inv_freq[None, :]
    cos, sin = jnp.cos(ang), jnp.sin(ang)
    return jnp.concatenate([cos, cos], axis=-1), jnp.concatenate([-sin, sin], axis=-1)


def _layer(h, norm_mix_g, w_in, b_forget, w_out, norm_ffn_g, w_router, b_router,
           w_exp_in, b_exp_in, w_exp_out, b_exp_out, final_g):
    B, S, D = h.shape
    R, Fw = RET_WIDTH, FOX_WIDTH
    cos, sin = _rotary_tables(S)
    wb = w_in.astype(BF16)
    wr = wb[:, :4 * R]
    wfq, wfk, wfv = (wb[:, 4 * R + i * Fw:4 * R + (i + 1) * Fw] for i in range(3))
    wzt = jnp.zeros((16, D), BF16).at[:FOX_HEADS, :].set(wb[:, 4 * R + 3 * Fw:].T)
    rq, rk, rv, rg, kaug, qaug, fvt = _in_proj(
        h, norm_mix_g[None, :], cos, sin, wr, wfk, wfq.T, wfv.T, wzt, b_forget)
    o_ret = _retention(rq, rk, rv, rg)
    o_fox = _fox_attn(qaug, kaug, fvt)

    T = B * S
    wo = w_out.astype(BF16)
    wrt = jnp.zeros((LANES, D), F32).at[:N_EXPERTS, :].set(w_router.T)
    wrh = wrt.astype(BF16)
    wrl = (wrt - wrh.astype(F32)).astype(BF16)
    br = jnp.zeros((LANES, 1), F32).at[:N_EXPERTS, 0].set(b_router)
    h1, u2, sel, cnt = _out_router(h.reshape(T, D), o_ret.reshape(T, R), o_fox.reshape(T, Fw),
                                   wo[:R], wo[R:], norm_ffn_g[None, :], wrh, wrl, br)
    cnt_tiles = cnt[:, 0, :N_EXPERTS].astype(jnp.int32)
    segdst, cnt_flat, big, paddst, padcnt, block_exp, n_used, exp_pos, exp_next, n_rows = _routing_tables(cnt_tiles)
    xs = _dispatch(u2, sel, segdst, cnt_flat, big, paddst, padcnt, n_used, n_rows)
    ys = _experts(xs, block_exp, n_used, exp_pos, exp_next, w_exp_in, b_exp_in, w_exp_out, b_exp_out)
    out = _combine(ys, sel, h1, final_g[None, :], segdst, cnt_flat, big)
    return out.reshape(B, S, D)


def kernel(x, norm_mix_g, w_in, b_forget, w_out, norm_ffn_g, w_router, b_router,
           w_exp_in, b_exp_in, w_exp_out, b_exp_out, norm_final_g):
    depth = w_in.shape[0]
    assert depth == 1, "the fused final RMSNorm assumes a single layer"
    return _layer(x, norm_mix_g[0], w_in[0], b_forget[0], w_out[0], norm_ffn_g[0], w_router[0], b_router[0],
                  w_exp_in[0], b_exp_in[0], w_exp_out[0], b_exp_out[0], norm_final_g)
```

```python
import functools

import numpy as np
import jax
import jax.numpy as jnp
from jax import lax
from jax.experimental import pallas as pl
from jax.experimental.pallas import tpu as pltpu

F32 = jnp.float32
BF16 = jnp.bfloat16

D_MODEL = 1024
RET_HEADS, RET_HEAD_DIM = 4, 128
RET_WIDTH = RET_HEADS * RET_HEAD_DIM
FOX_HEADS, FOX_HEAD_DIM = 8, 64
FOX_WIDTH = FOX_HEADS * FOX_HEAD_DIM
CHUNK = 64
ROPE_BASE = 10000.0
N_EXPERTS = 32
TOP_K = 4
D_FF = D_MODEL
SWIGLU_ALPHA = 1.702
SWIGLU_LIMIT = 7.0
RMS_EPS = 1e-5
GN_EPS = 1e-5

LANES = 128
SUBLANES = 8
ROW_TILES = D_MODEL // LANES
V7X_VMEM_BYTES = 64 * 1024 * 1024
VMEM_LIMIT = V7X_VMEM_BYTES * 3 // 4
VMEM_LIMIT_EXPERTS = V7X_VMEM_BYTES * 7 // 8

PROJ_TILE = 512
RET_BLOCK = 256
FOX_TQ = 512
FOX_TK = PROJ_TILE
AUG = 128
V_AUG = 80
LOG2E = 1.4426950408889634
MOE_TILE = 256
ROUTER_TILES = 4
SMALL_RUN = 64
EXPERT_BLOCK = 512
EXPERT_PASS_ROWS = 256

NT_DIMS = (((1,), (1,)), ((), ()))


def _split3(a):
    hi = a.astype(BF16)
    r1 = a - hi.astype(F32)
    mid = r1.astype(BF16)
    lo = (r1 - mid.astype(F32)).astype(BF16)
    return hi, mid, lo


def _dot(a, b):
    return jnp.dot(a, b, preferred_element_type=F32)


def _dot_nt(a, b):
    return lax.dot_general(a, b, NT_DIMS, preferred_element_type=F32)


def _rms(x, g):
    return x * lax.rsqrt(jnp.mean(x * x, axis=-1, keepdims=True) + RMS_EPS) * g


def _in_proj_kernel(x_ref, g_ref, cos_ref, sin_ref, wr_ref, wfk_ref, wfqt_ref, wfvt_ref, wzt_ref, bcol_ref,
                    selkf_ref, constk_ref, selqf_ref, constq_ref, dec_ref, qw_ref, kw_ref, cd_ref,
                    oret_ref, kaug_ref, qaug_ref, fvt_ref, ccol, state):
    TM = x_ref.shape[1]
    d, H = FOX_HEAD_DIM, FOX_HEADS

    @pl.when(pl.program_id(1) == 0)
    def _():
        ccol[...] = jnp.zeros_like(ccol)
        state[...] = jnp.zeros_like(state)

    u = _rms(x_ref[0], g_ref[...]).astype(BF16)
    r = _dot(u, wr_ref[...])
    cos, sin = cos_ref[...], sin_ref[...]
    k_scale = RET_HEAD_DIM ** -0.5
    L = RET_BLOCK
    for h in range(RET_HEADS):
        hs = slice(h * RET_HEAD_DIM, (h + 1) * RET_HEAD_DIM)
        q = r[:, hs]
        k = r[:, RET_WIDTH + h * RET_HEAD_DIM:RET_WIDTH + (h + 1) * RET_HEAD_DIM]
        qh = (q * cos + pltpu.roll(q, RET_HEAD_DIM // 2, 1) * sin).astype(BF16)
        kh = ((k * cos + pltpu.roll(k, RET_HEAD_DIM // 2, 1) * sin) * k_scale).astype(BF16)
        vh = r[:, 2 * RET_WIDTH + h * RET_HEAD_DIM:2 * RET_WIDTH + (h + 1) * RET_HEAD_DIM].astype(BF16)
        gh = r[:, 3 * RET_WIDTH + h * RET_HEAD_DIM:3 * RET_WIDTH + (h + 1) * RET_HEAD_DIM].astype(BF16)
        for rs in (slice(b0, b0 + L) for b0 in range(0, TM, L)):
            qb, kb, vb = qh[rs], kh[rs], vh[rs]
            scores = (_dot_nt(qb, kb) * dec_ref[h]).astype(BF16)
            st = state[h]
            o = _dot(scores, vb) + _dot((qb.astype(F32) * qw_ref[h]).astype(BF16), st.astype(BF16))
            kk = kb.astype(F32) * kw_ref[h]
            state[h] = st * cd_ref[h, 0:1, :] + _dot(kk.T.astype(BF16), vb)
            mu = jnp.mean(o, axis=-1, keepdims=True)
            oc = o - mu
            var = jnp.mean(oc * oc, axis=-1, keepdims=True)
            oret_ref[0, rs, hs] = (oc * lax.rsqrt(var + GN_EPS) * jax.nn.silu(gh[rs].astype(F32))).astype(BF16)
    fvt_ref[0, 0] = _dot_nt(wfvt_ref[...], u).astype(BF16)
    fk = _dot(u, wfk_ref[...])
    fqt = (_dot_nt(wfqt_ref[...], u) * (d ** -0.5 * LOG2E)).astype(BF16)
    zt = _dot_nt(wzt_ref[...], u)


    row = lax.broadcasted_iota(jnp.int32, (16, TM), 0)
    lft = jnp.where(row < H, jax.nn.log_sigmoid(zt + bcol_ref[...]), 0.0)
    utri = (lax.broadcasted_iota(jnp.int32, (TM, TM), 0) <= lax.broadcasted_iota(jnp.int32, (TM, TM), 1)).astype(BF16)
    t3 = _split3(lft)
    f_col = _dot(t3[0], utri) + _dot(t3[1], utri) + _dot(t3[2], utri) + ccol[:, 0:1]
    ccol[...] = jnp.broadcast_to(f_col[:, TM - 1:TM], ccol.shape)

    pieces_t = jnp.concatenate(_split3(f_col * LOG2E), axis=0)
    for h in range(H):
        extra = _dot(selqf_ref[h], pieces_t) + constq_ref[...]
        qaug_ref[0, h, 0] = jnp.concatenate([fqt[h * d:(h + 1) * d, :], extra.astype(BF16)], axis=0)

    f_row = jnp.concatenate([f_col, jnp.zeros((LANES - 16, TM), F32)], axis=0).T
    n3 = _split3(f_row * -LOG2E)
    pieces = (n3[0].astype(F32) + pltpu.roll(n3[1].astype(F32), H, 1)
              + pltpu.roll(n3[2].astype(F32), 2 * H, 1)).astype(BF16)
    lane = lax.broadcasted_iota(jnp.int32, (TM, LANES), 1)
    for g in range(H // 2):
        bias = _dot(pieces, selkf_ref[g])
        kg = fk[:, g * 2 * d:(g + 1) * 2 * d]
        for o in range(2):
            kh = kg if o == 0 else pltpu.roll(kg, d, 1)
            extra = bias[:, o * AUG:(o + 1) * AUG] + constk_ref[...]
            kaug_ref[0, 2 * g + o] = jnp.where(lane < d, kh, extra).astype(BF16)


def _in_proj(x, g, cos, sin, wr, wfk, wfqt, wfvt, wzt, b_forget):
    B, S, D = x.shape
    TM = PROJ_TILE
    ns = S // TM
    selkf, constk, selqf, constq = _fox_prep_constants()
    bcol = jnp.zeros((16, 1), F32).at[:FOX_HEADS, 0].set(b_forget)
    consts = (wr, wfk, wfqt, wfvt, wzt, bcol, selkf, constk, selqf, constq) + _retention_tables()
    const = lambda a: pl.BlockSpec(a.shape, lambda b, s: (0,) * a.ndim)
    tok = lambda w: pl.BlockSpec((1, TM, w), lambda b, s: (b, s, 0))
    out_shape = (
        jax.ShapeDtypeStruct((B, S, RET_WIDTH), BF16),
        jax.ShapeDtypeStruct((B, FOX_HEADS, S, AUG), BF16),
        jax.ShapeDtypeStruct((B, FOX_HEADS, ns, AUG, TM), BF16),
        jax.ShapeDtypeStruct((B, ns, FOX_WIDTH, TM), BF16),
    )
    return pl.pallas_call(
        _in_proj_kernel,
        grid=(B, ns),
        in_specs=[
            pl.BlockSpec((1, TM, D), lambda b, s: (b, s, 0)),
            pl.BlockSpec((1, D), lambda b, s: (0, 0)),
            pl.BlockSpec((TM, RET_HEAD_DIM), lambda b, s: (s, 0)),
            pl.BlockSpec((TM, RET_HEAD_DIM), lambda b, s: (s, 0)),
        ] + [const(a) for a in consts],
        out_specs=(
            tok(RET_WIDTH),
            pl.BlockSpec((1, FOX_HEADS, TM, AUG), lambda b, s: (b, 0, s, 0)),
            pl.BlockSpec((1, FOX_HEADS, 1, AUG, TM), lambda b, s: (b, 0, s, 0, 0)),
            pl.BlockSpec((1, 1, FOX_WIDTH, TM), lambda b, s: (b, s, 0, 0)),
        ),
        out_shape=out_shape,
        scratch_shapes=[pltpu.VMEM((16, LANES), F32),
                        pltpu.VMEM((RET_HEADS, RET_HEAD_DIM, RET_HEAD_DIM), F32)],
        compiler_params=pltpu.CompilerParams(
            dimension_semantics=("arbitrary", "arbitrary"), vmem_limit_bytes=VMEM_LIMIT),
        name="in_proj",
    )(x, g, cos, sin, *consts)


def _fox_prep_constants():
    d, H = FOX_HEAD_DIM, FOX_HEADS
    selkf = np.zeros((H // 2, LANES, 2 * AUG), np.float32)
    constk = np.zeros((1, AUG), np.float32)
    selqf = np.zeros((H, d, 48), np.float32)
    constq = np.zeros((d, 1), np.float32)
    for p in range(3):
        constk[0, d + p] = 1.0
        constq[3 + p, 0] = 1.0
        for h in range(H):
            selkf[h // 2, p * H + h, (h % 2) * AUG + d + 3 + p] = 1.0
            selqf[h, p, p * 16 + h] = 1.0
    return jnp.asarray(selkf, BF16), jnp.asarray(constk, F32), jnp.asarray(selqf, BF16), jnp.asarray(constq, F32)


def _retention_tables():
    L = RET_BLOCK
    log_gamma = jnp.log1p(-jnp.exp2(-5.0 - jnp.arange(RET_HEADS, dtype=F32)))
    p = jnp.arange(L, dtype=F32)
    dist = jnp.abs(p[:, None] - p[None, :])
    chunk = jnp.arange(L) // CHUNK
    allowed = (chunk[None, :] <= chunk[:, None]).astype(F32)
    dec = jnp.exp(log_gamma[:, None, None] * dist) * allowed
    lanes = lambda a: jnp.broadcast_to(a[:, :, None], (RET_HEADS, L, RET_HEAD_DIM))
    qw = lanes(jnp.exp(log_gamma[:, None] * (p[None, :] + 1.0)))
    kw = lanes(jnp.exp(log_gamma[:, None] * (L - 1.0 - p[None, :])))
    cd = jnp.broadcast_to(jnp.exp(log_gamma * L)[:, None, None], (RET_HEADS, SUBLANES, RET_HEAD_DIM))
    return dec, qw, kw, cd


def _fox_attn_kernel(q_ref, k_ref, v_ref, o_ref, s_a, s_b, s_c, cm_a, cm_b, cm_c, m_ref, acc_ref):
    T = FOX_TQ
    d = FOX_HEAD_DIM
    nq = q_ref.shape[2]
    ones_rows = (lax.broadcasted_iota(jnp.int32, (V_AUG - d, T), 0) == 0).astype(BF16)

    def scores(qi, j, s_ref, cm_ref):
        for hh in range(2):
            kj = k_ref[0, hh, pl.ds(pl.multiple_of(j * T, T), T), :]
            st = _dot(kj, q_ref[0, hh, qi])
            s_ref[hh] = st
            cm_ref[hh] = jnp.max(st, axis=0, keepdims=True)

    def consume(j, s_ref, cm_ref, masked):
        for hh in range(2):
            st = s_ref[hh]
            if masked:
                key = lax.broadcasted_iota(jnp.int32, (T, T), 0)
                qry = lax.broadcasted_iota(jnp.int32, (T, T), 1)
                st = jnp.where(key <= qry, st, -jnp.inf)
                cm = jnp.max(st, axis=0, keepdims=True)
            else:
                cm = cm_ref[hh]
            m = m_ref[hh]
            m_new = jnp.maximum(m, cm)
            p = jnp.exp2(st - m_new).astype(BF16)
            vj = jnp.concatenate([v_ref[0, j, hh * d:(hh + 1) * d, :], ones_rows], axis=0)
            acc_ref[hh] = jnp.exp2(m - m_new) * acc_ref[hh] + _dot(vj, p)
            m_ref[hh] = m_new

    def reset():
        m_ref[...] = jnp.full(m_ref.shape, -jnp.inf, F32)
        acc_ref[...] = jnp.zeros(acc_ref.shape, F32)

    def prefetch_next(qi):
        @pl.when(qi + 1 < nq)
        def _():
            scores(qi + 1, 0, s_c, cm_c)

    def finish(qi):
        outs = [acc_ref[hh, 0:d, :] / acc_ref[hh, d:d + 1, :] for hh in range(2)]
        o_ref[0, pl.ds(pl.multiple_of(qi * T, T), T), :] = jnp.concatenate(outs, axis=0).T.astype(BF16)

    reset()
    scores(0, 0, s_a, cm_a)
    prefetch_next(0)
    consume(0, s_a, cm_a, True)
    finish(0)

    def query_tile(qi, carry):
        reset()
        scores(qi, 1, s_a, cm_a)
        consume(0, s_c, cm_c, False)

        def pair(j):
            scores(qi, j + 1, s_b, cm_b)
            consume(j, s_a, cm_a, False)
            scores(qi, j + 2, s_a, cm_a)
            consume(j + 1, s_b, cm_b, False)

        def two_pairs(jj, c):
            pair(1 + 4 * jj)
            pair(3 + 4 * jj)
            return c

        def one_pair(jj, c):
            pair(1 + 4 * (n_pairs // 2) + 2 * jj)
            return c

        n_pairs = (qi - 1) // 2
        lax.fori_loop(0, n_pairs // 2, two_pairs, 0)
        lax.fori_loop(0, n_pairs % 2, one_pair, 0)

        @pl.when(qi % 2 == 1)
        def _():
            prefetch_next(qi)
            consume(qi, s_a, cm_a, True)

        @pl.when(qi % 2 == 0)
        def _():
            scores(qi, qi, s_b, cm_b)
            consume(qi - 1, s_a, cm_a, False)
            prefetch_next(qi)
            consume(qi, s_b, cm_b, True)

        finish(qi)
        return carry

    lax.fori_loop(1, nq, query_tile, 0)


def _fox_attn(qaug, kaug, fvt):
    B, H, S, _ = kaug.shape
    nk = S // FOX_TK
    nq = S // FOX_TQ
    score_buf = pltpu.VMEM((2, FOX_TK, FOX_TQ), F32)
    col_max = pltpu.VMEM((2, 1, FOX_TQ), F32)
    return pl.pallas_call(
        _fox_attn_kernel,
        grid=(B, H // 2),
        in_specs=[
            pl.BlockSpec((1, 2, nq, AUG, FOX_TQ), lambda b, p: (b, p, 0, 0, 0)),
            pl.BlockSpec((1, 2, S, AUG), lambda b, p: (b, p, 0, 0)),
            pl.BlockSpec((1, nk, 2 * FOX_HEAD_DIM, FOX_TK), lambda b, p: (b, 0, p, 0)),
        ],
        out_specs=pl.BlockSpec((1, S, 2 * FOX_HEAD_DIM), lambda b, p: (b, 0, p)),
        out_shape=jax.ShapeDtypeStruct((B, S, FOX_WIDTH), BF16),
        scratch_shapes=[
            score_buf, score_buf, score_buf, col_max, col_max, col_max,
            pltpu.VMEM((2, 1, FOX_TQ), F32), pltpu.VMEM((2, V_AUG, FOX_TQ), F32),
        ],
        compiler_params=pltpu.CompilerParams(
            dimension_semantics=("arbitrary",) * 2, vmem_limit_bytes=VMEM_LIMIT),
        name="fox_attn",
    )(qaug, kaug, fvt)


def _out_router_kernel(x_ref, oret_ref, ofox_ref, wor_ref, wof_ref, g_ref, wrh_ref, wrl_ref, br_ref,
                       h1_ref, u2_ref, sel_ref, cnt_ref):
    TM = MOE_TILE
    rows = lambda t: slice(t * TM, (t + 1) * TM)

    def out_proj(t):
        rs = rows(t)
        h1 = x_ref[rs] + _dot(oret_ref[rs], wor_ref[...]) + _dot(ofox_ref[rs], wof_ref[...])
        h1_ref[rs] = h1
        return h1

    def router_logits(t, h1):
        u2 = _rms(h1, g_ref[...])
        uh = u2.astype(BF16)
        u2_ref[rows(t)] = uh
        ul = (u2 - uh.astype(F32)).astype(BF16)
        return (_dot_nt(wrh_ref[...], uh) + _dot_nt(wrh_ref[...], ul) + _dot_nt(wrl_ref[...], uh)
                + br_ref[...])

    def top_k(t, logits):
        rs = rows(t)
        row = lax.broadcasted_iota(jnp.int32, (LANES, TM), 0).astype(F32)
        l = jnp.where(row < N_EXPERTS, logits, -jnp.inf)
        picks, vals = [], []
        for _ in range(TOP_K):
            m = jnp.max(l, axis=0, keepdims=True)
            idx = jnp.min(jnp.where(l == m, row, float(LANES)), axis=0, keepdims=True)
            pick = row == idx
            picks.append(pick)
            vals.append(m)
            l = jnp.where(pick, -jnp.inf, l)
        exps = [jnp.exp(v - vals[0]) for v in vals]
        den = exps[0] + exps[1] + exps[2] + exps[3]
        sel_t = jnp.full((LANES, TM), -1.0, F32)
        for pick, e in zip(picks, exps):
            sel_t = jnp.where(pick, e / den, sel_t)
        sel = sel_t.T
        sel_ref[rs] = sel
        cnt = jnp.sum((sel >= 0.0).astype(F32), axis=0, keepdims=True)
        cnt_ref[t] = jnp.broadcast_to(cnt, (SUBLANES, LANES))

    n = ROUTER_TILES
    h1s, lgs = {0: out_proj(0)}, {}
    for t in range(1, n + 2):
        if t < n:
            h1s[t] = out_proj(t)
        if 1 <= t <= n:
            lgs[t - 1] = router_logits(t - 1, h1s.pop(t - 1))
        if t >= 2:
            top_k(t - 2, lgs.pop(t - 2))


def _out_router(x2, o_ret, o_fox, wor, wof, g, wrh, wrl, br):
    T, D = x2.shape
    TM = MOE_TILE * ROUTER_TILES
    nT = T // MOE_TILE
    const = lambda a: pl.BlockSpec(a.shape, lambda i: (0,) * a.ndim)
    tok = lambda w: pl.BlockSpec((TM, w), lambda i: (i, 0))
    return pl.pallas_call(
        _out_router_kernel,
        grid=(T // TM,),
        in_specs=[tok(D), tok(RET_WIDTH), tok(FOX_WIDTH), const(wor), const(wof), const(g),
                  const(wrh), const(wrl), const(br)],
        out_specs=(tok(D), tok(D), tok(LANES),
                   pl.BlockSpec((ROUTER_TILES, SUBLANES, LANES), lambda i: (i, 0, 0))),
        out_shape=(
            jax.ShapeDtypeStruct((T, D), F32),
            jax.ShapeDtypeStruct((T, D), BF16),
            jax.ShapeDtypeStruct((T, LANES), F32),
            jax.ShapeDtypeStruct((nT, SUBLANES, LANES), F32),
        ),
        compiler_params=pltpu.CompilerParams(dimension_semantics=("arbitrary",)),
        name="out_router",
    )(x2, o_ret, o_fox, wor, wof, g, wrh, wrl, br)


def _tile_sort(sel):
    TM = sel.shape[0]
    NS = TOP_K * TM
    maskf = (sel >= 0.0).astype(F32)
    mask = maskf.astype(BF16)
    ri = lax.broadcasted_iota(jnp.int32, (TM, TM), 0)
    ci = lax.broadcasted_iota(jnp.int32, (TM, TM), 1)
    rank1 = maskf * _dot((ri >= ci).astype(BF16), mask)
    cnt = jnp.sum(maskf, axis=0, keepdims=True)
    ei = lax.broadcasted_iota(jnp.int32, (LANES, LANES), 0)
    ej = lax.broadcasted_iota(jnp.int32, (LANES, LANES), 1)
    cnt8 = jnp.broadcast_to(cnt, (SUBLANES, LANES)).astype(BF16)
    off = _dot(cnt8, (ei < ej).astype(BF16))[0:1, :]
    slot = lax.broadcasted_iota(jnp.int32, (NS, LANES), 0).astype(F32)
    esel = ((slot >= off) & (slot < off + cnt)).astype(BF16)
    return rank1.astype(BF16), esel, off, cnt


def _segment_wait(slot, local, remote_rows, sem, to_remote):
    whole = local.at[slot]
    rem = remote_rows.at[pl.ds(0, whole.shape[0]), :]
    cp = (pltpu.make_async_copy(whole, rem, sem.at[slot]) if to_remote
          else pltpu.make_async_copy(rem, whole, sem.at[slot]))
    cp.wait()


def _segment_dmas(step, slot, segdst_ref, cnt_ref, big_ref, local, remote_rows, sem, to_remote):
    big = big_ref[step] != 0
    for cond, top_bit in ((big, MOE_TILE), (jnp.logical_not(big), SMALL_RUN // 2)):
        pl.when(cond)(functools.partial(
            _segment_dma_path, step, slot, segdst_ref, cnt_ref, local, remote_rows, sem, to_remote, top_bit))


def _segment_dma_path(step, slot, segdst_ref, cnt_ref, local, remote_rows, sem, to_remote, top_bit):
    def body(e, off):
        c = cnt_ref[step * N_EXPERTS + e]
        dst = segdst_ref[step * N_EXPERTS + e]
        bit = top_bit
        while bit >= 1:
            done = c & (~(2 * bit - 1))

            @pl.when((c & bit) != 0)
            def _(bit=bit, done=done):
                loc = local.at[slot, pl.ds((off + done) * ROW_TILES, bit * ROW_TILES), :]
                rem = remote_rows.at[pl.ds((dst + done) * ROW_TILES, bit * ROW_TILES), :]
                cp = (pltpu.make_async_copy(loc, rem, sem.at[slot]) if to_remote
                      else pltpu.make_async_copy(rem, loc, sem.at[slot]))
                cp.start()
            bit //= 2
        return off + c

    off = 0
    for e in range(N_EXPERTS):
        off = body(e, off)


def _dispatch_kernel(segdst_ref, cnt_ref, big_ref, paddst_ref, padcnt_ref, nused_ref, u2_ref, sel_ref, xs_ref,
                     buf, zbuf, sems, zsem):
    i = pl.program_id(0)
    last = pl.num_programs(0) - 1
    slot = i % 2
    TM = MOE_TILE
    NS = TOP_K * TM
    rank1, esel, off, _ = _tile_sort(sel_ref[...])
    slot_id = lax.broadcasted_iota(jnp.int32, (NS, 1), 0).astype(F32)
    r_s = slot_id - jnp.sum(esel.astype(F32) * off, axis=1, keepdims=True)
    perm = (_dot_nt(esel, rank1) == r_s + 1.0).astype(BF16)

    @pl.when(i >= 2)
    def _():
        _segment_wait(slot, buf, xs_ref, sems, True)

    u2 = u2_ref[...]
    for c in range(NS // TM):
        rows = _dot(perm[c * TM:(c + 1) * TM], u2)
        for j in range(ROW_TILES):
            buf[slot, pl.ds(c * TM * ROW_TILES + j, TM, stride=ROW_TILES), :] = rows[:, j * LANES:(j + 1) * LANES]
    _segment_dmas(i, slot, segdst_ref, cnt_ref, big_ref, buf, xs_ref, sems, True)

    @pl.when(i == last)
    def _():
        @pl.when(i >= 1)
        def _():
            _segment_wait(1 - slot, buf, xs_ref, sems, True)
        _segment_wait(slot, buf, xs_ref, sems, True)
        zbuf[...] = jnp.zeros_like(zbuf)
        half = EXPERT_BLOCK // 2 * ROW_TILES
        n_blocks = xs_ref.shape[0] // (EXPERT_BLOCK * ROW_TILES)
        for wait in (False, True):
            def unused(hb, carry, wait=wait):
                cp = pltpu.make_async_copy(zbuf, xs_ref.at[pl.ds(hb * half, half), :], zsem.at[0])
                cp.wait() if wait else cp.start()
                return carry
            lax.fori_loop(2 * nused_ref[0], 2 * n_blocks, unused, 0)


            def body(e, carry, wait=wait):
                c = padcnt_ref[e]
                dst = paddst_ref[e]
                bit = EXPERT_BLOCK // 2
                while bit >= 1:
                    done = c & (~(2 * bit - 1))

                    @pl.when((c & bit) != 0)
                    def _(bit=bit, done=done):
                        cp = pltpu.make_async_copy(
                            zbuf.at[pl.ds(0, bit * ROW_TILES), :],
                            xs_ref.at[pl.ds((dst + done) * ROW_TILES, bit * ROW_TILES), :], zsem.at[0])
                        cp.wait() if wait else cp.start()
                    bit //= 2
                return carry
            lax.fori_loop(0, N_EXPERTS, body, 0)


def _dispatch(u2, sel, segdst, cnt, big, paddst, padcnt, n_used, n_rows):
    T, D = u2.shape
    TM = MOE_TILE
    NS = TOP_K * TM
    return pl.pallas_call(
        _dispatch_kernel,
        grid_spec=pltpu.PrefetchScalarGridSpec(
            num_scalar_prefetch=6,
            grid=(T // TM,),
            in_specs=[pl.BlockSpec((TM, D), lambda i, *_: (i, 0)),
                      pl.BlockSpec((TM, LANES), lambda i, *_: (i, 0))],
            out_specs=pl.BlockSpec(memory_space=pl.ANY),
            scratch_shapes=[pltpu.VMEM((2, NS * ROW_TILES, LANES), F32),
                            pltpu.VMEM((EXPERT_BLOCK // 2 * ROW_TILES, LANES), F32),
                            pltpu.SemaphoreType.DMA((2,)), pltpu.SemaphoreType.DMA((1,))],
        ),
        out_shape=jax.ShapeDtypeStruct((n_rows * ROW_TILES, LANES), F32),
        compiler_params=pltpu.CompilerParams(
            dimension_semantics=("arbitrary",), vmem_limit_bytes=VMEM_LIMIT),
        name="dispatch",
    )(segdst, cnt, big, paddst, padcnt, n_used, u2, sel)


def _expert_kernel(bexp_ref, nused_ref, epos_ref, enext_ref, xs_ref, w1_hbm, b1_ref, w2_hbm, b2_ref, ys_ref,
                   w1f, w2f, w1b, w2b, wsem):
    b = pl.program_id(0)
    BLK = EXPERT_BLOCK
    used = b < nused_ref[0]

    def weight_copies(e, slot):
        return (pltpu.make_async_copy(w1_hbm.at[e], w1f.at[slot], wsem.at[0, slot]),
                pltpu.make_async_copy(w2_hbm.at[e], w2f.at[slot], wsem.at[1, slot]))

    @pl.when(used)
    def _():
        e = bexp_ref[b]
        prev = bexp_ref[jnp.maximum(b - 1, 0)]
        slot = epos_ref[b] % 2

        @pl.when(b == 0)
        def _():
            for cp in weight_copies(e, slot):
                cp.start()

        @pl.when((b == 0) | (e != prev))
        def _():
            nxt = enext_ref[b]

            @pl.when(nxt >= 0)
            def _():
                for cp in weight_copies(nxt, 1 - slot):
                    cp.start()

            for cp in weight_copies(e, slot):
                cp.wait()
            rows = 128

            def cast(r, carry):
                sl = pl.ds(pl.multiple_of(r * rows, rows), rows)
                w1b[sl, :] = w1f[slot, sl, :].astype(BF16)
                w2b[sl, :] = w2f[slot, sl, :].astype(BF16)
                return carry
            lax.fori_loop(0, D_MODEL // rows, cast, 0)

        R = EXPERT_PASS_ROWS
        for rp in range(BLK // R):
            r0 = rp * R * ROW_TILES
            x = jnp.concatenate([xs_ref[pl.ds(r0 + j, R, stride=ROW_TILES), :] for j in range(ROW_TILES)],
                                axis=1).astype(BF16)
            h = _dot(x, w1b[...]) + b1_ref[0]
            glu = jnp.minimum(h[:, :D_FF], SWIGLU_LIMIT)
            lin = jnp.clip(h[:, D_FF:], -SWIGLU_LIMIT, SWIGLU_LIMIT)
            act = glu * jax.nn.sigmoid(SWIGLU_ALPHA * glu) * (lin + 1.0)
            y = _dot(act.astype(BF16), w2b[...]) + b2_ref[0]
            for j in range(ROW_TILES):
                ys_ref[pl.ds(r0 + j, R, stride=ROW_TILES), :] = y[:, j * LANES:(j + 1) * LANES]

    @pl.when(jnp.logical_not(used))
    def _():
        ys_ref[...] = jnp.zeros_like(ys_ref)


def _experts(xs, block_exp, n_used, exp_pos, exp_next, w1, b1, w2, b2):
    BLK = EXPERT_BLOCK
    NB = xs.shape[0] // (BLK * ROW_TILES)
    blk = lambda b, nused: jnp.minimum(b, nused[0] - 1)
    return pl.pallas_call(
        _expert_kernel,
        grid_spec=pltpu.PrefetchScalarGridSpec(
            num_scalar_prefetch=4,
            grid=(NB,),
            in_specs=[
                pl.BlockSpec((BLK * ROW_TILES, LANES), lambda b, bexp, nused, *_: (blk(b, nused), 0)),
                pl.BlockSpec(memory_space=pl.ANY),
                pl.BlockSpec((1, 1, 2 * D_FF), lambda b, bexp, nused, *_: (bexp[blk(b, nused)], 0, 0)),
                pl.BlockSpec(memory_space=pl.ANY),
                pl.BlockSpec((1, 1, D_MODEL), lambda b, bexp, nused, *_: (bexp[blk(b, nused)], 0, 0)),
            ],
            out_specs=pl.BlockSpec((BLK * ROW_TILES, LANES), lambda b, *_: (b, 0)),
            scratch_shapes=[pltpu.VMEM((2, D_MODEL, 2 * D_FF), F32), pltpu.VMEM((2, D_FF, D_MODEL), F32),
                            pltpu.VMEM((D_MODEL, 2 * D_FF), BF16), pltpu.VMEM((D_FF, D_MODEL), BF16),
                            pltpu.SemaphoreType.DMA((2, 2))],
        ),
        out_shape=jax.ShapeDtypeStruct(xs.shape, F32),
        compiler_params=pltpu.CompilerParams(
            dimension_semantics=("arbitrary",), vmem_limit_bytes=VMEM_LIMIT_EXPERTS),
        name="experts",
    )(block_exp, n_used, exp_pos, exp_next, xs, w1, b1[:, None, :], w2, b2[:, None, :])


def _combine_kernel(segdst_ref, cnt_ref, big_ref, ys_ref, sel_ref, h1_ref, g_ref, out_ref, buf, sems):
    i = pl.program_id(0)
    n = pl.num_programs(0)
    slot = i % 2
    TM = MOE_TILE
    NS = TOP_K * TM

    @pl.when(i == 0)
    def _():
        _segment_dmas(i, slot, segdst_ref, cnt_ref, big_ref, buf, ys_ref, sems, False)

    @pl.when(i + 1 < n)
    def _():
        _segment_dmas(i + 1, 1 - slot, segdst_ref, cnt_ref, big_ref, buf, ys_ref, sems, False)

    sel = sel_ref[...]
    rank1, esel, off, _ = _tile_sort(sel)
    gate = jnp.maximum(sel, 0.0).astype(BF16)
    o3 = _split3(jnp.broadcast_to(off, (SUBLANES, LANES)))
    off_s = (_dot_nt(o3[0], esel) + _dot_nt(o3[1], esel) + _dot_nt(o3[2], esel))[0:1, :]
    r_s = lax.broadcasted_iota(jnp.int32, (1, NS), 1).astype(F32) - off_s
    hit = _dot_nt(rank1, esel) == r_s + 1.0
    unperm = jnp.where(hit, _dot_nt(gate, esel), 0.0).astype(BF16)

    _segment_wait(slot, buf, ys_ref, sems, False)
    y = jnp.concatenate([buf[slot, pl.ds(j, NS, stride=ROW_TILES), :] for j in range(ROW_TILES)],
                        axis=1).astype(BF16)
    h2 = h1_ref[...] + _dot(unperm, y)
    out_ref[...] = _rms(h2, g_ref[...])


def _combine(ys, sel, h1, g, segdst, cnt, big):
    T, D = h1.shape
    TM = MOE_TILE
    NS = TOP_K * TM
    return pl.pallas_call(
        _combine_kernel,
        grid_spec=pltpu.PrefetchScalarGridSpec(
            num_scalar_prefetch=3,
            grid=(T // TM,),
            in_specs=[pl.BlockSpec(memory_space=pl.ANY),
                      pl.BlockSpec((TM, LANES), lambda i, *_: (i, 0)),
                      pl.BlockSpec((TM, D), lambda i, *_: (i, 0)),
                      pl.BlockSpec((1, D), lambda i, *_: (0, 0))],
            out_specs=pl.BlockSpec((TM, D), lambda i, *_: (i, 0)),
            scratch_shapes=[pltpu.VMEM((2, NS * ROW_TILES, LANES), F32), pltpu.SemaphoreType.DMA((2,))],
        ),
        out_shape=jax.ShapeDtypeStruct((T, D), F32),
        compiler_params=pltpu.CompilerParams(
            dimension_semantics=("arbitrary",), vmem_limit_bytes=VMEM_LIMIT),
        name="combine",
    )(segdst, cnt, big, ys, sel, h1, g)


def _routing_tables(cnt_tiles):
    BLK = EXPERT_BLOCK
    nT = cnt_tiles.shape[0]
    A = nT * MOE_TILE * TOP_K
    NB = A // BLK + N_EXPERTS
    total = jnp.sum(cnt_tiles, axis=0)
    padded = (total + BLK - 1) // BLK * BLK
    pad_ends = jnp.cumsum(padded)
    pad_starts = pad_ends - padded
    before = jnp.cumsum(cnt_tiles, axis=0) - cnt_tiles
    segdst = (pad_starts[None, :] + before).reshape(-1).astype(jnp.int32)
    block_start = jnp.arange(NB, dtype=jnp.int32) * BLK
    block_exp = jnp.minimum(jnp.sum(pad_ends[None, :] <= block_start[:, None], axis=1), N_EXPERTS - 1).astype(jnp.int32)
    n_used = (pad_ends[-1] // BLK).astype(jnp.int32).reshape(1)
    paddst = (pad_starts + total).astype(jnp.int32)
    padcnt = (padded - total).astype(jnp.int32)
    big = jnp.any(cnt_tiles >= SMALL_RUN, axis=1).astype(jnp.int32)
    has_rows = total > 0
    ids = jnp.arange(N_EXPERTS, dtype=jnp.int32)
    pos = (jnp.cumsum(has_rows) - has_rows).astype(jnp.int32)
    later = jnp.where(has_rows[None, :] & (ids[None, :] > ids[:, None]), ids[None, :], N_EXPERTS)
    nxt = jnp.min(later, axis=1)
    nxt = jnp.where(nxt < N_EXPERTS, nxt, -1).astype(jnp.int32)
    return (segdst, cnt_tiles.reshape(-1).astype(jnp.int32), big, paddst, padcnt, block_exp, n_used,
            pos[block_exp], nxt[block_exp], NB * BLK)


def _rotary_tables(S):
    half = RET_HEAD_DIM // 2
    inv_freq = ROPE_BASE ** (-jnp.arange(half, dtype=F32) / half)
    ang = jnp.arange(S, dtype=F32)[:, None] * inv_freq[None, :]
    cos, sin = jnp.cos(ang), jnp.sin(ang)
    return jnp.concatenate([cos, cos], axis=-1), jnp.concatenate([-sin, sin], axis=-1)


def _layer(h, norm_mix_g, w_in, b_forget, w_out, norm_ffn_g, w_router, b_router,
           w_exp_in, b_exp_in, w_exp_out, b_exp_out, final_g):
    B, S, D = h.shape
    R, Fw = RET_WIDTH, FOX_WIDTH
    cos, sin = _rotary_tables(S)
    wb = w_in.astype(BF16)
    wr = wb[:, :4 * R]
    wfq, wfk, wfv = (wb[:, 4 * R + i * Fw:4 * R + (i + 1) * Fw] for i in range(3))
    wzt = jnp.zeros((16, D), BF16).at[:FOX_HEADS, :].set(wb[:, 4 * R + 3 * Fw:].T)
    o_ret, kaug, qaug, fvt = _in_proj(
        h, norm_mix_g[None, :], cos, sin, wr, wfk, wfq.T, wfv.T, wzt, b_forget)
    o_fox = _fox_attn(qaug, kaug, fvt)

    T = B * S
    wo = w_out.astype(BF16)
    wrt = jnp.zeros((LANES, D), F32).at[:N_EXPERTS, :].set(w_router.T)
    wrh = wrt.astype(BF16)
    wrl = (wrt - wrh.astype(F32)).astype(BF16)
    br = jnp.zeros((LANES, 1), F32).at[:N_EXPERTS, 0].set(b_router)
    h1, u2, sel, cnt = _out_router(h.reshape(T, D), o_ret.reshape(T, R), o_fox.reshape(T, Fw),
                                   wo[:R], wo[R:], norm_ffn_g[None, :], wrh, wrl, br)
    cnt_tiles = cnt[:, 0, :N_EXPERTS].astype(jnp.int32)
    segdst, cnt_flat, big, paddst, padcnt, block_exp, n_used, exp_pos, exp_next, n_rows = _routing_tables(cnt_tiles)
    xs = _dispatch(u2, sel, segdst, cnt_flat, big, paddst, padcnt, n_used, n_rows)
    ys = _experts(xs, block_exp, n_used, exp_pos, exp_next, w_exp_in, b_exp_in, w_exp_out, b_exp_out)
    out = _combine(ys, sel, h1, final_g[None, :], segdst, cnt_flat, big)
    return out.reshape(B, S, D)


def kernel(x, norm_mix_g, w_in, b_forget, w_out, norm_ffn_g, w_router, b_router,
           w_exp_in, b_exp_in, w_exp_out, b_exp_out, norm_final_g):
    depth = w_in.shape[0]
    assert depth == 1, "the fused final RMSNorm assumes a single layer"
    return _layer(x, norm_mix_g[0], w_in[0], b_forget[0], w_out[0], norm_ffn_g[0], w_router[0], b_router[0],
                  w_exp_in[0], b_exp_in[0], w_exp_out[0], b_exp_out[0], norm_final_g)
```

```python
import functools

import numpy as np
import jax
import jax.numpy as jnp
from jax import lax
from jax.experimental import pallas as pl
from jax.experimental.pallas import tpu as pltpu

F32 = jnp.float32
BF16 = jnp.bfloat16

D_MODEL = 1024
RET_HEADS, RET_HEAD_DIM = 4, 128
RET_WIDTH = RET_HEADS * RET_HEAD_DIM
FOX_HEADS, FOX_HEAD_DIM = 8, 64
FOX_WIDTH = FOX_HEADS * FOX_HEAD_DIM
CHUNK = 64
ROPE_BASE = 10000.0
N_EXPERTS = 32
TOP_K = 4
D_FF = D_MODEL
SWIGLU_ALPHA = 1.702
SWIGLU_LIMIT = 7.0
RMS_EPS = 1e-5
GN_EPS = 1e-5

LANES = 128
SUBLANES = 8
ROW_TILES = D_MODEL // LANES
V7X_VMEM_BYTES = 64 * 1024 * 1024
VMEM_LIMIT = V7X_VMEM_BYTES * 3 // 4
VMEM_LIMIT_EXPERTS = V7X_VMEM_BYTES * 7 // 8

PROJ_TILE = 512
RET_BLOCK = 256
FOX_TQ = 512
FOX_TK = PROJ_TILE
AUG = 128
V_AUG = 80
LOG2E = 1.4426950408889634
MOE_TILE = 256
ROUTER_TILES = 4
SMALL_RUN = 64
EXPERT_BLOCK = 512
EXPERT_PASS_ROWS = 256

NT_DIMS = (((1,), (1,)), ((), ()))


def _split3(a):
    hi = a.astype(BF16)
    r1 = a - hi.astype(F32)
    mid = r1.astype(BF16)
    lo = (r1 - mid.astype(F32)).astype(BF16)
    return hi, mid, lo


def _dot(a, b):
    return jnp.dot(a, b, preferred_element_type=F32)


def _dot_nt(a, b):
    return lax.dot_general(a, b, NT_DIMS, preferred_element_type=F32)


def _rms(x, g):
    return x * lax.rsqrt(jnp.mean(x * x, axis=-1, keepdims=True) + RMS_EPS) * g


def _in_proj_kernel(x_ref, g_ref, cos_ref, sin_ref, wr_ref, wfk_ref, wfqt_ref, wfvt_ref, wzt_ref, bcol_ref,
                    selkf_ref, constk_ref, selqf_ref, constq_ref, dec_ref, qw_ref, kw_ref, cd_ref,
                    oret_ref, kaug_ref, qaug_ref, fvt_ref, ccol, state):
    TM = x_ref.shape[1]
    d, H = FOX_HEAD_DIM, FOX_HEADS

    @pl.when(pl.program_id(1) == 0)
    def _():
        ccol[...] = jnp.zeros_like(ccol)
        state[...] = jnp.zeros_like(state)

    u = _rms(x_ref[0], g_ref[...]).astype(BF16)
    r = _dot(u, wr_ref[...])
    cos, sin = cos_ref[...], sin_ref[...]
    k_scale = RET_HEAD_DIM ** -0.5
    L = RET_BLOCK
    for h in range(RET_HEADS):
        hs = slice(h * RET_HEAD_DIM, (h + 1) * RET_HEAD_DIM)
        q = r[:, hs]
        k = r[:, RET_WIDTH + h * RET_HEAD_DIM:RET_WIDTH + (h + 1) * RET_HEAD_DIM]
        qh = (q * cos + pltpu.roll(q, RET_HEAD_DIM // 2, 1) * sin).astype(BF16)
        kh = ((k * cos + pltpu.roll(k, RET_HEAD_DIM // 2, 1) * sin) * k_scale).astype(BF16)
        vh = r[:, 2 * RET_WIDTH + h * RET_HEAD_DIM:2 * RET_WIDTH + (h + 1) * RET_HEAD_DIM].astype(BF16)
        gh = r[:, 3 * RET_WIDTH + h * RET_HEAD_DIM:3 * RET_WIDTH + (h + 1) * RET_HEAD_DIM].astype(BF16)
        for rs in (slice(b0, b0 + L) for b0 in range(0, TM, L)):
            qb, kb, vb = qh[rs], kh[rs], vh[rs]
            scores = (_dot_nt(qb, kb) * dec_ref[h]).astype(BF16)
            st = state[h]
            o = _dot(scores, vb) + _dot((qb.astype(F32) * qw_ref[h]).astype(BF16), st.astype(BF16))
            kk = kb.astype(F32) * kw_ref[h]
            state[h] = st * cd_ref[h, 0:1, :] + _dot(kk.T.astype(BF16), vb)
            mu = jnp.mean(o, axis=-1, keepdims=True)
            oc = o - mu
            var = jnp.mean(oc * oc, axis=-1, keepdims=True)
            oret_ref[0, rs, hs] = (oc * lax.rsqrt(var + GN_EPS) * jax.nn.silu(gh[rs].astype(F32))).astype(BF16)
    fvt_ref[0, 0] = _dot_nt(wfvt_ref[...], u).astype(BF16)
    fk = _dot(u, wfk_ref[...])
    fqt = (_dot_nt(wfqt_ref[...], u) * (d ** -0.5 * LOG2E)).astype(BF16)
    zt = _dot_nt(wzt_ref[...], u)


    row = lax.broadcasted_iota(jnp.int32, (16, TM), 0)
    lft = jnp.where(row < H, jax.nn.log_sigmoid(zt + bcol_ref[...]), 0.0)
    utri = (lax.broadcasted_iota(jnp.int32, (TM, TM), 0) <= lax.broadcasted_iota(jnp.int32, (TM, TM), 1)).astype(BF16)
    t3 = _split3(lft)
    f_col = _dot(t3[0], utri) + _dot(t3[1], utri) + _dot(t3[2], utri) + ccol[:, 0:1]
    ccol[...] = jnp.broadcast_to(f_col[:, TM - 1:TM], ccol.shape)

    pieces_t = jnp.concatenate(_split3(f_col * LOG2E), axis=0)
    for h in range(H):
        extra = _dot(selqf_ref[h], pieces_t) + constq_ref[...]
        qaug_ref[0, h, 0] = jnp.concatenate([fqt[h * d:(h + 1) * d, :], extra.astype(BF16)], axis=0)

    f_row = jnp.concatenate([f_col, jnp.zeros((LANES - 16, TM), F32)], axis=0).T
    n3 = _split3(f_row * -LOG2E)
    pieces = (n3[0].astype(F32) + pltpu.roll(n3[1].astype(F32), H, 1)
              + pltpu.roll(n3[2].astype(F32), 2 * H, 1)).astype(BF16)
    lane = lax.broadcasted_iota(jnp.int32, (TM, LANES), 1)
    for g in range(H // 2):
        bias = _dot(pieces, selkf_ref[g])
        kg = fk[:, g * 2 * d:(g + 1) * 2 * d]
        for o in range(2):
            kh = kg if o == 0 else pltpu.roll(kg, d, 1)
            extra = bias[:, o * AUG:(o + 1) * AUG] + constk_ref[...]
            kaug_ref[0, 2 * g + o] = jnp.where(lane < d, kh, extra).astype(BF16)


def _in_proj(x, g, cos, sin, wr, wfk, wfqt, wfvt, wzt, b_forget):
    B, S, D = x.shape
    TM = PROJ_TILE
    ns = S // TM
    selkf, constk, selqf, constq = _fox_prep_constants()
    bcol = jnp.zeros((16, 1), F32).at[:FOX_HEADS, 0].set(b_forget)
    consts = (wr, wfk, wfqt, wfvt, wzt, bcol, selkf, constk, selqf, constq) + _retention_tables()
    const = lambda a: pl.BlockSpec(a.shape, lambda b, s: (0,) * a.ndim)
    tok = lambda w: pl.BlockSpec((1, TM, w), lambda b, s: (b, s, 0))
    out_shape = (
        jax.ShapeDtypeStruct((B, S, RET_WIDTH), BF16),
        jax.ShapeDtypeStruct((B, FOX_HEADS, S, AUG), BF16),
        jax.ShapeDtypeStruct((B, FOX_HEADS, ns, AUG, TM), BF16),
        jax.ShapeDtypeStruct((B, ns, FOX_WIDTH, TM), BF16),
    )
    return pl.pallas_call(
        _in_proj_kernel,
        grid=(B, ns),
        in_specs=[
            pl.BlockSpec((1, TM, D), lambda b, s: (b, s, 0)),
            pl.BlockSpec((1, D), lambda b, s: (0, 0)),
            pl.BlockSpec((TM, RET_HEAD_DIM), lambda b, s: (s, 0)),
            pl.BlockSpec((TM, RET_HEAD_DIM), lambda b, s: (s, 0)),
        ] + [const(a) for a in consts],
        out_specs=(
            tok(RET_WIDTH),
            pl.BlockSpec((1, FOX_HEADS, TM, AUG), lambda b, s: (b, 0, s, 0)),
            pl.BlockSpec((1, FOX_HEADS, 1, AUG, TM), lambda b, s: (b, 0, s, 0, 0)),
            pl.BlockSpec((1, 1, FOX_WIDTH, TM), lambda b, s: (b, s, 0, 0)),
        ),
        out_shape=out_shape,
        scratch_shapes=[pltpu.VMEM((16, LANES), F32),
                        pltpu.VMEM((RET_HEADS, RET_HEAD_DIM, RET_HEAD_DIM), F32)],
        compiler_params=pltpu.CompilerParams(
            dimension_semantics=("arbitrary", "arbitrary"), vmem_limit_bytes=VMEM_LIMIT),
        name="in_proj",
    )(x, g, cos, sin, *consts)


def _fox_prep_constants():
    d, H = FOX_HEAD_DIM, FOX_HEADS
    selkf = np.zeros((H // 2, LANES, 2 * AUG), np.float32)
    constk = np.zeros((1, AUG), np.float32)
    selqf = np.zeros((H, d, 48), np.float32)
    constq = np.zeros((d, 1), np.float32)
    for p in range(3):
        constk[0, d + p] = 1.0
        constq[3 + p, 0] = 1.0
        for h in range(H):
            selkf[h // 2, p * H + h, (h % 2) * AUG + d + 3 + p] = 1.0
            selqf[h, p, p * 16 + h] = 1.0
    return jnp.asarray(selkf, BF16), jnp.asarray(constk, F32), jnp.asarray(selqf, BF16), jnp.asarray(constq, F32)


def _retention_tables():
    L = RET_BLOCK
    f32 = np.float32
    log_gamma = np.log1p(-np.exp2(-5.0 - np.arange(RET_HEADS, dtype=f32))).astype(f32)
    p = np.arange(L, dtype=f32)
    dist = np.abs(p[:, None] - p[None, :])
    chunk = np.arange(L) // CHUNK
    allowed = (chunk[None, :] <= chunk[:, None]).astype(f32)
    dec = np.exp(log_gamma[:, None, None] * dist).astype(f32) * allowed
    lanes = lambda a: np.broadcast_to(a.astype(f32)[:, :, None], (RET_HEADS, L, RET_HEAD_DIM))
    qw = lanes(np.exp(log_gamma[:, None] * (p[None, :] + f32(1.0))))
    kw = lanes(np.exp(log_gamma[:, None] * (f32(L - 1.0) - p[None, :])))
    cd = np.broadcast_to(np.exp(log_gamma * f32(L)).astype(f32)[:, None, None], (RET_HEADS, SUBLANES, RET_HEAD_DIM))
    return tuple(jnp.asarray(a, F32) for a in (dec, qw, kw, cd))


def _fox_attn_kernel(q_ref, k_ref, v_ref, o_ref, s_a, s_b, s_c, cm_a, cm_b, cm_c, m_ref, acc_ref):
    T = FOX_TQ
    d = FOX_HEAD_DIM
    nq = q_ref.shape[2]
    ones_rows = (lax.broadcasted_iota(jnp.int32, (V_AUG - d, T), 0) == 0).astype(BF16)

    def scores(qi, j, s_ref, cm_ref):
        for hh in range(2):
            kj = k_ref[0, hh, pl.ds(pl.multiple_of(j * T, T), T), :]
            st = _dot(kj, q_ref[0, hh, qi])
            s_ref[hh] = st
            cm_ref[hh] = jnp.max(st, axis=0, keepdims=True)

    def consume(j, s_ref, cm_ref, masked):
        for hh in range(2):
            st = s_ref[hh]
            if masked:
                key = lax.broadcasted_iota(jnp.int32, (T, T), 0)
                qry = lax.broadcasted_iota(jnp.int32, (T, T), 1)
                st = jnp.where(key <= qry, st, -jnp.inf)
                cm = jnp.max(st, axis=0, keepdims=True)
            else:
                cm = cm_ref[hh]
            m = m_ref[hh]
            m_new = jnp.maximum(m, cm)
            p = jnp.exp2(st - m_new).astype(BF16)
            vj = jnp.concatenate([v_ref[0, j, hh * d:(hh + 1) * d, :], ones_rows], axis=0)
            acc_ref[hh] = jnp.exp2(m - m_new) * acc_ref[hh] + _dot(vj, p)
            m_ref[hh] = m_new

    def reset():
        m_ref[...] = jnp.full(m_ref.shape, -jnp.inf, F32)
        acc_ref[...] = jnp.zeros(acc_ref.shape, F32)

    def prefetch_next(qi):
        @pl.when(qi + 1 < nq)
        def _():
            scores(qi + 1, 0, s_c, cm_c)

    def finish(qi):
        outs = [acc_ref[hh, 0:d, :] / acc_ref[hh, d:d + 1, :] for hh in range(2)]
        o_ref[0, pl.ds(pl.multiple_of(qi * T, T), T), :] = jnp.concatenate(outs, axis=0).T.astype(BF16)

    reset()
    scores(0, 0, s_a, cm_a)
    prefetch_next(0)
    consume(0, s_a, cm_a, True)
    finish(0)

    def query_tile(qi, carry):
        reset()
        scores(qi, 1, s_a, cm_a)
        consume(0, s_c, cm_c, False)

        def pair(j):
            scores(qi, j + 1, s_b, cm_b)
            consume(j, s_a, cm_a, False)
            scores(qi, j + 2, s_a, cm_a)
            consume(j + 1, s_b, cm_b, False)

        def two_pairs(jj, c):
            pair(1 + 4 * jj)
            pair(3 + 4 * jj)
            return c

        def one_pair(jj, c):
            pair(1 + 4 * (n_pairs // 2) + 2 * jj)
            return c

        n_pairs = (qi - 1) // 2
        lax.fori_loop(0, n_pairs // 2, two_pairs, 0)
        lax.fori_loop(0, n_pairs % 2, one_pair, 0)

        @pl.when(qi % 2 == 1)
        def _():
            prefetch_next(qi)
            consume(qi, s_a, cm_a, True)

        @pl.when(qi % 2 == 0)
        def _():
            scores(qi, qi, s_b, cm_b)
            consume(qi - 1, s_a, cm_a, False)
            prefetch_next(qi)
            consume(qi, s_b, cm_b, True)

        finish(qi)
        return carry

    lax.fori_loop(1, nq, query_tile, 0)


def _fox_attn(qaug, kaug, fvt):
    B, H, S, _ = kaug.shape
    nk = S // FOX_TK
    nq = S // FOX_TQ
    score_buf = pltpu.VMEM((2, FOX_TK, FOX_TQ), F32)
    col_max = pltpu.VMEM((2, 1, FOX_TQ), F32)
    return pl.pallas_call(
        _fox_attn_kernel,
        grid=(B, H // 2),
        in_specs=[
            pl.BlockSpec((1, 2, nq, AUG, FOX_TQ), lambda b, p: (b, p, 0, 0, 0)),
            pl.BlockSpec((1, 2, S, AUG), lambda b, p: (b, p, 0, 0)),
            pl.BlockSpec((1, nk, 2 * FOX_HEAD_DIM, FOX_TK), lambda b, p: (b, 0, p, 0)),
        ],
        out_specs=pl.BlockSpec((1, S, 2 * FOX_HEAD_DIM), lambda b, p: (b, 0, p)),
        out_shape=jax.ShapeDtypeStruct((B, S, FOX_WIDTH), BF16),
        scratch_shapes=[
            score_buf, score_buf, score_buf, col_max, col_max, col_max,
            pltpu.VMEM((2, 1, FOX_TQ), F32), pltpu.VMEM((2, V_AUG, FOX_TQ), F32),
        ],
        compiler_params=pltpu.CompilerParams(
            dimension_semantics=("arbitrary",) * 2, vmem_limit_bytes=VMEM_LIMIT),
        name="fox_attn",
    )(qaug, kaug, fvt)


def _out_router_kernel(x_ref, oret_ref, ofox_ref, wor_ref, wof_ref, g_ref, wrh_ref, wrl_ref, br_ref,
                       h1_ref, u2_ref, sel_ref, cnt_ref):
    TM = MOE_TILE
    rows = lambda t: slice(t * TM, (t + 1) * TM)

    def out_proj(t):
        rs = rows(t)
        h1 = x_ref[rs] + _dot(oret_ref[rs], wor_ref[...]) + _dot(ofox_ref[rs], wof_ref[...])
        h1_ref[rs] = h1
        return h1

    def router_logits(t, h1):
        u2 = _rms(h1, g_ref[...])
        uh = u2.astype(BF16)
        u2_ref[rows(t)] = uh
        ul = (u2 - uh.astype(F32)).astype(BF16)
        return (_dot_nt(wrh_ref[...], uh) + _dot_nt(wrh_ref[...], ul) + _dot_nt(wrl_ref[...], uh)
                + br_ref[...])

    def top_k(t, logits):
        rs = rows(t)
        row = lax.broadcasted_iota(jnp.int32, (LANES, TM), 0).astype(F32)
        l = jnp.where(row < N_EXPERTS, logits, -jnp.inf)
        picks, vals = [], []
        for _ in range(TOP_K):
            m = jnp.max(l, axis=0, keepdims=True)
            idx = jnp.min(jnp.where(l == m, row, float(LANES)), axis=0, keepdims=True)
            pick = row == idx
            picks.append(pick)
            vals.append(m)
            l = jnp.where(pick, -jnp.inf, l)
        exps = [jnp.exp(v - vals[0]) for v in vals]
        den = exps[0] + exps[1] + exps[2] + exps[3]
        sel_t = jnp.full((LANES, TM), -1.0, F32)
        for pick, e in zip(picks, exps):
            sel_t = jnp.where(pick, e / den, sel_t)
        sel = sel_t.T
        sel_ref[rs] = sel
        cnt = jnp.sum((sel >= 0.0).astype(F32), axis=0, keepdims=True)
        cnt_ref[t] = jnp.broadcast_to(cnt, (SUBLANES, LANES))

    n = ROUTER_TILES
    h1s, lgs = {0: out_proj(0)}, {}
    for t in range(1, n + 2):
        if t < n:
            h1s[t] = out_proj(t)
        if 1 <= t <= n:
            lgs[t - 1] = router_logits(t - 1, h1s.pop(t - 1))
        if t >= 2:
            top_k(t - 2, lgs.pop(t - 2))


def _out_router(x2, o_ret, o_fox, wor, wof, g, wrh, wrl, br):
    T, D = x2.shape
    TM = MOE_TILE * ROUTER_TILES
    nT = T // MOE_TILE
    const = lambda a: pl.BlockSpec(a.shape, lambda i: (0,) * a.ndim)
    tok = lambda w: pl.BlockSpec((TM, w), lambda i: (i, 0))
    return pl.pallas_call(
        _out_router_kernel,
        grid=(T // TM,),
        in_specs=[tok(D), tok(RET_WIDTH), tok(FOX_WIDTH), const(wor), const(wof), const(g),
                  const(wrh), const(wrl), const(br)],
        out_specs=(tok(D), tok(D), tok(LANES),
                   pl.BlockSpec((ROUTER_TILES, SUBLANES, LANES), lambda i: (i, 0, 0))),
        out_shape=(
            jax.ShapeDtypeStruct((T, D), F32),
            jax.ShapeDtypeStruct((T, D), BF16),
            jax.ShapeDtypeStruct((T, LANES), F32),
            jax.ShapeDtypeStruct((nT, SUBLANES, LANES), F32),
        ),
        compiler_params=pltpu.CompilerParams(dimension_semantics=("arbitrary",)),
        name="out_router",
    )(x2, o_ret, o_fox, wor, wof, g, wrh, wrl, br)


def _tile_sort(sel):
    TM = sel.shape[0]
    NS = TOP_K * TM
    maskf = (sel >= 0.0).astype(F32)
    mask = maskf.astype(BF16)
    ri = lax.broadcasted_iota(jnp.int32, (TM, TM), 0)
    ci = lax.broadcasted_iota(jnp.int32, (TM, TM), 1)
    rank1 = maskf * _dot((ri >= ci).astype(BF16), mask)
    cnt = jnp.sum(maskf, axis=0, keepdims=True)
    ei = lax.broadcasted_iota(jnp.int32, (LANES, LANES), 0)
    ej = lax.broadcasted_iota(jnp.int32, (LANES, LANES), 1)
    cnt8 = jnp.broadcast_to(cnt, (SUBLANES, LANES)).astype(BF16)
    off = _dot(cnt8, (ei < ej).astype(BF16))[0:1, :]
    slot = lax.broadcasted_iota(jnp.int32, (NS, LANES), 0).astype(F32)
    esel = ((slot >= off) & (slot < off + cnt)).astype(BF16)
    return rank1.astype(BF16), esel, off, cnt


def _segment_wait(slot, local, remote_rows, sem, to_remote):
    whole = local.at[slot]
    rem = remote_rows.at[pl.ds(0, whole.shape[0]), :]
    cp = (pltpu.make_async_copy(whole, rem, sem.at[slot]) if to_remote
          else pltpu.make_async_copy(rem, whole, sem.at[slot]))
    cp.wait()


def _segment_dmas(step, slot, segdst_ref, cnt_ref, big_ref, local, remote_rows, sem, to_remote):
    big = big_ref[step] != 0
    for cond, top_bit in ((big, MOE_TILE), (jnp.logical_not(big), SMALL_RUN // 2)):
        pl.when(cond)(functools.partial(
            _segment_dma_path, step, slot, segdst_ref, cnt_ref, local, remote_rows, sem, to_remote, top_bit))


def _segment_dma_path(step, slot, segdst_ref, cnt_ref, local, remote_rows, sem, to_remote, top_bit):
    def body(e, off):
        c = cnt_ref[step * N_EXPERTS + e]
        dst = segdst_ref[step * N_EXPERTS + e]
        bit = top_bit
        while bit >= 1:
            done = c & (~(2 * bit - 1))

            @pl.when((c & bit) != 0)
            def _(bit=bit, done=done):
                loc = local.at[slot, pl.ds((off + done) * ROW_TILES, bit * ROW_TILES), :]
                rem = remote_rows.at[pl.ds((dst + done) * ROW_TILES, bit * ROW_TILES), :]
                cp = (pltpu.make_async_copy(loc, rem, sem.at[slot]) if to_remote
                      else pltpu.make_async_copy(rem, loc, sem.at[slot]))
                cp.start()
            bit //= 2
        return off + c

    off = 0
    for e in range(N_EXPERTS):
        off = body(e, off)


def _dispatch_kernel(segdst_ref, cnt_ref, big_ref, paddst_ref, padcnt_ref, nused_ref, u2_ref, sel_ref, xs_ref,
                     buf, zbuf, sems, zsem):
    i = pl.program_id(0)
    last = pl.num_programs(0) - 1
    slot = i % 2
    TM = MOE_TILE
    NS = TOP_K * TM
    rank1, esel, off, _ = _tile_sort(sel_ref[...])
    slot_id = lax.broadcasted_iota(jnp.int32, (NS, 1), 0).astype(F32)
    r_s = slot_id - jnp.sum(esel.astype(F32) * off, axis=1, keepdims=True)
    perm = (_dot_nt(esel, rank1) == r_s + 1.0).astype(BF16)

    @pl.when(i >= 2)
    def _():
        _segment_wait(slot, buf, xs_ref, sems, True)

    u2 = u2_ref[...]
    for c in range(NS // TM):
        rows = _dot(perm[c * TM:(c + 1) * TM], u2)
        for j in range(ROW_TILES):
            buf[slot, pl.ds(c * TM * ROW_TILES + j, TM, stride=ROW_TILES), :] = rows[:, j * LANES:(j + 1) * LANES]
    _segment_dmas(i, slot, segdst_ref, cnt_ref, big_ref, buf, xs_ref, sems, True)

    @pl.when(i == last)
    def _():
        @pl.when(i >= 1)
        def _():
            _segment_wait(1 - slot, buf, xs_ref, sems, True)
        _segment_wait(slot, buf, xs_ref, sems, True)
        zbuf[...] = jnp.zeros_like(zbuf)
        half = EXPERT_BLOCK // 2 * ROW_TILES
        n_blocks = xs_ref.shape[0] // (EXPERT_BLOCK * ROW_TILES)
        for wait in (False, True):
            def unused(hb, carry, wait=wait):
                cp = pltpu.make_async_copy(zbuf, xs_ref.at[pl.ds(hb * half, half), :], zsem.at[0])
                cp.wait() if wait else cp.start()
                return carry
            lax.fori_loop(2 * nused_ref[0], 2 * n_blocks, unused, 0)


            def body(e, carry, wait=wait):
                c = padcnt_ref[e]
                dst = paddst_ref[e]
                bit = EXPERT_BLOCK // 2
                while bit >= 1:
                    done = c & (~(2 * bit - 1))

                    @pl.when((c & bit) != 0)
                    def _(bit=bit, done=done):
                        cp = pltpu.make_async_copy(
                            zbuf.at[pl.ds(0, bit * ROW_TILES), :],
                            xs_ref.at[pl.ds((dst + done) * ROW_TILES, bit * ROW_TILES), :], zsem.at[0])
                        cp.wait() if wait else cp.start()
                    bit //= 2
                return carry
            lax.fori_loop(0, N_EXPERTS, body, 0)


def _dispatch(u2, sel, segdst, cnt, big, paddst, padcnt, n_used, n_rows):
    T, D = u2.shape
    TM = MOE_TILE
    NS = TOP_K * TM
    return pl.pallas_call(
        _dispatch_kernel,
        grid_spec=pltpu.PrefetchScalarGridSpec(
            num_scalar_prefetch=6,
            grid=(T // TM,),
            in_specs=[pl.BlockSpec((TM, D), lambda i, *_: (i, 0)),
                      pl.BlockSpec((TM, LANES), lambda i, *_: (i, 0))],
            out_specs=pl.BlockSpec(memory_space=pl.ANY),
            scratch_shapes=[pltpu.VMEM((2, NS * ROW_TILES, LANES), F32),
                            pltpu.VMEM((EXPERT_BLOCK // 2 * ROW_TILES, LANES), F32),
                            pltpu.SemaphoreType.DMA((2,)), pltpu.SemaphoreType.DMA((1,))],
        ),
        out_shape=jax.ShapeDtypeStruct((n_rows * ROW_TILES, LANES), F32),
        compiler_params=pltpu.CompilerParams(
            dimension_semantics=("arbitrary",), vmem_limit_bytes=VMEM_LIMIT),
        name="dispatch",
    )(segdst, cnt, big, paddst, padcnt, n_used, u2, sel)


def _expert_kernel(bexp_ref, nused_ref, epos_ref, enext_ref, xs_ref, w1_hbm, b1_ref, w2_hbm, b2_ref, ys_ref,
                   w1f, w2f, w1b, w2b, wsem):
    b = pl.program_id(0)
    BLK = EXPERT_BLOCK
    used = b < nused_ref[0]

    def weight_copies(e, slot):
        return (pltpu.make_async_copy(w1_hbm.at[e], w1f.at[slot], wsem.at[0, slot]),
                pltpu.make_async_copy(w2_hbm.at[e], w2f.at[slot], wsem.at[1, slot]))

    @pl.when(used)
    def _():
        e = bexp_ref[b]
        prev = bexp_ref[jnp.maximum(b - 1, 0)]
        slot = epos_ref[b] % 2

        @pl.when(b == 0)
        def _():
            for cp in weight_copies(e, slot):
                cp.start()

        @pl.when((b == 0) | (e != prev))
        def _():
            nxt = enext_ref[b]

            @pl.when(nxt >= 0)
            def _():
                for cp in weight_copies(nxt, 1 - slot):
                    cp.start()

            for cp in weight_copies(e, slot):
                cp.wait()
            rows = 128

            def cast(r, carry):
                sl = pl.ds(pl.multiple_of(r * rows, rows), rows)
                w1b[sl, :] = w1f[slot, sl, :].astype(BF16)
                w2b[sl, :] = w2f[slot, sl, :].astype(BF16)
                return carry
            lax.fori_loop(0, D_MODEL // rows, cast, 0)

        R = EXPERT_PASS_ROWS
        for rp in range(BLK // R):
            r0 = rp * R * ROW_TILES
            x = jnp.concatenate([xs_ref[pl.ds(r0 + j, R, stride=ROW_TILES), :] for j in range(ROW_TILES)],
                                axis=1).astype(BF16)
            h = _dot(x, w1b[...]) + b1_ref[0]
            glu = jnp.minimum(h[:, :D_FF], SWIGLU_LIMIT)
            lin = jnp.clip(h[:, D_FF:], -SWIGLU_LIMIT, SWIGLU_LIMIT)
            act = glu * jax.nn.sigmoid(SWIGLU_ALPHA * glu) * (lin + 1.0)
            y = _dot(act.astype(BF16), w2b[...]) + b2_ref[0]
            for j in range(ROW_TILES):
                ys_ref[pl.ds(r0 + j, R, stride=ROW_TILES), :] = y[:, j * LANES:(j + 1) * LANES]

    @pl.when(jnp.logical_not(used))
    def _():
        ys_ref[...] = jnp.zeros_like(ys_ref)


def _experts(xs, block_exp, n_used, exp_pos, exp_next, w1, b1, w2, b2):
    BLK = EXPERT_BLOCK
    NB = xs.shape[0] // (BLK * ROW_TILES)
    blk = lambda b, nused: jnp.minimum(b, nused[0] - 1)
    return pl.pallas_call(
        _expert_kernel,
        grid_spec=pltpu.PrefetchScalarGridSpec(
            num_scalar_prefetch=4,
            grid=(NB,),
            in_specs=[
                pl.BlockSpec((BLK * ROW_TILES, LANES), lambda b, bexp, nused, *_: (blk(b, nused), 0)),
                pl.BlockSpec(memory_space=pl.ANY),
                pl.BlockSpec((1, 1, 2 * D_FF), lambda b, bexp, nused, *_: (bexp[blk(b, nused)], 0, 0)),
                pl.BlockSpec(memory_space=pl.ANY),
                pl.BlockSpec((1, 1, D_MODEL), lambda b, bexp, nused, *_: (bexp[blk(b, nused)], 0, 0)),
            ],
            out_specs=pl.BlockSpec((BLK * ROW_TILES, LANES), lambda b, *_: (b, 0)),
            scratch_shapes=[pltpu.VMEM((2, D_MODEL, 2 * D_FF), F32), pltpu.VMEM((2, D_FF, D_MODEL), F32),
                            pltpu.VMEM((D_MODEL, 2 * D_FF), BF16), pltpu.VMEM((D_FF, D_MODEL), BF16),
                            pltpu.SemaphoreType.DMA((2, 2))],
        ),
        out_shape=jax.ShapeDtypeStruct(xs.shape, F32),
        compiler_params=pltpu.CompilerParams(
            dimension_semantics=("arbitrary",), vmem_limit_bytes=VMEM_LIMIT_EXPERTS),
        name="experts",
    )(block_exp, n_used, exp_pos, exp_next, xs, w1, b1[:, None, :], w2, b2[:, None, :])


def _combine_kernel(segdst_ref, cnt_ref, big_ref, ys_ref, sel_ref, h1_ref, g_ref, out_ref, buf, sems):
    i = pl.program_id(0)
    n = pl.num_programs(0)
    slot = i % 2
    TM = MOE_TILE
    NS = TOP_K * TM

    @pl.when(i == 0)
    def _():
        _segment_dmas(i, slot, segdst_ref, cnt_ref, big_ref, buf, ys_ref, sems, False)

    @pl.when(i + 1 < n)
    def _():
        _segment_dmas(i + 1, 1 - slot, segdst_ref, cnt_ref, big_ref, buf, ys_ref, sems, False)

    sel = sel_ref[...]
    rank1, esel, off, _ = _tile_sort(sel)
    gate = jnp.maximum(sel, 0.0).astype(BF16)
    o3 = _split3(jnp.broadcast_to(off, (SUBLANES, LANES)))
    off_s = (_dot_nt(o3[0], esel) + _dot_nt(o3[1], esel) + _dot_nt(o3[2], esel))[0:1, :]
    r_s = lax.broadcasted_iota(jnp.int32, (1, NS), 1).astype(F32) - off_s
    hit = _dot_nt(rank1, esel) == r_s + 1.0
    unperm = jnp.where(hit, _dot_nt(gate, esel), 0.0).astype(BF16)

    _segment_wait(slot, buf, ys_ref, sems, False)
    y = jnp.concatenate([buf[slot, pl.ds(j, NS, stride=ROW_TILES), :] for j in range(ROW_TILES)],
                        axis=1).astype(BF16)
    h2 = h1_ref[...] + _dot(unperm, y)
    out_ref[...] = _rms(h2, g_ref[...])


def _combine(ys, sel, h1, g, segdst, cnt, big):
    T, D = h1.shape
    TM = MOE_TILE
    NS = TOP_K * TM
    return pl.pallas_call(
        _combine_kernel,
        grid_spec=pltpu.PrefetchScalarGridSpec(
            num_scalar_prefetch=3,
            grid=(T // TM,),
            in_specs=[pl.BlockSpec(memory_space=pl.ANY),
                      pl.BlockSpec((TM, LANES), lambda i, *_: (i, 0)),
                      pl.BlockSpec((TM, D), lambda i, *_: (i, 0)),
                      pl.BlockSpec((1, D), lambda i, *_: (0, 0))],
            out_specs=pl.BlockSpec((TM, D), lambda i, *_: (i, 0)),
            scratch_shapes=[pltpu.VMEM((2, NS * ROW_TILES, LANES), F32), pltpu.SemaphoreType.DMA((2,))],
        ),
        out_shape=jax.ShapeDtypeStruct((T, D), F32),
        compiler_params=pltpu.CompilerParams(
            dimension_semantics=("arbitrary",), vmem_limit_bytes=VMEM_LIMIT),
        name="combine",
    )(segdst, cnt, big, ys, sel, h1, g)


def _routing_tables(cnt_tiles):
    BLK = EXPERT_BLOCK
    nT = cnt_tiles.shape[0]
    A = nT * MOE_TILE * TOP_K
    NB = A // BLK + N_EXPERTS
    total = jnp.sum(cnt_tiles, axis=0)
    padded = (total + BLK - 1) // BLK * BLK
    pad_ends = jnp.cumsum(padded)
    pad_starts = pad_ends - padded
    before = jnp.cumsum(cnt_tiles, axis=0) - cnt_tiles
    segdst = (pad_starts[None, :] + before).reshape(-1).astype(jnp.int32)
    block_start = jnp.arange(NB, dtype=jnp.int32) * BLK
    block_exp = jnp.minimum(jnp.sum(pad_ends[None, :] <= block_start[:, None], axis=1), N_EXPERTS - 1).astype(jnp.int32)
    n_used = (pad_ends[-1] // BLK).astype(jnp.int32).reshape(1)
    paddst = (pad_starts + total).astype(jnp.int32)
    padcnt = (padded - total).astype(jnp.int32)
    big = jnp.any(cnt_tiles >= SMALL_RUN, axis=1).astype(jnp.int32)
    has_rows = total > 0
    ids = jnp.arange(N_EXPERTS, dtype=jnp.int32)
    pos = (jnp.cumsum(has_rows) - has_rows).astype(jnp.int32)
    later = jnp.where(has_rows[None, :] & (ids[None, :] > ids[:, None]), ids[None, :], N_EXPERTS)
    nxt = jnp.min(later, axis=1)
    nxt = jnp.where(nxt < N_EXPERTS, nxt, -1).astype(jnp.int32)
    return (segdst, cnt_tiles.reshape(-1).astype(jnp.int32), big, paddst, padcnt, block_exp, n_used,
            pos[block_exp], nxt[block_exp], NB * BLK)


def _rotary_tables(S):
    half = RET_HEAD_DIM // 2
    f32 = np.float32
    inv_freq = np.power(f32(ROPE_BASE), -np.arange(half, dtype=f32) / f32(half)).astype(f32)
    ang = (np.arange(S, dtype=f32)[:, None] * inv_freq[None, :]).astype(f32)
    cos, sin = np.cos(ang).astype(f32), np.sin(ang).astype(f32)
    return (jnp.asarray(np.concatenate([cos, cos], axis=-1), F32),
            jnp.asarray(np.concatenate([-sin, sin], axis=-1), F32))


def _layer(h, norm_mix_g, w_in, b_forget, w_out, norm_ffn_g, w_router, b_router,
           w_exp_in, b_exp_in, w_exp_out, b_exp_out, final_g):
    B, S, D = h.shape
    R, Fw = RET_WIDTH, FOX_WIDTH
    cos, sin = _rotary_tables(S)
    wb = w_in.astype(BF16)
    wr = wb[:, :4 * R]
    wfq, wfk, wfv = (wb[:, 4 * R + i * Fw:4 * R + (i + 1) * Fw] for i in range(3))
    wzt = jnp.zeros((16, D), BF16).at[:FOX_HEADS, :].set(wb[:, 4 * R + 3 * Fw:].T)
    o_ret, kaug, qaug, fvt = _in_proj(
        h, norm_mix_g[None, :], cos, sin, wr, wfk, wfq.T, wfv.T, wzt, b_forget)
    o_fox = _fox_attn(qaug, kaug, fvt)

    T = B * S
    wo = w_out.astype(BF16)
    wrt = jnp.zeros((LANES, D), F32).at[:N_EXPERTS, :].set(w_router.T)
    wrh = wrt.astype(BF16)
    wrl = (wrt - wrh.astype(F32)).astype(BF16)
    br = jnp.zeros((LANES, 1), F32).at[:N_EXPERTS, 0].set(b_router)
    h1, u2, sel, cnt = _out_router(h.reshape(T, D), o_ret.reshape(T, R), o_fox.reshape(T, Fw),
                                   wo[:R], wo[R:], norm_ffn_g[None, :], wrh, wrl, br)
    cnt_tiles = cnt[:, 0, :N_EXPERTS].astype(jnp.int32)
    segdst, cnt_flat, big, paddst, padcnt, block_exp, n_used, exp_pos, exp_next, n_rows = _routing_tables(cnt_tiles)
    xs = _dispatch(u2, sel, segdst, cnt_flat, big, paddst, padcnt, n_used, n_rows)
    ys = _experts(xs, block_exp, n_used, exp_pos, exp_next, w_exp_in, b_exp_in, w_exp_out, b_exp_out)
    out = _combine(ys, sel, h1, final_g[None, :], segdst, cnt_flat, big)
    return out.reshape(B, S, D)


def kernel(x, norm_mix_g, w_in, b_forget, w_out, norm_ffn_g, w_router, b_router,
           w_exp_in, b_exp_in, w_exp_out, b_exp_out, norm_final_g):
    depth = w_in.shape[0]
    assert depth == 1, "the fused final RMSNorm assumes a single layer"
    return _layer(x, norm_mix_g[0], w_in[0], b_forget[0], w_out[0], norm_ffn_g[0], w_router[0], b_router[0],
                  w_exp_in[0], b_exp_in[0], w_exp_out[0], b_exp_out[0], norm_final_g)
```

```python
import functools

import numpy as np
import jax
import jax.numpy as jnp
from jax import lax
from jax.experimental import pallas as pl
from jax.experimental.pallas import tpu as pltpu

F32 = jnp.float32
BF16 = jnp.bfloat16

D_MODEL = 1024
RET_HEADS, RET_HEAD_DIM = 4, 128
RET_WIDTH = RET_HEADS * RET_HEAD_DIM
FOX_HEADS, FOX_HEAD_DIM = 8, 64
FOX_WIDTH = FOX_HEADS * FOX_HEAD_DIM
CHUNK = 64
ROPE_BASE = 10000.0
N_EXPERTS = 32
TOP_K = 4
D_FF = D_MODEL
SWIGLU_ALPHA = 1.702
SWIGLU_LIMIT = 7.0
RMS_EPS = 1e-5
GN_EPS = 1e-5

LANES = 128
SUBLANES = 8
ROW_TILES = D_MODEL // LANES
V7X_VMEM_BYTES = 64 * 1024 * 1024
VMEM_LIMIT = V7X_VMEM_BYTES * 3 // 4
VMEM_LIMIT_EXPERTS = V7X_VMEM_BYTES * 7 // 8

PROJ_TILE = 512
RET_BLOCK = 256
FOX_TQ = 512
FOX_TK = PROJ_TILE
AUG = 128
V_AUG = 80
LOG2E = 1.4426950408889634
MOE_TILE = 256
ROUTER_TILES = 4
SMALL_RUN = 64
EXPERT_BLOCK = 512
EXPERT_PASS_ROWS = 256

NT_DIMS = (((1,), (1,)), ((), ()))


def _split3(a):
    hi = a.astype(BF16)
    r1 = a - hi.astype(F32)
    mid = r1.astype(BF16)
    lo = (r1 - mid.astype(F32)).astype(BF16)
    return hi, mid, lo


def _dot(a, b):
    return jnp.dot(a, b, preferred_element_type=F32)


def _dot_nt(a, b):
    return lax.dot_general(a, b, NT_DIMS, preferred_element_type=F32)


def _rms(x, g):
    return x * lax.rsqrt(jnp.mean(x * x, axis=-1, keepdims=True) + RMS_EPS) * g


def _in_proj_kernel(x_ref, g_ref, cos_ref, sin_ref, wr_ref, wfk_ref, wfqt_ref, wfvt_ref, wzt_ref, bcol_ref,
                    selkf_ref, constk_ref, selqf_ref, constq_ref, dec_ref, qw_ref, kw_ref, cd_ref,
                    oret_ref, kaug_ref, qaug_ref, fvt_ref, ccol, state):
    TM = x_ref.shape[1]
    d, H = FOX_HEAD_DIM, FOX_HEADS

    @pl.when(pl.program_id(1) == 0)
    def _():
        ccol[...] = jnp.zeros_like(ccol)
        state[...] = jnp.zeros_like(state)

    u = _rms(x_ref[0], g_ref[...]).astype(BF16)
    r = _dot(u, wr_ref[...])
    cos, sin = cos_ref[...], sin_ref[...]
    k_scale = RET_HEAD_DIM ** -0.5
    L = RET_BLOCK
    for h in range(RET_HEADS):
        hs = slice(h * RET_HEAD_DIM, (h + 1) * RET_HEAD_DIM)
        q = r[:, hs]
        k = r[:, RET_WIDTH + h * RET_HEAD_DIM:RET_WIDTH + (h + 1) * RET_HEAD_DIM]
        qh = (q * cos + pltpu.roll(q, RET_HEAD_DIM // 2, 1) * sin).astype(BF16)
        kh = ((k * cos + pltpu.roll(k, RET_HEAD_DIM // 2, 1) * sin) * k_scale).astype(BF16)
        vh = r[:, 2 * RET_WIDTH + h * RET_HEAD_DIM:2 * RET_WIDTH + (h + 1) * RET_HEAD_DIM].astype(BF16)
        gh = r[:, 3 * RET_WIDTH + h * RET_HEAD_DIM:3 * RET_WIDTH + (h + 1) * RET_HEAD_DIM].astype(BF16)
        for rs in (slice(b0, b0 + L) for b0 in range(0, TM, L)):
            qb, kb, vb = qh[rs], kh[rs], vh[rs]
            scores = (_dot_nt(qb, kb) * dec_ref[h]).astype(BF16)
            st = state[h]
            o = _dot(scores, vb) + _dot((qb.astype(F32) * qw_ref[h]).astype(BF16), st.astype(BF16))
            kk = kb.astype(F32) * kw_ref[h]
            state[h] = st * cd_ref[h, 0:1, :] + _dot(kk.T.astype(BF16), vb)
            mu = jnp.mean(o, axis=-1, keepdims=True)
            oc = o - mu
            var = jnp.mean(oc * oc, axis=-1, keepdims=True)
            oret_ref[0, rs, hs] = (oc * lax.rsqrt(var + GN_EPS) * jax.nn.silu(gh[rs].astype(F32))).astype(BF16)
    fvt_ref[0, 0] = _dot_nt(wfvt_ref[...], u).astype(BF16)
    fk = _dot(u, wfk_ref[...])
    fqt = (_dot_nt(wfqt_ref[...], u) * (d ** -0.5 * LOG2E)).astype(BF16)
    zt = _dot_nt(wzt_ref[...], u)


    row = lax.broadcasted_iota(jnp.int32, (16, TM), 0)
    lft = jnp.where(row < H, jax.nn.log_sigmoid(zt + bcol_ref[...]), 0.0)
    utri = (lax.broadcasted_iota(jnp.int32, (TM, TM), 0) <= lax.broadcasted_iota(jnp.int32, (TM, TM), 1)).astype(BF16)
    t3 = _split3(lft)
    f_col = _dot(t3[0], utri) + _dot(t3[1], utri) + _dot(t3[2], utri) + ccol[:, 0:1]
    ccol[...] = jnp.broadcast_to(f_col[:, TM - 1:TM], ccol.shape)

    pieces_t = jnp.concatenate(_split3(f_col * LOG2E), axis=0)
    for h in range(H):
        extra = _dot(selqf_ref[h], pieces_t) + constq_ref[...]
        qaug_ref[0, h, 0] = jnp.concatenate([fqt[h * d:(h + 1) * d, :], extra.astype(BF16)], axis=0)

    f_row = jnp.concatenate([f_col, jnp.zeros((LANES - 16, TM), F32)], axis=0).T
    n3 = _split3(f_row * -LOG2E)
    pieces = (n3[0].astype(F32) + pltpu.roll(n3[1].astype(F32), H, 1)
              + pltpu.roll(n3[2].astype(F32), 2 * H, 1)).astype(BF16)
    lane = lax.broadcasted_iota(jnp.int32, (TM, LANES), 1)
    for g in range(H // 2):
        bias = _dot(pieces, selkf_ref[g])
        kg = fk[:, g * 2 * d:(g + 1) * 2 * d]
        for o in range(2):
            kh = kg if o == 0 else pltpu.roll(kg, d, 1)
            extra = bias[:, o * AUG:(o + 1) * AUG] + constk_ref[...]
            kaug_ref[0, 2 * g + o] = jnp.where(lane < d, kh, extra).astype(BF16)


def _in_proj(x, g, cos, sin, wr, wfk, wfqt, wfvt, wzt, b_forget):
    B, S, D = x.shape
    TM = PROJ_TILE
    ns = S // TM
    selkf, constk, selqf, constq = _fox_prep_constants()
    bcol = jnp.zeros((16, 1), F32).at[:FOX_HEADS, 0].set(b_forget)
    consts = (wr, wfk, wfqt, wfvt, wzt, bcol, selkf, constk, selqf, constq) + _retention_tables()
    const = lambda a: pl.BlockSpec(a.shape, lambda b, s: (0,) * a.ndim)
    tok = lambda w: pl.BlockSpec((1, TM, w), lambda b, s: (b, s, 0))
    out_shape = (
        jax.ShapeDtypeStruct((B, S, RET_WIDTH), BF16),
        jax.ShapeDtypeStruct((B, FOX_HEADS, S, AUG), BF16),
        jax.ShapeDtypeStruct((B, FOX_HEADS, ns, AUG, TM), BF16),
        jax.ShapeDtypeStruct((B, ns, FOX_WIDTH, TM), BF16),
    )
    return pl.pallas_call(
        _in_proj_kernel,
        grid=(B, ns),
        in_specs=[
            pl.BlockSpec((1, TM, D), lambda b, s: (b, s, 0)),
            pl.BlockSpec((1, D), lambda b, s: (0, 0)),
            pl.BlockSpec((TM, RET_HEAD_DIM), lambda b, s: (s, 0)),
            pl.BlockSpec((TM, RET_HEAD_DIM), lambda b, s: (s, 0)),
        ] + [const(a) for a in consts],
        out_specs=(
            tok(RET_WIDTH),
            pl.BlockSpec((1, FOX_HEADS, TM, AUG), lambda b, s: (b, 0, s, 0)),
            pl.BlockSpec((1, FOX_HEADS, 1, AUG, TM), lambda b, s: (b, 0, s, 0, 0)),
            pl.BlockSpec((1, 1, FOX_WIDTH, TM), lambda b, s: (b, s, 0, 0)),
        ),
        out_shape=out_shape,
        scratch_shapes=[pltpu.VMEM((16, LANES), F32),
                        pltpu.VMEM((RET_HEADS, RET_HEAD_DIM, RET_HEAD_DIM), F32)],
        compiler_params=pltpu.CompilerParams(
            dimension_semantics=("arbitrary", "arbitrary"), vmem_limit_bytes=VMEM_LIMIT),
        name="in_proj",
    )(x, g, cos, sin, *consts)


def _fox_prep_constants():
    d, H = FOX_HEAD_DIM, FOX_HEADS
    selkf = np.zeros((H // 2, LANES, 2 * AUG), np.float32)
    constk = np.zeros((1, AUG), np.float32)
    selqf = np.zeros((H, d, 48), np.float32)
    constq = np.zeros((d, 1), np.float32)
    for p in range(3):
        constk[0, d + p] = 1.0
        constq[3 + p, 0] = 1.0
        for h in range(H):
            selkf[h // 2, p * H + h, (h % 2) * AUG + d + 3 + p] = 1.0
            selqf[h, p, p * 16 + h] = 1.0
    return jnp.asarray(selkf, BF16), jnp.asarray(constk, F32), jnp.asarray(selqf, BF16), jnp.asarray(constq, F32)


def _retention_tables():
    L = RET_BLOCK
    f32 = np.float32
    log_gamma = np.log1p(-np.exp2(-5.0 - np.arange(RET_HEADS, dtype=f32))).astype(f32)
    p = np.arange(L, dtype=f32)
    dist = np.abs(p[:, None] - p[None, :])
    chunk = np.arange(L) // CHUNK
    allowed = (chunk[None, :] <= chunk[:, None]).astype(f32)
    dec = np.exp(log_gamma[:, None, None] * dist).astype(f32) * allowed
    lanes = lambda a: np.broadcast_to(a.astype(f32)[:, :, None], (RET_HEADS, L, RET_HEAD_DIM))
    qw = lanes(np.exp(log_gamma[:, None] * (p[None, :] + f32(1.0))))
    kw = lanes(np.exp(log_gamma[:, None] * (f32(L - 1.0) - p[None, :])))
    cd = np.broadcast_to(np.exp(log_gamma * f32(L)).astype(f32)[:, None, None], (RET_HEADS, SUBLANES, RET_HEAD_DIM))
    return tuple(jnp.asarray(a, F32) for a in (dec, qw, kw, cd))


def _fox_attn_kernel(q_ref, k_ref, v_ref, o_ref, s_a, s_b, s_c, cm_a, cm_b, cm_c, m_ref, acc_ref):
    T = FOX_TQ
    d = FOX_HEAD_DIM
    nq = q_ref.shape[2]
    ones_rows = (lax.broadcasted_iota(jnp.int32, (V_AUG - d, T), 0) == 0).astype(BF16)

    def scores(qi, j, s_ref, cm_ref):
        for hh in range(2):
            kj = k_ref[0, hh, pl.ds(pl.multiple_of(j * T, T), T), :]
            st = _dot(kj, q_ref[0, hh, qi])
            s_ref[hh] = st
            cm_ref[hh] = jnp.max(st, axis=0, keepdims=True)

    def consume(j, s_ref, cm_ref, masked):
        for hh in range(2):
            st = s_ref[hh]
            if masked:
                key = lax.broadcasted_iota(jnp.int32, (T, T), 0)
                qry = lax.broadcasted_iota(jnp.int32, (T, T), 1)
                st = jnp.where(key <= qry, st, -jnp.inf)
                cm = jnp.max(st, axis=0, keepdims=True)
            else:
                cm = cm_ref[hh]
            m = m_ref[hh]
            m_new = jnp.maximum(m, cm)
            p = jnp.exp2(st - m_new).astype(BF16)
            vj = jnp.concatenate([v_ref[0, j, hh * d:(hh + 1) * d, :], ones_rows], axis=0)
            acc_ref[hh] = jnp.exp2(m - m_new) * acc_ref[hh] + _dot(vj, p)
            m_ref[hh] = m_new

    def reset():
        m_ref[...] = jnp.full(m_ref.shape, -jnp.inf, F32)
        acc_ref[...] = jnp.zeros(acc_ref.shape, F32)

    def prefetch_next(qi):
        @pl.when(qi + 1 < nq)
        def _():
            scores(qi + 1, 0, s_c, cm_c)

    def finish(qi):
        outs = [acc_ref[hh, 0:d, :] / acc_ref[hh, d:d + 1, :] for hh in range(2)]
        o_ref[0, pl.ds(pl.multiple_of(qi * T, T), T), :] = jnp.concatenate(outs, axis=0).T.astype(BF16)

    reset()
    scores(0, 0, s_a, cm_a)
    prefetch_next(0)
    consume(0, s_a, cm_a, True)
    finish(0)

    def query_tile(qi, carry):
        reset()
        scores(qi, 1, s_a, cm_a)
        consume(0, s_c, cm_c, False)

        def pair(j):
            scores(qi, j + 1, s_b, cm_b)
            consume(j, s_a, cm_a, False)
            scores(qi, j + 2, s_a, cm_a)
            consume(j + 1, s_b, cm_b, False)

        def two_pairs(jj, c):
            pair(1 + 4 * jj)
            pair(3 + 4 * jj)
            return c

        def one_pair(jj, c):
            pair(1 + 4 * (n_pairs // 2) + 2 * jj)
            return c

        n_pairs = (qi - 1) // 2
        lax.fori_loop(0, n_pairs // 2, two_pairs, 0)
        lax.fori_loop(0, n_pairs % 2, one_pair, 0)

        @pl.when(qi % 2 == 1)
        def _():
            prefetch_next(qi)
            consume(qi, s_a, cm_a, True)

        @pl.when(qi % 2 == 0)
        def _():
            scores(qi, qi, s_b, cm_b)
            consume(qi - 1, s_a, cm_a, False)
            prefetch_next(qi)
            consume(qi, s_b, cm_b, True)

        finish(qi)
        return carry

    lax.fori_loop(1, nq, query_tile, 0)


def _fox_attn(qaug, kaug, fvt):
    B, H, S, _ = kaug.shape
    nk = S // FOX_TK
    nq = S // FOX_TQ
    score_buf = pltpu.VMEM((2, FOX_TK, FOX_TQ), F32)
    col_max = pltpu.VMEM((2, 1, FOX_TQ), F32)
    return pl.pallas_call(
        _fox_attn_kernel,
        grid=(B, H // 2),
        in_specs=[
            pl.BlockSpec((1, 2, nq, AUG, FOX_TQ), lambda b, p: (b, p, 0, 0, 0)),
            pl.BlockSpec((1, 2, S, AUG), lambda b, p: (b, p, 0, 0)),
            pl.BlockSpec((1, nk, 2 * FOX_HEAD_DIM, FOX_TK), lambda b, p: (b, 0, p, 0)),
        ],
        out_specs=pl.BlockSpec((1, S, 2 * FOX_HEAD_DIM), lambda b, p: (b, 0, p)),
        out_shape=jax.ShapeDtypeStruct((B, S, FOX_WIDTH), BF16),
        scratch_shapes=[
            score_buf, score_buf, score_buf, col_max, col_max, col_max,
            pltpu.VMEM((2, 1, FOX_TQ), F32), pltpu.VMEM((2, V_AUG, FOX_TQ), F32),
        ],
        compiler_params=pltpu.CompilerParams(
            dimension_semantics=("arbitrary",) * 2, vmem_limit_bytes=VMEM_LIMIT),
        name="fox_attn",
    )(qaug, kaug, fvt)


def _out_router_kernel(x_ref, oret_ref, ofox_ref, wor_ref, wof_ref, g_ref, wrh_ref, wrl_ref, br_ref,
                       h1_ref, u2_ref, sel_ref, cnt_ref):
    TM = MOE_TILE
    rows = lambda t: slice(t * TM, (t + 1) * TM)

    def out_proj(t):
        rs = rows(t)
        h1 = x_ref[rs] + _dot(oret_ref[rs], wor_ref[...]) + _dot(ofox_ref[rs], wof_ref[...])
        h1_ref[rs] = h1
        return h1

    def router_logits(t, h1):
        u2 = _rms(h1, g_ref[...])
        uh = u2.astype(BF16)
        u2_ref[rows(t)] = uh
        ul = (u2 - uh.astype(F32)).astype(BF16)
        return (_dot_nt(wrh_ref[...], uh) + _dot_nt(wrh_ref[...], ul) + _dot_nt(wrl_ref[...], uh)
                + br_ref[...])

    def top_k(t, logits):
        rs = rows(t)
        row = lax.broadcasted_iota(jnp.int32, (LANES, TM), 0).astype(F32)
        l = jnp.where(row < N_EXPERTS, logits, -jnp.inf)
        picks, vals = [], []
        for _ in range(TOP_K):
            m = jnp.max(l, axis=0, keepdims=True)
            idx = jnp.min(jnp.where(l == m, row, float(LANES)), axis=0, keepdims=True)
            pick = row == idx
            picks.append(pick)
            vals.append(m)
            l = jnp.where(pick, -jnp.inf, l)
        exps = [jnp.exp(v - vals[0]) for v in vals]
        den = exps[0] + exps[1] + exps[2] + exps[3]
        sel_t = jnp.full((LANES, TM), -1.0, F32)
        for pick, e in zip(picks, exps):
            sel_t = jnp.where(pick, e / den, sel_t)
        sel = sel_t.T
        sel_ref[rs] = sel
        cnt = jnp.sum((sel >= 0.0).astype(F32), axis=0, keepdims=True)
        cnt_ref[t] = jnp.broadcast_to(cnt, (SUBLANES, LANES))

    n = ROUTER_TILES
    h1s, lgs = {0: out_proj(0)}, {}
    for t in range(1, n + 2):
        if t < n:
            h1s[t] = out_proj(t)
        if 1 <= t <= n:
            lgs[t - 1] = router_logits(t - 1, h1s.pop(t - 1))
        if t >= 2:
            top_k(t - 2, lgs.pop(t - 2))


def _out_router(x2, o_ret, o_fox, wor, wof, g, wrh, wrl, br):
    T, D = x2.shape
    TM = MOE_TILE * ROUTER_TILES
    nT = T // MOE_TILE
    const = lambda a: pl.BlockSpec(a.shape, lambda i: (0,) * a.ndim)
    tok = lambda w: pl.BlockSpec((TM, w), lambda i: (i, 0))
    return pl.pallas_call(
        _out_router_kernel,
        grid=(T // TM,),
        in_specs=[tok(D), tok(RET_WIDTH), tok(FOX_WIDTH), const(wor), const(wof), const(g),
                  const(wrh), const(wrl), const(br)],
        out_specs=(tok(D), tok(D), tok(LANES),
                   pl.BlockSpec((ROUTER_TILES, SUBLANES, LANES), lambda i: (i, 0, 0))),
        out_shape=(
            jax.ShapeDtypeStruct((T, D), F32),
            jax.ShapeDtypeStruct((T, D), BF16),
            jax.ShapeDtypeStruct((T, LANES), F32),
            jax.ShapeDtypeStruct((nT, SUBLANES, LANES), F32),
        ),
        compiler_params=pltpu.CompilerParams(dimension_semantics=("arbitrary",)),
        name="out_router",
    )(x2, o_ret, o_fox, wor, wof, g, wrh, wrl, br)


def _tile_sort(sel):
    TM = sel.shape[0]
    NS = TOP_K * TM
    maskf = (sel >= 0.0).astype(F32)
    mask = maskf.astype(BF16)
    ri = lax.broadcasted_iota(jnp.int32, (TM, TM), 0)
    ci = lax.broadcasted_iota(jnp.int32, (TM, TM), 1)
    rank1 = maskf * _dot((ri >= ci).astype(BF16), mask)
    cnt = jnp.sum(maskf, axis=0, keepdims=True)
    ei = lax.broadcasted_iota(jnp.int32, (LANES, LANES), 0)
    ej = lax.broadcasted_iota(jnp.int32, (LANES, LANES), 1)
    cnt8 = jnp.broadcast_to(cnt, (SUBLANES, LANES)).astype(BF16)
    off = _dot(cnt8, (ei < ej).astype(BF16))[0:1, :]
    slot = lax.broadcasted_iota(jnp.int32, (NS, LANES), 0).astype(F32)
    esel = ((slot >= off) & (slot < off + cnt)).astype(BF16)
    return rank1.astype(BF16), esel, off, cnt


def _segment_wait(slot, local, remote_rows, sem, to_remote):
    whole = local.at[slot]
    rem = remote_rows.at[pl.ds(0, whole.shape[0]), :]
    cp = (pltpu.make_async_copy(whole, rem, sem.at[slot]) if to_remote
          else pltpu.make_async_copy(rem, whole, sem.at[slot]))
    cp.wait()


def _segment_dmas(step, slot, segdst_ref, cnt_ref, big_ref, local, remote_rows, sem, to_remote):
    big = big_ref[step] != 0
    for cond, top_bit in ((big, MOE_TILE), (jnp.logical_not(big), SMALL_RUN // 2)):
        pl.when(cond)(functools.partial(
            _segment_dma_path, step, slot, segdst_ref, cnt_ref, local, remote_rows, sem, to_remote, top_bit))


def _segment_dma_path(step, slot, segdst_ref, cnt_ref, local, remote_rows, sem, to_remote, top_bit):
    def body(e, off):
        c = cnt_ref[step * N_EXPERTS + e]
        dst = segdst_ref[step * N_EXPERTS + e]
        bit = top_bit
        while bit >= 1:
            done = c & (~(2 * bit - 1))

            @pl.when((c & bit) != 0)
            def _(bit=bit, done=done):
                loc = local.at[slot, pl.ds((off + done) * ROW_TILES, bit * ROW_TILES), :]
                rem = remote_rows.at[pl.ds((dst + done) * ROW_TILES, bit * ROW_TILES), :]
                cp = (pltpu.make_async_copy(loc, rem, sem.at[slot]) if to_remote
                      else pltpu.make_async_copy(rem, loc, sem.at[slot]))
                cp.start()
            bit //= 2
        return off + c

    off = 0
    for e in range(N_EXPERTS):
        off = body(e, off)


def _dispatch_kernel(segdst_ref, cnt_ref, big_ref, paddst_ref, padcnt_ref, nused_ref, u2_ref, sel_ref, xs_ref,
                     buf, zbuf, sems, zsem):
    i = pl.program_id(0)
    last = pl.num_programs(0) - 1
    slot = i % 2
    TM = MOE_TILE
    NS = TOP_K * TM
    rank1, esel, off, _ = _tile_sort(sel_ref[...])
    slot_id = lax.broadcasted_iota(jnp.int32, (NS, 1), 0).astype(F32)
    r_s = slot_id - jnp.sum(esel.astype(F32) * off, axis=1, keepdims=True)
    perm = (_dot_nt(esel, rank1) == r_s + 1.0).astype(BF16)

    @pl.when(i >= 2)
    def _():
        _segment_wait(slot, buf, xs_ref, sems, True)

    u2 = u2_ref[...]
    for c in range(NS // TM):
        rows = _dot(perm[c * TM:(c + 1) * TM], u2)
        for j in range(ROW_TILES):
            buf[slot, pl.ds(c * TM * ROW_TILES + j, TM, stride=ROW_TILES), :] = rows[:, j * LANES:(j + 1) * LANES]
    _segment_dmas(i, slot, segdst_ref, cnt_ref, big_ref, buf, xs_ref, sems, True)

    @pl.when(i == last)
    def _():
        @pl.when(i >= 1)
        def _():
            _segment_wait(1 - slot, buf, xs_ref, sems, True)
        _segment_wait(slot, buf, xs_ref, sems, True)
        zbuf[...] = jnp.zeros_like(zbuf)
        half = EXPERT_BLOCK // 2 * ROW_TILES
        n_blocks = xs_ref.shape[0] // (EXPERT_BLOCK * ROW_TILES)
        for wait in (False, True):
            def unused(hb, carry, wait=wait):
                cp = pltpu.make_async_copy(zbuf, xs_ref.at[pl.ds(hb * half, half), :], zsem.at[0])
                cp.wait() if wait else cp.start()
                return carry
            lax.fori_loop(2 * nused_ref[0], 2 * n_blocks, unused, 0)


            def body(e, carry, wait=wait):
                c = padcnt_ref[e]
                dst = paddst_ref[e]
                bit = EXPERT_BLOCK // 2
                while bit >= 1:
                    done = c & (~(2 * bit - 1))

                    @pl.when((c & bit) != 0)
                    def _(bit=bit, done=done):
                        cp = pltpu.make_async_copy(
                            zbuf.at[pl.ds(0, bit * ROW_TILES), :],
                            xs_ref.at[pl.ds((dst + done) * ROW_TILES, bit * ROW_TILES), :], zsem.at[0])
                        cp.wait() if wait else cp.start()
                    bit //= 2
                return carry
            lax.fori_loop(0, N_EXPERTS, body, 0)


def _dispatch(u2, sel, segdst, cnt, big, paddst, padcnt, n_used, n_rows):
    T, D = u2.shape
    TM = MOE_TILE
    NS = TOP_K * TM
    return pl.pallas_call(
        _dispatch_kernel,
        grid_spec=pltpu.PrefetchScalarGridSpec(
            num_scalar_prefetch=6,
            grid=(T // TM,),
            in_specs=[pl.BlockSpec((TM, D), lambda i, *_: (i, 0)),
                      pl.BlockSpec((TM, LANES), lambda i, *_: (i, 0))],
            out_specs=pl.BlockSpec(memory_space=pl.ANY),
            scratch_shapes=[pltpu.VMEM((2, NS * ROW_TILES, LANES), F32),
                            pltpu.VMEM((EXPERT_BLOCK // 2 * ROW_TILES, LANES), F32),
                            pltpu.SemaphoreType.DMA((2,)), pltpu.SemaphoreType.DMA((1,))],
        ),
        out_shape=jax.ShapeDtypeStruct((n_rows * ROW_TILES, LANES), F32),
        compiler_params=pltpu.CompilerParams(
            dimension_semantics=("arbitrary",), vmem_limit_bytes=VMEM_LIMIT),
        name="dispatch",
    )(segdst, cnt, big, paddst, padcnt, n_used, u2, sel)


def _expert_kernel(bexp_ref, nused_ref, epos_ref, enext_ref, xs_ref, w1_hbm, b1_ref, w2_hbm, b2_ref, ys_ref,
                   w1f, w2f, w1b, w2b, wsem):
    b = pl.program_id(0)
    BLK = EXPERT_BLOCK
    used = b < nused_ref[0]

    def weight_copies(e, slot):
        return (pltpu.make_async_copy(w1_hbm.at[e], w1f.at[slot], wsem.at[0, slot]),
                pltpu.make_async_copy(w2_hbm.at[e], w2f.at[slot], wsem.at[1, slot]))

    @pl.when(used)
    def _():
        e = bexp_ref[b]
        prev = bexp_ref[jnp.maximum(b - 1, 0)]
        slot = epos_ref[b] % 2

        @pl.when(b == 0)
        def _():
            for cp in weight_copies(e, slot):
                cp.start()

        @pl.when((b == 0) | (e != prev))
        def _():
            nxt = enext_ref[b]

            @pl.when(nxt >= 0)
            def _():
                for cp in weight_copies(nxt, 1 - slot):
                    cp.start()

            for cp in weight_copies(e, slot):
                cp.wait()
            rows = 128

            def cast(r, carry):
                sl = pl.ds(pl.multiple_of(r * rows, rows), rows)
                w1b[sl, :] = w1f[slot, sl, :].astype(BF16)
                w2b[sl, :] = w2f[slot, sl, :].astype(BF16)
                return carry
            lax.fori_loop(0, D_MODEL // rows, cast, 0)

        R = EXPERT_PASS_ROWS
        for rp in range(BLK // R):
            r0 = rp * R * ROW_TILES
            x = jnp.concatenate([xs_ref[pl.ds(r0 + j, R, stride=ROW_TILES), :] for j in range(ROW_TILES)],
                                axis=1).astype(BF16)
            h = _dot(x, w1b[...]) + b1_ref[0]
            glu = jnp.minimum(h[:, :D_FF], SWIGLU_LIMIT)
            lin = jnp.clip(h[:, D_FF:], -SWIGLU_LIMIT, SWIGLU_LIMIT)
            act = glu * jax.nn.sigmoid(SWIGLU_ALPHA * glu) * (lin + 1.0)
            y = _dot(act.astype(BF16), w2b[...]) + b2_ref[0]
            for j in range(ROW_TILES):
                ys_ref[pl.ds(r0 + j, R, stride=ROW_TILES), :] = y[:, j * LANES:(j + 1) * LANES]

    @pl.when(jnp.logical_not(used))
    def _():
        ys_ref[...] = jnp.zeros_like(ys_ref)


def _experts(xs, block_exp, n_used, exp_pos, exp_next, w1, b1, w2, b2):
    BLK = EXPERT_BLOCK
    NB = xs.shape[0] // (BLK * ROW_TILES)
    blk = lambda b, nused: jnp.minimum(b, nused[0] - 1)
    return pl.pallas_call(
        _expert_kernel,
        grid_spec=pltpu.PrefetchScalarGridSpec(
            num_scalar_prefetch=4,
            grid=(NB,),
            in_specs=[
                pl.BlockSpec((BLK * ROW_TILES, LANES), lambda b, bexp, nused, *_: (blk(b, nused), 0)),
                pl.BlockSpec(memory_space=pl.ANY),
                pl.BlockSpec((1, 1, 2 * D_FF), lambda b, bexp, nused, *_: (bexp[blk(b, nused)], 0, 0)),
                pl.BlockSpec(memory_space=pl.ANY),
                pl.BlockSpec((1, 1, D_MODEL), lambda b, bexp, nused, *_: (bexp[blk(b, nused)], 0, 0)),
            ],
            out_specs=pl.BlockSpec((BLK * ROW_TILES, LANES), lambda b, *_: (b, 0)),
            scratch_shapes=[pltpu.VMEM((2, D_MODEL, 2 * D_FF), F32), pltpu.VMEM((2, D_FF, D_MODEL), F32),
                            pltpu.VMEM((D_MODEL, 2 * D_FF), BF16), pltpu.VMEM((D_FF, D_MODEL), BF16),
                            pltpu.SemaphoreType.DMA((2, 2))],
        ),
        out_shape=jax.ShapeDtypeStruct(xs.shape, F32),
        compiler_params=pltpu.CompilerParams(
            dimension_semantics=("arbitrary",), vmem_limit_bytes=VMEM_LIMIT_EXPERTS),
        name="experts",
    )(block_exp, n_used, exp_pos, exp_next, xs, w1, b1[:, None, :], w2, b2[:, None, :])


def _combine_kernel(segdst_ref, cnt_ref, big_ref, ys_ref, sel_ref, h1_ref, g_ref, out_ref, buf, sems):
    i = pl.program_id(0)
    n = pl.num_programs(0)
    slot = i % 2
    TM = MOE_TILE
    NS = TOP_K * TM

    @pl.when(i == 0)
    def _():
        _segment_dmas(i, slot, segdst_ref, cnt_ref, big_ref, buf, ys_ref, sems, False)

    @pl.when(i + 1 < n)
    def _():
        _segment_dmas(i + 1, 1 - slot, segdst_ref, cnt_ref, big_ref, buf, ys_ref, sems, False)

    sel = sel_ref[...]
    rank1, esel, off, _ = _tile_sort(sel)
    gate = jnp.maximum(sel, 0.0).astype(BF16)
    o3 = _split3(jnp.broadcast_to(off, (SUBLANES, LANES)))
    off_s = (_dot_nt(o3[0], esel) + _dot_nt(o3[1], esel) + _dot_nt(o3[2], esel))[0:1, :]
    r_s = lax.broadcasted_iota(jnp.int32, (1, NS), 1).astype(F32) - off_s
    hit = _dot_nt(rank1, esel) == r_s + 1.0
    unperm = jnp.where(hit, _dot_nt(gate, esel), 0.0).astype(BF16)

    _segment_wait(slot, buf, ys_ref, sems, False)
    y = jnp.concatenate([buf[slot, pl.ds(j, NS, stride=ROW_TILES), :] for j in range(ROW_TILES)],
                        axis=1).astype(BF16)
    h2 = h1_ref[...] + _dot(unperm, y)
    out_ref[...] = _rms(h2, g_ref[...])


def _combine(ys, sel, h1, g, segdst, cnt, big):
    T, D = h1.shape
    TM = MOE_TILE
    NS = TOP_K * TM
    return pl.pallas_call(
        _combine_kernel,
        grid_spec=pltpu.PrefetchScalarGridSpec(
            num_scalar_prefetch=3,
            grid=(T // TM,),
            in_specs=[pl.BlockSpec(memory_space=pl.ANY),
                      pl.BlockSpec((TM, LANES), lambda i, *_: (i, 0)),
                      pl.BlockSpec((TM, D), lambda i, *_: (i, 0)),
                      pl.BlockSpec((1, D), lambda i, *_: (0, 0))],
            out_specs=pl.BlockSpec((TM, D), lambda i, *_: (i, 0)),
            scratch_shapes=[pltpu.VMEM((2, NS * ROW_TILES, LANES), F32), pltpu.SemaphoreType.DMA((2,))],
        ),
        out_shape=jax.ShapeDtypeStruct((T, D), F32),
        compiler_params=pltpu.CompilerParams(
            dimension_semantics=("arbitrary",), vmem_limit_bytes=VMEM_LIMIT),
        name="combine",
    )(segdst, cnt, big, ys, sel, h1, g)


def _routing_tables(cnt_tiles):
    BLK = EXPERT_BLOCK
    nT = cnt_tiles.shape[0]
    A = nT * MOE_TILE * TOP_K
    NB = A // BLK + N_EXPERTS
    total = jnp.sum(cnt_tiles, axis=0)
    padded = (total + BLK - 1) // BLK * BLK
    pad_ends = jnp.cumsum(padded)
    pad_starts = pad_ends - padded
    before = jnp.cumsum(cnt_tiles, axis=0) - cnt_tiles
    segdst = (pad_starts[None, :] + before).reshape(-1).astype(jnp.int32)
    block_start = jnp.arange(NB, dtype=jnp.int32) * BLK
    block_exp = jnp.minimum(jnp.sum(pad_ends[None, :] <= block_start[:, None], axis=1), N_EXPERTS - 1).astype(jnp.int32)
    n_used = (pad_ends[-1] // BLK).astype(jnp.int32).reshape(1)
    paddst = (pad_starts + total).astype(jnp.int32)
    padcnt = (padded - total).astype(jnp.int32)
    big = jnp.any(cnt_tiles >= SMALL_RUN, axis=1).astype(jnp.int32)
    has_rows = total > 0
    ids = jnp.arange(N_EXPERTS, dtype=jnp.int32)
    pos = (jnp.cumsum(has_rows) - has_rows).astype(jnp.int32)
    later = jnp.where(has_rows[None, :] & (ids[None, :] > ids[:, None]), ids[None, :], N_EXPERTS)
    nxt = jnp.min(later, axis=1)
    nxt = jnp.where(nxt < N_EXPERTS, nxt, -1).astype(jnp.int32)
    of_block = block_exp[:, None] == ids[None, :]
    pos_b = jnp.sum(jnp.where(of_block, pos[None, :], 0), axis=1).astype(jnp.int32)
    nxt_b = jnp.sum(jnp.where(of_block, nxt[None, :], 0), axis=1).astype(jnp.int32)
    return (segdst, cnt_tiles.reshape(-1).astype(jnp.int32), big, paddst, padcnt, block_exp, n_used,
            pos_b, nxt_b, NB * BLK)


def _rotary_tables(S):
    half = RET_HEAD_DIM // 2
    f32 = np.float32
    inv_freq = np.power(f32(ROPE_BASE), -np.arange(half, dtype=f32) / f32(half)).astype(f32)
    ang = (np.arange(S, dtype=f32)[:, None] * inv_freq[None, :]).astype(f32)
    cos, sin = np.cos(ang).astype(f32), np.sin(ang).astype(f32)
    return (jnp.asarray(np.concatenate([cos, cos], axis=-1), F32),
            jnp.asarray(np.concatenate([-sin, sin], axis=-1), F32))


def _layer(h, norm_mix_g, w_in, b_forget, w_out, norm_ffn_g, w_router, b_router,
           w_exp_in, b_exp_in, w_exp_out, b_exp_out, final_g):
    B, S, D = h.shape
    R, Fw = RET_WIDTH, FOX_WIDTH
    cos, sin = _rotary_tables(S)
    wr = w_in[:, :4 * R].astype(BF16)
    wfq, wfk, wfv = (w_in[:, 4 * R + i * Fw:4 * R + (i + 1) * Fw].astype(BF16) for i in range(3))
    wzt = jnp.zeros((16, D), BF16).at[:FOX_HEADS, :].set(w_in[:, 4 * R + 3 * Fw:].astype(BF16).T)
    o_ret, kaug, qaug, fvt = _in_proj(
        h, norm_mix_g[None, :], cos, sin, wr, wfk, wfq.T, wfv.T, wzt, b_forget)
    o_fox = _fox_attn(qaug, kaug, fvt)

    T = B * S
    wo = w_out.astype(BF16)
    wrt = jnp.zeros((LANES, D), F32).at[:N_EXPERTS, :].set(w_router.T)
    wrh = wrt.astype(BF16)
    wrl = (wrt - wrh.astype(F32)).astype(BF16)
    br = jnp.zeros((LANES, 1), F32).at[:N_EXPERTS, 0].set(b_router)
    h1, u2, sel, cnt = _out_router(h.reshape(T, D), o_ret.reshape(T, R), o_fox.reshape(T, Fw),
                                   wo[:R], wo[R:], norm_ffn_g[None, :], wrh, wrl, br)
    cnt_tiles = cnt[:, 0, :N_EXPERTS].astype(jnp.int32)
    segdst, cnt_flat, big, paddst, padcnt, block_exp, n_used, exp_pos, exp_next, n_rows = _routing_tables(cnt_tiles)
    xs = _dispatch(u2, sel, segdst, cnt_flat, big, paddst, padcnt, n_used, n_rows)
    ys = _experts(xs, block_exp, n_used, exp_pos, exp_next, w_exp_in, b_exp_in, w_exp_out, b_exp_out)
    out = _combine(ys, sel, h1, final_g[None, :], segdst, cnt_flat, big)
    return out.reshape(B, S, D)


def kernel(x, norm_mix_g, w_in, b_forget, w_out, norm_ffn_g, w_router, b_router,
           w_exp_in, b_exp_in, w_exp_out, b_exp_out, norm_final_g):
    depth = w_in.shape[0]
    assert depth == 1, "the fused final RMSNorm assumes a single layer"
    return _layer(x, norm_mix_g[0], w_in[0], b_forget[0], w_out[0], norm_ffn_g[0], w_router[0], b_router[0],
                  w_exp_in[0], b_exp_in[0], w_exp_out[0], b_exp_out[0], norm_final_g)
```

```python
import functools

import numpy as np
import jax
import jax.numpy as jnp
from jax import lax
from jax.experimental import pallas as pl
from jax.experimental.pallas import tpu as pltpu

F32 = jnp.float32
BF16 = jnp.bfloat16

D_MODEL = 1024
RET_HEADS, RET_HEAD_DIM = 4, 128
RET_WIDTH = RET_HEADS * RET_HEAD_DIM
FOX_HEADS, FOX_HEAD_DIM = 8, 64
FOX_WIDTH = FOX_HEADS * FOX_HEAD_DIM
CHUNK = 64
ROPE_BASE = 10000.0
N_EXPERTS = 32
TOP_K = 4
D_FF = D_MODEL
SWIGLU_ALPHA = 1.702
SWIGLU_LIMIT = 7.0
RMS_EPS = 1e-5
GN_EPS = 1e-5

LANES = 128
SUBLANES = 8
ROW_TILES = D_MODEL // LANES
V7X_VMEM_BYTES = 64 * 1024 * 1024
VMEM_LIMIT = V7X_VMEM_BYTES * 3 // 4
VMEM_LIMIT_EXPERTS = V7X_VMEM_BYTES * 7 // 8

PROJ_TILE = 512
RET_BLOCK = 256
FOX_TQ = 512
FOX_TK = PROJ_TILE
AUG = 128
V_AUG = 80
FZ_ROWS = 16
LOG2E = 1.4426950408889634
MOE_TILE = 256
ROUTER_TILES = 4
SMALL_RUN = 64
EXPERT_BLOCK = 512
EXPERT_PASS_ROWS = 256

NT_DIMS = (((1,), (1,)), ((), ()))


def _split3(a):
    hi = a.astype(BF16)
    r1 = a - hi.astype(F32)
    mid = r1.astype(BF16)
    lo = (r1 - mid.astype(F32)).astype(BF16)
    return hi, mid, lo


def _dot(a, b):
    return jnp.dot(a, b, preferred_element_type=F32)


def _dot_nt(a, b):
    return lax.dot_general(a, b, NT_DIMS, preferred_element_type=F32)


def _rms(x, g):
    return x * lax.rsqrt(jnp.mean(x * x, axis=-1, keepdims=True) + RMS_EPS) * g


def _in_proj_kernel(x_ref, g_ref, cos_ref, sin_ref, wr_ref, wfk_ref, wfqt_ref, wfvt_ref, wzt_ref, bcol_ref,
                    selkf_ref, constk_ref, selqf_ref, constq_ref, dec_ref, qw_ref, kw_ref, cd_ref,
                    oret_ref, kaug_ref, qaug_ref, fvt_ref, ccol, state):
    TM = x_ref.shape[1]
    d, H = FOX_HEAD_DIM, FOX_HEADS

    @pl.when(pl.program_id(1) == 0)
    def _():
        ccol[...] = jnp.zeros_like(ccol)
        state[...] = jnp.zeros_like(state)

    u = _rms(x_ref[0], g_ref[...]).astype(BF16)
    r = _dot(u, wr_ref[...])
    cos, sin = cos_ref[...], sin_ref[...]
    k_scale = RET_HEAD_DIM ** -0.5
    L = RET_BLOCK
    for h in range(RET_HEADS):
        hs = slice(h * RET_HEAD_DIM, (h + 1) * RET_HEAD_DIM)
        q = r[:, hs]
        k = r[:, RET_WIDTH + h * RET_HEAD_DIM:RET_WIDTH + (h + 1) * RET_HEAD_DIM]
        qh = (q * cos + pltpu.roll(q, RET_HEAD_DIM // 2, 1) * sin).astype(BF16)
        kh = ((k * cos + pltpu.roll(k, RET_HEAD_DIM // 2, 1) * sin) * k_scale).astype(BF16)
        vh = r[:, 2 * RET_WIDTH + h * RET_HEAD_DIM:2 * RET_WIDTH + (h + 1) * RET_HEAD_DIM].astype(BF16)
        gh = r[:, 3 * RET_WIDTH + h * RET_HEAD_DIM:3 * RET_WIDTH + (h + 1) * RET_HEAD_DIM].astype(BF16)
        for rs in (slice(b0, b0 + L) for b0 in range(0, TM, L)):
            qb, kb, vb = qh[rs], kh[rs], vh[rs]
            scores = (_dot_nt(qb, kb) * dec_ref[h]).astype(BF16)
            st = state[h]
            o = _dot(scores, vb) + _dot((qb.astype(F32) * qw_ref[h]).astype(BF16), st.astype(BF16))
            kk = kb.astype(F32) * kw_ref[h]
            state[h] = st * cd_ref[h, 0:1, :] + _dot(kk.T.astype(BF16), vb)
            mu = jnp.mean(o, axis=-1, keepdims=True)
            oc = o - mu
            var = jnp.mean(oc * oc, axis=-1, keepdims=True)
            oret_ref[0, rs, hs] = (oc * lax.rsqrt(var + GN_EPS) * jax.nn.silu(gh[rs].astype(F32))).astype(BF16)
    fvt_ref[0, 0] = _dot_nt(wfvt_ref[...], u).astype(BF16)
    fk = _dot(u, wfk_ref[...])
    fqt = (_dot_nt(wfqt_ref[...], u) * (d ** -0.5 * LOG2E)).astype(BF16)
    zt = _dot_nt(wzt_ref[...], u)


    row = lax.broadcasted_iota(jnp.int32, (FZ_ROWS, TM), 0)
    lft = jnp.where(row < H, jax.nn.log_sigmoid(zt + bcol_ref[...]), 0.0)
    utri = (lax.broadcasted_iota(jnp.int32, (TM, TM), 0) <= lax.broadcasted_iota(jnp.int32, (TM, TM), 1)).astype(BF16)
    t3 = _split3(lft)
    f_col = _dot(t3[0], utri) + _dot(t3[1], utri) + _dot(t3[2], utri) + ccol[:, 0:1]
    ccol[...] = jnp.broadcast_to(f_col[:, TM - 1:TM], ccol.shape)

    pieces_t = jnp.concatenate(_split3(f_col * LOG2E), axis=0)
    for h in range(H):
        extra = _dot(selqf_ref[h], pieces_t) + constq_ref[...]
        qaug_ref[0, h, 0] = jnp.concatenate([fqt[h * d:(h + 1) * d, :], extra.astype(BF16)], axis=0)

    f_row = jnp.concatenate([f_col, jnp.zeros((LANES - FZ_ROWS, TM), F32)], axis=0).T
    n3 = _split3(f_row * -LOG2E)
    pieces = (n3[0].astype(F32) + pltpu.roll(n3[1].astype(F32), H, 1)
              + pltpu.roll(n3[2].astype(F32), 2 * H, 1)).astype(BF16)
    lane = lax.broadcasted_iota(jnp.int32, (TM, LANES), 1)
    for g in range(H // 2):
        bias = _dot(pieces, selkf_ref[g])
        kg = fk[:, g * 2 * d:(g + 1) * 2 * d]
        for o in range(2):
            kh = kg if o == 0 else pltpu.roll(kg, d, 1)
            extra = bias[:, o * AUG:(o + 1) * AUG] + constk_ref[...]
            kaug_ref[0, 2 * g + o] = jnp.where(lane < d, kh, extra).astype(BF16)


def _in_proj(x, g, cos, sin, wr, wfk, wfqt, wfvt, wzt, b_forget):
    B, S, D = x.shape
    TM = PROJ_TILE
    ns = S // TM
    selkf, constk, selqf, constq = _fox_prep_constants()
    bcol = jnp.zeros((FZ_ROWS, 1), F32).at[:FOX_HEADS, 0].set(b_forget)
    consts = (wr, wfk, wfqt, wfvt, wzt, bcol, selkf, constk, selqf, constq) + _retention_tables()
    const = lambda a: pl.BlockSpec(a.shape, lambda b, s: (0,) * a.ndim)
    tok = lambda w: pl.BlockSpec((1, TM, w), lambda b, s: (b, s, 0))
    out_shape = (
        jax.ShapeDtypeStruct((B, S, RET_WIDTH), BF16),
        jax.ShapeDtypeStruct((B, FOX_HEADS, S, AUG), BF16),
        jax.ShapeDtypeStruct((B, FOX_HEADS, ns, AUG, TM), BF16),
        jax.ShapeDtypeStruct((B, ns, FOX_WIDTH, TM), BF16),
    )
    return pl.pallas_call(
        _in_proj_kernel,
        grid=(B, ns),
        in_specs=[
            pl.BlockSpec((1, TM, D), lambda b, s: (b, s, 0)),
            pl.BlockSpec((1, D), lambda b, s: (0, 0)),
            pl.BlockSpec((TM, RET_HEAD_DIM), lambda b, s: (s, 0)),
            pl.BlockSpec((TM, RET_HEAD_DIM), lambda b, s: (s, 0)),
        ] + [const(a) for a in consts],
        out_specs=(
            tok(RET_WIDTH),
            pl.BlockSpec((1, FOX_HEADS, TM, AUG), lambda b, s: (b, 0, s, 0)),
            pl.BlockSpec((1, FOX_HEADS, 1, AUG, TM), lambda b, s: (b, 0, s, 0, 0)),
            pl.BlockSpec((1, 1, FOX_WIDTH, TM), lambda b, s: (b, s, 0, 0)),
        ),
        out_shape=out_shape,
        scratch_shapes=[pltpu.VMEM((FZ_ROWS, LANES), F32),
                        pltpu.VMEM((RET_HEADS, RET_HEAD_DIM, RET_HEAD_DIM), F32)],
        compiler_params=pltpu.CompilerParams(
            dimension_semantics=("arbitrary", "arbitrary"), vmem_limit_bytes=VMEM_LIMIT),
        name="in_proj",
    )(x, g, cos, sin, *consts)


def _fox_prep_constants():
    d, H = FOX_HEAD_DIM, FOX_HEADS
    selkf = np.zeros((H // 2, LANES, 2 * AUG), np.float32)
    constk = np.zeros((1, AUG), np.float32)
    selqf = np.zeros((H, d, 3 * FZ_ROWS), np.float32)
    constq = np.zeros((d, 1), np.float32)
    for p in range(3):
        constk[0, d + p] = 1.0
        constq[3 + p, 0] = 1.0
        for h in range(H):
            selkf[h // 2, p * H + h, (h % 2) * AUG + d + 3 + p] = 1.0
            selqf[h, p, p * FZ_ROWS + h] = 1.0
    return jnp.asarray(selkf, BF16), jnp.asarray(constk, F32), jnp.asarray(selqf, BF16), jnp.asarray(constq, F32)


def _retention_tables():
    L = RET_BLOCK
    f32 = np.float32
    log_gamma = np.log1p(-np.exp2(-5.0 - np.arange(RET_HEADS, dtype=f32))).astype(f32)
    p = np.arange(L, dtype=f32)
    dist = np.abs(p[:, None] - p[None, :])
    chunk = np.arange(L) // CHUNK
    allowed = (chunk[None, :] <= chunk[:, None]).astype(f32)
    dec = np.exp(log_gamma[:, None, None] * dist).astype(f32) * allowed
    lanes = lambda a: np.broadcast_to(a.astype(f32)[:, :, None], (RET_HEADS, L, RET_HEAD_DIM))
    qw = lanes(np.exp(log_gamma[:, None] * (p[None, :] + f32(1.0))))
    kw = lanes(np.exp(log_gamma[:, None] * (f32(L - 1.0) - p[None, :])))
    cd = np.broadcast_to(np.exp(log_gamma * f32(L)).astype(f32)[:, None, None], (RET_HEADS, SUBLANES, RET_HEAD_DIM))
    return tuple(jnp.asarray(a, F32) for a in (dec, qw, kw, cd))


def _fox_attn_kernel(q_ref, k_ref, v_ref, o_ref, s_a, s_b, s_c, cm_a, cm_b, cm_c, m_ref, acc_ref):
    T = FOX_TQ
    d = FOX_HEAD_DIM
    nq = q_ref.shape[2]
    ones_rows = (lax.broadcasted_iota(jnp.int32, (V_AUG - d, T), 0) == 0).astype(BF16)

    def scores(qi, j, s_ref, cm_ref):
        for hh in range(2):
            kj = k_ref[0, hh, pl.ds(pl.multiple_of(j * T, T), T), :]
            st = _dot(kj, q_ref[0, hh, qi])
            s_ref[hh] = st
            cm_ref[hh] = jnp.max(st, axis=0, keepdims=True)

    def consume(j, s_ref, cm_ref, masked):
        for hh in range(2):
            st = s_ref[hh]
            if masked:
                key = lax.broadcasted_iota(jnp.int32, (T, T), 0)
                qry = lax.broadcasted_iota(jnp.int32, (T, T), 1)
                st = jnp.where(key <= qry, st, -jnp.inf)
                cm = jnp.max(st, axis=0, keepdims=True)
            else:
                cm = cm_ref[hh]
            m = m_ref[hh]
            m_new = jnp.maximum(m, cm)
            p = jnp.exp2(st - m_new).astype(BF16)
            vj = jnp.concatenate([v_ref[0, j, hh * d:(hh + 1) * d, :], ones_rows], axis=0)
            acc_ref[hh] = jnp.exp2(m - m_new) * acc_ref[hh] + _dot(vj, p)
            m_ref[hh] = m_new

    def reset():
        m_ref[...] = jnp.full(m_ref.shape, -jnp.inf, F32)
        acc_ref[...] = jnp.zeros(acc_ref.shape, F32)

    def prefetch_next(qi):
        @pl.when(qi + 1 < nq)
        def _():
            scores(qi + 1, 0, s_c, cm_c)

    def finish(qi):
        outs = [acc_ref[hh, 0:d, :] / acc_ref[hh, d:d + 1, :] for hh in range(2)]
        o_ref[0, pl.ds(pl.multiple_of(qi * T, T), T), :] = jnp.concatenate(outs, axis=0).T.astype(BF16)

    reset()
    scores(0, 0, s_a, cm_a)
    prefetch_next(0)
    consume(0, s_a, cm_a, True)
    finish(0)

    def query_tile(qi, carry):
        reset()
        scores(qi, 1, s_a, cm_a)
        consume(0, s_c, cm_c, False)

        def pair(j):
            scores(qi, j + 1, s_b, cm_b)
            consume(j, s_a, cm_a, False)
            scores(qi, j + 2, s_a, cm_a)
            consume(j + 1, s_b, cm_b, False)

        def two_pairs(jj, c):
            pair(1 + 4 * jj)
            pair(3 + 4 * jj)
            return c

        def one_pair(jj, c):
            pair(1 + 4 * (n_pairs // 2) + 2 * jj)
            return c

        n_pairs = (qi - 1) // 2
        lax.fori_loop(0, n_pairs // 2, two_pairs, 0)
        lax.fori_loop(0, n_pairs % 2, one_pair, 0)

        @pl.when(qi % 2 == 1)
        def _():
            prefetch_next(qi)
            consume(qi, s_a, cm_a, True)

        @pl.when(qi % 2 == 0)
        def _():
            scores(qi, qi, s_b, cm_b)
            consume(qi - 1, s_a, cm_a, False)
            prefetch_next(qi)
            consume(qi, s_b, cm_b, True)

        finish(qi)
        return carry

    lax.fori_loop(1, nq, query_tile, 0)


def _fox_attn(qaug, kaug, fvt):
    B, H, S, _ = kaug.shape
    nk = S // FOX_TK
    nq = S // FOX_TQ
    score_buf = pltpu.VMEM((2, FOX_TK, FOX_TQ), F32)
    col_max = pltpu.VMEM((2, 1, FOX_TQ), F32)
    return pl.pallas_call(
        _fox_attn_kernel,
        grid=(B, H // 2),
        in_specs=[
            pl.BlockSpec((1, 2, nq, AUG, FOX_TQ), lambda b, p: (b, p, 0, 0, 0)),
            pl.BlockSpec((1, 2, S, AUG), lambda b, p: (b, p, 0, 0)),
            pl.BlockSpec((1, nk, 2 * FOX_HEAD_DIM, FOX_TK), lambda b, p: (b, 0, p, 0)),
        ],
        out_specs=pl.BlockSpec((1, S, 2 * FOX_HEAD_DIM), lambda b, p: (b, 0, p)),
        out_shape=jax.ShapeDtypeStruct((B, S, FOX_WIDTH), BF16),
        scratch_shapes=[
            score_buf, score_buf, score_buf, col_max, col_max, col_max,
            pltpu.VMEM((2, 1, FOX_TQ), F32), pltpu.VMEM((2, V_AUG, FOX_TQ), F32),
        ],
        compiler_params=pltpu.CompilerParams(
            dimension_semantics=("arbitrary",) * 2, vmem_limit_bytes=VMEM_LIMIT),
        name="fox_attn",
    )(qaug, kaug, fvt)


def _out_router_kernel(x_ref, oret_ref, ofox_ref, wor_ref, wof_ref, g_ref, wrh_ref, wrl_ref, br_ref,
                       h1_ref, u2_ref, sel_ref, cnt_ref):
    TM = MOE_TILE
    rows = lambda t: slice(t * TM, (t + 1) * TM)

    def out_proj(t):
        rs = rows(t)
        h1 = x_ref[rs] + _dot(oret_ref[rs], wor_ref[...]) + _dot(ofox_ref[rs], wof_ref[...])
        h1_ref[rs] = h1
        return h1

    def router_logits(t, h1):
        u2 = _rms(h1, g_ref[...])
        uh = u2.astype(BF16)
        u2_ref[rows(t)] = uh
        ul = (u2 - uh.astype(F32)).astype(BF16)
        return (_dot_nt(wrh_ref[...], uh) + _dot_nt(wrh_ref[...], ul) + _dot_nt(wrl_ref[...], uh)
                + br_ref[...])

    def top_k(t, logits):
        rs = rows(t)
        row = lax.broadcasted_iota(jnp.int32, (LANES, TM), 0).astype(F32)
        l = jnp.where(row < N_EXPERTS, logits, -jnp.inf)
        picks, vals = [], []
        for _ in range(TOP_K):
            m = jnp.max(l, axis=0, keepdims=True)
            idx = jnp.min(jnp.where(l == m, row, float(LANES)), axis=0, keepdims=True)
            pick = row == idx
            picks.append(pick)
            vals.append(m)
            l = jnp.where(pick, -jnp.inf, l)
        exps = [jnp.exp(v - vals[0]) for v in vals]
        den = exps[0] + exps[1] + exps[2] + exps[3]
        sel_t = jnp.full((LANES, TM), -1.0, F32)
        for pick, e in zip(picks, exps):
            sel_t = jnp.where(pick, e / den, sel_t)
        sel = sel_t.T
        sel_ref[rs] = sel
        cnt = jnp.sum((sel >= 0.0).astype(F32), axis=0, keepdims=True)
        cnt_ref[t] = jnp.broadcast_to(cnt, (SUBLANES, LANES))

    n = ROUTER_TILES
    h1s, lgs = {0: out_proj(0)}, {}
    for t in range(1, n + 2):
        if t < n:
            h1s[t] = out_proj(t)
        if 1 <= t <= n:
            lgs[t - 1] = router_logits(t - 1, h1s.pop(t - 1))
        if t >= 2:
            top_k(t - 2, lgs.pop(t - 2))


def _out_router(x2, o_ret, o_fox, wor, wof, g, wrh, wrl, br):
    T, D = x2.shape
    TM = MOE_TILE * ROUTER_TILES
    nT = T // MOE_TILE
    const = lambda a: pl.BlockSpec(a.shape, lambda i: (0,) * a.ndim)
    tok = lambda w: pl.BlockSpec((TM, w), lambda i: (i, 0))
    return pl.pallas_call(
        _out_router_kernel,
        grid=(T // TM,),
        in_specs=[tok(D), tok(RET_WIDTH), tok(FOX_WIDTH), const(wor), const(wof), const(g),
                  const(wrh), const(wrl), const(br)],
        out_specs=(tok(D), tok(D), tok(LANES),
                   pl.BlockSpec((ROUTER_TILES, SUBLANES, LANES), lambda i: (i, 0, 0))),
        out_shape=(
            jax.ShapeDtypeStruct((T, D), F32),
            jax.ShapeDtypeStruct((T, D), BF16),
            jax.ShapeDtypeStruct((T, LANES), F32),
            jax.ShapeDtypeStruct((nT, SUBLANES, LANES), F32),
        ),
        compiler_params=pltpu.CompilerParams(dimension_semantics=("arbitrary",)),
        name="out_router",
    )(x2, o_ret, o_fox, wor, wof, g, wrh, wrl, br)


def _tile_sort(sel):
    TM = sel.shape[0]
    NS = TOP_K * TM
    maskf = (sel >= 0.0).astype(F32)
    mask = maskf.astype(BF16)
    ri = lax.broadcasted_iota(jnp.int32, (TM, TM), 0)
    ci = lax.broadcasted_iota(jnp.int32, (TM, TM), 1)
    rank1 = maskf * _dot((ri >= ci).astype(BF16), mask)
    cnt = jnp.sum(maskf, axis=0, keepdims=True)
    ei = lax.broadcasted_iota(jnp.int32, (LANES, LANES), 0)
    ej = lax.broadcasted_iota(jnp.int32, (LANES, LANES), 1)
    cnt8 = jnp.broadcast_to(cnt, (SUBLANES, LANES)).astype(BF16)
    off = _dot(cnt8, (ei < ej).astype(BF16))[0:1, :]
    slot = lax.broadcasted_iota(jnp.int32, (NS, LANES), 0).astype(F32)
    esel = ((slot >= off) & (slot < off + cnt)).astype(BF16)
    return rank1.astype(BF16), esel, off, cnt


def _segment_wait(slot, local, remote_rows, sem, to_remote):
    whole = local.at[slot]
    rem = remote_rows.at[pl.ds(0, whole.shape[0]), :]
    cp = (pltpu.make_async_copy(whole, rem, sem.at[slot]) if to_remote
          else pltpu.make_async_copy(rem, whole, sem.at[slot]))
    cp.wait()


def _segment_dmas(step, slot, segdst_ref, cnt_ref, big_ref, local, remote_rows, sem, to_remote):
    big = big_ref[step] != 0
    for cond, top_bit in ((big, MOE_TILE), (jnp.logical_not(big), SMALL_RUN // 2)):
        pl.when(cond)(functools.partial(
            _segment_dma_path, step, slot, segdst_ref, cnt_ref, local, remote_rows, sem, to_remote, top_bit))


def _segment_dma_path(step, slot, segdst_ref, cnt_ref, local, remote_rows, sem, to_remote, top_bit):
    def body(e, off):
        c = cnt_ref[step * N_EXPERTS + e]
        dst = segdst_ref[step * N_EXPERTS + e]
        bit = top_bit
        while bit >= 1:
            done = c & (~(2 * bit - 1))

            @pl.when((c & bit) != 0)
            def _(bit=bit, done=done):
                loc = local.at[slot, pl.ds((off + done) * ROW_TILES, bit * ROW_TILES), :]
                rem = remote_rows.at[pl.ds((dst + done) * ROW_TILES, bit * ROW_TILES), :]
                cp = (pltpu.make_async_copy(loc, rem, sem.at[slot]) if to_remote
                      else pltpu.make_async_copy(rem, loc, sem.at[slot]))
                cp.start()
            bit //= 2
        return off + c

    off = 0
    for e in range(N_EXPERTS):
        off = body(e, off)


def _dispatch_kernel(segdst_ref, cnt_ref, big_ref, paddst_ref, padcnt_ref, nused_ref, u2_ref, sel_ref, xs_ref,
                     buf, zbuf, sems, zsem):
    i = pl.program_id(0)
    last = pl.num_programs(0) - 1
    slot = i % 2
    TM = MOE_TILE
    NS = TOP_K * TM
    rank1, esel, off, _ = _tile_sort(sel_ref[...])
    slot_id = lax.broadcasted_iota(jnp.int32, (NS, 1), 0).astype(F32)
    r_s = slot_id - jnp.sum(esel.astype(F32) * off, axis=1, keepdims=True)
    perm = (_dot_nt(esel, rank1) == r_s + 1.0).astype(BF16)

    @pl.when(i >= 2)
    def _():
        _segment_wait(slot, buf, xs_ref, sems, True)

    u2 = u2_ref[...]
    for c in range(NS // TM):
        rows = _dot(perm[c * TM:(c + 1) * TM], u2)
        for j in range(ROW_TILES):
            buf[slot, pl.ds(c * TM * ROW_TILES + j, TM, stride=ROW_TILES), :] = rows[:, j * LANES:(j + 1) * LANES]
    _segment_dmas(i, slot, segdst_ref, cnt_ref, big_ref, buf, xs_ref, sems, True)

    @pl.when(i == last)
    def _():
        @pl.when(i >= 1)
        def _():
            _segment_wait(1 - slot, buf, xs_ref, sems, True)
        _segment_wait(slot, buf, xs_ref, sems, True)
        zbuf[...] = jnp.zeros_like(zbuf)
        half = EXPERT_BLOCK // 2 * ROW_TILES
        n_blocks = xs_ref.shape[0] // (EXPERT_BLOCK * ROW_TILES)
        for wait in (False, True):
            def unused(hb, carry, wait=wait):
                cp = pltpu.make_async_copy(zbuf, xs_ref.at[pl.ds(hb * half, half), :], zsem.at[0])
                cp.wait() if wait else cp.start()
                return carry
            lax.fori_loop(2 * nused_ref[0], 2 * n_blocks, unused, 0)


            def body(e, carry, wait=wait):
                c = padcnt_ref[e]
                dst = paddst_ref[e]
                bit = EXPERT_BLOCK // 2
                while bit >= 1:
                    done = c & (~(2 * bit - 1))

                    @pl.when((c & bit) != 0)
                    def _(bit=bit, done=done):
                        cp = pltpu.make_async_copy(
                            zbuf.at[pl.ds(0, bit * ROW_TILES), :],
                            xs_ref.at[pl.ds((dst + done) * ROW_TILES, bit * ROW_TILES), :], zsem.at[0])
                        cp.wait() if wait else cp.start()
                    bit //= 2
                return carry
            lax.fori_loop(0, N_EXPERTS, body, 0)


def _dispatch(u2, sel, segdst, cnt, big, paddst, padcnt, n_used, n_rows):
    T, D = u2.shape
    TM = MOE_TILE
    NS = TOP_K * TM
    return pl.pallas_call(
        _dispatch_kernel,
        grid_spec=pltpu.PrefetchScalarGridSpec(
            num_scalar_prefetch=6,
            grid=(T // TM,),
            in_specs=[pl.BlockSpec((TM, D), lambda i, *_: (i, 0)),
                      pl.BlockSpec((TM, LANES), lambda i, *_: (i, 0))],
            out_specs=pl.BlockSpec(memory_space=pl.ANY),
            scratch_shapes=[pltpu.VMEM((2, NS * ROW_TILES, LANES), F32),
                            pltpu.VMEM((EXPERT_BLOCK // 2 * ROW_TILES, LANES), F32),
                            pltpu.SemaphoreType.DMA((2,)), pltpu.SemaphoreType.DMA((1,))],
        ),
        out_shape=jax.ShapeDtypeStruct((n_rows * ROW_TILES, LANES), F32),
        compiler_params=pltpu.CompilerParams(
            dimension_semantics=("arbitrary",), vmem_limit_bytes=VMEM_LIMIT),
        name="dispatch",
    )(segdst, cnt, big, paddst, padcnt, n_used, u2, sel)


def _expert_kernel(bexp_ref, nused_ref, epos_ref, enext_ref, xs_ref, w1_hbm, b1_ref, w2_hbm, b2_ref, ys_ref,
                   w1f, w2f, w1b, w2b, wsem):
    b = pl.program_id(0)
    BLK = EXPERT_BLOCK
    used = b < nused_ref[0]

    def weight_copies(e, slot):
        return (pltpu.make_async_copy(w1_hbm.at[e], w1f.at[slot], wsem.at[0, slot]),
                pltpu.make_async_copy(w2_hbm.at[e], w2f.at[slot], wsem.at[1, slot]))

    @pl.when(used)
    def _():
        e = bexp_ref[b]
        prev = bexp_ref[jnp.maximum(b - 1, 0)]
        slot = epos_ref[b] % 2

        @pl.when(b == 0)
        def _():
            for cp in weight_copies(e, slot):
                cp.start()

        @pl.when((b == 0) | (e != prev))
        def _():
            nxt = enext_ref[b]

            @pl.when(nxt >= 0)
            def _():
                for cp in weight_copies(nxt, 1 - slot):
                    cp.start()

            for cp in weight_copies(e, slot):
                cp.wait()
            rows = LANES

            def cast(r, carry):
                sl = pl.ds(pl.multiple_of(r * rows, rows), rows)
                w1b[sl, :] = w1f[slot, sl, :].astype(BF16)
                w2b[sl, :] = w2f[slot, sl, :].astype(BF16)
                return carry
            lax.fori_loop(0, D_MODEL // rows, cast, 0)

        R = EXPERT_PASS_ROWS
        for rp in range(BLK // R):
            r0 = rp * R * ROW_TILES
            x = jnp.concatenate([xs_ref[pl.ds(r0 + j, R, stride=ROW_TILES), :] for j in range(ROW_TILES)],
                                axis=1).astype(BF16)
            h = _dot(x, w1b[...]) + b1_ref[0]
            glu = jnp.minimum(h[:, :D_FF], SWIGLU_LIMIT)
            lin = jnp.clip(h[:, D_FF:], -SWIGLU_LIMIT, SWIGLU_LIMIT)
            act = glu * jax.nn.sigmoid(SWIGLU_ALPHA * glu) * (lin + 1.0)
            y = _dot(act.astype(BF16), w2b[...]) + b2_ref[0]
            for j in range(ROW_TILES):
                ys_ref[pl.ds(r0 + j, R, stride=ROW_TILES), :] = y[:, j * LANES:(j + 1) * LANES]

    @pl.when(jnp.logical_not(used))
    def _():
        ys_ref[...] = jnp.zeros_like(ys_ref)


def _experts(xs, block_exp, n_used, exp_pos, exp_next, w1, b1, w2, b2):
    BLK = EXPERT_BLOCK
    NB = xs.shape[0] // (BLK * ROW_TILES)
    blk = lambda b, nused: jnp.minimum(b, nused[0] - 1)
    return pl.pallas_call(
        _expert_kernel,
        grid_spec=pltpu.PrefetchScalarGridSpec(
            num_scalar_prefetch=4,
            grid=(NB,),
            in_specs=[
                pl.BlockSpec((BLK * ROW_TILES, LANES), lambda b, bexp, nused, *_: (blk(b, nused), 0)),
                pl.BlockSpec(memory_space=pl.ANY),
                pl.BlockSpec((1, 1, 2 * D_FF), lambda b, bexp, nused, *_: (bexp[blk(b, nused)], 0, 0)),
                pl.BlockSpec(memory_space=pl.ANY),
                pl.BlockSpec((1, 1, D_MODEL), lambda b, bexp, nused, *_: (bexp[blk(b, nused)], 0, 0)),
            ],
            out_specs=pl.BlockSpec((BLK * ROW_TILES, LANES), lambda b, *_: (b, 0)),
            scratch_shapes=[pltpu.VMEM((2, D_MODEL, 2 * D_FF), F32), pltpu.VMEM((2, D_FF, D_MODEL), F32),
                            pltpu.VMEM((D_MODEL, 2 * D_FF), BF16), pltpu.VMEM((D_FF, D_MODEL), BF16),
                            pltpu.SemaphoreType.DMA((2, 2))],
        ),
        out_shape=jax.ShapeDtypeStruct(xs.shape, F32),
        compiler_params=pltpu.CompilerParams(
            dimension_semantics=("arbitrary",), vmem_limit_bytes=VMEM_LIMIT_EXPERTS),
        name="experts",
    )(block_exp, n_used, exp_pos, exp_next, xs, w1, b1[:, None, :], w2, b2[:, None, :])


def _combine_kernel(segdst_ref, cnt_ref, big_ref, ys_ref, sel_ref, h1_ref, g_ref, out_ref, buf, sems):
    i = pl.program_id(0)
    n = pl.num_programs(0)
    slot = i % 2
    TM = MOE_TILE
    NS = TOP_K * TM

    @pl.when(i == 0)
    def _():
        _segment_dmas(i, slot, segdst_ref, cnt_ref, big_ref, buf, ys_ref, sems, False)

    @pl.when(i + 1 < n)
    def _():
        _segment_dmas(i + 1, 1 - slot, segdst_ref, cnt_ref, big_ref, buf, ys_ref, sems, False)

    sel = sel_ref[...]
    rank1, esel, off, _ = _tile_sort(sel)
    gate = jnp.maximum(sel, 0.0).astype(BF16)
    o3 = _split3(jnp.broadcast_to(off, (SUBLANES, LANES)))
    off_s = (_dot_nt(o3[0], esel) + _dot_nt(o3[1], esel) + _dot_nt(o3[2], esel))[0:1, :]
    r_s = lax.broadcasted_iota(jnp.int32, (1, NS), 1).astype(F32) - off_s
    hit = _dot_nt(rank1, esel) == r_s + 1.0
    unperm = jnp.where(hit, _dot_nt(gate, esel), 0.0).astype(BF16)

    _segment_wait(slot, buf, ys_ref, sems, False)
    y = jnp.concatenate([buf[slot, pl.ds(j, NS, stride=ROW_TILES), :] for j in range(ROW_TILES)],
                        axis=1).astype(BF16)
    h2 = h1_ref[...] + _dot(unperm, y)
    out_ref[...] = _rms(h2, g_ref[...])


def _combine(ys, sel, h1, g, segdst, cnt, big):
    T, D = h1.shape
    TM = MOE_TILE
    NS = TOP_K * TM
    return pl.pallas_call(
        _combine_kernel,
        grid_spec=pltpu.PrefetchScalarGridSpec(
            num_scalar_prefetch=3,
            grid=(T // TM,),
            in_specs=[pl.BlockSpec(memory_space=pl.ANY),
                      pl.BlockSpec((TM, LANES), lambda i, *_: (i, 0)),
                      pl.BlockSpec((TM, D), lambda i, *_: (i, 0)),
                      pl.BlockSpec((1, D), lambda i, *_: (0, 0))],
            out_specs=pl.BlockSpec((TM, D), lambda i, *_: (i, 0)),
            scratch_shapes=[pltpu.VMEM((2, NS * ROW_TILES, LANES), F32), pltpu.SemaphoreType.DMA((2,))],
        ),
        out_shape=jax.ShapeDtypeStruct((T, D), F32),
        compiler_params=pltpu.CompilerParams(
            dimension_semantics=("arbitrary",), vmem_limit_bytes=VMEM_LIMIT),
        name="combine",
    )(segdst, cnt, big, ys, sel, h1, g)


def _routing_tables(cnt_tiles):
    BLK = EXPERT_BLOCK
    nT = cnt_tiles.shape[0]
    A = nT * MOE_TILE * TOP_K
    NB = A // BLK + N_EXPERTS
    total = jnp.sum(cnt_tiles, axis=0)
    padded = (total + BLK - 1) // BLK * BLK
    pad_ends = jnp.cumsum(padded)
    pad_starts = pad_ends - padded
    before = jnp.cumsum(cnt_tiles, axis=0) - cnt_tiles
    segdst = (pad_starts[None, :] + before).reshape(-1).astype(jnp.int32)
    block_start = jnp.arange(NB, dtype=jnp.int32) * BLK
    block_exp = jnp.minimum(jnp.sum(pad_ends[None, :] <= block_start[:, None], axis=1), N_EXPERTS - 1).astype(jnp.int32)
    n_used = (pad_ends[-1] // BLK).astype(jnp.int32).reshape(1)
    paddst = (pad_starts + total).astype(jnp.int32)
    padcnt = (padded - total).astype(jnp.int32)
    big = jnp.any(cnt_tiles >= SMALL_RUN, axis=1).astype(jnp.int32)
    has_rows = total > 0
    ids = jnp.arange(N_EXPERTS, dtype=jnp.int32)
    pos = (jnp.cumsum(has_rows) - has_rows).astype(jnp.int32)
    later = jnp.where(has_rows[None, :] & (ids[None, :] > ids[:, None]), ids[None, :], N_EXPERTS)
    nxt = jnp.min(later, axis=1)
    nxt = jnp.where(nxt < N_EXPERTS, nxt, -1).astype(jnp.int32)
    of_block = block_exp[:, None] == ids[None, :]
    pos_b = jnp.sum(jnp.where(of_block, pos[None, :], 0), axis=1).astype(jnp.int32)
    nxt_b = jnp.sum(jnp.where(of_block, nxt[None, :], 0), axis=1).astype(jnp.int32)
    return (segdst, cnt_tiles.reshape(-1).astype(jnp.int32), big, paddst, padcnt, block_exp, n_used,
            pos_b, nxt_b, NB * BLK)


def _rotary_tables(S):
    half = RET_HEAD_DIM // 2
    f32 = np.float32
    inv_freq = np.power(f32(ROPE_BASE), -np.arange(half, dtype=f32) / f32(half)).astype(f32)
    ang = (np.arange(S, dtype=f32)[:, None] * inv_freq[None, :]).astype(f32)
    cos, sin = np.cos(ang).astype(f32), np.sin(ang).astype(f32)
    return (jnp.asarray(np.concatenate([cos, cos], axis=-1), F32),
            jnp.asarray(np.concatenate([-sin, sin], axis=-1), F32))


def _layer(h, norm_mix_g, w_in, b_forget, w_out, norm_ffn_g, w_router, b_router,
           w_exp_in, b_exp_in, w_exp_out, b_exp_out, final_g):
    B, S, D = h.shape
    R, Fw = RET_WIDTH, FOX_WIDTH
    cos, sin = _rotary_tables(S)
    wr = w_in[:, :4 * R].astype(BF16)
    wfq, wfk, wfv = (w_in[:, 4 * R + i * Fw:4 * R + (i + 1) * Fw].astype(BF16) for i in range(3))
    wzt = jnp.zeros((FZ_ROWS, D), BF16).at[:FOX_HEADS, :].set(w_in[:, 4 * R + 3 * Fw:].astype(BF16).T)
    o_ret, kaug, qaug, fvt = _in_proj(
        h, norm_mix_g[None, :], cos, sin, wr, wfk, wfq.T, wfv.T, wzt, b_forget)
    o_fox = _fox_attn(qaug, kaug, fvt)

    T = B * S
    wo = w_out.astype(BF16)
    wrt = jnp.zeros((LANES, D), F32).at[:N_EXPERTS, :].set(w_router.T)
    wrh = wrt.astype(BF16)
    wrl = (wrt - wrh.astype(F32)).astype(BF16)
    br = jnp.zeros((LANES, 1), F32).at[:N_EXPERTS, 0].set(b_router)
    h1, u2, sel, cnt = _out_router(h.reshape(T, D), o_ret.reshape(T, R), o_fox.reshape(T, Fw),
                                   wo[:R], wo[R:], norm_ffn_g[None, :], wrh, wrl, br)
    cnt_tiles = cnt[:, 0, :N_EXPERTS].astype(jnp.int32)
    segdst, cnt_flat, big, paddst, padcnt, block_exp, n_used, exp_pos, exp_next, n_rows = _routing_tables(cnt_tiles)
    xs = _dispatch(u2, sel, segdst, cnt_flat, big, paddst, padcnt, n_used, n_rows)
    ys = _experts(xs, block_exp, n_used, exp_pos, exp_next, w_exp_in, b_exp_in, w_exp_out, b_exp_out)
    out = _combine(ys, sel, h1, final_g[None, :], segdst, cnt_flat, big)
    return out.reshape(B, S, D)


def kernel(x, norm_mix_g, w_in, b_forget, w_out, norm_ffn_g, w_router, b_router,
           w_exp_in, b_exp_in, w_exp_out, b_exp_out, norm_final_g):
    depth = w_in.shape[0]
    assert depth == 1, "the fused final RMSNorm assumes a single layer"
    return _layer(x, norm_mix_g[0], w_in[0], b_forget[0], w_out[0], norm_ffn_g[0], w_router[0], b_router[0],
                  w_exp_in[0], b_exp_in[0], w_exp_out[0], b_exp_out[0], norm_final_g)
```

```python
import functools

import numpy as np
import jax
import jax.numpy as jnp
from jax import lax
from jax.experimental import pallas as pl
from jax.experimental.pallas import tpu as pltpu

F32 = jnp.float32
BF16 = jnp.bfloat16

D_MODEL = 1024
RET_HEADS, RET_HEAD_DIM = 4, 128
RET_WIDTH = RET_HEADS * RET_HEAD_DIM
FOX_HEADS, FOX_HEAD_DIM = 8, 64
FOX_WIDTH = FOX_HEADS * FOX_HEAD_DIM
CHUNK = 64
ROPE_BASE = 10000.0
N_EXPERTS = 32
TOP_K = 4
D_FF = D_MODEL
SWIGLU_ALPHA = 1.702
SWIGLU_LIMIT = 7.0
RMS_EPS = 1e-5
GN_EPS = 1e-5

LANES = 128
SUBLANES = 8
ROW_TILES = D_MODEL // LANES
V7X_VMEM_BYTES = 64 * 1024 * 1024
VMEM_LIMIT = V7X_VMEM_BYTES * 3 // 4
VMEM_LIMIT_EXPERTS = V7X_VMEM_BYTES * 7 // 8

PROJ_TILE = 512
RET_BLOCK = 256
FOX_TQ = 512
FOX_TK = PROJ_TILE
AUG = 128
V_AUG = 80
FZ_ROWS = 16
LOG2E = 1.4426950408889634
MOE_TILE = 256
ROUTER_TILES = 4
SMALL_RUN = 64
EXPERT_BLOCK = 512
EXPERT_PASS_ROWS = 256

NT_DIMS = (((1,), (1,)), ((), ()))


def _split3(a):
    hi = a.astype(BF16)
    r1 = a - hi.astype(F32)
    mid = r1.astype(BF16)
    lo = (r1 - mid.astype(F32)).astype(BF16)
    return hi, mid, lo


def _dot(a, b):
    return jnp.dot(a, b, preferred_element_type=F32)


def _dot_nt(a, b):
    return lax.dot_general(a, b, NT_DIMS, preferred_element_type=F32)


def _rms(x, g):
    return x * lax.rsqrt(jnp.mean(x * x, axis=-1, keepdims=True) + RMS_EPS) * g


def _in_proj_kernel(x_ref, g_ref, cos_ref, sin_ref, wr_ref, wfk_ref, wfqt_ref, wfvt_ref, wzt_ref, bcol_ref,
                    selkf_ref, constk_ref, selqf_ref, constq_ref, dec_ref, qw_ref, kw_ref, cd_ref,
                    oret_ref, kaug_ref, qaug_ref, fvt_ref, ccol, state):
    TM = x_ref.shape[1]
    d, H = FOX_HEAD_DIM, FOX_HEADS

    @pl.when(pl.program_id(1) == 0)
    def _():
        ccol[...] = jnp.zeros_like(ccol)
        state[...] = jnp.zeros_like(state)

    u = _rms(x_ref[0], g_ref[...]).astype(BF16)
    r = _dot(u, wr_ref[...])
    cos, sin = cos_ref[...], sin_ref[...]
    k_scale = RET_HEAD_DIM ** -0.5
    L = RET_BLOCK
    for h in range(RET_HEADS):
        hs = slice(h * RET_HEAD_DIM, (h + 1) * RET_HEAD_DIM)
        q = r[:, hs]
        k = r[:, RET_WIDTH + h * RET_HEAD_DIM:RET_WIDTH + (h + 1) * RET_HEAD_DIM]
        qh = (q * cos + pltpu.roll(q, RET_HEAD_DIM // 2, 1) * sin).astype(BF16)
        kh = ((k * cos + pltpu.roll(k, RET_HEAD_DIM // 2, 1) * sin) * k_scale).astype(BF16)
        vh = r[:, 2 * RET_WIDTH + h * RET_HEAD_DIM:2 * RET_WIDTH + (h + 1) * RET_HEAD_DIM].astype(BF16)
        gh = r[:, 3 * RET_WIDTH + h * RET_HEAD_DIM:3 * RET_WIDTH + (h + 1) * RET_HEAD_DIM].astype(BF16)
        for rs in (slice(b0, b0 + L) for b0 in range(0, TM, L)):
            qb, kb, vb = qh[rs], kh[rs], vh[rs]
            scores = (_dot_nt(qb, kb) * dec_ref[h]).astype(BF16)
            st = state[h]
            o = _dot(scores, vb) + _dot((qb.astype(F32) * qw_ref[h]).astype(BF16), st.astype(BF16))
            kk = kb.astype(F32) * kw_ref[h]
            state[h] = st * cd_ref[h, 0:1, :] + _dot(kk.T.astype(BF16), vb)
            mu = jnp.mean(o, axis=-1, keepdims=True)
            oc = o - mu
            var = jnp.mean(oc * oc, axis=-1, keepdims=True)
            oret_ref[0, rs, hs] = (oc * lax.rsqrt(var + GN_EPS) * jax.nn.silu(gh[rs].astype(F32))).astype(BF16)
    fvt_ref[0, 0] = _dot_nt(wfvt_ref[...], u).astype(BF16)
    fk = _dot(u, wfk_ref[...])
    fqt = (_dot_nt(wfqt_ref[...], u) * (d ** -0.5 * LOG2E)).astype(BF16)
    zt = _dot_nt(wzt_ref[...], u)


    row = lax.broadcasted_iota(jnp.int32, (FZ_ROWS, TM), 0)
    lft = jnp.where(row < H, jax.nn.log_sigmoid(zt + bcol_ref[...]), 0.0)
    utri = (lax.broadcasted_iota(jnp.int32, (TM, TM), 0) <= lax.broadcasted_iota(jnp.int32, (TM, TM), 1)).astype(BF16)
    t3 = _split3(lft)
    f_col = _dot(t3[0], utri) + _dot(t3[1], utri) + _dot(t3[2], utri) + ccol[:, 0:1]
    ccol[...] = jnp.broadcast_to(f_col[:, TM - 1:TM], ccol.shape)

    pieces_t = jnp.concatenate(_split3(f_col * LOG2E), axis=0)
    for h in range(H):
        extra = _dot(selqf_ref[h], pieces_t) + constq_ref[...]
        qaug_ref[0, h, 0] = jnp.concatenate([fqt[h * d:(h + 1) * d, :], extra.astype(BF16)], axis=0)

    f_row = jnp.concatenate([f_col, jnp.zeros((LANES - FZ_ROWS, TM), F32)], axis=0).T
    n3 = _split3(f_row * -LOG2E)
    pieces = (n3[0].astype(F32) + pltpu.roll(n3[1].astype(F32), H, 1)
              + pltpu.roll(n3[2].astype(F32), 2 * H, 1)).astype(BF16)
    lane = lax.broadcasted_iota(jnp.int32, (TM, LANES), 1)
    for g in range(H // 2):
        bias = _dot(pieces, selkf_ref[g])
        kg = fk[:, g * 2 * d:(g + 1) * 2 * d]
        for o in range(2):
            kh = kg if o == 0 else pltpu.roll(kg, d, 1)
            extra = bias[:, o * AUG:(o + 1) * AUG] + constk_ref[...]
            kaug_ref[0, 2 * g + o] = jnp.where(lane < d, kh, extra).astype(BF16)


def _in_proj(x, g, cos, sin, wr, wfk, wfqt, wfvt, wzt, b_forget):
    B, S, D = x.shape
    TM = PROJ_TILE
    ns = S // TM
    selkf, constk, selqf, constq = _fox_prep_constants()
    bcol = jnp.zeros((FZ_ROWS, 1), F32).at[:FOX_HEADS, 0].set(b_forget)
    consts = (wr, wfk, wfqt, wfvt, wzt, bcol, selkf, constk, selqf, constq) + _retention_tables()
    const = lambda a: pl.BlockSpec(a.shape, lambda b, s: (0,) * a.ndim)
    tok = lambda w: pl.BlockSpec((1, TM, w), lambda b, s: (b, s, 0))
    out_shape = (
        jax.ShapeDtypeStruct((B, S, RET_WIDTH), BF16),
        jax.ShapeDtypeStruct((B, FOX_HEADS, S, AUG), BF16),
        jax.ShapeDtypeStruct((B, FOX_HEADS, ns, AUG, TM), BF16),
        jax.ShapeDtypeStruct((B, ns, FOX_WIDTH, TM), BF16),
    )
    return pl.pallas_call(
        _in_proj_kernel,
        grid=(B, ns),
        in_specs=[
            pl.BlockSpec((1, TM, D), lambda b, s: (b, s, 0)),
            pl.BlockSpec((1, D), lambda b, s: (0, 0)),
            pl.BlockSpec((TM, RET_HEAD_DIM), lambda b, s: (s, 0)),
            pl.BlockSpec((TM, RET_HEAD_DIM), lambda b, s: (s, 0)),
        ] + [const(a) for a in consts],
        out_specs=(
            tok(RET_WIDTH),
            pl.BlockSpec((1, FOX_HEADS, TM, AUG), lambda b, s: (b, 0, s, 0)),
            pl.BlockSpec((1, FOX_HEADS, 1, AUG, TM), lambda b, s: (b, 0, s, 0, 0)),
            pl.BlockSpec((1, 1, FOX_WIDTH, TM), lambda b, s: (b, s, 0, 0)),
        ),
        out_shape=out_shape,
        scratch_shapes=[pltpu.VMEM((FZ_ROWS, LANES), F32),
                        pltpu.VMEM((RET_HEADS, RET_HEAD_DIM, RET_HEAD_DIM), F32)],
        compiler_params=pltpu.CompilerParams(
            dimension_semantics=("arbitrary", "arbitrary"), vmem_limit_bytes=VMEM_LIMIT),
        name="in_proj",
    )(x, g, cos, sin, *consts)


def _fox_prep_constants():
    d, H = FOX_HEAD_DIM, FOX_HEADS
    selkf = np.zeros((H // 2, LANES, 2 * AUG), np.float32)
    constk = np.zeros((1, AUG), np.float32)
    selqf = np.zeros((H, d, 3 * FZ_ROWS), np.float32)
    constq = np.zeros((d, 1), np.float32)
    for p in range(3):
        constk[0, d + p] = 1.0
        constq[3 + p, 0] = 1.0
        for h in range(H):
            selkf[h // 2, p * H + h, (h % 2) * AUG + d + 3 + p] = 1.0
            selqf[h, p, p * FZ_ROWS + h] = 1.0
    return jnp.asarray(selkf, BF16), jnp.asarray(constk, F32), jnp.asarray(selqf, BF16), jnp.asarray(constq, F32)


def _retention_tables():
    L = RET_BLOCK
    f32 = np.float32
    log_gamma = np.log1p(-np.exp2(-5.0 - np.arange(RET_HEADS, dtype=f32))).astype(f32)
    p = np.arange(L, dtype=f32)
    dist = np.abs(p[:, None] - p[None, :])
    chunk = np.arange(L) // CHUNK
    allowed = (chunk[None, :] <= chunk[:, None]).astype(f32)
    dec = np.exp(log_gamma[:, None, None] * dist).astype(f32) * allowed
    lanes = lambda a: np.broadcast_to(a.astype(f32)[:, :, None], (RET_HEADS, L, RET_HEAD_DIM))
    qw = lanes(np.exp(log_gamma[:, None] * (p[None, :] + f32(1.0))))
    kw = lanes(np.exp(log_gamma[:, None] * (f32(L - 1.0) - p[None, :])))
    cd = np.broadcast_to(np.exp(log_gamma * f32(L)).astype(f32)[:, None, None], (RET_HEADS, SUBLANES, RET_HEAD_DIM))
    return tuple(jnp.asarray(a, F32) for a in (dec, qw, kw, cd))


def _fox_attn_kernel(q_ref, k_ref, v_ref, o_ref, s_a, s_b, s_c, cm_a, cm_b, cm_c, m_ref, acc_ref):
    T = FOX_TQ
    d = FOX_HEAD_DIM
    nq = q_ref.shape[2]
    ones_rows = (lax.broadcasted_iota(jnp.int32, (V_AUG - d, T), 0) == 0).astype(BF16)

    def scores(qi, j, s_ref, cm_ref):
        for hh in range(2):
            kj = k_ref[0, hh, pl.ds(pl.multiple_of(j * T, T), T), :]
            st = _dot(kj, q_ref[0, hh, qi])
            s_ref[hh] = st
            cm_ref[hh] = jnp.max(st, axis=0, keepdims=True)

    def consume(j, s_ref, cm_ref, masked):
        for hh in range(2):
            st = s_ref[hh]
            if masked:
                key = lax.broadcasted_iota(jnp.int32, (T, T), 0)
                qry = lax.broadcasted_iota(jnp.int32, (T, T), 1)
                st = jnp.where(key <= qry, st, -jnp.inf)
                cm = jnp.max(st, axis=0, keepdims=True)
            else:
                cm = cm_ref[hh]
            m = m_ref[hh]
            m_new = jnp.maximum(m, cm)
            p = jnp.exp2(st - m_new).astype(BF16)
            vj = jnp.concatenate([v_ref[0, j, hh * d:(hh + 1) * d, :], ones_rows], axis=0)
            acc_ref[hh] = jnp.exp2(m - m_new) * acc_ref[hh] + _dot(vj, p)
            m_ref[hh] = m_new

    def reset():
        m_ref[...] = jnp.full(m_ref.shape, -jnp.inf, F32)
        acc_ref[...] = jnp.zeros(acc_ref.shape, F32)

    def prefetch_next(qi):
        @pl.when(qi + 1 < nq)
        def _():
            scores(qi + 1, 0, s_c, cm_c)

    def finish(qi):
        outs = [acc_ref[hh, 0:d, :] / acc_ref[hh, d:d + 1, :] for hh in range(2)]
        o_ref[0, pl.ds(pl.multiple_of(qi * T, T), T), :] = jnp.concatenate(outs, axis=0).T.astype(BF16)

    reset()
    scores(0, 0, s_a, cm_a)
    prefetch_next(0)
    consume(0, s_a, cm_a, True)
    finish(0)

    def query_tile(qi, carry):
        reset()
        scores(qi, 1, s_a, cm_a)
        consume(0, s_c, cm_c, False)

        def pair(j):
            scores(qi, j + 1, s_b, cm_b)
            consume(j, s_a, cm_a, False)
            scores(qi, j + 2, s_a, cm_a)
            consume(j + 1, s_b, cm_b, False)

        def two_pairs(jj, c):
            pair(1 + 4 * jj)
            pair(3 + 4 * jj)
            return c

        def one_pair(jj, c):
            pair(1 + 4 * (n_pairs // 2) + 2 * jj)
            return c

        n_pairs = (qi - 1) // 2
        lax.fori_loop(0, n_pairs // 2, two_pairs, 0)
        lax.fori_loop(0, n_pairs % 2, one_pair, 0)

        @pl.when(qi % 2 == 1)
        def _():
            prefetch_next(qi)
            consume(qi, s_a, cm_a, True)

        @pl.when(qi % 2 == 0)
        def _():
            scores(qi, qi, s_b, cm_b)
            consume(qi - 1, s_a, cm_a, False)
            prefetch_next(qi)
            consume(qi, s_b, cm_b, True)

        finish(qi)
        return carry

    lax.fori_loop(1, nq, query_tile, 0)


def _fox_attn(qaug, kaug, fvt):
    B, H, S, _ = kaug.shape
    nk = S // FOX_TK
    nq = S // FOX_TQ
    score_buf = pltpu.VMEM((2, FOX_TK, FOX_TQ), F32)
    col_max = pltpu.VMEM((2, 1, FOX_TQ), F32)
    return pl.pallas_call(
        _fox_attn_kernel,
        grid=(B, H // 2),
        in_specs=[
            pl.BlockSpec((1, 2, nq, AUG, FOX_TQ), lambda b, p: (b, p, 0, 0, 0)),
            pl.BlockSpec((1, 2, S, AUG), lambda b, p: (b, p, 0, 0)),
            pl.BlockSpec((1, nk, 2 * FOX_HEAD_DIM, FOX_TK), lambda b, p: (b, 0, p, 0)),
        ],
        out_specs=pl.BlockSpec((1, S, 2 * FOX_HEAD_DIM), lambda b, p: (b, 0, p)),
        out_shape=jax.ShapeDtypeStruct((B, S, FOX_WIDTH), BF16),
        scratch_shapes=[
            score_buf, score_buf, score_buf, col_max, col_max, col_max,
            pltpu.VMEM((2, 1, FOX_TQ), F32), pltpu.VMEM((2, V_AUG, FOX_TQ), F32),
        ],
        compiler_params=pltpu.CompilerParams(
            dimension_semantics=("arbitrary",) * 2, vmem_limit_bytes=VMEM_LIMIT),
        name="fox_attn",
    )(qaug, kaug, fvt)


def _out_router_kernel(x_ref, oret_ref, ofox_ref, wor_ref, wof_ref, g_ref, wrh_ref, wrl_ref, br_ref,
                       h1_ref, u2_ref, sel_ref, cnt_ref):
    TM = MOE_TILE
    rows = lambda t: slice(t * TM, (t + 1) * TM)

    def out_proj(t):
        rs = rows(t)
        h1 = x_ref[rs] + _dot(oret_ref[rs], wor_ref[...]) + _dot(ofox_ref[rs], wof_ref[...])
        h1_ref[rs] = h1
        return h1

    def router_logits(t, h1):
        u2 = _rms(h1, g_ref[...])
        uh = u2.astype(BF16)
        u2_ref[rows(t)] = uh
        ul = (u2 - uh.astype(F32)).astype(BF16)
        return (_dot_nt(wrh_ref[...], uh) + _dot_nt(wrh_ref[...], ul) + _dot_nt(wrl_ref[...], uh)
                + br_ref[...])

    def top_k(t, logits):
        rs = rows(t)
        E = N_EXPERTS
        row = lax.broadcasted_iota(jnp.int32, (E, TM), 0).astype(F32)
        l = logits
        picks, vals = [], []
        for _ in range(TOP_K):
            m = jnp.max(l, axis=0, keepdims=True)
            idx = jnp.min(jnp.where(l == m, row, float(LANES)), axis=0, keepdims=True)
            pick = row == idx
            picks.append(pick)
            vals.append(m)
            l = jnp.where(pick, -jnp.inf, l)
        exps = [jnp.exp(v - vals[0]) for v in vals]
        den = exps[0] + exps[1] + exps[2] + exps[3]
        sel_t = jnp.full((E, TM), -1.0, F32)
        for pick, e in zip(picks, exps):
            sel_t = jnp.where(pick, e / den, sel_t)
        sel_t = jnp.concatenate([sel_t, jnp.full((LANES - E, TM), -1.0, F32)], axis=0)
        sel = sel_t.T
        sel_ref[rs] = sel
        cnt = jnp.sum((sel >= 0.0).astype(F32), axis=0, keepdims=True)
        cnt_ref[t] = jnp.broadcast_to(cnt, (SUBLANES, LANES))

    n = ROUTER_TILES
    h1s, lgs = {0: out_proj(0)}, {}
    for t in range(1, n + 2):
        if t < n:
            h1s[t] = out_proj(t)
        if 1 <= t <= n:
            lgs[t - 1] = router_logits(t - 1, h1s.pop(t - 1))
        if t >= 2:
            top_k(t - 2, lgs.pop(t - 2))


def _out_router(x2, o_ret, o_fox, wor, wof, g, wrh, wrl, br):
    T, D = x2.shape
    TM = MOE_TILE * ROUTER_TILES
    nT = T // MOE_TILE
    const = lambda a: pl.BlockSpec(a.shape, lambda i: (0,) * a.ndim)
    tok = lambda w: pl.BlockSpec((TM, w), lambda i: (i, 0))
    return pl.pallas_call(
        _out_router_kernel,
        grid=(T // TM,),
        in_specs=[tok(D), tok(RET_WIDTH), tok(FOX_WIDTH), const(wor), const(wof), const(g),
                  const(wrh), const(wrl), const(br)],
        out_specs=(tok(D), tok(D), tok(LANES),
                   pl.BlockSpec((ROUTER_TILES, SUBLANES, LANES), lambda i: (i, 0, 0))),
        out_shape=(
            jax.ShapeDtypeStruct((T, D), F32),
            jax.ShapeDtypeStruct((T, D), BF16),
            jax.ShapeDtypeStruct((T, LANES), F32),
            jax.ShapeDtypeStruct((nT, SUBLANES, LANES), F32),
        ),
        compiler_params=pltpu.CompilerParams(dimension_semantics=("arbitrary",)),
        name="out_router",
    )(x2, o_ret, o_fox, wor, wof, g, wrh, wrl, br)


def _tile_sort(sel):
    TM = sel.shape[0]
    NS = TOP_K * TM
    maskf = (sel >= 0.0).astype(F32)
    mask = maskf.astype(BF16)
    ri = lax.broadcasted_iota(jnp.int32, (TM, TM), 0)
    ci = lax.broadcasted_iota(jnp.int32, (TM, TM), 1)
    rank1 = maskf * _dot((ri >= ci).astype(BF16), mask)
    cnt = jnp.sum(maskf, axis=0, keepdims=True)
    ei = lax.broadcasted_iota(jnp.int32, (LANES, LANES), 0)
    ej = lax.broadcasted_iota(jnp.int32, (LANES, LANES), 1)
    cnt8 = jnp.broadcast_to(cnt, (SUBLANES, LANES)).astype(BF16)
    off = _dot(cnt8, (ei < ej).astype(BF16))[0:1, :]
    slot = lax.broadcasted_iota(jnp.int32, (NS, LANES), 0).astype(F32)
    esel = ((slot >= off) & (slot < off + cnt)).astype(BF16)
    return rank1.astype(BF16), esel, off, cnt


def _segment_wait(slot, local, remote_rows, sem, to_remote):
    whole = local.at[slot]
    rem = remote_rows.at[pl.ds(0, whole.shape[0]), :]
    cp = (pltpu.make_async_copy(whole, rem, sem.at[slot]) if to_remote
          else pltpu.make_async_copy(rem, whole, sem.at[slot]))
    cp.wait()


def _segment_dmas(step, slot, segdst_ref, cnt_ref, big_ref, local, remote_rows, sem, to_remote):
    big = big_ref[step] != 0
    for cond, top_bit in ((big, MOE_TILE), (jnp.logical_not(big), SMALL_RUN // 2)):
        pl.when(cond)(functools.partial(
            _segment_dma_path, step, slot, segdst_ref, cnt_ref, local, remote_rows, sem, to_remote, top_bit))


def _segment_dma_path(step, slot, segdst_ref, cnt_ref, local, remote_rows, sem, to_remote, top_bit):
    def body(e, off):
        c = cnt_ref[step * N_EXPERTS + e]
        dst = segdst_ref[step * N_EXPERTS + e]
        bit = top_bit
        while bit >= 1:
            done = c & (~(2 * bit - 1))

            @pl.when((c & bit) != 0)
            def _(bit=bit, done=done):
                loc = local.at[slot, pl.ds((off + done) * ROW_TILES, bit * ROW_TILES), :]
                rem = remote_rows.at[pl.ds((dst + done) * ROW_TILES, bit * ROW_TILES), :]
                cp = (pltpu.make_async_copy(loc, rem, sem.at[slot]) if to_remote
                      else pltpu.make_async_copy(rem, loc, sem.at[slot]))
                cp.start()
            bit //= 2
        return off + c

    off = 0
    for e in range(N_EXPERTS):
        off = body(e, off)


def _dispatch_kernel(segdst_ref, cnt_ref, big_ref, paddst_ref, padcnt_ref, nused_ref, u2_ref, sel_ref, xs_ref,
                     buf, zbuf, sems, zsem):
    i = pl.program_id(0)
    last = pl.num_programs(0) - 1
    slot = i % 2
    TM = MOE_TILE
    NS = TOP_K * TM
    rank1, esel, off, _ = _tile_sort(sel_ref[...])
    slot_id = lax.broadcasted_iota(jnp.int32, (NS, 1), 0).astype(F32)
    r_s = slot_id - jnp.sum(esel.astype(F32) * off, axis=1, keepdims=True)
    perm = (_dot_nt(esel, rank1) == r_s + 1.0).astype(BF16)

    @pl.when(i >= 2)
    def _():
        _segment_wait(slot, buf, xs_ref, sems, True)

    u2 = u2_ref[...]
    for c in range(NS // TM):
        rows = _dot(perm[c * TM:(c + 1) * TM], u2)
        for j in range(ROW_TILES):
            buf[slot, pl.ds(c * TM * ROW_TILES + j, TM, stride=ROW_TILES), :] = rows[:, j * LANES:(j + 1) * LANES]
    _segment_dmas(i, slot, segdst_ref, cnt_ref, big_ref, buf, xs_ref, sems, True)

    @pl.when(i == last)
    def _():
        @pl.when(i >= 1)
        def _():
            _segment_wait(1 - slot, buf, xs_ref, sems, True)
        _segment_wait(slot, buf, xs_ref, sems, True)
        zbuf[...] = jnp.zeros_like(zbuf)
        half = EXPERT_BLOCK // 2 * ROW_TILES
        n_blocks = xs_ref.shape[0] // (EXPERT_BLOCK * ROW_TILES)
        for wait in (False, True):
            def unused(hb, carry, wait=wait):
                cp = pltpu.make_async_copy(zbuf, xs_ref.at[pl.ds(hb * half, half), :], zsem.at[0])
                cp.wait() if wait else cp.start()
                return carry
            lax.fori_loop(2 * nused_ref[0], 2 * n_blocks, unused, 0)


            def body(e, carry, wait=wait):
                c = padcnt_ref[e]
                dst = paddst_ref[e]
                bit = EXPERT_BLOCK // 2
                while bit >= 1:
                    done = c & (~(2 * bit - 1))

                    @pl.when((c & bit) != 0)
                    def _(bit=bit, done=done):
                        cp = pltpu.make_async_copy(
                            zbuf.at[pl.ds(0, bit * ROW_TILES), :],
                            xs_ref.at[pl.ds((dst + done) * ROW_TILES, bit * ROW_TILES), :], zsem.at[0])
                        cp.wait() if wait else cp.start()
                    bit //= 2
                return carry
            lax.fori_loop(0, N_EXPERTS, body, 0)


def _dispatch(u2, sel, segdst, cnt, big, paddst, padcnt, n_used, n_rows):
    T, D = u2.shape
    TM = MOE_TILE
    NS = TOP_K * TM
    return pl.pallas_call(
        _dispatch_kernel,
        grid_spec=pltpu.PrefetchScalarGridSpec(
            num_scalar_prefetch=6,
            grid=(T // TM,),
            in_specs=[pl.BlockSpec((TM, D), lambda i, *_: (i, 0)),
                      pl.BlockSpec((TM, LANES), lambda i, *_: (i, 0))],
            out_specs=pl.BlockSpec(memory_space=pl.ANY),
            scratch_shapes=[pltpu.VMEM((2, NS * ROW_TILES, LANES), F32),
                            pltpu.VMEM((EXPERT_BLOCK // 2 * ROW_TILES, LANES), F32),
                            pltpu.SemaphoreType.DMA((2,)), pltpu.SemaphoreType.DMA((1,))],
        ),
        out_shape=jax.ShapeDtypeStruct((n_rows * ROW_TILES, LANES), F32),
        compiler_params=pltpu.CompilerParams(
            dimension_semantics=("arbitrary",), vmem_limit_bytes=VMEM_LIMIT),
        name="dispatch",
    )(segdst, cnt, big, paddst, padcnt, n_used, u2, sel)


def _expert_kernel(bexp_ref, nused_ref, epos_ref, enext_ref, xs_ref, w1_hbm, b1_ref, w2_hbm, b2_ref, ys_ref,
                   w1f, w2f, w1b, w2b, wsem):
    b = pl.program_id(0)
    BLK = EXPERT_BLOCK
    used = b < nused_ref[0]

    def weight_copies(e, slot):
        return (pltpu.make_async_copy(w1_hbm.at[e], w1f.at[slot], wsem.at[0, slot]),
                pltpu.make_async_copy(w2_hbm.at[e], w2f.at[slot], wsem.at[1, slot]))

    @pl.when(used)
    def _():
        e = bexp_ref[b]
        prev = bexp_ref[jnp.maximum(b - 1, 0)]
        slot = epos_ref[b] % 2

        @pl.when(b == 0)
        def _():
            for cp in weight_copies(e, slot):
                cp.start()

        @pl.when((b == 0) | (e != prev))
        def _():
            nxt = enext_ref[b]

            @pl.when(nxt >= 0)
            def _():
                for cp in weight_copies(nxt, 1 - slot):
                    cp.start()

            for cp in weight_copies(e, slot):
                cp.wait()
            rows = LANES

            def cast(r, carry):
                sl = pl.ds(pl.multiple_of(r * rows, rows), rows)
                w1b[sl, :] = w1f[slot, sl, :].astype(BF16)
                w2b[sl, :] = w2f[slot, sl, :].astype(BF16)
                return carry
            lax.fori_loop(0, D_MODEL // rows, cast, 0)

        R = EXPERT_PASS_ROWS
        for rp in range(BLK // R):
            r0 = rp * R * ROW_TILES
            x = jnp.concatenate([xs_ref[pl.ds(r0 + j, R, stride=ROW_TILES), :] for j in range(ROW_TILES)],
                                axis=1).astype(BF16)
            h = _dot(x, w1b[...]) + b1_ref[0]
            glu = jnp.minimum(h[:, :D_FF], SWIGLU_LIMIT)
            lin = jnp.clip(h[:, D_FF:], -SWIGLU_LIMIT, SWIGLU_LIMIT)
            act = glu * jax.nn.sigmoid(SWIGLU_ALPHA * glu) * (lin + 1.0)
            y = _dot(act.astype(BF16), w2b[...]) + b2_ref[0]
            for j in range(ROW_TILES):
                ys_ref[pl.ds(r0 + j, R, stride=ROW_TILES), :] = y[:, j * LANES:(j + 1) * LANES]

    @pl.when(jnp.logical_not(used))
    def _():
        ys_ref[...] = jnp.zeros_like(ys_ref)


def _experts(xs, block_exp, n_used, exp_pos, exp_next, w1, b1, w2, b2):
    BLK = EXPERT_BLOCK
    NB = xs.shape[0] // (BLK * ROW_TILES)
    blk = lambda b, nused: jnp.minimum(b, nused[0] - 1)
    return pl.pallas_call(
        _expert_kernel,
        grid_spec=pltpu.PrefetchScalarGridSpec(
            num_scalar_prefetch=4,
            grid=(NB,),
            in_specs=[
                pl.BlockSpec((BLK * ROW_TILES, LANES), lambda b, bexp, nused, *_: (blk(b, nused), 0)),
                pl.BlockSpec(memory_space=pl.ANY),
                pl.BlockSpec((1, 1, 2 * D_FF), lambda b, bexp, nused, *_: (bexp[blk(b, nused)], 0, 0)),
                pl.BlockSpec(memory_space=pl.ANY),
                pl.BlockSpec((1, 1, D_MODEL), lambda b, bexp, nused, *_: (bexp[blk(b, nused)], 0, 0)),
            ],
            out_specs=pl.BlockSpec((BLK * ROW_TILES, LANES), lambda b, *_: (b, 0)),
            scratch_shapes=[pltpu.VMEM((2, D_MODEL, 2 * D_FF), F32), pltpu.VMEM((2, D_FF, D_MODEL), F32),
                            pltpu.VMEM((D_MODEL, 2 * D_FF), BF16), pltpu.VMEM((D_FF, D_MODEL), BF16),
                            pltpu.SemaphoreType.DMA((2, 2))],
        ),
        out_shape=jax.ShapeDtypeStruct(xs.shape, F32),
        compiler_params=pltpu.CompilerParams(
            dimension_semantics=("arbitrary",), vmem_limit_bytes=VMEM_LIMIT_EXPERTS),
        name="experts",
    )(block_exp, n_used, exp_pos, exp_next, xs, w1, b1[:, None, :], w2, b2[:, None, :])


def _combine_kernel(segdst_ref, cnt_ref, big_ref, ys_ref, sel_ref, h1_ref, g_ref, out_ref, buf, sems):
    i = pl.program_id(0)
    n = pl.num_programs(0)
    slot = i % 2
    TM = MOE_TILE
    NS = TOP_K * TM

    @pl.when(i == 0)
    def _():
        _segment_dmas(i, slot, segdst_ref, cnt_ref, big_ref, buf, ys_ref, sems, False)

    @pl.when(i + 1 < n)
    def _():
        _segment_dmas(i + 1, 1 - slot, segdst_ref, cnt_ref, big_ref, buf, ys_ref, sems, False)

    sel = sel_ref[...]
    rank1, esel, off, _ = _tile_sort(sel)
    gate = jnp.maximum(sel, 0.0).astype(BF16)
    o3 = _split3(jnp.broadcast_to(off, (SUBLANES, LANES)))
    off_s = (_dot_nt(o3[0], esel) + _dot_nt(o3[1], esel) + _dot_nt(o3[2], esel))[0:1, :]
    r_s = lax.broadcasted_iota(jnp.int32, (1, NS), 1).astype(F32) - off_s
    hit = _dot_nt(rank1, esel) == r_s + 1.0
    unperm = jnp.where(hit, _dot_nt(gate, esel), 0.0).astype(BF16)

    _segment_wait(slot, buf, ys_ref, sems, False)
    y = jnp.concatenate([buf[slot, pl.ds(j, NS, stride=ROW_TILES), :] for j in range(ROW_TILES)],
                        axis=1).astype(BF16)
    h2 = h1_ref[...] + _dot(unperm, y)
    out_ref[...] = _rms(h2, g_ref[...])


def _combine(ys, sel, h1, g, segdst, cnt, big):
    T, D = h1.shape
    TM = MOE_TILE
    NS = TOP_K * TM
    return pl.pallas_call(
        _combine_kernel,
        grid_spec=pltpu.PrefetchScalarGridSpec(
            num_scalar_prefetch=3,
            grid=(T // TM,),
            in_specs=[pl.BlockSpec(memory_space=pl.ANY),
                      pl.BlockSpec((TM, LANES), lambda i, *_: (i, 0)),
                      pl.BlockSpec((TM, D), lambda i, *_: (i, 0)),
                      pl.BlockSpec((1, D), lambda i, *_: (0, 0))],
            out_specs=pl.BlockSpec((TM, D), lambda i, *_: (i, 0)),
            scratch_shapes=[pltpu.VMEM((2, NS * ROW_TILES, LANES), F32), pltpu.SemaphoreType.DMA((2,))],
        ),
        out_shape=jax.ShapeDtypeStruct((T, D), F32),
        compiler_params=pltpu.CompilerParams(
            dimension_semantics=("arbitrary",), vmem_limit_bytes=VMEM_LIMIT),
        name="combine",
    )(segdst, cnt, big, ys, sel, h1, g)


def _routing_tables(cnt_tiles):
    BLK = EXPERT_BLOCK
    nT = cnt_tiles.shape[0]
    A = nT * MOE_TILE * TOP_K
    NB = A // BLK + N_EXPERTS
    total = jnp.sum(cnt_tiles, axis=0)
    padded = (total + BLK - 1) // BLK * BLK
    pad_ends = jnp.cumsum(padded)
    pad_starts = pad_ends - padded
    before = jnp.cumsum(cnt_tiles, axis=0) - cnt_tiles
    segdst = (pad_starts[None, :] + before).reshape(-1).astype(jnp.int32)
    block_start = jnp.arange(NB, dtype=jnp.int32) * BLK
    block_exp = jnp.minimum(jnp.sum(pad_ends[None, :] <= block_start[:, None], axis=1), N_EXPERTS - 1).astype(jnp.int32)
    n_used = (pad_ends[-1] // BLK).astype(jnp.int32).reshape(1)
    paddst = (pad_starts + total).astype(jnp.int32)
    padcnt = (padded - total).astype(jnp.int32)
    big = jnp.any(cnt_tiles >= SMALL_RUN, axis=1).astype(jnp.int32)
    has_rows = total > 0
    ids = jnp.arange(N_EXPERTS, dtype=jnp.int32)
    pos = (jnp.cumsum(has_rows) - has_rows).astype(jnp.int32)
    later = jnp.where(has_rows[None, :] & (ids[None, :] > ids[:, None]), ids[None, :], N_EXPERTS)
    nxt = jnp.min(later, axis=1)
    nxt = jnp.where(nxt < N_EXPERTS, nxt, -1).astype(jnp.int32)
    of_block = block_exp[:, None] == ids[None, :]
    pos_b = jnp.sum(jnp.where(of_block, pos[None, :], 0), axis=1).astype(jnp.int32)
    nxt_b = jnp.sum(jnp.where(of_block, nxt[None, :], 0), axis=1).astype(jnp.int32)
    return (segdst, cnt_tiles.reshape(-1).astype(jnp.int32), big, paddst, padcnt, block_exp, n_used,
            pos_b, nxt_b, NB * BLK)


def _rotary_tables(S):
    half = RET_HEAD_DIM // 2
    f32 = np.float32
    inv_freq = np.power(f32(ROPE_BASE), -np.arange(half, dtype=f32) / f32(half)).astype(f32)
    ang = (np.arange(S, dtype=f32)[:, None] * inv_freq[None, :]).astype(f32)
    cos, sin = np.cos(ang).astype(f32), np.sin(ang).astype(f32)
    return (jnp.asarray(np.concatenate([cos, cos], axis=-1), F32),
            jnp.asarray(np.concatenate([-sin, sin], axis=-1), F32))


def _layer(h, norm_mix_g, w_in, b_forget, w_out, norm_ffn_g, w_router, b_router,
           w_exp_in, b_exp_in, w_exp_out, b_exp_out, final_g):
    B, S, D = h.shape
    R, Fw = RET_WIDTH, FOX_WIDTH
    cos, sin = _rotary_tables(S)
    wr = w_in[:, :4 * R].astype(BF16)
    wfq, wfk, wfv = (w_in[:, 4 * R + i * Fw:4 * R + (i + 1) * Fw].astype(BF16) for i in range(3))
    wzt = jnp.zeros((FZ_ROWS, D), BF16).at[:FOX_HEADS, :].set(w_in[:, 4 * R + 3 * Fw:].astype(BF16).T)
    o_ret, kaug, qaug, fvt = _in_proj(
        h, norm_mix_g[None, :], cos, sin, wr, wfk, wfq.T, wfv.T, wzt, b_forget)
    o_fox = _fox_attn(qaug, kaug, fvt)

    T = B * S
    wo = w_out.astype(BF16)
    wrt = w_router.T.astype(F32)
    wrh = wrt.astype(BF16)
    wrl = (wrt - wrh.astype(F32)).astype(BF16)
    br = b_router.astype(F32)[:, None]
    h1, u2, sel, cnt = _out_router(h.reshape(T, D), o_ret.reshape(T, R), o_fox.reshape(T, Fw),
                                   wo[:R], wo[R:], norm_ffn_g[None, :], wrh, wrl, br)
    cnt_tiles = cnt[:, 0, :N_EXPERTS].astype(jnp.int32)
    segdst, cnt_flat, big, paddst, padcnt, block_exp, n_used, exp_pos, exp_next, n_rows = _routing_tables(cnt_tiles)
    xs = _dispatch(u2, sel, segdst, cnt_flat, big, paddst, padcnt, n_used, n_rows)
    ys = _experts(xs, block_exp, n_used, exp_pos, exp_next, w_exp_in, b_exp_in, w_exp_out, b_exp_out)
    out = _combine(ys, sel, h1, final_g[None, :], segdst, cnt_flat, big)
    return out.reshape(B, S, D)


def kernel(x, norm_mix_g, w_in, b_forget, w_out, norm_ffn_g, w_router, b_router,
           w_exp_in, b_exp_in, w_exp_out, b_exp_out, norm_final_g):
    depth = w_in.shape[0]
    assert depth == 1, "the fused final RMSNorm assumes a single layer"
    return _layer(x, norm_mix_g[0], w_in[0], b_forget[0], w_out[0], norm_ffn_g[0], w_router[0], b_router[0],
                  w_exp_in[0], b_exp_in[0], w_exp_out[0], b_exp_out[0], norm_final_g)
```

```python
import functools

import numpy as np
import jax
import jax.numpy as jnp
from jax import lax
from jax.experimental import pallas as pl
from jax.experimental.pallas import tpu as pltpu

F32 = jnp.float32
BF16 = jnp.bfloat16

D_MODEL = 1024
RET_HEADS, RET_HEAD_DIM = 4, 128
RET_WIDTH = RET_HEADS * RET_HEAD_DIM
FOX_HEADS, FOX_HEAD_DIM = 8, 64
FOX_WIDTH = FOX_HEADS * FOX_HEAD_DIM
CHUNK = 64
ROPE_BASE = 10000.0
N_EXPERTS = 32
TOP_K = 4
D_FF = D_MODEL
SWIGLU_ALPHA = 1.702
SWIGLU_LIMIT = 7.0
RMS_EPS = 1e-5
GN_EPS = 1e-5

LANES = 128
SUBLANES = 8
ROW_TILES = D_MODEL // LANES
V7X_VMEM_BYTES = 64 * 1024 * 1024
VMEM_LIMIT = V7X_VMEM_BYTES * 3 // 4
VMEM_LIMIT_EXPERTS = V7X_VMEM_BYTES * 7 // 8

PROJ_TILE = 512
RET_BLOCK = 256
FOX_TQ = 512
FOX_TK = PROJ_TILE
AUG = 128
V_AUG = 80
FZ_ROWS = 16
LOG2E = 1.4426950408889634
MOE_TILE = 256
ROUTER_TILES = 4
SMALL_RUN = 64
EXPERT_BLOCK = 512
EXPERT_PASS_ROWS = 256

NT_DIMS = (((1,), (1,)), ((), ()))


def _split3(a):
    hi = a.astype(BF16)
    r1 = a - hi.astype(F32)
    mid = r1.astype(BF16)
    lo = (r1 - mid.astype(F32)).astype(BF16)
    return hi, mid, lo


def _dot(a, b):
    return jnp.dot(a, b, preferred_element_type=F32)


def _dot_nt(a, b):
    return lax.dot_general(a, b, NT_DIMS, preferred_element_type=F32)


def _rms(x, g):
    return x * lax.rsqrt(jnp.mean(x * x, axis=-1, keepdims=True) + RMS_EPS) * g


def _in_proj_kernel(x_ref, g_ref, cos_ref, sin_ref, wr_ref, wfk_ref, wfqt_ref, wfvt_ref, wzt_ref, bcol_ref,
                    selkf_ref, constk_ref, selqf_ref, constq_ref, dec_ref, qw_ref, kw_ref, cd_ref,
                    oret_ref, kaug_ref, qaug_ref, fvt_ref, ccol, state):
    TM = x_ref.shape[1]
    d, H = FOX_HEAD_DIM, FOX_HEADS

    @pl.when(pl.program_id(1) == 0)
    def _():
        ccol[...] = jnp.zeros_like(ccol)
        state[...] = jnp.zeros_like(state)

    u = _rms(x_ref[0], g_ref[...]).astype(BF16)
    r = _dot(u, wr_ref[...])
    cos, sin = cos_ref[...], sin_ref[...]
    k_scale = RET_HEAD_DIM ** -0.5
    L = RET_BLOCK
    for h in range(RET_HEADS):
        hs = slice(h * RET_HEAD_DIM, (h + 1) * RET_HEAD_DIM)
        q = r[:, hs]
        k = r[:, RET_WIDTH + h * RET_HEAD_DIM:RET_WIDTH + (h + 1) * RET_HEAD_DIM]
        qh = (q * cos + pltpu.roll(q, RET_HEAD_DIM // 2, 1) * sin).astype(BF16)
        kh = ((k * cos + pltpu.roll(k, RET_HEAD_DIM // 2, 1) * sin) * k_scale).astype(BF16)
        vh = r[:, 2 * RET_WIDTH + h * RET_HEAD_DIM:2 * RET_WIDTH + (h + 1) * RET_HEAD_DIM].astype(BF16)
        gh = r[:, 3 * RET_WIDTH + h * RET_HEAD_DIM:3 * RET_WIDTH + (h + 1) * RET_HEAD_DIM].astype(BF16)
        for rs in (slice(b0, b0 + L) for b0 in range(0, TM, L)):
            qb, kb, vb = qh[rs], kh[rs], vh[rs]
            scores = (_dot_nt(qb, kb) * dec_ref[h]).astype(BF16)
            st = state[h]
            o = _dot(scores, vb) + _dot((qb.astype(F32) * qw_ref[h]).astype(BF16), st.astype(BF16))
            kk = kb.astype(F32) * kw_ref[h]
            state[h] = st * cd_ref[h, 0:1, :] + _dot(kk.T.astype(BF16), vb)
            mu = jnp.mean(o, axis=-1, keepdims=True)
            oc = o - mu
            var = jnp.mean(oc * oc, axis=-1, keepdims=True)
            oret_ref[0, rs, hs] = (oc * lax.rsqrt(var + GN_EPS) * jax.nn.silu(gh[rs].astype(F32))).astype(BF16)
    fvt_ref[0, 0] = _dot_nt(wfvt_ref[...], u).astype(BF16)
    fk = _dot(u, wfk_ref[...])
    fqt = (_dot_nt(wfqt_ref[...], u) * (d ** -0.5 * LOG2E)).astype(BF16)
    zt = _dot_nt(wzt_ref[...], u)


    row = lax.broadcasted_iota(jnp.int32, (FZ_ROWS, TM), 0)
    lft = jnp.where(row < H, jax.nn.log_sigmoid(zt + bcol_ref[...]), 0.0)
    utri = (lax.broadcasted_iota(jnp.int32, (TM, TM), 0) <= lax.broadcasted_iota(jnp.int32, (TM, TM), 1)).astype(BF16)
    t3 = _split3(lft)
    f_col = _dot(t3[0], utri) + _dot(t3[1], utri) + _dot(t3[2], utri) + ccol[:, 0:1]
    ccol[...] = jnp.broadcast_to(f_col[:, TM - 1:TM], ccol.shape)

    pieces_t = jnp.concatenate(_split3(f_col * LOG2E), axis=0)
    for h in range(H):
        extra = _dot(selqf_ref[h], pieces_t) + constq_ref[...]
        qaug_ref[0, h, 0] = jnp.concatenate([fqt[h * d:(h + 1) * d, :], extra.astype(BF16)], axis=0)

    f_row = jnp.concatenate([f_col, jnp.zeros((LANES - FZ_ROWS, TM), F32)], axis=0).T
    n3 = _split3(f_row * -LOG2E)
    pieces = (n3[0].astype(F32) + pltpu.roll(n3[1].astype(F32), H, 1)
              + pltpu.roll(n3[2].astype(F32), 2 * H, 1)).astype(BF16)
    lane = lax.broadcasted_iota(jnp.int32, (TM, LANES), 1)
    for g in range(H // 2):
        bias = _dot(pieces, selkf_ref[g])
        kg = fk[:, g * 2 * d:(g + 1) * 2 * d]
        for o in range(2):
            kh = kg if o == 0 else pltpu.roll(kg, d, 1)
            extra = bias[:, o * AUG:(o + 1) * AUG] + constk_ref[...]
            kaug_ref[0, 2 * g + o] = jnp.where(lane < d, kh, extra).astype(BF16)


def _in_proj(x, g, cos, sin, wr, wfk, wfqt, wfvt, wzt, b_forget):
    B, S, D = x.shape
    TM = PROJ_TILE
    ns = S // TM
    selkf, constk, selqf, constq = _fox_prep_constants()
    bcol = jnp.zeros((FZ_ROWS, 1), F32).at[:FOX_HEADS, 0].set(b_forget)
    consts = (wr, wfk, wfqt, wfvt, wzt, bcol, selkf, constk, selqf, constq) + _retention_tables()
    const = lambda a: pl.BlockSpec(a.shape, lambda b, s: (0,) * a.ndim)
    tok = lambda w: pl.BlockSpec((1, TM, w), lambda b, s: (b, s, 0))
    out_shape = (
        jax.ShapeDtypeStruct((B, S, RET_WIDTH), BF16),
        jax.ShapeDtypeStruct((B, FOX_HEADS, S, AUG), BF16),
        jax.ShapeDtypeStruct((B, FOX_HEADS, ns, AUG, TM), BF16),
        jax.ShapeDtypeStruct((B, ns, FOX_WIDTH, TM), BF16),
    )
    return pl.pallas_call(
        _in_proj_kernel,
        grid=(B, ns),
        in_specs=[
            pl.BlockSpec((1, TM, D), lambda b, s: (b, s, 0)),
            pl.BlockSpec((1, D), lambda b, s: (0, 0)),
            pl.BlockSpec((TM, RET_HEAD_DIM), lambda b, s: (s, 0)),
            pl.BlockSpec((TM, RET_HEAD_DIM), lambda b, s: (s, 0)),
        ] + [const(a) for a in consts],
        out_specs=(
            tok(RET_WIDTH),
            pl.BlockSpec((1, FOX_HEADS, TM, AUG), lambda b, s: (b, 0, s, 0)),
            pl.BlockSpec((1, FOX_HEADS, 1, AUG, TM), lambda b, s: (b, 0, s, 0, 0)),
            pl.BlockSpec((1, 1, FOX_WIDTH, TM), lambda b, s: (b, s, 0, 0)),
        ),
        out_shape=out_shape,
        scratch_shapes=[pltpu.VMEM((FZ_ROWS, LANES), F32),
                        pltpu.VMEM((RET_HEADS, RET_HEAD_DIM, RET_HEAD_DIM), F32)],
        compiler_params=pltpu.CompilerParams(
            dimension_semantics=("arbitrary", "arbitrary"), vmem_limit_bytes=VMEM_LIMIT),
        name="in_proj",
    )(x, g, cos, sin, *consts)


def _fox_prep_constants():
    d, H = FOX_HEAD_DIM, FOX_HEADS
    selkf = np.zeros((H // 2, LANES, 2 * AUG), np.float32)
    constk = np.zeros((1, AUG), np.float32)
    selqf = np.zeros((H, d, 3 * FZ_ROWS), np.float32)
    constq = np.zeros((d, 1), np.float32)
    for p in range(3):
        constk[0, d + p] = 1.0
        constq[3 + p, 0] = 1.0
        for h in range(H):
            selkf[h // 2, p * H + h, (h % 2) * AUG + d + 3 + p] = 1.0
            selqf[h, p, p * FZ_ROWS + h] = 1.0
    return jnp.asarray(selkf, BF16), jnp.asarray(constk, F32), jnp.asarray(selqf, BF16), jnp.asarray(constq, F32)


def _retention_tables():
    L = RET_BLOCK
    f32 = np.float32
    log_gamma = np.log1p(-np.exp2(-5.0 - np.arange(RET_HEADS, dtype=f32))).astype(f32)
    p = np.arange(L, dtype=f32)
    dist = np.abs(p[:, None] - p[None, :])
    chunk = np.arange(L) // CHUNK
    allowed = (chunk[None, :] <= chunk[:, None]).astype(f32)
    dec = np.exp(log_gamma[:, None, None] * dist).astype(f32) * allowed
    lanes = lambda a: np.broadcast_to(a.astype(f32)[:, :, None], (RET_HEADS, L, RET_HEAD_DIM))
    qw = lanes(np.exp(log_gamma[:, None] * (p[None, :] + f32(1.0))))
    kw = lanes(np.exp(log_gamma[:, None] * (f32(L - 1.0) - p[None, :])))
    cd = np.broadcast_to(np.exp(log_gamma * f32(L)).astype(f32)[:, None, None], (RET_HEADS, SUBLANES, RET_HEAD_DIM))
    return tuple(jnp.asarray(a, F32) for a in (dec, qw, kw, cd))


def _fox_attn_kernel(q_ref, k_ref, v_ref, o_ref, s_a, s_b, s_c, cm_a, cm_b, cm_c, m_ref, acc_ref):
    T = FOX_TQ
    d = FOX_HEAD_DIM
    nq = q_ref.shape[2]
    ones_rows = (lax.broadcasted_iota(jnp.int32, (V_AUG - d, T), 0) == 0).astype(BF16)

    def scores(qi, j, s_ref, cm_ref):
        for hh in range(2):
            kj = k_ref[0, hh, pl.ds(pl.multiple_of(j * T, T), T), :]
            st = _dot(kj, q_ref[0, hh, qi])
            s_ref[hh] = st
            cm_ref[hh] = jnp.max(st, axis=0, keepdims=True)

    def consume(j, s_ref, cm_ref, masked):
        for hh in range(2):
            st = s_ref[hh]
            if masked:
                key = lax.broadcasted_iota(jnp.int32, (T, T), 0)
                qry = lax.broadcasted_iota(jnp.int32, (T, T), 1)
                st = jnp.where(key <= qry, st, -jnp.inf)
                cm = jnp.max(st, axis=0, keepdims=True)
            else:
                cm = cm_ref[hh]
            m = m_ref[hh]
            m_new = jnp.maximum(m, cm)
            p = jnp.exp2(st - m_new).astype(BF16)
            vj = jnp.concatenate([v_ref[0, j, hh * d:(hh + 1) * d, :], ones_rows], axis=0)
            acc_ref[hh] = jnp.exp2(m - m_new) * acc_ref[hh] + _dot(vj, p)
            m_ref[hh] = m_new

    def reset():
        m_ref[...] = jnp.full(m_ref.shape, -jnp.inf, F32)
        acc_ref[...] = jnp.zeros(acc_ref.shape, F32)

    def prefetch_next(qi):
        @pl.when(qi + 1 < nq)
        def _():
            scores(qi + 1, 0, s_c, cm_c)

    def finish(qi):
        outs = [acc_ref[hh, 0:d, :] / acc_ref[hh, d:d + 1, :] for hh in range(2)]
        o_ref[0, pl.ds(pl.multiple_of(qi * T, T), T), :] = jnp.concatenate(outs, axis=0).T.astype(BF16)

    reset()
    scores(0, 0, s_a, cm_a)
    prefetch_next(0)
    consume(0, s_a, cm_a, True)
    finish(0)

    def query_tile(qi, carry):
        reset()
        scores(qi, 1, s_a, cm_a)
        consume(0, s_c, cm_c, False)

        def pair(j):
            scores(qi, j + 1, s_b, cm_b)
            consume(j, s_a, cm_a, False)
            scores(qi, j + 2, s_a, cm_a)
            consume(j + 1, s_b, cm_b, False)

        def two_pairs(jj, c):
            pair(1 + 4 * jj)
            pair(3 + 4 * jj)
            return c

        def one_pair(jj, c):
            pair(1 + 4 * (n_pairs // 2) + 2 * jj)
            return c

        n_pairs = (qi - 1) // 2
        lax.fori_loop(0, n_pairs // 2, two_pairs, 0)
        lax.fori_loop(0, n_pairs % 2, one_pair, 0)

        @pl.when(qi % 2 == 1)
        def _():
            prefetch_next(qi)
            consume(qi, s_a, cm_a, True)

        @pl.when(qi % 2 == 0)
        def _():
            scores(qi, qi, s_b, cm_b)
            consume(qi - 1, s_a, cm_a, False)
            prefetch_next(qi)
            consume(qi, s_b, cm_b, True)

        finish(qi)
        return carry

    lax.fori_loop(1, nq, query_tile, 0)


def _fox_attn(qaug, kaug, fvt):
    B, H, S, _ = kaug.shape
    nk = S // FOX_TK
    nq = S // FOX_TQ
    score_buf = pltpu.VMEM((2, FOX_TK, FOX_TQ), F32)
    col_max = pltpu.VMEM((2, 1, FOX_TQ), F32)
    return pl.pallas_call(
        _fox_attn_kernel,
        grid=(B, H // 2),
        in_specs=[
            pl.BlockSpec((1, 2, nq, AUG, FOX_TQ), lambda b, p: (b, p, 0, 0, 0)),
            pl.BlockSpec((1, 2, S, AUG), lambda b, p: (b, p, 0, 0)),
            pl.BlockSpec((1, nk, 2 * FOX_HEAD_DIM, FOX_TK), lambda b, p: (b, 0, p, 0)),
        ],
        out_specs=pl.BlockSpec((1, S, 2 * FOX_HEAD_DIM), lambda b, p: (b, 0, p)),
        out_shape=jax.ShapeDtypeStruct((B, S, FOX_WIDTH), BF16),
        scratch_shapes=[
            score_buf, score_buf, score_buf, col_max, col_max, col_max,
            pltpu.VMEM((2, 1, FOX_TQ), F32), pltpu.VMEM((2, V_AUG, FOX_TQ), F32),
        ],
        compiler_params=pltpu.CompilerParams(
            dimension_semantics=("arbitrary",) * 2, vmem_limit_bytes=VMEM_LIMIT),
        name="fox_attn",
    )(qaug, kaug, fvt)


def _out_router_kernel(x_ref, oret_ref, ofox_ref, wo_ref, g_ref, wrh_ref, wrl_ref, br_ref,
                       h1_ref, u2_ref, sel_ref, cnt_ref):
    TM = MOE_TILE
    rows = lambda t: slice(t * TM, (t + 1) * TM)

    def out_proj(t):
        rs = rows(t)
        mix = jnp.concatenate([oret_ref[rs], ofox_ref[rs]], axis=1)
        h1 = x_ref[rs] + _dot(mix, wo_ref[...])
        h1_ref[rs] = h1
        return h1

    def router_logits(t, h1):
        u2 = _rms(h1, g_ref[...])
        uh = u2.astype(BF16)
        u2_ref[rows(t)] = uh
        ul = (u2 - uh.astype(F32)).astype(BF16)
        return (_dot_nt(wrh_ref[...], uh) + _dot_nt(wrh_ref[...], ul) + _dot_nt(wrl_ref[...], uh)
                + br_ref[...])

    def top_k(t, logits):
        rs = rows(t)
        E = N_EXPERTS
        row = lax.broadcasted_iota(jnp.int32, (E, TM), 0).astype(F32)
        l = logits
        picks, vals = [], []
        for _ in range(TOP_K):
            m = jnp.max(l, axis=0, keepdims=True)
            idx = jnp.min(jnp.where(l == m, row, float(LANES)), axis=0, keepdims=True)
            pick = row == idx
            picks.append(pick)
            vals.append(m)
            l = jnp.where(pick, -jnp.inf, l)
        exps = [jnp.exp(v - vals[0]) for v in vals]
        den = exps[0] + exps[1] + exps[2] + exps[3]
        sel_t = jnp.full((E, TM), -1.0, F32)
        for pick, e in zip(picks, exps):
            sel_t = jnp.where(pick, e / den, sel_t)
        sel_t = jnp.concatenate([sel_t, jnp.full((LANES - E, TM), -1.0, F32)], axis=0)
        sel = sel_t.T
        sel_ref[rs] = sel
        cnt = jnp.sum((sel >= 0.0).astype(F32), axis=0, keepdims=True)
        cnt_ref[t] = jnp.broadcast_to(cnt, (SUBLANES, LANES))

    n = ROUTER_TILES
    h1s, lgs = {0: out_proj(0)}, {}
    for t in range(1, n + 2):
        if t < n:
            h1s[t] = out_proj(t)
        if 1 <= t <= n:
            lgs[t - 1] = router_logits(t - 1, h1s.pop(t - 1))
        if t >= 2:
            top_k(t - 2, lgs.pop(t - 2))


def _out_router(x2, o_ret, o_fox, wo, g, wrh, wrl, br):
    T, D = x2.shape
    TM = MOE_TILE * ROUTER_TILES
    nT = T // MOE_TILE
    const = lambda a: pl.BlockSpec(a.shape, lambda i: (0,) * a.ndim)
    tok = lambda w: pl.BlockSpec((TM, w), lambda i: (i, 0))
    return pl.pallas_call(
        _out_router_kernel,
        grid=(T // TM,),
        in_specs=[tok(D), tok(RET_WIDTH), tok(FOX_WIDTH), const(wo), const(g),
                  const(wrh), const(wrl), const(br)],
        out_specs=(tok(D), tok(D), tok(LANES),
                   pl.BlockSpec((ROUTER_TILES, SUBLANES, LANES), lambda i: (i, 0, 0))),
        out_shape=(
            jax.ShapeDtypeStruct((T, D), F32),
            jax.ShapeDtypeStruct((T, D), BF16),
            jax.ShapeDtypeStruct((T, LANES), F32),
            jax.ShapeDtypeStruct((nT, SUBLANES, LANES), F32),
        ),
        compiler_params=pltpu.CompilerParams(dimension_semantics=("arbitrary",)),
        name="out_router",
    )(x2, o_ret, o_fox, wo, g, wrh, wrl, br)


def _tile_sort(sel):
    TM = sel.shape[0]
    NS = TOP_K * TM
    maskf = (sel >= 0.0).astype(F32)
    mask = maskf.astype(BF16)
    ri = lax.broadcasted_iota(jnp.int32, (TM, TM), 0)
    ci = lax.broadcasted_iota(jnp.int32, (TM, TM), 1)
    rank1 = maskf * _dot((ri >= ci).astype(BF16), mask)
    cnt = jnp.sum(maskf, axis=0, keepdims=True)
    ei = lax.broadcasted_iota(jnp.int32, (LANES, LANES), 0)
    ej = lax.broadcasted_iota(jnp.int32, (LANES, LANES), 1)
    cnt8 = jnp.broadcast_to(cnt, (SUBLANES, LANES)).astype(BF16)
    off = _dot(cnt8, (ei < ej).astype(BF16))[0:1, :]
    slot = lax.broadcasted_iota(jnp.int32, (NS, LANES), 0).astype(F32)
    esel = ((slot >= off) & (slot < off + cnt)).astype(BF16)
    return rank1.astype(BF16), esel, off, cnt


def _segment_wait(slot, local, remote_rows, sem, to_remote):
    whole = local.at[slot]
    rem = remote_rows.at[pl.ds(0, whole.shape[0]), :]
    cp = (pltpu.make_async_copy(whole, rem, sem.at[slot]) if to_remote
          else pltpu.make_async_copy(rem, whole, sem.at[slot]))
    cp.wait()


def _segment_dmas(step, slot, segdst_ref, cnt_ref, big_ref, local, remote_rows, sem, to_remote):
    big = big_ref[step] != 0
    for cond, top_bit in ((big, MOE_TILE), (jnp.logical_not(big), SMALL_RUN // 2)):
        pl.when(cond)(functools.partial(
            _segment_dma_path, step, slot, segdst_ref, cnt_ref, local, remote_rows, sem, to_remote, top_bit))


def _segment_dma_path(step, slot, segdst_ref, cnt_ref, local, remote_rows, sem, to_remote, top_bit):
    def body(e, off):
        c = cnt_ref[step * N_EXPERTS + e]
        dst = segdst_ref[step * N_EXPERTS + e]
        bit = top_bit
        while bit >= 1:
            done = c & (~(2 * bit - 1))

            @pl.when((c & bit) != 0)
            def _(bit=bit, done=done):
                loc = local.at[slot, pl.ds((off + done) * ROW_TILES, bit * ROW_TILES), :]
                rem = remote_rows.at[pl.ds((dst + done) * ROW_TILES, bit * ROW_TILES), :]
                cp = (pltpu.make_async_copy(loc, rem, sem.at[slot]) if to_remote
                      else pltpu.make_async_copy(rem, loc, sem.at[slot]))
                cp.start()
            bit //= 2
        return off + c

    off = 0
    for e in range(N_EXPERTS):
        off = body(e, off)


def _dispatch_kernel(segdst_ref, cnt_ref, big_ref, paddst_ref, padcnt_ref, nused_ref, u2_ref, sel_ref, xs_ref,
                     buf, zbuf, sems, zsem):
    i = pl.program_id(0)
    last = pl.num_programs(0) - 1
    slot = i % 2
    TM = MOE_TILE
    NS = TOP_K * TM
    rank1, esel, off, _ = _tile_sort(sel_ref[...])
    slot_id = lax.broadcasted_iota(jnp.int32, (NS, 1), 0).astype(F32)
    r_s = slot_id - jnp.sum(esel.astype(F32) * off, axis=1, keepdims=True)
    perm = (_dot_nt(esel, rank1) == r_s + 1.0).astype(BF16)

    @pl.when(i >= 2)
    def _():
        _segment_wait(slot, buf, xs_ref, sems, True)

    u2 = u2_ref[...]
    for c in range(NS // TM):
        rows = _dot(perm[c * TM:(c + 1) * TM], u2)
        for j in range(ROW_TILES):
            buf[slot, pl.ds(c * TM * ROW_TILES + j, TM, stride=ROW_TILES), :] = rows[:, j * LANES:(j + 1) * LANES]
    _segment_dmas(i, slot, segdst_ref, cnt_ref, big_ref, buf, xs_ref, sems, True)

    @pl.when(i == last)
    def _():
        @pl.when(i >= 1)
        def _():
            _segment_wait(1 - slot, buf, xs_ref, sems, True)
        _segment_wait(slot, buf, xs_ref, sems, True)
        zbuf[...] = jnp.zeros_like(zbuf)
        half = EXPERT_BLOCK // 2 * ROW_TILES
        n_blocks = xs_ref.shape[0] // (EXPERT_BLOCK * ROW_TILES)
        for wait in (False, True):
            def unused(hb, carry, wait=wait):
                cp = pltpu.make_async_copy(zbuf, xs_ref.at[pl.ds(hb * half, half), :], zsem.at[0])
                cp.wait() if wait else cp.start()
                return carry
            lax.fori_loop(2 * nused_ref[0], 2 * n_blocks, unused, 0)


            def body(e, carry, wait=wait):
                c = padcnt_ref[e]
                dst = paddst_ref[e]
                bit = EXPERT_BLOCK // 2
                while bit >= 1:
                    done = c & (~(2 * bit - 1))

                    @pl.when((c & bit) != 0)
                    def _(bit=bit, done=done):
                        cp = pltpu.make_async_copy(
                            zbuf.at[pl.ds(0, bit * ROW_TILES), :],
                            xs_ref.at[pl.ds((dst + done) * ROW_TILES, bit * ROW_TILES), :], zsem.at[0])
                        cp.wait() if wait else cp.start()
                    bit //= 2
                return carry
            lax.fori_loop(0, N_EXPERTS, body, 0)


def _dispatch(u2, sel, segdst, cnt, big, paddst, padcnt, n_used, n_rows):
    T, D = u2.shape
    TM = MOE_TILE
    NS = TOP_K * TM
    return pl.pallas_call(
        _dispatch_kernel,
        grid_spec=pltpu.PrefetchScalarGridSpec(
            num_scalar_prefetch=6,
            grid=(T // TM,),
            in_specs=[pl.BlockSpec((TM, D), lambda i, *_: (i, 0)),
                      pl.BlockSpec((TM, LANES), lambda i, *_: (i, 0))],
            out_specs=pl.BlockSpec(memory_space=pl.ANY),
            scratch_shapes=[pltpu.VMEM((2, NS * ROW_TILES, LANES), F32),
                            pltpu.VMEM((EXPERT_BLOCK // 2 * ROW_TILES, LANES), F32),
                            pltpu.SemaphoreType.DMA((2,)), pltpu.SemaphoreType.DMA((1,))],
        ),
        out_shape=jax.ShapeDtypeStruct((n_rows * ROW_TILES, LANES), F32),
        compiler_params=pltpu.CompilerParams(
            dimension_semantics=("arbitrary",), vmem_limit_bytes=VMEM_LIMIT),
        name="dispatch",
    )(segdst, cnt, big, paddst, padcnt, n_used, u2, sel)


def _expert_kernel(bexp_ref, nused_ref, epos_ref, enext_ref, xs_ref, w1_hbm, b1_ref, w2_hbm, b2_ref, ys_ref,
                   w1f, w2f, w1b, w2b, wsem):
    b = pl.program_id(0)
    BLK = EXPERT_BLOCK
    used = b < nused_ref[0]

    def weight_copies(e, slot):
        return (pltpu.make_async_copy(w1_hbm.at[e], w1f.at[slot], wsem.at[0, slot]),
                pltpu.make_async_copy(w2_hbm.at[e], w2f.at[slot], wsem.at[1, slot]))

    @pl.when(used)
    def _():
        e = bexp_ref[b]
        prev = bexp_ref[jnp.maximum(b - 1, 0)]
        slot = epos_ref[b] % 2

        @pl.when(b == 0)
        def _():
            for cp in weight_copies(e, slot):
                cp.start()

        @pl.when((b == 0) | (e != prev))
        def _():
            nxt = enext_ref[b]

            @pl.when(nxt >= 0)
            def _():
                for cp in weight_copies(nxt, 1 - slot):
                    cp.start()

            for cp in weight_copies(e, slot):
                cp.wait()
            rows = LANES

            def cast(r, carry):
                sl = pl.ds(pl.multiple_of(r * rows, rows), rows)
                w1b[sl, :] = w1f[slot, sl, :].astype(BF16)
                w2b[sl, :] = w2f[slot, sl, :].astype(BF16)
                return carry
            lax.fori_loop(0, D_MODEL // rows, cast, 0)

        R = EXPERT_PASS_ROWS
        for rp in range(BLK // R):
            r0 = rp * R * ROW_TILES
            x = jnp.concatenate([xs_ref[pl.ds(r0 + j, R, stride=ROW_TILES), :] for j in range(ROW_TILES)],
                                axis=1).astype(BF16)
            h = _dot(x, w1b[...]) + b1_ref[0]
            glu = jnp.minimum(h[:, :D_FF], SWIGLU_LIMIT)
            lin = jnp.clip(h[:, D_FF:], -SWIGLU_LIMIT, SWIGLU_LIMIT)
            act = glu * jax.nn.sigmoid(SWIGLU_ALPHA * glu) * (lin + 1.0)
            y = _dot(act.astype(BF16), w2b[...]) + b2_ref[0]
            for j in range(ROW_TILES):
                ys_ref[pl.ds(r0 + j, R, stride=ROW_TILES), :] = y[:, j * LANES:(j + 1) * LANES]

    @pl.when(jnp.logical_not(used))
    def _():
        ys_ref[...] = jnp.zeros_like(ys_ref)


def _experts(xs, block_exp, n_used, exp_pos, exp_next, w1, b1, w2, b2):
    BLK = EXPERT_BLOCK
    NB = xs.shape[0] // (BLK * ROW_TILES)
    blk = lambda b, nused: jnp.minimum(b, nused[0] - 1)
    return pl.pallas_call(
        _expert_kernel,
        grid_spec=pltpu.PrefetchScalarGridSpec(
            num_scalar_prefetch=4,
            grid=(NB,),
            in_specs=[
                pl.BlockSpec((BLK * ROW_TILES, LANES), lambda b, bexp, nused, *_: (blk(b, nused), 0)),
                pl.BlockSpec(memory_space=pl.ANY),
                pl.BlockSpec((1, 1, 2 * D_FF), lambda b, bexp, nused, *_: (bexp[blk(b, nused)], 0, 0)),
                pl.BlockSpec(memory_space=pl.ANY),
                pl.BlockSpec((1, 1, D_MODEL), lambda b, bexp, nused, *_: (bexp[blk(b, nused)], 0, 0)),
            ],
            out_specs=pl.BlockSpec((BLK * ROW_TILES, LANES), lambda b, *_: (b, 0)),
            scratch_shapes=[pltpu.VMEM((2, D_MODEL, 2 * D_FF), F32), pltpu.VMEM((2, D_FF, D_MODEL), F32),
                            pltpu.VMEM((D_MODEL, 2 * D_FF), BF16), pltpu.VMEM((D_FF, D_MODEL), BF16),
                            pltpu.SemaphoreType.DMA((2, 2))],
        ),
        out_shape=jax.ShapeDtypeStruct(xs.shape, F32),
        compiler_params=pltpu.CompilerParams(
            dimension_semantics=("arbitrary",), vmem_limit_bytes=VMEM_LIMIT_EXPERTS),
        name="experts",
    )(block_exp, n_used, exp_pos, exp_next, xs, w1, b1[:, None, :], w2, b2[:, None, :])


def _combine_kernel(segdst_ref, cnt_ref, big_ref, ys_ref, sel_ref, h1_ref, g_ref, out_ref, buf, sems):
    i = pl.program_id(0)
    n = pl.num_programs(0)
    slot = i % 2
    TM = MOE_TILE
    NS = TOP_K * TM

    @pl.when(i == 0)
    def _():
        _segment_dmas(i, slot, segdst_ref, cnt_ref, big_ref, buf, ys_ref, sems, False)

    @pl.when(i + 1 < n)
    def _():
        _segment_dmas(i + 1, 1 - slot, segdst_ref, cnt_ref, big_ref, buf, ys_ref, sems, False)

    sel = sel_ref[...]
    rank1, esel, off, _ = _tile_sort(sel)
    gate = jnp.maximum(sel, 0.0).astype(BF16)
    o3 = _split3(jnp.broadcast_to(off, (SUBLANES, LANES)))
    off_s = (_dot_nt(o3[0], esel) + _dot_nt(o3[1], esel) + _dot_nt(o3[2], esel))[0:1, :]
    r_s = lax.broadcasted_iota(jnp.int32, (1, NS), 1).astype(F32) - off_s
    hit = _dot_nt(rank1, esel) == r_s + 1.0
    unperm = jnp.where(hit, _dot_nt(gate, esel), 0.0).astype(BF16)

    _segment_wait(slot, buf, ys_ref, sems, False)
    y = jnp.concatenate([buf[slot, pl.ds(j, NS, stride=ROW_TILES), :] for j in range(ROW_TILES)],
                        axis=1).astype(BF16)
    h2 = h1_ref[...] + _dot(unperm, y)
    out_ref[...] = _rms(h2, g_ref[...])


def _combine(ys, sel, h1, g, segdst, cnt, big):
    T, D = h1.shape
    TM = MOE_TILE
    NS = TOP_K * TM
    return pl.pallas_call(
        _combine_kernel,
        grid_spec=pltpu.PrefetchScalarGridSpec(
            num_scalar_prefetch=3,
            grid=(T // TM,),
            in_specs=[pl.BlockSpec(memory_space=pl.ANY),
                      pl.BlockSpec((TM, LANES), lambda i, *_: (i, 0)),
                      pl.BlockSpec((TM, D), lambda i, *_: (i, 0)),
                      pl.BlockSpec((1, D), lambda i, *_: (0, 0))],
            out_specs=pl.BlockSpec((TM, D), lambda i, *_: (i, 0)),
            scratch_shapes=[pltpu.VMEM((2, NS * ROW_TILES, LANES), F32), pltpu.SemaphoreType.DMA((2,))],
        ),
        out_shape=jax.ShapeDtypeStruct((T, D), F32),
        compiler_params=pltpu.CompilerParams(
            dimension_semantics=("arbitrary",), vmem_limit_bytes=VMEM_LIMIT),
        name="combine",
    )(segdst, cnt, big, ys, sel, h1, g)


def _routing_tables(cnt_tiles):
    BLK = EXPERT_BLOCK
    nT = cnt_tiles.shape[0]
    A = nT * MOE_TILE * TOP_K
    NB = A // BLK + N_EXPERTS
    total = jnp.sum(cnt_tiles, axis=0)
    padded = (total + BLK - 1) // BLK * BLK
    pad_ends = jnp.cumsum(padded)
    pad_starts = pad_ends - padded
    before = jnp.cumsum(cnt_tiles, axis=0) - cnt_tiles
    segdst = (pad_starts[None, :] + before).reshape(-1).astype(jnp.int32)
    block_start = jnp.arange(NB, dtype=jnp.int32) * BLK
    block_exp = jnp.minimum(jnp.sum(pad_ends[None, :] <= block_start[:, None], axis=1), N_EXPERTS - 1).astype(jnp.int32)
    n_used = (pad_ends[-1] // BLK).astype(jnp.int32).reshape(1)
    paddst = (pad_starts + total).astype(jnp.int32)
    padcnt = (padded - total).astype(jnp.int32)
    big = jnp.any(cnt_tiles >= SMALL_RUN, axis=1).astype(jnp.int32)
    has_rows = total > 0
    ids = jnp.arange(N_EXPERTS, dtype=jnp.int32)
    pos = (jnp.cumsum(has_rows) - has_rows).astype(jnp.int32)
    later = jnp.where(has_rows[None, :] & (ids[None, :] > ids[:, None]), ids[None, :], N_EXPERTS)
    nxt = jnp.min(later, axis=1)
    nxt = jnp.where(nxt < N_EXPERTS, nxt, -1).astype(jnp.int32)
    of_block = block_exp[:, None] == ids[None, :]
    pos_b = jnp.sum(jnp.where(of_block, pos[None, :], 0), axis=1).astype(jnp.int32)
    nxt_b = jnp.sum(jnp.where(of_block, nxt[None, :], 0), axis=1).astype(jnp.int32)
    return (segdst, cnt_tiles.reshape(-1).astype(jnp.int32), big, paddst, padcnt, block_exp, n_used,
            pos_b, nxt_b, NB * BLK)


def _rotary_tables(S):
    half = RET_HEAD_DIM // 2
    f32 = np.float32
    inv_freq = np.power(f32(ROPE_BASE), -np.arange(half, dtype=f32) / f32(half)).astype(f32)
    ang = (np.arange(S, dtype=f32)[:, None] * inv_freq[None, :]).astype(f32)
    cos, sin = np.cos(ang).astype(f32), np.sin(ang).astype(f32)
    return (jnp.asarray(np.concatenate([cos, cos], axis=-1), F32),
            jnp.asarray(np.concatenate([-sin, sin], axis=-1), F32))


def _layer(h, norm_mix_g, w_in, b_forget, w_out, norm_ffn_g, w_router, b_router,
           w_exp_in, b_exp_in, w_exp_out, b_exp_out, final_g):
    B, S, D = h.shape
    R, Fw = RET_WIDTH, FOX_WIDTH
    cos, sin = _rotary_tables(S)
    wr = w_in[:, :4 * R].astype(BF16)
    wfq, wfk, wfv = (w_in[:, 4 * R + i * Fw:4 * R + (i + 1) * Fw].astype(BF16) for i in range(3))
    wzt = jnp.zeros((FZ_ROWS, D), BF16).at[:FOX_HEADS, :].set(w_in[:, 4 * R + 3 * Fw:].astype(BF16).T)
    o_ret, kaug, qaug, fvt = _in_proj(
        h, norm_mix_g[None, :], cos, sin, wr, wfk, wfq.T, wfv.T, wzt, b_forget)
    o_fox = _fox_attn(qaug, kaug, fvt)

    T = B * S
    wo = w_out.astype(BF16)
    wrt = w_router.T.astype(F32)
    wrh = wrt.astype(BF16)
    wrl = (wrt - wrh.astype(F32)).astype(BF16)
    br = b_router.astype(F32)[:, None]
    h1, u2, sel, cnt = _out_router(h.reshape(T, D), o_ret.reshape(T, R), o_fox.reshape(T, Fw),
                                   wo, norm_ffn_g[None, :], wrh, wrl, br)
    cnt_tiles = cnt[:, 0, :N_EXPERTS].astype(jnp.int32)
    segdst, cnt_flat, big, paddst, padcnt, block_exp, n_used, exp_pos, exp_next, n_rows = _routing_tables(cnt_tiles)
    xs = _dispatch(u2, sel, segdst, cnt_flat, big, paddst, padcnt, n_used, n_rows)
    ys = _experts(xs, block_exp, n_used, exp_pos, exp_next, w_exp_in, b_exp_in, w_exp_out, b_exp_out)
    out = _combine(ys, sel, h1, final_g[None, :], segdst, cnt_flat, big)
    return out.reshape(B, S, D)


def kernel(x, norm_mix_g, w_in, b_forget, w_out, norm_ffn_g, w_router, b_router,
           w_exp_in, b_exp_in, w_exp_out, b_exp_out, norm_final_g):
    depth = w_in.shape[0]
    assert depth == 1, "the fused final RMSNorm assumes a single layer"
    return _layer(x, norm_mix_g[0], w_in[0], b_forget[0], w_out[0], norm_ffn_g[0], w_router[0], b_router[0],
                  w_exp_in[0], b_exp_in[0], w_exp_out[0], b_exp_out[0], norm_final_g)
```

```python
import functools

import numpy as np
import jax
import jax.numpy as jnp
from jax import lax
from jax.experimental import pallas as pl
from jax.experimental.pallas import tpu as pltpu

F32 = jnp.float32
BF16 = jnp.bfloat16

D_MODEL = 1024
RET_HEADS, RET_HEAD_DIM = 4, 128
RET_WIDTH = RET_HEADS * RET_HEAD_DIM
FOX_HEADS, FOX_HEAD_DIM = 8, 64
FOX_WIDTH = FOX_HEADS * FOX_HEAD_DIM
CHUNK = 64
ROPE_BASE = 10000.0
N_EXPERTS = 32
TOP_K = 4
D_FF = D_MODEL
SWIGLU_ALPHA = 1.702
SWIGLU_LIMIT = 7.0
RMS_EPS = 1e-5
GN_EPS = 1e-5

LANES = 128
SUBLANES = 8
ROW_TILES = D_MODEL // LANES
V7X_VMEM_BYTES = 64 * 1024 * 1024
VMEM_LIMIT = V7X_VMEM_BYTES * 3 // 4
VMEM_LIMIT_EXPERTS = V7X_VMEM_BYTES * 7 // 8

PROJ_TILE = 512
RET_BLOCK = 256
FOX_TQ = 512
FOX_TK = PROJ_TILE
AUG = 128
V_AUG = 80
FZ_ROWS = 16
LOG2E = 1.4426950408889634
MOE_TILE = 256
ROUTER_TILES = 4
SMALL_RUN = 64
EXPERT_BLOCK = 512
EXPERT_PASS_ROWS = 256
EXPERT_ROW_BUFFERS = 3

NT_DIMS = (((1,), (1,)), ((), ()))


def _split3(a):
    hi = a.astype(BF16)
    r1 = a - hi.astype(F32)
    mid = r1.astype(BF16)
    lo = (r1 - mid.astype(F32)).astype(BF16)
    return hi, mid, lo


def _dot(a, b):
    return jnp.dot(a, b, preferred_element_type=F32)


def _dot_nt(a, b):
    return lax.dot_general(a, b, NT_DIMS, preferred_element_type=F32)


def _rms(x, g):
    return x * lax.rsqrt(jnp.mean(x * x, axis=-1, keepdims=True) + RMS_EPS) * g


def _in_proj_kernel(x_ref, g_ref, cos_ref, sin_ref, wr_ref, wfk_ref, wfqt_ref, wfvt_ref, wzt_ref, bcol_ref,
                    selkf_ref, constk_ref, selqf_ref, constq_ref, dec_ref, qw_ref, kw_ref, cd_ref,
                    oret_ref, kaug_ref, qaug_ref, fvt_ref, ccol, state):
    TM = x_ref.shape[1]
    d, H = FOX_HEAD_DIM, FOX_HEADS

    @pl.when(pl.program_id(1) == 0)
    def _():
        ccol[...] = jnp.zeros_like(ccol)
        state[...] = jnp.zeros_like(state)

    u = _rms(x_ref[0], g_ref[...]).astype(BF16)
    r = _dot(u, wr_ref[...])
    cos, sin = cos_ref[...], sin_ref[...]
    k_scale = RET_HEAD_DIM ** -0.5
    L = RET_BLOCK
    for h in range(RET_HEADS):
        hs = slice(h * RET_HEAD_DIM, (h + 1) * RET_HEAD_DIM)
        q = r[:, hs]
        k = r[:, RET_WIDTH + h * RET_HEAD_DIM:RET_WIDTH + (h + 1) * RET_HEAD_DIM]
        qh = (q * cos + pltpu.roll(q, RET_HEAD_DIM // 2, 1) * sin).astype(BF16)
        kh = ((k * cos + pltpu.roll(k, RET_HEAD_DIM // 2, 1) * sin) * k_scale).astype(BF16)
        vh = r[:, 2 * RET_WIDTH + h * RET_HEAD_DIM:2 * RET_WIDTH + (h + 1) * RET_HEAD_DIM].astype(BF16)
        gh = r[:, 3 * RET_WIDTH + h * RET_HEAD_DIM:3 * RET_WIDTH + (h + 1) * RET_HEAD_DIM].astype(BF16)
        for rs in (slice(b0, b0 + L) for b0 in range(0, TM, L)):
            qb, kb, vb = qh[rs], kh[rs], vh[rs]
            scores = (_dot_nt(qb, kb) * dec_ref[h]).astype(BF16)
            st = state[h]
            o = _dot(scores, vb) + _dot((qb.astype(F32) * qw_ref[h]).astype(BF16), st.astype(BF16))
            kk = kb.astype(F32) * kw_ref[h]
            state[h] = st * cd_ref[h, 0:1, :] + _dot(kk.T.astype(BF16), vb)
            mu = jnp.mean(o, axis=-1, keepdims=True)
            oc = o - mu
            var = jnp.mean(oc * oc, axis=-1, keepdims=True)
            oret_ref[0, rs, hs] = (oc * lax.rsqrt(var + GN_EPS) * jax.nn.silu(gh[rs].astype(F32))).astype(BF16)
    fvt_ref[0, 0] = _dot_nt(wfvt_ref[...], u).astype(BF16)
    fk = _dot(u, wfk_ref[...])
    fqt = (_dot_nt(wfqt_ref[...], u) * (d ** -0.5 * LOG2E)).astype(BF16)
    zt = _dot_nt(wzt_ref[...], u)


    row = lax.broadcasted_iota(jnp.int32, (FZ_ROWS, TM), 0)
    lft = jnp.where(row < H, jax.nn.log_sigmoid(zt + bcol_ref[...]), 0.0)
    utri = (lax.broadcasted_iota(jnp.int32, (TM, TM), 0) <= lax.broadcasted_iota(jnp.int32, (TM, TM), 1)).astype(BF16)
    t3 = _split3(lft)
    f_col = _dot(t3[0], utri) + _dot(t3[1], utri) + _dot(t3[2], utri) + ccol[:, 0:1]
    ccol[...] = jnp.broadcast_to(f_col[:, TM - 1:TM], ccol.shape)

    pieces_t = jnp.concatenate(_split3(f_col * LOG2E), axis=0)
    for h in range(H):
        extra = _dot(selqf_ref[h], pieces_t) + constq_ref[...]
        qaug_ref[0, h, 0] = jnp.concatenate([fqt[h * d:(h + 1) * d, :], extra.astype(BF16)], axis=0)

    f_row = jnp.concatenate([f_col, jnp.zeros((LANES - FZ_ROWS, TM), F32)], axis=0).T
    n3 = _split3(f_row * -LOG2E)
    pieces = (n3[0].astype(F32) + pltpu.roll(n3[1].astype(F32), H, 1)
              + pltpu.roll(n3[2].astype(F32), 2 * H, 1)).astype(BF16)
    lane = lax.broadcasted_iota(jnp.int32, (TM, LANES), 1)
    for g in range(H // 2):
        bias = _dot(pieces, selkf_ref[g])
        kg = fk[:, g * 2 * d:(g + 1) * 2 * d]
        for o in range(2):
            kh = kg if o == 0 else pltpu.roll(kg, d, 1)
            extra = bias[:, o * AUG:(o + 1) * AUG] + constk_ref[...]
            kaug_ref[0, 2 * g + o] = jnp.where(lane < d, kh, extra).astype(BF16)


def _in_proj(x, g, cos, sin, wr, wfk, wfqt, wfvt, wzt, b_forget):
    B, S, D = x.shape
    TM = PROJ_TILE
    ns = S // TM
    selkf, constk, selqf, constq = _fox_prep_constants()
    bcol = jnp.zeros((FZ_ROWS, 1), F32).at[:FOX_HEADS, 0].set(b_forget)
    consts = (wr, wfk, wfqt, wfvt, wzt, bcol, selkf, constk, selqf, constq) + _retention_tables()
    const = lambda a: pl.BlockSpec(a.shape, lambda b, s: (0,) * a.ndim)
    tok = lambda w: pl.BlockSpec((1, TM, w), lambda b, s: (b, s, 0))
    out_shape = (
        jax.ShapeDtypeStruct((B, S, RET_WIDTH), BF16),
        jax.ShapeDtypeStruct((B, FOX_HEADS, S, AUG), BF16),
        jax.ShapeDtypeStruct((B, FOX_HEADS, ns, AUG, TM), BF16),
        jax.ShapeDtypeStruct((B, ns, FOX_WIDTH, TM), BF16),
    )
    return pl.pallas_call(
        _in_proj_kernel,
        grid=(B, ns),
        in_specs=[
            pl.BlockSpec((1, TM, D), lambda b, s: (b, s, 0)),
            pl.BlockSpec((1, D), lambda b, s: (0, 0)),
            pl.BlockSpec((TM, RET_HEAD_DIM), lambda b, s: (s, 0)),
            pl.BlockSpec((TM, RET_HEAD_DIM), lambda b, s: (s, 0)),
        ] + [const(a) for a in consts],
        out_specs=(
            tok(RET_WIDTH),
            pl.BlockSpec((1, FOX_HEADS, TM, AUG), lambda b, s: (b, 0, s, 0)),
            pl.BlockSpec((1, FOX_HEADS, 1, AUG, TM), lambda b, s: (b, 0, s, 0, 0)),
            pl.BlockSpec((1, 1, FOX_WIDTH, TM), lambda b, s: (b, s, 0, 0)),
        ),
        out_shape=out_shape,
        scratch_shapes=[pltpu.VMEM((FZ_ROWS, LANES), F32),
                        pltpu.VMEM((RET_HEADS, RET_HEAD_DIM, RET_HEAD_DIM), F32)],
        compiler_params=pltpu.CompilerParams(
            dimension_semantics=("arbitrary", "arbitrary"), vmem_limit_bytes=VMEM_LIMIT),
        name="in_proj",
    )(x, g, cos, sin, *consts)


def _fox_prep_constants():
    d, H = FOX_HEAD_DIM, FOX_HEADS
    selkf = np.zeros((H // 2, LANES, 2 * AUG), np.float32)
    constk = np.zeros((1, AUG), np.float32)
    selqf = np.zeros((H, d, 3 * FZ_ROWS), np.float32)
    constq = np.zeros((d, 1), np.float32)
    for p in range(3):
        constk[0, d + p] = 1.0
        constq[3 + p, 0] = 1.0
        for h in range(H):
            selkf[h // 2, p * H + h, (h % 2) * AUG + d + 3 + p] = 1.0
            selqf[h, p, p * FZ_ROWS + h] = 1.0
    return jnp.asarray(selkf, BF16), jnp.asarray(constk, F32), jnp.asarray(selqf, BF16), jnp.asarray(constq, F32)


def _retention_tables():
    L = RET_BLOCK
    f32 = np.float32
    log_gamma = np.log1p(-np.exp2(-5.0 - np.arange(RET_HEADS, dtype=f32))).astype(f32)
    p = np.arange(L, dtype=f32)
    dist = np.abs(p[:, None] - p[None, :])
    chunk = np.arange(L) // CHUNK
    allowed = (chunk[None, :] <= chunk[:, None]).astype(f32)
    dec = np.exp(log_gamma[:, None, None] * dist).astype(f32) * allowed
    lanes = lambda a: np.broadcast_to(a.astype(f32)[:, :, None], (RET_HEADS, L, RET_HEAD_DIM))
    qw = lanes(np.exp(log_gamma[:, None] * (p[None, :] + f32(1.0))))
    kw = lanes(np.exp(log_gamma[:, None] * (f32(L - 1.0) - p[None, :])))
    cd = np.broadcast_to(np.exp(log_gamma * f32(L)).astype(f32)[:, None, None], (RET_HEADS, SUBLANES, RET_HEAD_DIM))
    return tuple(jnp.asarray(a, F32) for a in (dec, qw, kw, cd))


def _fox_attn_kernel(q_ref, k_ref, v_ref, o_ref, s_a, s_b, s_c, cm_a, cm_b, cm_c, m_ref, acc_ref):
    T = FOX_TQ
    d = FOX_HEAD_DIM
    nq = q_ref.shape[2]
    ones_rows = (lax.broadcasted_iota(jnp.int32, (V_AUG - d, T), 0) == 0).astype(BF16)

    def scores(qi, j, s_ref, cm_ref):
        for hh in range(2):
            kj = k_ref[0, hh, pl.ds(pl.multiple_of(j * T, T), T), :]
            st = _dot(kj, q_ref[0, hh, qi])
            s_ref[hh] = st
            cm_ref[hh] = jnp.max(st, axis=0, keepdims=True)

    def consume(j, s_ref, cm_ref, masked):
        for hh in range(2):
            st = s_ref[hh]
            if masked:
                key = lax.broadcasted_iota(jnp.int32, (T, T), 0)
                qry = lax.broadcasted_iota(jnp.int32, (T, T), 1)
                st = jnp.where(key <= qry, st, -jnp.inf)
                cm = jnp.max(st, axis=0, keepdims=True)
            else:
                cm = cm_ref[hh]
            m = m_ref[hh]
            m_new = jnp.maximum(m, cm)
            p = jnp.exp2(st - m_new).astype(BF16)
            vj = jnp.concatenate([v_ref[0, j, hh * d:(hh + 1) * d, :], ones_rows], axis=0)
            acc_ref[hh] = jnp.exp2(m - m_new) * acc_ref[hh] + _dot(vj, p)
            m_ref[hh] = m_new

    def reset():
        m_ref[...] = jnp.full(m_ref.shape, -jnp.inf, F32)
        acc_ref[...] = jnp.zeros(acc_ref.shape, F32)

    def prefetch_next(qi):
        @pl.when(qi + 1 < nq)
        def _():
            scores(qi + 1, 0, s_c, cm_c)

    def finish(qi):
        outs = [acc_ref[hh, 0:d, :] / acc_ref[hh, d:d + 1, :] for hh in range(2)]
        o_ref[0, pl.ds(pl.multiple_of(qi * T, T), T), :] = jnp.concatenate(outs, axis=0).T.astype(BF16)

    reset()
    scores(0, 0, s_a, cm_a)
    prefetch_next(0)
    consume(0, s_a, cm_a, True)
    finish(0)

    def query_tile(qi, carry):
        reset()
        scores(qi, 1, s_a, cm_a)
        consume(0, s_c, cm_c, False)

        def pair(j):
            scores(qi, j + 1, s_b, cm_b)
            consume(j, s_a, cm_a, False)
            scores(qi, j + 2, s_a, cm_a)
            consume(j + 1, s_b, cm_b, False)

        def two_pairs(jj, c):
            pair(1 + 4 * jj)
            pair(3 + 4 * jj)
            return c

        def one_pair(jj, c):
            pair(1 + 4 * (n_pairs // 2) + 2 * jj)
            return c

        n_pairs = (qi - 1) // 2
        lax.fori_loop(0, n_pairs // 2, two_pairs, 0)
        lax.fori_loop(0, n_pairs % 2, one_pair, 0)

        @pl.when(qi % 2 == 1)
        def _():
            prefetch_next(qi)
            consume(qi, s_a, cm_a, True)

        @pl.when(qi % 2 == 0)
        def _():
            scores(qi, qi, s_b, cm_b)
            consume(qi - 1, s_a, cm_a, False)
            prefetch_next(qi)
            consume(qi, s_b, cm_b, True)

        finish(qi)
        return carry

    lax.fori_loop(1, nq, query_tile, 0)


def _fox_attn(qaug, kaug, fvt):
    B, H, S, _ = kaug.shape
    nk = S // FOX_TK
    nq = S // FOX_TQ
    score_buf = pltpu.VMEM((2, FOX_TK, FOX_TQ), F32)
    col_max = pltpu.VMEM((2, 1, FOX_TQ), F32)
    return pl.pallas_call(
        _fox_attn_kernel,
        grid=(B, H // 2),
        in_specs=[
            pl.BlockSpec((1, 2, nq, AUG, FOX_TQ), lambda b, p: (b, p, 0, 0, 0)),
            pl.BlockSpec((1, 2, S, AUG), lambda b, p: (b, p, 0, 0)),
            pl.BlockSpec((1, nk, 2 * FOX_HEAD_DIM, FOX_TK), lambda b, p: (b, 0, p, 0)),
        ],
        out_specs=pl.BlockSpec((1, S, 2 * FOX_HEAD_DIM), lambda b, p: (b, 0, p)),
        out_shape=jax.ShapeDtypeStruct((B, S, FOX_WIDTH), BF16),
        scratch_shapes=[
            score_buf, score_buf, score_buf, col_max, col_max, col_max,
            pltpu.VMEM((2, 1, FOX_TQ), F32), pltpu.VMEM((2, V_AUG, FOX_TQ), F32),
        ],
        compiler_params=pltpu.CompilerParams(
            dimension_semantics=("arbitrary",) * 2, vmem_limit_bytes=VMEM_LIMIT),
        name="fox_attn",
    )(qaug, kaug, fvt)


def _out_router_kernel(x_ref, oret_ref, ofox_ref, wo_ref, g_ref, wrh_ref, wrl_ref, br_ref,
                       h1_ref, u2_ref, sel_ref, cnt_ref):
    TM = MOE_TILE
    rows = lambda t: slice(t * TM, (t + 1) * TM)

    def out_proj(t):
        rs = rows(t)
        mix = jnp.concatenate([oret_ref[rs], ofox_ref[rs]], axis=1)
        h1 = x_ref[rs] + _dot(mix, wo_ref[...])
        h1_ref[rs] = h1
        return h1

    def router_logits(t, h1):
        u2 = _rms(h1, g_ref[...])
        uh = u2.astype(BF16)
        u2_ref[rows(t)] = uh
        ul = (u2 - uh.astype(F32)).astype(BF16)
        return (_dot_nt(wrh_ref[...], uh) + _dot_nt(wrh_ref[...], ul) + _dot_nt(wrl_ref[...], uh)
                + br_ref[...])

    def top_k(t, logits):
        rs = rows(t)
        E = N_EXPERTS
        row = lax.broadcasted_iota(jnp.int32, (E, TM), 0).astype(F32)
        l = logits
        picks, vals = [], []
        for _ in range(TOP_K):
            m = jnp.max(l, axis=0, keepdims=True)
            idx = jnp.min(jnp.where(l == m, row, float(LANES)), axis=0, keepdims=True)
            pick = row == idx
            picks.append(pick)
            vals.append(m)
            l = jnp.where(pick, -jnp.inf, l)
        exps = [jnp.exp(v - vals[0]) for v in vals]
        den = exps[0] + exps[1] + exps[2] + exps[3]
        sel_t = jnp.full((E, TM), -1.0, F32)
        for pick, e in zip(picks, exps):
            sel_t = jnp.where(pick, e / den, sel_t)
        sel_t = jnp.concatenate([sel_t, jnp.full((LANES - E, TM), -1.0, F32)], axis=0)
        sel = sel_t.T
        sel_ref[rs] = sel
        cnt = jnp.sum((sel >= 0.0).astype(F32), axis=0, keepdims=True)
        cnt_ref[t] = jnp.broadcast_to(cnt, (SUBLANES, LANES))

    n = ROUTER_TILES
    h1s, lgs = {0: out_proj(0)}, {}
    for t in range(1, n + 2):
        if t < n:
            h1s[t] = out_proj(t)
        if 1 <= t <= n:
            lgs[t - 1] = router_logits(t - 1, h1s.pop(t - 1))
        if t >= 2:
            top_k(t - 2, lgs.pop(t - 2))


def _out_router(x2, o_ret, o_fox, wo, g, wrh, wrl, br):
    T, D = x2.shape
    TM = MOE_TILE * ROUTER_TILES
    nT = T // MOE_TILE
    const = lambda a: pl.BlockSpec(a.shape, lambda i: (0,) * a.ndim)
    tok = lambda w: pl.BlockSpec((TM, w), lambda i: (i, 0))
    return pl.pallas_call(
        _out_router_kernel,
        grid=(T // TM,),
        in_specs=[tok(D), tok(RET_WIDTH), tok(FOX_WIDTH), const(wo), const(g),
                  const(wrh), const(wrl), const(br)],
        out_specs=(tok(D), tok(D), tok(LANES),
                   pl.BlockSpec((ROUTER_TILES, SUBLANES, LANES), lambda i: (i, 0, 0))),
        out_shape=(
            jax.ShapeDtypeStruct((T, D), F32),
            jax.ShapeDtypeStruct((T, D), BF16),
            jax.ShapeDtypeStruct((T, LANES), F32),
            jax.ShapeDtypeStruct((nT, SUBLANES, LANES), F32),
        ),
        compiler_params=pltpu.CompilerParams(dimension_semantics=("arbitrary",)),
        name="out_router",
    )(x2, o_ret, o_fox, wo, g, wrh, wrl, br)


def _tile_sort(sel):
    TM = sel.shape[0]
    NS = TOP_K * TM
    maskf = (sel >= 0.0).astype(F32)
    mask = maskf.astype(BF16)
    ri = lax.broadcasted_iota(jnp.int32, (TM, TM), 0)
    ci = lax.broadcasted_iota(jnp.int32, (TM, TM), 1)
    rank1 = maskf * _dot((ri >= ci).astype(BF16), mask)
    cnt = jnp.sum(maskf, axis=0, keepdims=True)
    ei = lax.broadcasted_iota(jnp.int32, (LANES, LANES), 0)
    ej = lax.broadcasted_iota(jnp.int32, (LANES, LANES), 1)
    cnt8 = jnp.broadcast_to(cnt, (SUBLANES, LANES)).astype(BF16)
    off = _dot(cnt8, (ei < ej).astype(BF16))[0:1, :]
    slot = lax.broadcasted_iota(jnp.int32, (NS, LANES), 0).astype(F32)
    esel = ((slot >= off) & (slot < off + cnt)).astype(BF16)
    return rank1.astype(BF16), esel, off, cnt


def _segment_wait(slot, local, remote_rows, sem, to_remote):
    whole = local.at[slot]
    rem = remote_rows.at[pl.ds(0, whole.shape[0]), :]
    cp = (pltpu.make_async_copy(whole, rem, sem.at[slot]) if to_remote
          else pltpu.make_async_copy(rem, whole, sem.at[slot]))
    cp.wait()


def _segment_dmas(step, slot, segdst_ref, cnt_ref, big_ref, local, remote_rows, sem, to_remote):
    big = big_ref[step] != 0
    for cond, top_bit in ((big, MOE_TILE), (jnp.logical_not(big), SMALL_RUN // 2)):
        pl.when(cond)(functools.partial(
            _segment_dma_path, step, slot, segdst_ref, cnt_ref, local, remote_rows, sem, to_remote, top_bit))


def _segment_dma_path(step, slot, segdst_ref, cnt_ref, local, remote_rows, sem, to_remote, top_bit):
    def body(e, off):
        c = cnt_ref[step * N_EXPERTS + e]
        dst = segdst_ref[step * N_EXPERTS + e]
        bit = top_bit
        while bit >= 1:
            done = c & (~(2 * bit - 1))

            @pl.when((c & bit) != 0)
            def _(bit=bit, done=done):
                loc = local.at[slot, pl.ds((off + done) * ROW_TILES, bit * ROW_TILES), :]
                rem = remote_rows.at[pl.ds((dst + done) * ROW_TILES, bit * ROW_TILES), :]
                cp = (pltpu.make_async_copy(loc, rem, sem.at[slot]) if to_remote
                      else pltpu.make_async_copy(rem, loc, sem.at[slot]))
                cp.start()
            bit //= 2
        return off + c

    off = 0
    for e in range(N_EXPERTS):
        off = body(e, off)


def _dispatch_kernel(segdst_ref, cnt_ref, big_ref, paddst_ref, padcnt_ref, nused_ref, u2_ref, sel_ref, xs_ref,
                     buf, zbuf, sems, zsem):
    i = pl.program_id(0)
    last = pl.num_programs(0) - 1
    slot = i % 2
    TM = MOE_TILE
    NS = TOP_K * TM
    rank1, esel, off, _ = _tile_sort(sel_ref[...])
    slot_id = lax.broadcasted_iota(jnp.int32, (NS, 1), 0).astype(F32)
    r_s = slot_id - jnp.sum(esel.astype(F32) * off, axis=1, keepdims=True)
    perm = (_dot_nt(esel, rank1) == r_s + 1.0).astype(BF16)

    @pl.when(i >= 2)
    def _():
        _segment_wait(slot, buf, xs_ref, sems, True)

    u2 = u2_ref[...]
    for c in range(NS // TM):
        rows = _dot(perm[c * TM:(c + 1) * TM], u2)
        for j in range(ROW_TILES):
            buf[slot, pl.ds(c * TM * ROW_TILES + j, TM, stride=ROW_TILES), :] = rows[:, j * LANES:(j + 1) * LANES]
    _segment_dmas(i, slot, segdst_ref, cnt_ref, big_ref, buf, xs_ref, sems, True)

    @pl.when(i == last)
    def _():
        @pl.when(i >= 1)
        def _():
            _segment_wait(1 - slot, buf, xs_ref, sems, True)
        _segment_wait(slot, buf, xs_ref, sems, True)
        zbuf[...] = jnp.zeros_like(zbuf)
        half = EXPERT_BLOCK // 2 * ROW_TILES
        n_blocks = xs_ref.shape[0] // (EXPERT_BLOCK * ROW_TILES)
        for wait in (False, True):
            def unused(hb, carry, wait=wait):
                cp = pltpu.make_async_copy(zbuf, xs_ref.at[pl.ds(hb * half, half), :], zsem.at[0])
                cp.wait() if wait else cp.start()
                return carry
            lax.fori_loop(2 * nused_ref[0], 2 * n_blocks, unused, 0)


            def body(e, carry, wait=wait):
                c = padcnt_ref[e]
                dst = paddst_ref[e]
                bit = EXPERT_BLOCK // 2
                while bit >= 1:
                    done = c & (~(2 * bit - 1))

                    @pl.when((c & bit) != 0)
                    def _(bit=bit, done=done):
                        cp = pltpu.make_async_copy(
                            zbuf.at[pl.ds(0, bit * ROW_TILES), :],
                            xs_ref.at[pl.ds((dst + done) * ROW_TILES, bit * ROW_TILES), :], zsem.at[0])
                        cp.wait() if wait else cp.start()
                    bit //= 2
                return carry
            lax.fori_loop(0, N_EXPERTS, body, 0)


def _dispatch(u2, sel, segdst, cnt, big, paddst, padcnt, n_used, n_rows):
    T, D = u2.shape
    TM = MOE_TILE
    NS = TOP_K * TM
    return pl.pallas_call(
        _dispatch_kernel,
        grid_spec=pltpu.PrefetchScalarGridSpec(
            num_scalar_prefetch=6,
            grid=(T // TM,),
            in_specs=[pl.BlockSpec((TM, D), lambda i, *_: (i, 0)),
                      pl.BlockSpec((TM, LANES), lambda i, *_: (i, 0))],
            out_specs=pl.BlockSpec(memory_space=pl.ANY),
            scratch_shapes=[pltpu.VMEM((2, NS * ROW_TILES, LANES), F32),
                            pltpu.VMEM((EXPERT_BLOCK // 2 * ROW_TILES, LANES), F32),
                            pltpu.SemaphoreType.DMA((2,)), pltpu.SemaphoreType.DMA((1,))],
        ),
        out_shape=jax.ShapeDtypeStruct((n_rows * ROW_TILES, LANES), F32),
        compiler_params=pltpu.CompilerParams(
            dimension_semantics=("arbitrary",), vmem_limit_bytes=VMEM_LIMIT),
        name="dispatch",
    )(segdst, cnt, big, paddst, padcnt, n_used, u2, sel)


def _expert_kernel(bexp_ref, nused_ref, epos_ref, enext_ref, xs_hbm, w1_hbm, b1_ref, w2_hbm, b2_ref, ys_ref,
                   w1f, w2f, w1b, w2b, wsem, xbuf, xsem):
    b = pl.program_id(0)
    BLK = EXPERT_BLOCK
    n_used = nused_ref[0]
    used = b < n_used
    blk_rows = BLK * ROW_TILES
    n_xbuf = xbuf.shape[0]

    def x_copy(blk):
        src = xs_hbm.at[pl.ds(pl.multiple_of(blk * blk_rows, blk_rows), blk_rows), :]
        return pltpu.make_async_copy(src, xbuf.at[blk % n_xbuf], xsem.at[blk % n_xbuf])

    def weight_copies(e, slot):
        return (pltpu.make_async_copy(w1_hbm.at[e], w1f.at[slot], wsem.at[0, slot]),
                pltpu.make_async_copy(w2_hbm.at[e], w2f.at[slot], wsem.at[1, slot]))

    @pl.when(used)
    def _():
        e = bexp_ref[b]
        prev = bexp_ref[jnp.maximum(b - 1, 0)]
        slot = epos_ref[b] % 2

        @pl.when(b == 0)
        def _():
            for cp in weight_copies(e, slot):
                cp.start()

        @pl.when((b == 0) | (e != prev))
        def _():
            nxt = enext_ref[b]

            @pl.when(nxt >= 0)
            def _():
                for cp in weight_copies(nxt, 1 - slot):
                    cp.start()

            for cp in weight_copies(e, slot):
                cp.wait()
            rows = LANES

            def cast(r, carry):
                sl = pl.ds(pl.multiple_of(r * rows, rows), rows)
                w1b[sl, :] = w1f[slot, sl, :].astype(BF16)
                w2b[sl, :] = w2f[slot, sl, :].astype(BF16)
                return carry
            lax.fori_loop(0, D_MODEL // rows, cast, 0)

        @pl.when(b == 0)
        def _():
            for ahead in range(n_xbuf - 1):
                @pl.when(ahead < n_used)
                def _(ahead=ahead):
                    x_copy(ahead).start()

        @pl.when(b + n_xbuf - 1 < n_used)
        def _():
            x_copy(b + n_xbuf - 1).start()

        x_copy(b).wait()
        xs_ref = xbuf.at[b % n_xbuf]
        R = EXPERT_PASS_ROWS
        for rp in range(BLK // R):
            r0 = rp * R * ROW_TILES
            x = jnp.concatenate([xs_ref[pl.ds(r0 + j, R, stride=ROW_TILES), :] for j in range(ROW_TILES)],
                                axis=1).astype(BF16)
            h = _dot(x, w1b[...]) + b1_ref[0]
            glu = jnp.minimum(h[:, :D_FF], SWIGLU_LIMIT)
            lin = jnp.clip(h[:, D_FF:], -SWIGLU_LIMIT, SWIGLU_LIMIT)
            act = glu * jax.nn.sigmoid(SWIGLU_ALPHA * glu) * (lin + 1.0)
            y = _dot(act.astype(BF16), w2b[...]) + b2_ref[0]
            for j in range(ROW_TILES):
                ys_ref[pl.ds(r0 + j, R, stride=ROW_TILES), :] = y[:, j * LANES:(j + 1) * LANES]

    @pl.when(jnp.logical_not(used))
    def _():
        ys_ref[...] = jnp.zeros_like(ys_ref)


def _experts(xs, block_exp, n_used, exp_pos, exp_next, w1, b1, w2, b2):
    BLK = EXPERT_BLOCK
    NB = xs.shape[0] // (BLK * ROW_TILES)
    blk = lambda b, nused: jnp.minimum(b, nused[0] - 1)
    return pl.pallas_call(
        _expert_kernel,
        grid_spec=pltpu.PrefetchScalarGridSpec(
            num_scalar_prefetch=4,
            grid=(NB,),
            in_specs=[
                pl.BlockSpec(memory_space=pl.ANY),
                pl.BlockSpec(memory_space=pl.ANY),
                pl.BlockSpec((1, 1, 2 * D_FF), lambda b, bexp, nused, *_: (bexp[blk(b, nused)], 0, 0)),
                pl.BlockSpec(memory_space=pl.ANY),
                pl.BlockSpec((1, 1, D_MODEL), lambda b, bexp, nused, *_: (bexp[blk(b, nused)], 0, 0)),
            ],
            out_specs=pl.BlockSpec((BLK * ROW_TILES, LANES), lambda b, *_: (b, 0)),
            scratch_shapes=[pltpu.VMEM((2, D_MODEL, 2 * D_FF), F32), pltpu.VMEM((2, D_FF, D_MODEL), F32),
                            pltpu.VMEM((D_MODEL, 2 * D_FF), BF16), pltpu.VMEM((D_FF, D_MODEL), BF16),
                            pltpu.SemaphoreType.DMA((2, 2)),
                            pltpu.VMEM((EXPERT_ROW_BUFFERS, BLK * ROW_TILES, LANES), F32),
                            pltpu.SemaphoreType.DMA((EXPERT_ROW_BUFFERS,))],
        ),
        out_shape=jax.ShapeDtypeStruct(xs.shape, F32),
        compiler_params=pltpu.CompilerParams(
            dimension_semantics=("arbitrary",), vmem_limit_bytes=VMEM_LIMIT_EXPERTS),
        name="experts",
    )(block_exp, n_used, exp_pos, exp_next, xs, w1, b1[:, None, :], w2, b2[:, None, :])


def _combine_kernel(segdst_ref, cnt_ref, big_ref, ys_ref, sel_ref, h1_ref, g_ref, out_ref, buf, sems):
    i = pl.program_id(0)
    n = pl.num_programs(0)
    slot = i % 2
    TM = MOE_TILE
    NS = TOP_K * TM

    @pl.when(i == 0)
    def _():
        _segment_dmas(i, slot, segdst_ref, cnt_ref, big_ref, buf, ys_ref, sems, False)

    @pl.when(i + 1 < n)
    def _():
        _segment_dmas(i + 1, 1 - slot, segdst_ref, cnt_ref, big_ref, buf, ys_ref, sems, False)

    sel = sel_ref[...]
    rank1, esel, off, _ = _tile_sort(sel)
    gate = jnp.maximum(sel, 0.0).astype(BF16)
    o3 = _split3(jnp.broadcast_to(off, (SUBLANES, LANES)))
    off_s = (_dot_nt(o3[0], esel) + _dot_nt(o3[1], esel) + _dot_nt(o3[2], esel))[0:1, :]
    r_s = lax.broadcasted_iota(jnp.int32, (1, NS), 1).astype(F32) - off_s
    hit = _dot_nt(rank1, esel) == r_s + 1.0
    unperm = jnp.where(hit, _dot_nt(gate, esel), 0.0).astype(BF16)

    _segment_wait(slot, buf, ys_ref, sems, False)
    y = jnp.concatenate([buf[slot, pl.ds(j, NS, stride=ROW_TILES), :] for j in range(ROW_TILES)],
                        axis=1).astype(BF16)
    h2 = h1_ref[...] + _dot(unperm, y)
    out_ref[...] = _rms(h2, g_ref[...])


def _combine(ys, sel, h1, g, segdst, cnt, big):
    T, D = h1.shape
    TM = MOE_TILE
    NS = TOP_K * TM
    return pl.pallas_call(
        _combine_kernel,
        grid_spec=pltpu.PrefetchScalarGridSpec(
            num_scalar_prefetch=3,
            grid=(T // TM,),
            in_specs=[pl.BlockSpec(memory_space=pl.ANY),
                      pl.BlockSpec((TM, LANES), lambda i, *_: (i, 0)),
                      pl.BlockSpec((TM, D), lambda i, *_: (i, 0)),
                      pl.BlockSpec((1, D), lambda i, *_: (0, 0))],
            out_specs=pl.BlockSpec((TM, D), lambda i, *_: (i, 0)),
            scratch_shapes=[pltpu.VMEM((2, NS * ROW_TILES, LANES), F32), pltpu.SemaphoreType.DMA((2,))],
        ),
        out_shape=jax.ShapeDtypeStruct((T, D), F32),
        compiler_params=pltpu.CompilerParams(
            dimension_semantics=("arbitrary",), vmem_limit_bytes=VMEM_LIMIT),
        name="combine",
    )(segdst, cnt, big, ys, sel, h1, g)


def _routing_tables(cnt_tiles):
    BLK = EXPERT_BLOCK
    nT = cnt_tiles.shape[0]
    A = nT * MOE_TILE * TOP_K
    NB = A // BLK + N_EXPERTS
    total = jnp.sum(cnt_tiles, axis=0)
    padded = (total + BLK - 1) // BLK * BLK
    pad_ends = jnp.cumsum(padded)
    pad_starts = pad_ends - padded
    before = jnp.cumsum(cnt_tiles, axis=0) - cnt_tiles
    segdst = (pad_starts[None, :] + before).reshape(-1).astype(jnp.int32)
    block_start = jnp.arange(NB, dtype=jnp.int32) * BLK
    block_exp = jnp.minimum(jnp.sum(pad_ends[None, :] <= block_start[:, None], axis=1), N_EXPERTS - 1).astype(jnp.int32)
    n_used = (pad_ends[-1] // BLK).astype(jnp.int32).reshape(1)
    paddst = (pad_starts + total).astype(jnp.int32)
    padcnt = (padded - total).astype(jnp.int32)
    big = jnp.any(cnt_tiles >= SMALL_RUN, axis=1).astype(jnp.int32)
    has_rows = total > 0
    ids = jnp.arange(N_EXPERTS, dtype=jnp.int32)
    pos = (jnp.cumsum(has_rows) - has_rows).astype(jnp.int32)
    later = jnp.where(has_rows[None, :] & (ids[None, :] > ids[:, None]), ids[None, :], N_EXPERTS)
    nxt = jnp.min(later, axis=1)
    nxt = jnp.where(nxt < N_EXPERTS, nxt, -1).astype(jnp.int32)
    of_block = block_exp[:, None] == ids[None, :]
    pos_b = jnp.sum(jnp.where(of_block, pos[None, :], 0), axis=1).astype(jnp.int32)
    nxt_b = jnp.sum(jnp.where(of_block, nxt[None, :], 0), axis=1).astype(jnp.int32)
    return (segdst, cnt_tiles.reshape(-1).astype(jnp.int32), big, paddst, padcnt, block_exp, n_used,
            pos_b, nxt_b, NB * BLK)


def _rotary_tables(S):
    half = RET_HEAD_DIM // 2
    f32 = np.float32
    inv_freq = np.power(f32(ROPE_BASE), -np.arange(half, dtype=f32) / f32(half)).astype(f32)
    ang = (np.arange(S, dtype=f32)[:, None] * inv_freq[None, :]).astype(f32)
    cos, sin = np.cos(ang).astype(f32), np.sin(ang).astype(f32)
    return (jnp.asarray(np.concatenate([cos, cos], axis=-1), F32),
            jnp.asarray(np.concatenate([-sin, sin], axis=-1), F32))


def _layer(h, norm_mix_g, w_in, b_forget, w_out, norm_ffn_g, w_router, b_router,
           w_exp_in, b_exp_in, w_exp_out, b_exp_out, final_g):
    B, S, D = h.shape
    R, Fw = RET_WIDTH, FOX_WIDTH
    cos, sin = _rotary_tables(S)
    wr = w_in[:, :4 * R].astype(BF16)
    wfq, wfk, wfv = (w_in[:, 4 * R + i * Fw:4 * R + (i + 1) * Fw].astype(BF16) for i in range(3))
    wzt = jnp.zeros((FZ_ROWS, D), BF16).at[:FOX_HEADS, :].set(w_in[:, 4 * R + 3 * Fw:].astype(BF16).T)
    o_ret, kaug, qaug, fvt = _in_proj(
        h, norm_mix_g[None, :], cos, sin, wr, wfk, wfq.T, wfv.T, wzt, b_forget)
    o_fox = _fox_attn(qaug, kaug, fvt)

    T = B * S
    wo = w_out.astype(BF16)
    wrt = w_router.T.astype(F32)
    wrh = wrt.astype(BF16)
    wrl = (wrt - wrh.astype(F32)).astype(BF16)
    br = b_router.astype(F32)[:, None]
    h1, u2, sel, cnt = _out_router(h.reshape(T, D), o_ret.reshape(T, R), o_fox.reshape(T, Fw),
                                   wo, norm_ffn_g[None, :], wrh, wrl, br)
    cnt_tiles = cnt[:, 0, :N_EXPERTS].astype(jnp.int32)
    segdst, cnt_flat, big, paddst, padcnt, block_exp, n_used, exp_pos, exp_next, n_rows = _routing_tables(cnt_tiles)
    xs = _dispatch(u2, sel, segdst, cnt_flat, big, paddst, padcnt, n_used, n_rows)
    ys = _experts(xs, block_exp, n_used, exp_pos, exp_next, w_exp_in, b_exp_in, w_exp_out, b_exp_out)
    out = _combine(ys, sel, h1, final_g[None, :], segdst, cnt_flat, big)
    return out.reshape(B, S, D)


def kernel(x, norm_mix_g, w_in, b_forget, w_out, norm_ffn_g, w_router, b_router,
           w_exp_in, b_exp_in, w_exp_out, b_exp_out, norm_final_g):
    depth = w_in.shape[0]
    assert depth == 1, "the fused final RMSNorm assumes a single layer"
    return _layer(x, norm_mix_g[0], w_in[0], b_forget[0], w_out[0], norm_ffn_g[0], w_router[0], b_router[0],
                  w_exp_in[0], b_exp_in[0], w_exp_out[0], b_exp_out[0], norm_final_g)
```

```python
import functools

import numpy as np
import jax
import jax.numpy as jnp
from jax import lax
from jax.experimental import pallas as pl
from jax.experimental.pallas import tpu as pltpu

F32 = jnp.float32
BF16 = jnp.bfloat16

D_MODEL = 1024
RET_HEADS, RET_HEAD_DIM = 4, 128
RET_WIDTH = RET_HEADS * RET_HEAD_DIM
FOX_HEADS, FOX_HEAD_DIM = 8, 64
FOX_WIDTH = FOX_HEADS * FOX_HEAD_DIM
CHUNK = 64
ROPE_BASE = 10000.0
N_EXPERTS = 32
TOP_K = 4
D_FF = D_MODEL
SWIGLU_ALPHA = 1.702
SWIGLU_LIMIT = 7.0
RMS_EPS = 1e-5
GN_EPS = 1e-5

LANES = 128
SUBLANES = 8
ROW_TILES = D_MODEL // LANES
V7X_VMEM_BYTES = 64 * 1024 * 1024
VMEM_LIMIT = V7X_VMEM_BYTES * 3 // 4
VMEM_LIMIT_EXPERTS = V7X_VMEM_BYTES * 7 // 8

PROJ_TILE = 512
RET_BLOCK = 256
FOX_TQ = 512
FOX_TK = PROJ_TILE
AUG = 128
V_AUG = 80
FZ_ROWS = 16
LOG2E = 1.4426950408889634
MOE_TILE = 256
ROUTER_TILES = 8
SMALL_RUN = 64
EXPERT_BLOCK = 512
EXPERT_PASS_ROWS = 256

NT_DIMS = (((1,), (1,)), ((), ()))


def _split3(a):
    hi = a.astype(BF16)
    r1 = a - hi.astype(F32)
    mid = r1.astype(BF16)
    lo = (r1 - mid.astype(F32)).astype(BF16)
    return hi, mid, lo


def _dot(a, b):
    return jnp.dot(a, b, preferred_element_type=F32)


def _dot_nt(a, b):
    return lax.dot_general(a, b, NT_DIMS, preferred_element_type=F32)


def _rms(x, g):
    return x * lax.rsqrt(jnp.mean(x * x, axis=-1, keepdims=True) + RMS_EPS) * g


def _in_proj_kernel(x_ref, g_ref, cos_ref, sin_ref, wr_ref, wfk_ref, wfqt_ref, wfvt_ref, wzt_ref, bcol_ref,
                    selkf_ref, constk_ref, selqf_ref, constq_ref, dec_ref, qw_ref, kw_ref, cd_ref,
                    oret_ref, kaug_ref, qaug_ref, fvt_ref, ccol, state):
    TM = x_ref.shape[1]
    d, H = FOX_HEAD_DIM, FOX_HEADS

    @pl.when(pl.program_id(1) == 0)
    def _():
        ccol[...] = jnp.zeros_like(ccol)
        state[...] = jnp.zeros_like(state)

    u = _rms(x_ref[0], g_ref[...]).astype(BF16)
    r = _dot(u, wr_ref[...])
    cos, sin = cos_ref[...], sin_ref[...]
    k_scale = RET_HEAD_DIM ** -0.5
    L = RET_BLOCK
    for h in range(RET_HEADS):
        hs = slice(h * RET_HEAD_DIM, (h + 1) * RET_HEAD_DIM)
        q = r[:, hs]
        k = r[:, RET_WIDTH + h * RET_HEAD_DIM:RET_WIDTH + (h + 1) * RET_HEAD_DIM]
        qh = (q * cos + pltpu.roll(q, RET_HEAD_DIM // 2, 1) * sin).astype(BF16)
        kh = ((k * cos + pltpu.roll(k, RET_HEAD_DIM // 2, 1) * sin) * k_scale).astype(BF16)
        vh = r[:, 2 * RET_WIDTH + h * RET_HEAD_DIM:2 * RET_WIDTH + (h + 1) * RET_HEAD_DIM].astype(BF16)
        gh = r[:, 3 * RET_WIDTH + h * RET_HEAD_DIM:3 * RET_WIDTH + (h + 1) * RET_HEAD_DIM].astype(BF16)
        for rs in (slice(b0, b0 + L) for b0 in range(0, TM, L)):
            qb, kb, vb = qh[rs], kh[rs], vh[rs]
            scores = (_dot_nt(qb, kb) * dec_ref[h]).astype(BF16)
            st = state[h]
            o = _dot(scores, vb) + _dot((qb.astype(F32) * qw_ref[h]).astype(BF16), st.astype(BF16))
            kk = kb.astype(F32) * kw_ref[h]
            state[h] = st * cd_ref[h, 0:1, :] + _dot(kk.T.astype(BF16), vb)
            mu = jnp.mean(o, axis=-1, keepdims=True)
            oc = o - mu
            var = jnp.mean(oc * oc, axis=-1, keepdims=True)
            oret_ref[0, rs, hs] = (oc * lax.rsqrt(var + GN_EPS) * jax.nn.silu(gh[rs].astype(F32))).astype(BF16)
    fvt_ref[0, 0] = _dot_nt(wfvt_ref[...], u).astype(BF16)
    fk = _dot(u, wfk_ref[...])
    fqt = (_dot_nt(wfqt_ref[...], u) * (d ** -0.5 * LOG2E)).astype(BF16)
    zt = _dot_nt(wzt_ref[...], u)


    row = lax.broadcasted_iota(jnp.int32, (FZ_ROWS, TM), 0)
    lft = jnp.where(row < H, jax.nn.log_sigmoid(zt + bcol_ref[...]), 0.0)
    utri = (lax.broadcasted_iota(jnp.int32, (TM, TM), 0) <= lax.broadcasted_iota(jnp.int32, (TM, TM), 1)).astype(BF16)
    t3 = _split3(lft)
    f_col = _dot(t3[0], utri) + _dot(t3[1], utri) + _dot(t3[2], utri) + ccol[:, 0:1]
    ccol[...] = jnp.broadcast_to(f_col[:, TM - 1:TM], ccol.shape)

    pieces_t = jnp.concatenate(_split3(f_col * LOG2E), axis=0)
    for h in range(H):
        extra = _dot(selqf_ref[h], pieces_t) + constq_ref[...]
        qaug_ref[0, h, 0] = jnp.concatenate([fqt[h * d:(h + 1) * d, :], extra.astype(BF16)], axis=0)

    f_row = jnp.concatenate([f_col, jnp.zeros((LANES - FZ_ROWS, TM), F32)], axis=0).T
    n3 = _split3(f_row * -LOG2E)
    pieces = (n3[0].astype(F32) + pltpu.roll(n3[1].astype(F32), H, 1)
              + pltpu.roll(n3[2].astype(F32), 2 * H, 1)).astype(BF16)
    lane = lax.broadcasted_iota(jnp.int32, (TM, LANES), 1)
    for g in range(H // 2):
        bias = _dot(pieces, selkf_ref[g])
        kg = fk[:, g * 2 * d:(g + 1) * 2 * d]
        for o in range(2):
            kh = kg if o == 0 else pltpu.roll(kg, d, 1)
            extra = bias[:, o * AUG:(o + 1) * AUG] + constk_ref[...]
            kaug_ref[0, 2 * g + o] = jnp.where(lane < d, kh, extra).astype(BF16)


def _in_proj(x, g, cos, sin, wr, wfk, wfqt, wfvt, wzt, b_forget):
    B, S, D = x.shape
    TM = PROJ_TILE
    ns = S // TM
    selkf, constk, selqf, constq = _fox_prep_constants()
    bcol = jnp.zeros((FZ_ROWS, 1), F32).at[:FOX_HEADS, 0].set(b_forget)
    consts = (wr, wfk, wfqt, wfvt, wzt, bcol, selkf, constk, selqf, constq) + _retention_tables()
    const = lambda a: pl.BlockSpec(a.shape, lambda b, s: (0,) * a.ndim)
    tok = lambda w: pl.BlockSpec((1, TM, w), lambda b, s: (b, s, 0))
    out_shape = (
        jax.ShapeDtypeStruct((B, S, RET_WIDTH), BF16),
        jax.ShapeDtypeStruct((B, FOX_HEADS, S, AUG), BF16),
        jax.ShapeDtypeStruct((B, FOX_HEADS, ns, AUG, TM), BF16),
        jax.ShapeDtypeStruct((B, ns, FOX_WIDTH, TM), BF16),
    )
    return pl.pallas_call(
        _in_proj_kernel,
        grid=(B, ns),
        in_specs=[
            pl.BlockSpec((1, TM, D), lambda b, s: (b, s, 0)),
            pl.BlockSpec((1, D), lambda b, s: (0, 0)),
            pl.BlockSpec((TM, RET_HEAD_DIM), lambda b, s: (s, 0)),
            pl.BlockSpec((TM, RET_HEAD_DIM), lambda b, s: (s, 0)),
        ] + [const(a) for a in consts],
        out_specs=(
            tok(RET_WIDTH),
            pl.BlockSpec((1, FOX_HEADS, TM, AUG), lambda b, s: (b, 0, s, 0)),
            pl.BlockSpec((1, FOX_HEADS, 1, AUG, TM), lambda b, s: (b, 0, s, 0, 0)),
            pl.BlockSpec((1, 1, FOX_WIDTH, TM), lambda b, s: (b, s, 0, 0)),
        ),
        out_shape=out_shape,
        scratch_shapes=[pltpu.VMEM((FZ_ROWS, LANES), F32),
                        pltpu.VMEM((RET_HEADS, RET_HEAD_DIM, RET_HEAD_DIM), F32)],
        compiler_params=pltpu.CompilerParams(
            dimension_semantics=("arbitrary", "arbitrary"), vmem_limit_bytes=VMEM_LIMIT),
        name="in_proj",
    )(x, g, cos, sin, *consts)


def _fox_prep_constants():
    d, H = FOX_HEAD_DIM, FOX_HEADS
    selkf = np.zeros((H // 2, LANES, 2 * AUG), np.float32)
    constk = np.zeros((1, AUG), np.float32)
    selqf = np.zeros((H, d, 3 * FZ_ROWS), np.float32)
    constq = np.zeros((d, 1), np.float32)
    for p in range(3):
        constk[0, d + p] = 1.0
        constq[3 + p, 0] = 1.0
        for h in range(H):
            selkf[h // 2, p * H + h, (h % 2) * AUG + d + 3 + p] = 1.0
            selqf[h, p, p * FZ_ROWS + h] = 1.0
    return jnp.asarray(selkf, BF16), jnp.asarray(constk, F32), jnp.asarray(selqf, BF16), jnp.asarray(constq, F32)


def _retention_tables():
    L = RET_BLOCK
    f32 = np.float32
    log_gamma = np.log1p(-np.exp2(-5.0 - np.arange(RET_HEADS, dtype=f32))).astype(f32)
    p = np.arange(L, dtype=f32)
    dist = np.abs(p[:, None] - p[None, :])
    chunk = np.arange(L) // CHUNK
    allowed = (chunk[None, :] <= chunk[:, None]).astype(f32)
    dec = np.exp(log_gamma[:, None, None] * dist).astype(f32) * allowed
    lanes = lambda a: np.broadcast_to(a.astype(f32)[:, :, None], (RET_HEADS, L, RET_HEAD_DIM))
    qw = lanes(np.exp(log_gamma[:, None] * (p[None, :] + f32(1.0))))
    kw = lanes(np.exp(log_gamma[:, None] * (f32(L - 1.0) - p[None, :])))
    cd = np.broadcast_to(np.exp(log_gamma * f32(L)).astype(f32)[:, None, None], (RET_HEADS, SUBLANES, RET_HEAD_DIM))
    return tuple(jnp.asarray(a, F32) for a in (dec, qw, kw, cd))


def _fox_attn_kernel(q_ref, k_ref, v_ref, o_ref, s_a, s_b, s_c, cm_a, cm_b, cm_c, m_ref, acc_ref):
    T = FOX_TQ
    d = FOX_HEAD_DIM
    nq = q_ref.shape[2]
    ones_rows = (lax.broadcasted_iota(jnp.int32, (V_AUG - d, T), 0) == 0).astype(BF16)

    def scores(qi, j, s_ref, cm_ref):
        for hh in range(2):
            kj = k_ref[0, hh, pl.ds(pl.multiple_of(j * T, T), T), :]
            st = _dot(kj, q_ref[0, hh, qi])
            s_ref[hh] = st
            cm_ref[hh] = jnp.max(st, axis=0, keepdims=True)

    def consume(j, s_ref, cm_ref, masked):
        for hh in range(2):
            st = s_ref[hh]
            if masked:
                key = lax.broadcasted_iota(jnp.int32, (T, T), 0)
                qry = lax.broadcasted_iota(jnp.int32, (T, T), 1)
                st = jnp.where(key <= qry, st, -jnp.inf)
                cm = jnp.max(st, axis=0, keepdims=True)
            else:
                cm = cm_ref[hh]
            m = m_ref[hh]
            m_new = jnp.maximum(m, cm)
            p = jnp.exp2(st - m_new).astype(BF16)
            vj = jnp.concatenate([v_ref[0, j, hh * d:(hh + 1) * d, :], ones_rows], axis=0)
            acc_ref[hh] = jnp.exp2(m - m_new) * acc_ref[hh] + _dot(vj, p)
            m_ref[hh] = m_new

    def reset():
        m_ref[...] = jnp.full(m_ref.shape, -jnp.inf, F32)
        acc_ref[...] = jnp.zeros(acc_ref.shape, F32)

    def prefetch_next(qi):
        @pl.when(qi + 1 < nq)
        def _():
            scores(qi + 1, 0, s_c, cm_c)

    def finish(qi):
        outs = [acc_ref[hh, 0:d, :] / acc_ref[hh, d:d + 1, :] for hh in range(2)]
        o_ref[0, pl.ds(pl.multiple_of(qi * T, T), T), :] = jnp.concatenate(outs, axis=0).T.astype(BF16)

    reset()
    scores(0, 0, s_a, cm_a)
    prefetch_next(0)
    consume(0, s_a, cm_a, True)
    finish(0)

    def query_tile(qi, carry):
        reset()
        scores(qi, 1, s_a, cm_a)
        consume(0, s_c, cm_c, False)

        def pair(j):
            scores(qi, j + 1, s_b, cm_b)
            consume(j, s_a, cm_a, False)
            scores(qi, j + 2, s_a, cm_a)
            consume(j + 1, s_b, cm_b, False)

        def two_pairs(jj, c):
            pair(1 + 4 * jj)
            pair(3 + 4 * jj)
            return c

        def one_pair(jj, c):
            pair(1 + 4 * (n_pairs // 2) + 2 * jj)
            return c

        n_pairs = (qi - 1) // 2
        lax.fori_loop(0, n_pairs // 2, two_pairs, 0)
        lax.fori_loop(0, n_pairs % 2, one_pair, 0)

        @pl.when(qi % 2 == 1)
        def _():
            prefetch_next(qi)
            consume(qi, s_a, cm_a, True)

        @pl.when(qi % 2 == 0)
        def _():
            scores(qi, qi, s_b, cm_b)
            consume(qi - 1, s_a, cm_a, False)
            prefetch_next(qi)
            consume(qi, s_b, cm_b, True)

        finish(qi)
        return carry

    lax.fori_loop(1, nq, query_tile, 0)


def _fox_attn(qaug, kaug, fvt):
    B, H, S, _ = kaug.shape
    nk = S // FOX_TK
    nq = S // FOX_TQ
    score_buf = pltpu.VMEM((2, FOX_TK, FOX_TQ), F32)
    col_max = pltpu.VMEM((2, 1, FOX_TQ), F32)
    return pl.pallas_call(
        _fox_attn_kernel,
        grid=(B, H // 2),
        in_specs=[
            pl.BlockSpec((1, 2, nq, AUG, FOX_TQ), lambda b, p: (b, p, 0, 0, 0)),
            pl.BlockSpec((1, 2, S, AUG), lambda b, p: (b, p, 0, 0)),
            pl.BlockSpec((1, nk, 2 * FOX_HEAD_DIM, FOX_TK), lambda b, p: (b, 0, p, 0)),
        ],
        out_specs=pl.BlockSpec((1, S, 2 * FOX_HEAD_DIM), lambda b, p: (b, 0, p)),
        out_shape=jax.ShapeDtypeStruct((B, S, FOX_WIDTH), BF16),
        scratch_shapes=[
            score_buf, score_buf, score_buf, col_max, col_max, col_max,
            pltpu.VMEM((2, 1, FOX_TQ), F32), pltpu.VMEM((2, V_AUG, FOX_TQ), F32),
        ],
        compiler_params=pltpu.CompilerParams(
            dimension_semantics=("arbitrary",) * 2, vmem_limit_bytes=VMEM_LIMIT),
        name="fox_attn",
    )(qaug, kaug, fvt)


def _out_router_kernel(x_ref, oret_ref, ofox_ref, wo_ref, g_ref, wrh_ref, wrl_ref, br_ref,
                       h1_ref, u2_ref, sel_ref, cnt_ref):
    TM = MOE_TILE
    rows = lambda t: slice(t * TM, (t + 1) * TM)

    def out_proj(t):
        rs = rows(t)
        mix = jnp.concatenate([oret_ref[rs], ofox_ref[rs]], axis=1)
        h1 = x_ref[rs] + _dot(mix, wo_ref[...])
        h1_ref[rs] = h1
        return h1

    def router_logits(t, h1):
        u2 = _rms(h1, g_ref[...])
        uh = u2.astype(BF16)
        u2_ref[rows(t)] = uh
        ul = (u2 - uh.astype(F32)).astype(BF16)
        return (_dot_nt(wrh_ref[...], uh) + _dot_nt(wrh_ref[...], ul) + _dot_nt(wrl_ref[...], uh)
                + br_ref[...])

    def top_k(t, logits):
        rs = rows(t)
        E = N_EXPERTS
        row = lax.broadcasted_iota(jnp.int32, (E, TM), 0).astype(F32)
        l = logits
        picks, vals = [], []
        for _ in range(TOP_K):
            m = jnp.max(l, axis=0, keepdims=True)
            idx = jnp.min(jnp.where(l == m, row, float(LANES)), axis=0, keepdims=True)
            pick = row == idx
            picks.append(pick)
            vals.append(m)
            l = jnp.where(pick, -jnp.inf, l)
        exps = [jnp.exp(v - vals[0]) for v in vals]
        den = exps[0] + exps[1] + exps[2] + exps[3]
        sel_t = jnp.full((E, TM), -1.0, F32)
        for pick, e in zip(picks, exps):
            sel_t = jnp.where(pick, e / den, sel_t)
        sel_t = jnp.concatenate([sel_t, jnp.full((LANES - E, TM), -1.0, F32)], axis=0)
        sel = sel_t.T
        sel_ref[rs] = sel
        cnt = jnp.sum((sel >= 0.0).astype(F32), axis=0, keepdims=True)
        cnt_ref[t] = jnp.broadcast_to(cnt, (SUBLANES, LANES))

    n = ROUTER_TILES
    h1s, lgs = {0: out_proj(0)}, {}
    for t in range(1, n + 2):
        if t < n:
            h1s[t] = out_proj(t)
        if 1 <= t <= n:
            lgs[t - 1] = router_logits(t - 1, h1s.pop(t - 1))
        if t >= 2:
            top_k(t - 2, lgs.pop(t - 2))


def _out_router(x2, o_ret, o_fox, wo, g, wrh, wrl, br):
    T, D = x2.shape
    TM = MOE_TILE * ROUTER_TILES
    nT = T // MOE_TILE
    const = lambda a: pl.BlockSpec(a.shape, lambda i: (0,) * a.ndim)
    tok = lambda w: pl.BlockSpec((TM, w), lambda i: (i, 0))
    return pl.pallas_call(
        _out_router_kernel,
        grid=(T // TM,),
        in_specs=[tok(D), tok(RET_WIDTH), tok(FOX_WIDTH), const(wo), const(g),
                  const(wrh), const(wrl), const(br)],
        out_specs=(tok(D), tok(D), tok(LANES),
                   pl.BlockSpec((ROUTER_TILES, SUBLANES, LANES), lambda i: (i, 0, 0))),
        out_shape=(
            jax.ShapeDtypeStruct((T, D), F32),
            jax.ShapeDtypeStruct((T, D), BF16),
            jax.ShapeDtypeStruct((T, LANES), F32),
            jax.ShapeDtypeStruct((nT, SUBLANES, LANES), F32),
        ),
        compiler_params=pltpu.CompilerParams(dimension_semantics=("arbitrary",)),
        name="out_router",
    )(x2, o_ret, o_fox, wo, g, wrh, wrl, br)


def _tile_sort(sel):
    TM = sel.shape[0]
    NS = TOP_K * TM
    maskf = (sel >= 0.0).astype(F32)
    mask = maskf.astype(BF16)
    ri = lax.broadcasted_iota(jnp.int32, (TM, TM), 0)
    ci = lax.broadcasted_iota(jnp.int32, (TM, TM), 1)
    rank1 = maskf * _dot((ri >= ci).astype(BF16), mask)
    cnt = jnp.sum(maskf, axis=0, keepdims=True)
    ei = lax.broadcasted_iota(jnp.int32, (LANES, LANES), 0)
    ej = lax.broadcasted_iota(jnp.int32, (LANES, LANES), 1)
    cnt8 = jnp.broadcast_to(cnt, (SUBLANES, LANES)).astype(BF16)
    off = _dot(cnt8, (ei < ej).astype(BF16))[0:1, :]
    slot = lax.broadcasted_iota(jnp.int32, (NS, LANES), 0).astype(F32)
    esel = ((slot >= off) & (slot < off + cnt)).astype(BF16)
    return rank1.astype(BF16), esel, off, cnt


def _segment_wait(slot, local, remote_rows, sem, to_remote):
    whole = local.at[slot]
    rem = remote_rows.at[pl.ds(0, whole.shape[0]), :]
    cp = (pltpu.make_async_copy(whole, rem, sem.at[slot]) if to_remote
          else pltpu.make_async_copy(rem, whole, sem.at[slot]))
    cp.wait()


def _segment_dmas(step, slot, segdst_ref, cnt_ref, big_ref, local, remote_rows, sem, to_remote):
    big = big_ref[step] != 0
    for cond, top_bit in ((big, MOE_TILE), (jnp.logical_not(big), SMALL_RUN // 2)):
        pl.when(cond)(functools.partial(
            _segment_dma_path, step, slot, segdst_ref, cnt_ref, local, remote_rows, sem, to_remote, top_bit))


def _segment_dma_path(step, slot, segdst_ref, cnt_ref, local, remote_rows, sem, to_remote, top_bit):
    def body(e, off):
        c = cnt_ref[step * N_EXPERTS + e]
        dst = segdst_ref[step * N_EXPERTS + e]
        bit = top_bit
        while bit >= 1:
            done = c & (~(2 * bit - 1))

            @pl.when((c & bit) != 0)
            def _(bit=bit, done=done):
                loc = local.at[slot, pl.ds((off + done) * ROW_TILES, bit * ROW_TILES), :]
                rem = remote_rows.at[pl.ds((dst + done) * ROW_TILES, bit * ROW_TILES), :]
                cp = (pltpu.make_async_copy(loc, rem, sem.at[slot]) if to_remote
                      else pltpu.make_async_copy(rem, loc, sem.at[slot]))
                cp.start()
            bit //= 2
        return off + c

    off = 0
    for e in range(N_EXPERTS):
        off = body(e, off)


def _dispatch_kernel(segdst_ref, cnt_ref, big_ref, paddst_ref, padcnt_ref, nused_ref, u2_ref, sel_ref, xs_ref,
                     buf, zbuf, sems, zsem):
    i = pl.program_id(0)
    last = pl.num_programs(0) - 1
    slot = i % 2
    TM = MOE_TILE
    NS = TOP_K * TM
    rank1, esel, off, _ = _tile_sort(sel_ref[...])
    slot_id = lax.broadcasted_iota(jnp.int32, (NS, 1), 0).astype(F32)
    r_s = slot_id - jnp.sum(esel.astype(F32) * off, axis=1, keepdims=True)
    perm = (_dot_nt(esel, rank1) == r_s + 1.0).astype(BF16)

    @pl.when(i >= 2)
    def _():
        _segment_wait(slot, buf, xs_ref, sems, True)

    u2 = u2_ref[...]
    for c in range(NS // TM):
        rows = _dot(perm[c * TM:(c + 1) * TM], u2)
        for j in range(ROW_TILES):
            buf[slot, pl.ds(c * TM * ROW_TILES + j, TM, stride=ROW_TILES), :] = rows[:, j * LANES:(j + 1) * LANES]
    _segment_dmas(i, slot, segdst_ref, cnt_ref, big_ref, buf, xs_ref, sems, True)

    @pl.when(i == last)
    def _():
        @pl.when(i >= 1)
        def _():
            _segment_wait(1 - slot, buf, xs_ref, sems, True)
        _segment_wait(slot, buf, xs_ref, sems, True)
        zbuf[...] = jnp.zeros_like(zbuf)
        half = EXPERT_BLOCK // 2 * ROW_TILES
        n_blocks = xs_ref.shape[0] // (EXPERT_BLOCK * ROW_TILES)
        for wait in (False, True):
            def unused(hb, carry, wait=wait):
                cp = pltpu.make_async_copy(zbuf, xs_ref.at[pl.ds(hb * half, half), :], zsem.at[0])
                cp.wait() if wait else cp.start()
                return carry
            lax.fori_loop(2 * nused_ref[0], 2 * n_blocks, unused, 0)


            def body(e, carry, wait=wait):
                c = padcnt_ref[e]
                dst = paddst_ref[e]
                bit = EXPERT_BLOCK // 2
                while bit >= 1:
                    done = c & (~(2 * bit - 1))

                    @pl.when((c & bit) != 0)
                    def _(bit=bit, done=done):
                        cp = pltpu.make_async_copy(
                            zbuf.at[pl.ds(0, bit * ROW_TILES), :],
                            xs_ref.at[pl.ds((dst + done) * ROW_TILES, bit * ROW_TILES), :], zsem.at[0])
                        cp.wait() if wait else cp.start()
                    bit //= 2
                return carry
            lax.fori_loop(0, N_EXPERTS, body, 0)


def _dispatch(u2, sel, segdst, cnt, big, paddst, padcnt, n_used, n_rows):
    T, D = u2.shape
    TM = MOE_TILE
    NS = TOP_K * TM
    return pl.pallas_call(
        _dispatch_kernel,
        grid_spec=pltpu.PrefetchScalarGridSpec(
            num_scalar_prefetch=6,
            grid=(T // TM,),
            in_specs=[pl.BlockSpec((TM, D), lambda i, *_: (i, 0)),
                      pl.BlockSpec((TM, LANES), lambda i, *_: (i, 0))],
            out_specs=pl.BlockSpec(memory_space=pl.ANY),
            scratch_shapes=[pltpu.VMEM((2, NS * ROW_TILES, LANES), F32),
                            pltpu.VMEM((EXPERT_BLOCK // 2 * ROW_TILES, LANES), F32),
                            pltpu.SemaphoreType.DMA((2,)), pltpu.SemaphoreType.DMA((1,))],
        ),
        out_shape=jax.ShapeDtypeStruct((n_rows * ROW_TILES, LANES), F32),
        compiler_params=pltpu.CompilerParams(
            dimension_semantics=("arbitrary",), vmem_limit_bytes=VMEM_LIMIT),
        name="dispatch",
    )(segdst, cnt, big, paddst, padcnt, n_used, u2, sel)


def _expert_kernel(bexp_ref, nused_ref, epos_ref, enext_ref, xs_ref, w1_hbm, b1_ref, w2_hbm, b2_ref, ys_ref,
                   w1f, w2f, w1b, w2b, wsem):
    b = pl.program_id(0)
    BLK = EXPERT_BLOCK
    used = b < nused_ref[0]

    def weight_copies(e, slot):
        return (pltpu.make_async_copy(w1_hbm.at[e], w1f.at[slot], wsem.at[0, slot]),
                pltpu.make_async_copy(w2_hbm.at[e], w2f.at[slot], wsem.at[1, slot]))

    @pl.when(used)
    def _():
        e = bexp_ref[b]
        prev = bexp_ref[jnp.maximum(b - 1, 0)]
        slot = epos_ref[b] % 2

        @pl.when(b == 0)
        def _():
            for cp in weight_copies(e, slot):
                cp.start()

        @pl.when((b == 0) | (e != prev))
        def _():
            nxt = enext_ref[b]

            @pl.when(nxt >= 0)
            def _():
                for cp in weight_copies(nxt, 1 - slot):
                    cp.start()

            for cp in weight_copies(e, slot):
                cp.wait()
            rows = LANES

            def cast(r, carry):
                sl = pl.ds(pl.multiple_of(r * rows, rows), rows)
                w1b[sl, :] = w1f[slot, sl, :].astype(BF16)
                w2b[sl, :] = w2f[slot, sl, :].astype(BF16)
                return carry
            lax.fori_loop(0, D_MODEL // rows, cast, 0)

        R = EXPERT_PASS_ROWS
        for rp in range(BLK // R):
            r0 = rp * R * ROW_TILES
            x = jnp.concatenate([xs_ref[pl.ds(r0 + j, R, stride=ROW_TILES), :] for j in range(ROW_TILES)],
                                axis=1).astype(BF16)
            h = _dot(x, w1b[...]) + b1_ref[0]
            glu = jnp.minimum(h[:, :D_FF], SWIGLU_LIMIT)
            lin = jnp.clip(h[:, D_FF:], -SWIGLU_LIMIT, SWIGLU_LIMIT)
            act = glu * jax.nn.sigmoid(SWIGLU_ALPHA * glu) * (lin + 1.0)
            y = _dot(act.astype(BF16), w2b[...]) + b2_ref[0]
            for j in range(ROW_TILES):
                ys_ref[pl.ds(r0 + j, R, stride=ROW_TILES), :] = y[:, j * LANES:(j + 1) * LANES]

    @pl.when(jnp.logical_not(used))
    def _():
        ys_ref[...] = jnp.zeros_like(ys_ref)


def _experts(xs, block_exp, n_used, exp_pos, exp_next, w1, b1, w2, b2):
    BLK = EXPERT_BLOCK
    NB = xs.shape[0] // (BLK * ROW_TILES)
    blk = lambda b, nused: jnp.minimum(b, nused[0] - 1)
    return pl.pallas_call(
        _expert_kernel,
        grid_spec=pltpu.PrefetchScalarGridSpec(
            num_scalar_prefetch=4,
            grid=(NB,),
            in_specs=[
                pl.BlockSpec((BLK * ROW_TILES, LANES), lambda b, bexp, nused, *_: (blk(b, nused), 0)),
                pl.BlockSpec(memory_space=pl.ANY),
                pl.BlockSpec((1, 1, 2 * D_FF), lambda b, bexp, nused, *_: (bexp[blk(b, nused)], 0, 0)),
                pl.BlockSpec(memory_space=pl.ANY),
                pl.BlockSpec((1, 1, D_MODEL), lambda b, bexp, nused, *_: (bexp[blk(b, nused)], 0, 0)),
            ],
            out_specs=pl.BlockSpec((BLK * ROW_TILES, LANES), lambda b, *_: (b, 0)),
            scratch_shapes=[pltpu.VMEM((2, D_MODEL, 2 * D_FF), F32), pltpu.VMEM((2, D_FF, D_MODEL), F32),
                            pltpu.VMEM((D_MODEL, 2 * D_FF), BF16), pltpu.VMEM((D_FF, D_MODEL), BF16),
                            pltpu.SemaphoreType.DMA((2, 2))],
        ),
        out_shape=jax.ShapeDtypeStruct(xs.shape, F32),
        compiler_params=pltpu.CompilerParams(
            dimension_semantics=("arbitrary",), vmem_limit_bytes=VMEM_LIMIT_EXPERTS),
        name="experts",
    )(block_exp, n_used, exp_pos, exp_next, xs, w1, b1[:, None, :], w2, b2[:, None, :])


def _combine_kernel(segdst_ref, cnt_ref, big_ref, ys_ref, sel_ref, h1_ref, g_ref, out_ref, buf, sems):
    i = pl.program_id(0)
    n = pl.num_programs(0)
    slot = i % 2
    TM = MOE_TILE
    NS = TOP_K * TM

    @pl.when(i == 0)
    def _():
        _segment_dmas(i, slot, segdst_ref, cnt_ref, big_ref, buf, ys_ref, sems, False)

    @pl.when(i + 1 < n)
    def _():
        _segment_dmas(i + 1, 1 - slot, segdst_ref, cnt_ref, big_ref, buf, ys_ref, sems, False)

    sel = sel_ref[...]
    rank1, esel, off, _ = _tile_sort(sel)
    gate = jnp.maximum(sel, 0.0).astype(BF16)
    o3 = _split3(jnp.broadcast_to(off, (SUBLANES, LANES)))
    off_s = (_dot_nt(o3[0], esel) + _dot_nt(o3[1], esel) + _dot_nt(o3[2], esel))[0:1, :]
    r_s = lax.broadcasted_iota(jnp.int32, (1, NS), 1).astype(F32) - off_s
    hit = _dot_nt(rank1, esel) == r_s + 1.0
    unperm = jnp.where(hit, _dot_nt(gate, esel), 0.0).astype(BF16)

    _segment_wait(slot, buf, ys_ref, sems, False)
    y = jnp.concatenate([buf[slot, pl.ds(j, NS, stride=ROW_TILES), :] for j in range(ROW_TILES)],
                        axis=1).astype(BF16)
    h2 = h1_ref[...] + _dot(unperm, y)
    out_ref[...] = _rms(h2, g_ref[...])


def _combine(ys, sel, h1, g, segdst, cnt, big):
    T, D = h1.shape
    TM = MOE_TILE
    NS = TOP_K * TM
    return pl.pallas_call(
        _combine_kernel,
        grid_spec=pltpu.PrefetchScalarGridSpec(
            num_scalar_prefetch=3,
            grid=(T // TM,),
            in_specs=[pl.BlockSpec(memory_space=pl.ANY),
                      pl.BlockSpec((TM, LANES), lambda i, *_: (i, 0)),
                      pl.BlockSpec((TM, D), lambda i, *_: (i, 0)),
                      pl.BlockSpec((1, D), lambda i, *_: (0, 0))],
            out_specs=pl.BlockSpec((TM, D), lambda i, *_: (i, 0)),
            scratch_shapes=[pltpu.VMEM((2, NS * ROW_TILES, LANES), F32), pltpu.SemaphoreType.DMA((2,))],
        ),
        out_shape=jax.ShapeDtypeStruct((T, D), F32),
        compiler_params=pltpu.CompilerParams(
            dimension_semantics=("arbitrary",), vmem_limit_bytes=VMEM_LIMIT),
        name="combine",
    )(segdst, cnt, big, ys, sel, h1, g)


def _routing_tables(cnt_tiles):
    BLK = EXPERT_BLOCK
    nT = cnt_tiles.shape[0]
    A = nT * MOE_TILE * TOP_K
    NB = A // BLK + N_EXPERTS
    total = jnp.sum(cnt_tiles, axis=0)
    padded = (total + BLK - 1) // BLK * BLK
    pad_ends = jnp.cumsum(padded)
    pad_starts = pad_ends - padded
    before = jnp.cumsum(cnt_tiles, axis=0) - cnt_tiles
    segdst = (pad_starts[None, :] + before).reshape(-1).astype(jnp.int32)
    block_start = jnp.arange(NB, dtype=jnp.int32) * BLK
    block_exp = jnp.minimum(jnp.sum(pad_ends[None, :] <= block_start[:, None], axis=1), N_EXPERTS - 1).astype(jnp.int32)
    n_used = (pad_ends[-1] // BLK).astype(jnp.int32).reshape(1)
    paddst = (pad_starts + total).astype(jnp.int32)
    padcnt = (padded - total).astype(jnp.int32)
    big = jnp.any(cnt_tiles >= SMALL_RUN, axis=1).astype(jnp.int32)
    has_rows = total > 0
    ids = jnp.arange(N_EXPERTS, dtype=jnp.int32)
    pos = (jnp.cumsum(has_rows) - has_rows).astype(jnp.int32)
    later = jnp.where(has_rows[None, :] & (ids[None, :] > ids[:, None]), ids[None, :], N_EXPERTS)
    nxt = jnp.min(later, axis=1)
    nxt = jnp.where(nxt < N_EXPERTS, nxt, -1).astype(jnp.int32)
    of_block = block_exp[:, None] == ids[None, :]
    pos_b = jnp.sum(jnp.where(of_block, pos[None, :], 0), axis=1).astype(jnp.int32)
    nxt_b = jnp.sum(jnp.where(of_block, nxt[None, :], 0), axis=1).astype(jnp.int32)
    return (segdst, cnt_tiles.reshape(-1).astype(jnp.int32), big, paddst, padcnt, block_exp, n_used,
            pos_b, nxt_b, NB * BLK)


def _rotary_tables(S):
    half = RET_HEAD_DIM // 2
    f32 = np.float32
    inv_freq = np.power(f32(ROPE_BASE), -np.arange(half, dtype=f32) / f32(half)).astype(f32)
    ang = (np.arange(S, dtype=f32)[:, None] * inv_freq[None, :]).astype(f32)
    cos, sin = np.cos(ang).astype(f32), np.sin(ang).astype(f32)
    return (jnp.asarray(np.concatenate([cos, cos], axis=-1), F32),
            jnp.asarray(np.concatenate([-sin, sin], axis=-1), F32))


def _layer(h, norm_mix_g, w_in, b_forget, w_out, norm_ffn_g, w_router, b_router,
           w_exp_in, b_exp_in, w_exp_out, b_exp_out, final_g):
    B, S, D = h.shape
    R, Fw = RET_WIDTH, FOX_WIDTH
    cos, sin = _rotary_tables(S)
    wr = w_in[:, :4 * R].astype(BF16)
    wfq, wfk, wfv = (w_in[:, 4 * R + i * Fw:4 * R + (i + 1) * Fw].astype(BF16) for i in range(3))
    wzt = jnp.zeros((FZ_ROWS, D), BF16).at[:FOX_HEADS, :].set(w_in[:, 4 * R + 3 * Fw:].astype(BF16).T)
    o_ret, kaug, qaug, fvt = _in_proj(
        h, norm_mix_g[None, :], cos, sin, wr, wfk, wfq.T, wfv.T, wzt, b_forget)
    o_fox = _fox_attn(qaug, kaug, fvt)

    T = B * S
    wo = w_out.astype(BF16)
    wrt = w_router.T.astype(F32)
    wrh = wrt.astype(BF16)
    wrl = (wrt - wrh.astype(F32)).astype(BF16)
    br = b_router.astype(F32)[:, None]
    h1, u2, sel, cnt = _out_router(h.reshape(T, D), o_ret.reshape(T, R), o_fox.reshape(T, Fw),
                                   wo, norm_ffn_g[None, :], wrh, wrl, br)
    cnt_tiles = cnt[:, 0, :N_EXPERTS].astype(jnp.int32)
    segdst, cnt_flat, big, paddst, padcnt, block_exp, n_used, exp_pos, exp_next, n_rows = _routing_tables(cnt_tiles)
    xs = _dispatch(u2, sel, segdst, cnt_flat, big, paddst, padcnt, n_used, n_rows)
    ys = _experts(xs, block_exp, n_used, exp_pos, exp_next, w_exp_in, b_exp_in, w_exp_out, b_exp_out)
    out = _combine(ys, sel, h1, final_g[None, :], segdst, cnt_flat, big)
    return out.reshape(B, S, D)


def kernel(x, norm_mix_g, w_in, b_forget, w_out, norm_ffn_g, w_router, b_router,
           w_exp_in, b_exp_in, w_exp_out, b_exp_out, norm_final_g):
    depth = w_in.shape[0]
    assert depth == 1, "the fused final RMSNorm assumes a single layer"
    return _layer(x, norm_mix_g[0], w_in[0], b_forget[0], w_out[0], norm_ffn_g[0], w_router[0], b_router[0],
                  w_exp_in[0], b_exp_in[0], w_exp_out[0], b_exp_out[0], norm_final_g)
```
